```python
import jax, jax.numpy as jnp
from jax import lax
import numpy as np

D_MODEL = 1024
BATCH = 8
SEQ = 16384
DEPTH = 4

N_A_LAYERS = DEPTH // 2
N_B_LAYERS = DEPTH - N_A_LAYERS
POOL_WINDOWS = (2, 4, 8, 16)
N_POOL_GROUPS = len(POOL_WINDOWS)
POOL_GROUP = D_MODEL // N_POOL_GROUPS
HEAD_DIM = 64
N_HEADS = D_MODEL // HEAD_DIM
N_KV_HEADS = 4
GQA_GROUP = N_HEADS // N_KV_HEADS
WINDOW = 128
BLOCK = 128
ROPE_THETA = 10000.0
D_FF = ((8 * D_MODEL + 3 * 256 - 1) // (3 * 256)) * 256
PLE_DIM = 256
RMS_EPS = 1e-6
NEG_INF = -1e30

kernel_name = "yoco_pool_swa_sink_hybrid"


def rms_norm(x, g):
    xf = x.astype(jnp.float32)
    y = xf * lax.rsqrt(jnp.mean(xf * xf, axis=-1, keepdims=True) + RMS_EPS)
    return (y * g.astype(jnp.float32)).astype(x.dtype)


def rope_tables(seq):
    inv = 1.0 / (ROPE_THETA ** (jnp.arange(0, HEAD_DIM, 2, dtype=jnp.float32) / HEAD_DIM))
    ang = jnp.arange(seq, dtype=jnp.float32)[:, None] * inv[None, :]
    return jnp.cos(ang), jnp.sin(ang)


def apply_rope(x, cos, sin):
    xf = x.astype(jnp.float32)
    half = HEAD_DIM // 2
    x1, x2 = xf[..., :half], xf[..., half:]
    c = cos[None, :, None, :]
    s = sin[None, :, None, :]
    return jnp.concatenate([x1 * c - x2 * s, x2 * c + x1 * s], axis=-1).astype(x.dtype)


def multiscale_pool(x, w_pool, pool_scale):
    b, s, _ = x.shape
    xf = x.astype(jnp.float32).reshape(b, s, N_POOL_GROUPS, POOL_GROUP)
    cs = jnp.cumsum(xf, axis=1)
    t = jnp.arange(s)
    pooled = []
    for g, w in enumerate(POOL_WINDOWS):
        csg = cs[:, :, g]
        lower = jnp.concatenate([jnp.zeros((b, w, POOL_GROUP), jnp.float32), csg[:, :s - w]], axis=1)
        cnt = jnp.minimum(t + 1, w).astype(jnp.float32)[None, :, None]
        pooled.append((csg - lower) / cnt - xf[:, :, g])
    pooled = jnp.stack(pooled, axis=2).astype(x.dtype)
    y = jnp.einsum('bsgc,gcd->bsgd', pooled, w_pool).reshape(b, s, D_MODEL)
    return y * pool_scale


def sliding_window_sink_attention(q, k, v, sinks):
    b, s = q.shape[0], q.shape[1]
    nb = s // BLOCK
    qb = q.reshape(b, nb, BLOCK, N_KV_HEADS, GQA_GROUP, HEAD_DIM)

    def band(t):
        tb = t.reshape(b, nb, BLOCK, N_KV_HEADS, HEAD_DIM)
        prev = jnp.concatenate([jnp.zeros_like(tb[:, :1]), tb[:, :-1]], axis=1)
        return jnp.concatenate([prev, tb], axis=2)

    kb, vb = band(k), band(v)
    scores = jnp.einsum('bnqkgd,bnskd->bnkgqs', qb, kb).astype(jnp.float32) * (HEAD_DIM ** -0.5)
    blk = jnp.arange(nb)[:, None, None] * BLOCK
    qpos = blk + jnp.arange(BLOCK)[None, :, None]
    kpos = blk - BLOCK + jnp.arange(2 * BLOCK)[None, None, :]
    diff = qpos - kpos
    valid = (diff >= 0) & (diff < WINDOW) & (kpos >= 0)
    scores = jnp.where(valid[None, :, None, None], scores, NEG_INF)
    sink = sinks.astype(jnp.float32).reshape(N_KV_HEADS, GQA_GROUP)[None, None, :, :, None, None]
    sink = jnp.broadcast_to(sink, scores.shape[:-1] + (1,))
    probs = jax.nn.softmax(jnp.concatenate([scores, sink], axis=-1), axis=-1)[..., :-1]
    out = jnp.einsum('bnkgqs,bnskd->bnqkgd', probs.astype(v.dtype), vb)
    return out.reshape(b, s, N_HEADS * HEAD_DIM)


def swiglu(x, w_gate, w_up, w_down):
    return (jax.nn.silu(x @ w_gate) * (x @ w_up)) @ w_down


def _fwd_setup_inputs(seed: int = 0) -> dict:
    key = jax.random.key(seed)
    ks = jax.random.split(key, 20)
    f32 = jnp.float32

    def nrm(k, shape, fan_in):
        return jax.random.normal(k, shape, f32) * (fan_in ** -0.5)

    def gain(k, shape):
        return 1.0 + 0.05 * jax.random.normal(k, shape, f32)

    return {
        "x": jax.random.normal(ks[0], (BATCH, SEQ, D_MODEL), f32),
        "p": jax.random.normal(ks[1], (DEPTH, BATCH, SEQ, PLE_DIM), f32),
        "mix_pre_g": gain(ks[2], (DEPTH, D_MODEL)),
        "mix_post_g": gain(ks[3], (DEPTH, D_MODEL)),
        "ffn_pre_g": gain(ks[4], (DEPTH, D_MODEL)),
        "ffn_post_g": gain(ks[5], (DEPTH, D_MODEL)),
        "pool_w": nrm(ks[6], (N_A_LAYERS, N_POOL_GROUPS, POOL_GROUP, POOL_GROUP), POOL_GROUP),
        "pool_scale": 1.0 + 0.1 * jax.random.normal(ks[7], (N_A_LAYERS, D_MODEL), f32),
        "kv_norm_g": gain(ks[8], (D_MODEL,)),
        "w_k": nrm(ks[9], (D_MODEL, N_KV_HEADS * HEAD_DIM), D_MODEL),
        "w_v": nrm(ks[10], (D_MODEL, N_KV_HEADS * HEAD_DIM), D_MODEL),
        "w_q": nrm(ks[11], (N_B_LAYERS, D_MODEL, N_HEADS * HEAD_DIM), D_MODEL),
        "w_o": nrm(ks[12], (N_B_LAYERS, N_HEADS * HEAD_DIM, D_MODEL), N_HEADS * HEAD_DIM),
        "sinks": 0.5 * jax.random.normal(ks[13], (N_B_LAYERS, N_HEADS), f32),
        "w_ff_gate": nrm(ks[14], (DEPTH, D_MODEL, D_FF), D_MODEL),
        "w_ff_up": nrm(ks[15], (DEPTH, D_MODEL, D_FF), D_MODEL),
        "w_ff_down": nrm(ks[16], (DEPTH, D_FF, D_MODEL), D_FF),
        "ple_norm_g": gain(ks[17], (DEPTH, D_MODEL)),
        "w_ple_gate": nrm(ks[18], (DEPTH, D_MODEL, D_MODEL), D_MODEL),
        "w_ple_proj": nrm(ks[19], (DEPTH, PLE_DIM, D_MODEL), PLE_DIM),
    }


def _fwd_reference(x, p, mix_pre_g, mix_post_g, ffn_pre_g, ffn_post_g, pool_w, pool_scale,
              kv_norm_g, w_k, w_v, w_q, w_o, sinks, w_ff_gate, w_ff_up, w_ff_down,
              ple_norm_g, w_ple_gate, w_ple_proj):
    b, s, _ = x.shape
    cos, sin = rope_tables(s)
    h = x
    k_shared = None
    v_shared = None
    for i in range(DEPTH):
        hn = rms_norm(h, mix_pre_g[i])
        if i < N_A_LAYERS:
            m = multiscale_pool(hn, pool_w[i], pool_scale[i])
        else:
            j = i - N_A_LAYERS
            q = apply_rope((hn @ w_q[j]).reshape(b, s, N_HEADS, HEAD_DIM), cos, sin)
            m = sliding_window_sink_attention(q, k_shared, v_shared, sinks[j]) @ w_o[j]
        h = h + rms_norm(m, mix_post_g[i])
        f = swiglu(rms_norm(h, ffn_pre_g[i]), w_ff_gate[i], w_ff_up[i], w_ff_down[i])
        h = h + rms_norm(f, ffn_post_g[i])
        gate = jax.nn.sigmoid(rms_norm(h, ple_norm_g[i]) @ w_ple_gate[i])
        h = h + (p[i] @ w_ple_proj[i]) * gate
        if i == N_A_LAYERS - 1:
            hk = rms_norm(h, kv_norm_g)
            k_shared = apply_rope((hk @ w_k).reshape(b, s, N_KV_HEADS, HEAD_DIM), cos, sin)
            v_shared = (hk @ w_v).reshape(b, s, N_KV_HEADS, HEAD_DIM)
    return h


import jax as _jax
import jax.numpy as _jnp

TWIN_FORMAT = 'train_step'
FWD_PARAMS = ['x', 'p', 'mix_pre_g', 'mix_post_g', 'ffn_pre_g', 'ffn_post_g', 'pool_w', 'pool_scale', 'kv_norm_g', 'w_k', 'w_v', 'w_q', 'w_o', 'sinks', 'w_ff_gate', 'w_ff_up', 'w_ff_down', 'ple_norm_g', 'w_ple_gate', 'w_ple_proj']
TWIN_WEIGHTS = ['mix_pre_g', 'mix_post_g', 'ffn_pre_g', 'ffn_post_g', 'pool_w', 'pool_scale', 'kv_norm_g', 'w_k', 'w_v', 'w_q', 'w_o', 'sinks', 'w_ff_gate', 'w_ff_up', 'w_ff_down', 'ple_norm_g', 'w_ple_gate', 'w_ple_proj']
TWIN_DIFF_INPUT = 'x'
TWIN_INPUTS = ['x', 'p', 'mix_pre_g', 'mix_post_g', 'ffn_pre_g', 'ffn_post_g', 'pool_w', 'pool_scale', 'kv_norm_g', 'w_k', 'w_v', 'w_q', 'w_o', 'sinks', 'w_ff_gate', 'w_ff_up', 'w_ff_down', 'ple_norm_g', 'w_ple_gate', 'w_ple_proj', 'loss_target', 'm_mix_pre_g', 'm_mix_post_g', 'm_ffn_pre_g', 'm_ffn_post_g', 'm_pool_w', 'm_pool_scale', 'm_kv_norm_g', 'm_w_k', 'm_w_v', 'm_w_q', 'm_w_o', 'm_sinks', 'm_w_ff_gate', 'm_w_ff_up', 'm_w_ff_down', 'm_ple_norm_g', 'm_w_ple_gate', 'm_w_ple_proj', 'v_mix_pre_g', 'v_mix_post_g', 'v_ffn_pre_g', 'v_ffn_post_g', 'v_pool_w', 'v_pool_scale', 'v_kv_norm_g', 'v_w_k', 'v_w_v', 'v_w_q', 'v_w_o', 'v_sinks', 'v_w_ff_gate', 'v_w_ff_up', 'v_w_ff_down', 'v_ple_norm_g', 'v_w_ple_gate', 'v_w_ple_proj']
TWIN_OUTPUTS = ['loss', 'grad_x', 'grad_mix_pre_g', 'grad_mix_post_g', 'grad_ffn_pre_g', 'grad_ffn_post_g', 'grad_pool_w', 'grad_pool_scale', 'grad_kv_norm_g', 'grad_w_k', 'grad_w_v', 'grad_w_q', 'grad_w_o', 'grad_sinks', 'grad_w_ff_gate', 'grad_w_ff_up', 'grad_w_ff_down', 'grad_ple_norm_g', 'grad_w_ple_gate', 'grad_w_ple_proj', 'delta_mix_pre_g', 'delta_mix_post_g', 'delta_ffn_pre_g', 'delta_ffn_post_g', 'delta_pool_w', 'delta_pool_scale', 'delta_kv_norm_g', 'delta_w_k', 'delta_w_v', 'delta_w_q', 'delta_w_o', 'delta_sinks', 'delta_w_ff_gate', 'delta_w_ff_up', 'delta_w_ff_down', 'delta_ple_norm_g', 'delta_w_ple_gate', 'delta_w_ple_proj', 'new_m_mix_pre_g', 'new_m_mix_post_g', 'new_m_ffn_pre_g', 'new_m_ffn_post_g', 'new_m_pool_w', 'new_m_pool_scale', 'new_m_kv_norm_g', 'new_m_w_k', 'new_m_w_v', 'new_m_w_q', 'new_m_w_o', 'new_m_sinks', 'new_m_w_ff_gate', 'new_m_w_ff_up', 'new_m_w_ff_down', 'new_m_ple_norm_g', 'new_m_w_ple_gate', 'new_m_w_ple_proj', 'new_v_mix_pre_g', 'new_v_mix_post_g', 'new_v_ffn_pre_g', 'new_v_ffn_post_g', 'new_v_pool_w', 'new_v_pool_scale', 'new_v_kv_norm_g', 'new_v_w_k', 'new_v_w_v', 'new_v_w_q', 'new_v_w_o', 'new_v_sinks', 'new_v_w_ff_gate', 'new_v_w_ff_up', 'new_v_w_ff_down', 'new_v_ple_norm_g', 'new_v_w_ple_gate', 'new_v_w_ple_proj']
TWIN_LEAF_KINDS = {'loss': 'loss', 'grad_x': 'grad_x', 'grad_mix_pre_g': 'grad_w', 'grad_mix_post_g': 'grad_w', 'grad_ffn_pre_g': 'grad_w', 'grad_ffn_post_g': 'grad_w', 'grad_pool_w': 'grad_w', 'grad_pool_scale': 'grad_w', 'grad_kv_norm_g': 'grad_w', 'grad_w_k': 'grad_w', 'grad_w_v': 'grad_w', 'grad_w_q': 'grad_w', 'grad_w_o': 'grad_w', 'grad_sinks': 'grad_w', 'grad_w_ff_gate': 'grad_w', 'grad_w_ff_up': 'grad_w', 'grad_w_ff_down': 'grad_w', 'grad_ple_norm_g': 'grad_w', 'grad_w_ple_gate': 'grad_w', 'grad_w_ple_proj': 'grad_w', 'delta_mix_pre_g': 'delta_w', 'delta_mix_post_g': 'delta_w', 'delta_ffn_pre_g': 'delta_w', 'delta_ffn_post_g': 'delta_w', 'delta_pool_w': 'delta_w', 'delta_pool_scale': 'delta_w', 'delta_kv_norm_g': 'delta_w', 'delta_w_k': 'delta_w', 'delta_w_v': 'delta_w', 'delta_w_q': 'delta_w', 'delta_w_o': 'delta_w', 'delta_sinks': 'delta_w', 'delta_w_ff_gate': 'delta_w', 'delta_w_ff_up': 'delta_w', 'delta_w_ff_down': 'delta_w', 'delta_ple_norm_g': 'delta_w', 'delta_w_ple_gate': 'delta_w', 'delta_w_ple_proj': 'delta_w', 'new_m_mix_pre_g': 'new_m', 'new_m_mix_post_g': 'new_m', 'new_m_ffn_pre_g': 'new_m', 'new_m_ffn_post_g': 'new_m', 'new_m_pool_w': 'new_m', 'new_m_pool_scale': 'new_m', 'new_m_kv_norm_g': 'new_m', 'new_m_w_k': 'new_m', 'new_m_w_v': 'new_m', 'new_m_w_q': 'new_m', 'new_m_w_o': 'new_m', 'new_m_sinks': 'new_m', 'new_m_w_ff_gate': 'new_m', 'new_m_w_ff_up': 'new_m', 'new_m_w_ff_down': 'new_m', 'new_m_ple_norm_g': 'new_m', 'new_m_w_ple_gate': 'new_m', 'new_m_w_ple_proj': 'new_m', 'new_v_mix_pre_g': 'new_v', 'new_v_mix_post_g': 'new_v', 'new_v_ffn_pre_g': 'new_v', 'new_v_ffn_post_g': 'new_v', 'new_v_pool_w': 'new_v', 'new_v_pool_scale': 'new_v', 'new_v_kv_norm_g': 'new_v', 'new_v_w_k': 'new_v', 'new_v_w_v': 'new_v', 'new_v_w_q': 'new_v', 'new_v_w_o': 'new_v', 'new_v_sinks': 'new_v', 'new_v_w_ff_gate': 'new_v', 'new_v_w_ff_up': 'new_v', 'new_v_w_ff_down': 'new_v', 'new_v_ple_norm_g': 'new_v', 'new_v_w_ple_gate': 'new_v', 'new_v_w_ple_proj': 'new_v'}


def _forward(args):
    return _fwd_reference(*[args[k] for k in FWD_PARAMS])


def _output_shape():
    def fwd():
        inp = _fwd_setup_inputs(0)
        return _fwd_reference(*[inp[k] for k in FWD_PARAMS])
    out = _jax.eval_shape(fwd)
    return out.shape, out.dtype

N_MICROBATCH = 1
ADAM_LR = 0.001
ADAM_B1 = 0.9
ADAM_B2 = 0.999
ADAM_EPS = 1e-08
ADAM_WD = 0.01
ADAM_STEP = 10
PER_EXAMPLE_BATCH_AXIS = {'x': 0, 'p': 1, 'loss_target': 0}
SHARED_INPUTS = []
_WEIGHT_DTYPES = {'mix_pre_g': _jnp.float32, 'mix_post_g': _jnp.float32, 'ffn_pre_g': _jnp.float32, 'ffn_post_g': _jnp.float32, 'pool_w': _jnp.float32, 'pool_scale': _jnp.float32, 'kv_norm_g': _jnp.float32, 'w_k': _jnp.float32, 'w_v': _jnp.float32, 'w_q': _jnp.float32, 'w_o': _jnp.float32, 'sinks': _jnp.float32, 'w_ff_gate': _jnp.float32, 'w_ff_up': _jnp.float32, 'w_ff_down': _jnp.float32, 'ple_norm_g': _jnp.float32, 'w_ple_gate': _jnp.float32, 'w_ple_proj': _jnp.float32}
MOMENT_SCALE = {'mix_pre_g': 6.565965e+00, 'mix_post_g': 1.305026e+02, 'ffn_pre_g': 4.830092e+00, 'ffn_post_g': 1.271836e+02, 'pool_w': 8.331521e+00, 'pool_scale': 1.335702e+01, 'kv_norm_g': 5.163485e+00, 'w_k': 5.417277e+00, 'w_v': 8.166167e+00, 'w_q': 1.919992e+00, 'w_o': 3.397055e+00, 'sinks': 4.778217e-01, 'w_ff_gate': 1.526113e+00, 'w_ff_up': 1.912183e+00, 'w_ff_down': 3.254737e+00, 'ple_norm_g': 3.020119e+00, 'w_ple_gate': 5.547140e-01, 'w_ple_proj': 1.612344e+00}


def _to_microbatches(a, axis):
    t = _jnp.moveaxis(a, axis, 0)
    t = t.reshape((N_MICROBATCH, t.shape[0] // N_MICROBATCH) + t.shape[1:])
    return _jnp.moveaxis(t, 1, axis + 1)


def setup_inputs(seed: int = 0) -> dict:
    inp = _fwd_setup_inputs(seed)
    key = _jax.random.fold_in(_jax.random.key(seed), 7919)
    shape, _ = _output_shape()
    out = dict(inp)
    out["loss_target"] = _jax.random.normal(_jax.random.fold_in(key, 0), shape, _jnp.float32)
    for i, name in enumerate(TWIN_WEIGHTS):
        w = inp[name].astype(_jnp.float32)
        if MOMENT_SCALE is None:
            s = _jnp.sqrt(_jnp.mean(_jnp.square(w)) + 1e-30)
        else:
            s = MOMENT_SCALE[name]
        km, kv = _jax.random.split(_jax.random.fold_in(key, i + 1))
        out[name] = w
        out["m_" + name] = s * _jax.random.normal(km, w.shape, _jnp.float32)
        out["v_" + name] = (s * s) * _jax.random.uniform(kv, w.shape, _jnp.float32, 0.5, 1.5)
    if N_MICROBATCH > 1:
        for name, axis in PER_EXAMPLE_BATCH_AXIS.items():
            out[name] = _to_microbatches(out[name], axis)
    return {'x': out['x'], 'p': out['p'], 'mix_pre_g': out['mix_pre_g'], 'mix_post_g': out['mix_post_g'], 'ffn_pre_g': out['ffn_pre_g'], 'ffn_post_g': out['ffn_post_g'], 'pool_w': out['pool_w'], 'pool_scale': out['pool_scale'], 'kv_norm_g': out['kv_norm_g'], 'w_k': out['w_k'], 'w_v': out['w_v'], 'w_q': out['w_q'], 'w_o': out['w_o'], 'sinks': out['sinks'], 'w_ff_gate': out['w_ff_gate'], 'w_ff_up': out['w_ff_up'], 'w_ff_down': out['w_ff_down'], 'ple_norm_g': out['ple_norm_g'], 'w_ple_gate': out['w_ple_gate'], 'w_ple_proj': out['w_ple_proj'], 'loss_target': out['loss_target'], 'm_mix_pre_g': out['m_mix_pre_g'], 'm_mix_post_g': out['m_mix_post_g'], 'm_ffn_pre_g': out['m_ffn_pre_g'], 'm_ffn_post_g': out['m_ffn_post_g'], 'm_pool_w': out['m_pool_w'], 'm_pool_scale': out['m_pool_scale'], 'm_kv_norm_g': out['m_kv_norm_g'], 'm_w_k': out['m_w_k'], 'm_w_v': out['m_w_v'], 'm_w_q': out['m_w_q'], 'm_w_o': out['m_w_o'], 'm_sinks': out['m_sinks'], 'm_w_ff_gate': out['m_w_ff_gate'], 'm_w_ff_up': out['m_w_ff_up'], 'm_w_ff_down': out['m_w_ff_down'], 'm_ple_norm_g': out['m_ple_norm_g'], 'm_w_ple_gate': out['m_w_ple_gate'], 'm_w_ple_proj': out['m_w_ple_proj'], 'v_mix_pre_g': out['v_mix_pre_g'], 'v_mix_post_g': out['v_mix_post_g'], 'v_ffn_pre_g': out['v_ffn_pre_g'], 'v_ffn_post_g': out['v_ffn_post_g'], 'v_pool_w': out['v_pool_w'], 'v_pool_scale': out['v_pool_scale'], 'v_kv_norm_g': out['v_kv_norm_g'], 'v_w_k': out['v_w_k'], 'v_w_v': out['v_w_v'], 'v_w_q': out['v_w_q'], 'v_w_o': out['v_w_o'], 'v_sinks': out['v_sinks'], 'v_w_ff_gate': out['v_w_ff_gate'], 'v_w_ff_up': out['v_w_ff_up'], 'v_w_ff_down': out['v_w_ff_down'], 'v_ple_norm_g': out['v_ple_norm_g'], 'v_w_ple_gate': out['v_w_ple_gate'], 'v_w_ple_proj': out['v_w_ple_proj']}


def _loss(weights, diff, rest, loss_target):
    with _jax.named_scope("forward"):
        args = {**rest, TWIN_DIFF_INPUT: diff, **{k: w.astype(_WEIGHT_DTYPES[k]) for k, w in weights.items()}}
        y = _forward(args)
    with _jax.named_scope("loss_head"):
        err = _jnp.square(y.astype(_jnp.float32) - loss_target)
        return 0.5 * _jnp.sum(_jnp.mean(err, axis=-1)) if err.ndim else 0.5 * err


def _adamw(w, g, m, v):
    m = ADAM_B1 * m + (1.0 - ADAM_B1) * g
    v = ADAM_B2 * v + (1.0 - ADAM_B2) * _jnp.square(g)
    m_hat = m / (1.0 - ADAM_B1 ** ADAM_STEP)
    v_hat = v / (1.0 - ADAM_B2 ** ADAM_STEP)
    delta = -ADAM_LR * (m_hat / (_jnp.sqrt(v_hat) + ADAM_EPS) + ADAM_WD * w)
    return delta, m, v


def reference(x, p, mix_pre_g, mix_post_g, ffn_pre_g, ffn_post_g, pool_w, pool_scale, kv_norm_g, w_k, w_v, w_q, w_o, sinks, w_ff_gate, w_ff_up, w_ff_down, ple_norm_g, w_ple_gate, w_ple_proj, loss_target, m_mix_pre_g, m_mix_post_g, m_ffn_pre_g, m_ffn_post_g, m_pool_w, m_pool_scale, m_kv_norm_g, m_w_k, m_w_v, m_w_q, m_w_o, m_sinks, m_w_ff_gate, m_w_ff_up, m_w_ff_down, m_ple_norm_g, m_w_ple_gate, m_w_ple_proj, v_mix_pre_g, v_mix_post_g, v_ffn_pre_g, v_ffn_post_g, v_pool_w, v_pool_scale, v_kv_norm_g, v_w_k, v_w_v, v_w_q, v_w_o, v_sinks, v_w_ff_gate, v_w_ff_up, v_w_ff_down, v_ple_norm_g, v_w_ple_gate, v_w_ple_proj):
    given = dict(x=x, p=p, mix_pre_g=mix_pre_g, mix_post_g=mix_post_g, ffn_pre_g=ffn_pre_g, ffn_post_g=ffn_post_g, pool_w=pool_w, pool_scale=pool_scale, kv_norm_g=kv_norm_g, w_k=w_k, w_v=w_v, w_q=w_q, w_o=w_o, sinks=sinks, w_ff_gate=w_ff_gate, w_ff_up=w_ff_up, w_ff_down=w_ff_down, ple_norm_g=ple_norm_g, w_ple_gate=w_ple_gate, w_ple_proj=w_ple_proj, loss_target=loss_target, m_mix_pre_g=m_mix_pre_g, m_mix_post_g=m_mix_post_g, m_ffn_pre_g=m_ffn_pre_g, m_ffn_post_g=m_ffn_post_g, m_pool_w=m_pool_w, m_pool_scale=m_pool_scale, m_kv_norm_g=m_kv_norm_g, m_w_k=m_w_k, m_w_v=m_w_v, m_w_q=m_w_q, m_w_o=m_w_o, m_sinks=m_sinks, m_w_ff_gate=m_w_ff_gate, m_w_ff_up=m_w_ff_up, m_w_ff_down=m_w_ff_down, m_ple_norm_g=m_ple_norm_g, m_w_ple_gate=m_w_ple_gate, m_w_ple_proj=m_w_ple_proj, v_mix_pre_g=v_mix_pre_g, v_mix_post_g=v_mix_post_g, v_ffn_pre_g=v_ffn_pre_g, v_ffn_post_g=v_ffn_post_g, v_pool_w=v_pool_w, v_pool_scale=v_pool_scale, v_kv_norm_g=v_kv_norm_g, v_w_k=v_w_k, v_w_v=v_w_v, v_w_q=v_w_q, v_w_o=v_w_o, v_sinks=v_sinks, v_w_ff_gate=v_w_ff_gate, v_w_ff_up=v_w_ff_up, v_w_ff_down=v_w_ff_down, v_ple_norm_g=v_ple_norm_g, v_w_ple_gate=v_w_ple_gate, v_w_ple_proj=v_w_ple_proj)
    weights = {n: given[n] for n in TWIN_WEIGHTS}
    shared = {n: given[n] for n in SHARED_INPUTS}
    per_example = {n: given[n] for n in ['x', 'p']}
    grad_fn = _jax.value_and_grad(_loss, argnums=(0, 1))

    def one_microbatch(ex, loss_target):
        ex = dict(ex)
        diff = ex.pop(TWIN_DIFF_INPUT)
        return grad_fn(weights, diff, {**shared, **ex}, loss_target)

    if N_MICROBATCH == 1:
        loss, (grad_w, grad_x) = one_microbatch(per_example, given["loss_target"])
    else:
        def body(carry, xs):
            loss_sum, grad_sum = carry
            l_k, (gw_k, gx_k) = one_microbatch(xs[0], xs[1])
            with _jax.named_scope("update"):
                return (loss_sum + l_k, _jax.tree.map(_jnp.add, grad_sum, gw_k)), gx_k

        init = (_jnp.zeros((), _jnp.float32), _jax.tree.map(_jnp.zeros_like, weights))
        (loss, grad_w), grad_x = _jax.lax.scan(body, init, (per_example, given["loss_target"]))
    with _jax.named_scope("update"):
        delta_w, new_m, new_v = {}, {}, {}
        for n in TWIN_WEIGHTS:
            delta_w[n], new_m[n], new_v[n] = _adamw(weights[n], grad_w[n], given["m_" + n], given["v_" + n])
    return (loss, grad_x, *[grad_w[n] for n in TWIN_WEIGHTS], *[delta_w[n] for n in TWIN_WEIGHTS],
            *[new_m[n] for n in TWIN_WEIGHTS], *[new_v[n] for n in TWIN_WEIGHTS])
```

```python
import functools

import jax
import jax.numpy as jnp
from jax import lax
from jax.experimental import pallas as pl
from jax.experimental.pallas import tpu as pltpu

F32 = jnp.float32
CDT = jnp.bfloat16

D = 1024
PG = 256
POOL_WINDOWS = (2, 4, 8, 16)
HALO = 16
HD = 64
NKV = 4
WIN = 128
FS = 704
NSH = 4
PLE = 256
ROPE_THETA = 10000.0
EPS = 1e-6
NEG = -1e30
QSCALE = HD ** -0.5

ADAM_LR, ADAM_B1, ADAM_B2, ADAM_EPS, ADAM_WD, ADAM_STEP = 0.001, 0.9, 0.999, 1e-08, 0.01, 10

VMEM_LIMIT = 56 * 1024 * 1024
TM = 512
TM_FFN_BWD = 256
TQ = 256
TT = 1024
MESH = pl.DeviceIdType.MESH


def _cp(n_axes=1):
    return pltpu.CompilerParams(dimension_semantics=("arbitrary",) * n_axes, vmem_limit_bytes=VMEM_LIMIT)


def _dot(a, b):
    return jnp.dot(a, b, preferred_element_type=F32)


def _dot_nt(a, b):
    return lax.dot_general(a, b, (((1,), (1,)), ((), ())), preferred_element_type=F32)


def _dot_tn(a, b):
    return lax.dot_general(a, b, (((0,), (0,)), ((), ())), preferred_element_type=F32)


def _rms_fwd(x, g):
    r = lax.rsqrt(jnp.mean(x * x, axis=-1, keepdims=True) + EPS)
    return x * r * g


def _rms_bwd(dy, x, g):
    r = lax.rsqrt(jnp.mean(x * x, axis=-1, keepdims=True) + EPS)
    xh = x * r
    dg = jnp.sum(dy * xh, axis=0, keepdims=True)
    u = dy * g
    dx = r * (u - xh * jnp.mean(u * xh, axis=-1, keepdims=True))
    return dx, dg


def _sigmoid(a):
    return 1.0 / (1.0 + jnp.exp(-a))


def _swap_half(x):
    w = x.shape[1]
    lane = lax.broadcasted_iota(jnp.int32, x.shape, 1)
    return jnp.where((lane % HD) < HD // 2, pltpu.roll(x, w - HD // 2, axis=1), pltpu.roll(x, HD // 2, axis=1))


def _rope_fwd(x, cos, sin):
    return x * cos + _swap_half(x) * sin


def _rope_bwd(d, cos, sin):
    return d * cos + _swap_half(d * sin)


def _tile(tm, w):
    return pl.BlockSpec((tm, w), lambda i: (i, 0))


def _res(shape):
    return pl.BlockSpec(shape, lambda i: (0,) * len(shape), pipeline_mode=pl.Buffered(1))


def _layer_w(shape, l):
    return pl.BlockSpec((NSH, None) + shape, lambda i: (0, l, 0, 0), pipeline_mode=pl.Buffered(1))


def _acc(w):
    return pl.BlockSpec((1, w), lambda i: (0, 0))


def _zero_first(*refs):
    @pl.when(pl.program_id(0) == 0)
    def _():
        for r in refs:
            r[...] = jnp.zeros_like(r)


def _pool_normed_ext(h_ref, halo_ref, g, tm):
    xe = jnp.concatenate([halo_ref[...], h_ref[...]], axis=0)
    hn = _rms_fwd(xe, g)
    row = lax.broadcasted_iota(jnp.int32, (tm + HALO, 1), 0)
    return jnp.where((row >= HALO) | (pl.program_id(0) > 0), hn, 0.0)


def _pool_windows(hn_e, tm):
    t = lax.broadcasted_iota(jnp.int32, (tm, PG), 0) + pl.program_id(0) * tm
    outs = []
    for g, w in enumerate(POOL_WINDOWS):
        s = hn_e[:, g * PG:(g + 1) * PG]
        x = s[HALO:, :]
        k = 1
        while k < w:
            s = s + pltpu.roll(s, k, axis=0)
            k *= 2
        cnt = jnp.minimum(t + 1, w).astype(F32)
        outs.append(s[HALO:, :] / cnt - x)
    return jnp.concatenate(outs, axis=1)


def _pool_apply(pooled, w_ref):
    return jnp.concatenate([_dot(pooled[:, g * PG:(g + 1) * PG], w_ref[g]) for g in range(len(POOL_WINDOWS))], axis=1)


def _halo_prev(tm):
    return pl.BlockSpec((HALO, D), lambda i: (jnp.maximum(i * (tm // HALO) - 1, 0), 0))


def _pool_fwd(h, g_pre, pw, pscale, g_post, l):
    T = h.shape[0]
    tm = min(TM, T)

    def body(h_ref, halo_ref, gpre_ref, w_ref, sc_ref, gpost_ref, o_ref):
        hn_e = _pool_normed_ext(h_ref, halo_ref, gpre_ref[l:l + 1, :], tm)
        pooled = _pool_windows(hn_e, tm).astype(CDT)
        m = _pool_apply(pooled, w_ref) * sc_ref[l:l + 1, :]
        o_ref[...] = h_ref[...] + _rms_fwd(m, gpost_ref[l:l + 1, :])

    return pl.pallas_call(
        body, grid=(T // tm,), name=f"pool_fwd_{l}",
        in_specs=[_tile(tm, D), _halo_prev(tm), _res(g_pre.shape),
                  pl.BlockSpec((None,) + pw.shape[1:], lambda i: (l, 0, 0, 0), pipeline_mode=pl.Buffered(1)),
                  _res(pscale.shape), _res(g_post.shape)],
        out_specs=_tile(tm, D), out_shape=jax.ShapeDtypeStruct((T, D), F32), compiler_params=_cp(),
    )(h, h, g_pre, pw, pscale, g_post)


def _pool_bwd_a(dh1, h, g_pre, pw, pscale, g_post, l):
    T = h.shape[0]
    tm = min(TM, T)
    ng = len(POOL_WINDOWS)

    def body(dh_ref, h_ref, halo_ref, gpre_ref, w_ref, sc_ref, gpost_ref, dp_ref, dw_ref, dsc_ref, dgpost_ref):
        _zero_first(dw_ref, dsc_ref, dgpost_ref)
        hn_e = _pool_normed_ext(h_ref, halo_ref, gpre_ref[l:l + 1, :], tm)
        pooled = _pool_windows(hn_e, tm).astype(CDT)
        y = _pool_apply(pooled, w_ref)
        sc = sc_ref[l:l + 1, :]
        dm, dg = _rms_bwd(dh_ref[...], y * sc, gpost_ref[l:l + 1, :])
        dgpost_ref[...] += dg
        dsc_ref[...] += jnp.sum(dm * y, axis=0, keepdims=True)
        dy = (dm * sc).astype(CDT)
        dps = []
        for g in range(ng):
            dyg = dy[:, g * PG:(g + 1) * PG]
            dw_ref[g] += _dot_tn(pooled[:, g * PG:(g + 1) * PG], dyg)
            dps.append(_dot_nt(dyg, w_ref[g]))
        dp_ref[...] = jnp.concatenate(dps, axis=1)

    return pl.pallas_call(
        body, grid=(T // tm,), name=f"pool_bwd_a_{l}",
        in_specs=[_tile(tm, D), _tile(tm, D), _halo_prev(tm), _res(g_pre.shape),
                  pl.BlockSpec((None,) + pw.shape[1:], lambda i: (l, 0, 0, 0), pipeline_mode=pl.Buffered(1)),
                  _res(pscale.shape), _res(g_post.shape)],
        out_specs=[_tile(tm, D), pl.BlockSpec((ng, PG, PG), lambda i: (0, 0, 0)), _acc(D), _acc(D)],
        out_shape=[jax.ShapeDtypeStruct((T, D), F32), jax.ShapeDtypeStruct((ng, PG, PG), F32),
                   jax.ShapeDtypeStruct((1, D), F32), jax.ShapeDtypeStruct((1, D), F32)],
        compiler_params=_cp(),
    )(dh1, h, h, g_pre, pw, pscale, g_post)


def _pool_bwd_b(dpool, dh1, h, g_pre, l):
    T = h.shape[0]
    tm = min(TM, T)
    nt = T // tm

    def body(dp_ref, nxt_ref, dh_ref, h_ref, gpre_ref, o_ref, dgpre_ref):
        _zero_first(dgpre_ref)
        i = pl.program_id(0)
        dp = dp_ref[...]
        e_e = jnp.concatenate([dp, jnp.where(i < nt - 1, nxt_ref[...], 0.0)], axis=0)
        t = lax.broadcasted_iota(jnp.int32, (tm + HALO, PG), 0) + i * tm
        outs = []
        for g, w in enumerate(POOL_WINDOWS):
            s = e_e[:, g * PG:(g + 1) * PG] / jnp.minimum(t + 1, w).astype(F32)
            k = 1
            while k < w:
                s = s + pltpu.roll(s, tm + HALO - k, axis=0)
                k *= 2
            outs.append(s[:tm, :] - dp[:, g * PG:(g + 1) * PG])
        dx, dg = _rms_bwd(jnp.concatenate(outs, axis=1), h_ref[...], gpre_ref[l:l + 1, :])
        dgpre_ref[...] += dg
        o_ref[...] = dh_ref[...] + dx

    nxt = pl.BlockSpec((HALO, D), lambda i: (jnp.minimum((i + 1) * (tm // HALO), T // HALO - 1), 0))
    return pl.pallas_call(
        body, grid=(nt,), name=f"pool_bwd_b_{l}",
        in_specs=[_tile(tm, D), nxt, _tile(tm, D), _tile(tm, D), _res(g_pre.shape)],
        out_specs=[_tile(tm, D), _acc(D)],
        out_shape=[jax.ShapeDtypeStruct((T, D), F32), jax.ShapeDtypeStruct((1, D), F32)],
        compiler_params=_cp(),
    )(dpool, dpool, dh1, h, g_pre)


def _proj_rows(x, w_ref):
    k = w_ref.shape[1]
    out = _dot(x[:, :k], w_ref[0])
    for s in range(1, NSH):
        out = out + _dot(x[:, s * k:(s + 1) * k], w_ref[s])
    return out


def _proj_rows_t(dy, w_ref):
    return jnp.concatenate([_dot_nt(dy, w_ref[s]) for s in range(NSH)], axis=1)


def _rope_tiles(cos_ref, sin_ref, width):
    reps = width // cos_ref.shape[1]
    return jnp.tile(cos_ref[...], (1, reps)), jnp.tile(sin_ref[...], (1, reps))


def _q_fwd(h, g_pre, wq, cos, sin, l, j):
    T = h.shape[0]
    tm = min(TM, T)

    def body(h_ref, g_ref, w_ref, cos_ref, sin_ref, q_ref):
        hn = _rms_fwd(h_ref[...], g_ref[l:l + 1, :]).astype(CDT)
        c, s = _rope_tiles(cos_ref, sin_ref, D)
        q_ref[...] = (_rope_fwd(_proj_rows(hn, w_ref), c, s) * QSCALE).astype(CDT)

    return pl.pallas_call(
        body, grid=(T // tm,), name=f"q_fwd_{l}",
        in_specs=[_tile(tm, D), _res(g_pre.shape), _layer_w((D // NSH, D), j), _tile(tm, 2 * HD), _tile(tm, 2 * HD)],
        out_specs=_tile(tm, D), out_shape=jax.ShapeDtypeStruct((T, D), CDT), compiler_params=_cp(),
    )(h, g_pre, wq, cos, sin)


def _q_bwd(dq, dh1, h, g_pre, wq, cos, sin, l, j):
    T = h.shape[0]
    tm = min(TM, T)

    def body(dq_ref, dh_ref, h_ref, g_ref, w_ref, cos_ref, sin_ref, o_ref, hn_ref, dq0_ref, dg_ref):
        _zero_first(dg_ref)
        g = g_ref[l:l + 1, :]
        x = h_ref[...]
        hn_ref[...] = _rms_fwd(x, g).astype(CDT)
        c, s = _rope_tiles(cos_ref, sin_ref, D)
        dq0 = _rope_bwd(dq_ref[...].astype(F32) * QSCALE, c, s).astype(CDT)
        dq0_ref[...] = dq0
        dx, dg = _rms_bwd(_proj_rows_t(dq0, w_ref), x, g)
        dg_ref[...] += dg
        o_ref[...] = dh_ref[...] + dx

    return pl.pallas_call(
        body, grid=(T // tm,), name=f"q_bwd_{l}",
        in_specs=[_tile(tm, D), _tile(tm, D), _tile(tm, D), _res(g_pre.shape), _layer_w((D // NSH, D), j),
                  _tile(tm, 2 * HD), _tile(tm, 2 * HD)],
        out_specs=[_tile(tm, D), _tile(tm, D), _tile(tm, D), _acc(D)],
        out_shape=[jax.ShapeDtypeStruct((T, D), F32), jax.ShapeDtypeStruct((T, D), CDT),
                   jax.ShapeDtypeStruct((T, D), CDT), jax.ShapeDtypeStruct((1, D), F32)],
        compiler_params=_cp(),
    )(dq, dh1, h, g_pre, wq, cos, sin)


def _o_fwd(o, h, wo, g_post, l, j):
    T = h.shape[0]
    tm = min(TM, T)

    def body(o_ref, h_ref, w_ref, g_ref, h1_ref, m_ref):
        m = _proj_rows(o_ref[...], w_ref)
        m_ref[...] = m.astype(CDT)
        h1_ref[...] = h_ref[...] + _rms_fwd(m, g_ref[l:l + 1, :])

    return pl.pallas_call(
        body, grid=(T // tm,), name=f"o_fwd_{l}",
        in_specs=[_tile(tm, D), _tile(tm, D), _layer_w((D // NSH, D), j), _res(g_post.shape)],
        out_specs=[_tile(tm, D), _tile(tm, D)],
        out_shape=[jax.ShapeDtypeStruct((T, D), F32), jax.ShapeDtypeStruct((T, D), CDT)], compiler_params=_cp(),
    )(o, h, wo, g_post)


def _o_bwd(dh1, m, wo, g_post, l, j):
    T = m.shape[0]
    tm = min(TM, T)

    def body(dh_ref, m_ref, w_ref, g_ref, do_ref, dm_ref, dg_ref):
        _zero_first(dg_ref)
        dm, dg = _rms_bwd(dh_ref[...], m_ref[...].astype(F32), g_ref[l:l + 1, :])
        dg_ref[...] += dg
        dmc = dm.astype(CDT)
        dm_ref[...] = dmc
        do_ref[...] = _proj_rows_t(dmc, w_ref).astype(CDT)

    return pl.pallas_call(
        body, grid=(T // tm,), name=f"o_bwd_{l}",
        in_specs=[_tile(tm, D), _tile(tm, D), _layer_w((D // NSH, D), j), _res(g_post.shape)],
        out_specs=[_tile(tm, D), _tile(tm, D), _acc(D)],
        out_shape=[jax.ShapeDtypeStruct((T, D), CDT), jax.ShapeDtypeStruct((T, D), CDT),
                   jax.ShapeDtypeStruct((1, D), F32)],
        compiler_params=_cp(),
    )(dh1, m, wo, g_post)


KVW = 2 * NKV * HD


def _kv_fwd(h, g_kv, wk2, wv2, cos, sin):
    T = h.shape[0]
    tm = min(TM, T)

    def body(h_ref, g_ref, wk_ref, wv_ref, cos_ref, sin_ref, k_ref, v_ref):
        hk = _rms_fwd(h_ref[...], g_ref[...]).astype(CDT)
        c, s = _rope_tiles(cos_ref, sin_ref, KVW)
        k_ref[...] = _rope_fwd(_dot(hk, wk_ref[...]), c, s).astype(CDT)
        v_ref[...] = _dot(hk, wv_ref[...]).astype(CDT)

    return pl.pallas_call(
        body, grid=(T // tm,), name="kv_fwd",
        in_specs=[_tile(tm, D), _res(g_kv.shape), _res(wk2.shape), _res(wv2.shape), _tile(tm, 2 * HD), _tile(tm, 2 * HD)],
        out_specs=[_tile(tm, KVW), _tile(tm, KVW)],
        out_shape=[jax.ShapeDtypeStruct((T, KVW), CDT)] * 2, compiler_params=_cp(),
    )(h, g_kv, wk2, wv2, cos, sin)


def _kv_bwd(dh, h, dks, dkhs, dvs, dvhs, g_kv, wk2, wv2, cos, sin):
    T = h.shape[0]
    tq = min(TQ, T)
    nt = T // tq
    n = len(dks)

    def body(*refs):
        dh_ref, h_ref = refs[:2]
        main = refs[2:2 + 2 * n]
        halo = refs[2 + 2 * n:2 + 4 * n]
        g_ref, wk_ref, wv_ref, cos_ref, sin_ref, o_ref, hk_ref, dk0_ref, dv_ref, dg_ref = refs[2 + 4 * n:]
        _zero_first(dg_ref)
        i = pl.program_id(0)

        def total(mains, halos):
            d = mains[0][...]
            for r in mains[1:]:
                d = d + r[...]
            hl = halos[0][...]
            for r in halos[1:]:
                hl = hl + r[...]
            row = lax.broadcasted_iota(jnp.int32, (tq, 1), 0)
            if tq > WIN:
                hl = jnp.concatenate([jnp.zeros((tq - WIN, KVW), F32), hl], axis=0)
            return d + jnp.where((row >= tq - WIN) & (i < nt - 1), hl, 0.0)

        dk = total(main[:n], halo[:n])
        dv = total(main[n:], halo[n:])
        x = h_ref[...]
        g = g_ref[...]
        hk_ref[...] = _rms_fwd(x, g).astype(CDT)
        c, s = _rope_tiles(cos_ref, sin_ref, KVW)
        dk0 = _rope_bwd(dk, c, s).astype(CDT)
        dvc = dv.astype(CDT)
        dk0_ref[...] = dk0
        dv_ref[...] = dvc
        dx, dg = _rms_bwd(_dot_nt(dk0, wk_ref[...]) + _dot_nt(dvc, wv_ref[...]), x, g)
        dg_ref[...] += dg
        o_ref[...] = dh_ref[...] + dx

    nxt = pl.BlockSpec((WIN, KVW), lambda i: (jnp.minimum(i + 1, nt - 1), 0))
    return pl.pallas_call(
        body, grid=(nt,), name="kv_bwd",
        in_specs=[_tile(tq, D), _tile(tq, D)] + [_tile(tq, KVW)] * (2 * n) + [nxt] * (2 * n)
        + [_res(g_kv.shape), _res(wk2.shape), _res(wv2.shape), _tile(tq, 2 * HD), _tile(tq, 2 * HD)],
        out_specs=[_tile(tq, D), _tile(tq, D), _tile(tq, KVW), _tile(tq, KVW), _acc(D)],
        out_shape=[jax.ShapeDtypeStruct((T, D), F32), jax.ShapeDtypeStruct((T, D), CDT),
                   jax.ShapeDtypeStruct((T, KVW), CDT), jax.ShapeDtypeStruct((T, KVW), CDT),
                   jax.ShapeDtypeStruct((1, D), F32)],
        compiler_params=_cp(),
    )(dh, h, *dks, *dvs, *dkhs, *dvhs, g_kv, wk2, wv2, cos, sin)


GQ = 4
LW = 2 * HD


def _band_mask(first):
    r = lax.broadcasted_iota(jnp.int32, (GQ * WIN, 2 * WIN), 0) % WIN
    c = lax.broadcasted_iota(jnp.int32, (GQ * WIN, 2 * WIN), 1)
    return (c > r) & (c <= r + WIN) & ((c >= WIN) | jnp.logical_not(first))


def _stack_heads(ref, rows, g):
    lane = lax.broadcasted_iota(jnp.int32, (WIN, LW), 1)
    parts = []
    for pr in range(2):
        x = ref[rows, (2 * g + pr) * LW:(2 * g + pr + 1) * LW]
        parts += [jnp.where(lane < HD, x, jnp.zeros_like(x)), jnp.where(lane >= HD, x, jnp.zeros_like(x))]
    return jnp.concatenate(parts, axis=0)


def _unstack_heads(x4):
    lane = lax.broadcasted_iota(jnp.int32, (WIN, LW), 1)
    return [jnp.where(lane < HD, x4[(2 * pr) * WIN:(2 * pr + 1) * WIN], x4[(2 * pr + 1) * WIN:(2 * pr + 2) * WIN])
            for pr in range(2)]


def _sink_col(sink_ref, j, g):
    blk = lax.broadcasted_iota(jnp.int32, (GQ * WIN, 1), 0) // WIN
    col = jnp.zeros((GQ * WIN, 1), F32)
    for a in range(GQ):
        col = jnp.where(blk == a, sink_ref[j, GQ * g + a], col)
    return col


def _softmax_sink(s, mask, sink):
    s = jnp.where(mask, s, NEG)
    m = jnp.maximum(jnp.max(s, axis=-1, keepdims=True), sink)
    e = jnp.exp(s - m)
    es = jnp.exp(sink - m)
    l = jnp.sum(e, axis=-1, keepdims=True) + es
    return e / l, es / l


def _kv_halo(tq):
    return pl.BlockSpec((WIN, KVW), lambda i: (jnp.maximum(i * (tq // WIN) - 1, 0), 0))


def _attn_fwd(q, kk, vv, sinks, l, j):
    T = q.shape[0]
    tq = min(TQ, T)

    def body(sink_ref, q_ref, k_ref, kh_ref, v_ref, vh_ref, o_ref):
        i = pl.program_id(0)
        ke = jnp.concatenate([kh_ref[...], k_ref[...]], axis=0)
        ve = jnp.concatenate([vh_ref[...], v_ref[...]], axis=0)
        for n in range(tq // WIN):
            rows = slice(n * WIN, (n + 1) * WIN)
            mask = _band_mask((i == 0) & (n == 0))
            for g in range(NKV):
                kg = ke[n * WIN:(n + 2) * WIN, g * LW:(g + 1) * LW]
                vg = ve[n * WIN:(n + 2) * WIN, g * LW:(g + 1) * LW]
                p, _ = _softmax_sink(_dot_nt(_stack_heads(q_ref, rows, g), kg), mask, _sink_col(sink_ref, j, g))
                pairs = _unstack_heads(_dot(p.astype(CDT), vg))
                for pr in range(2):
                    o_ref[rows, (2 * g + pr) * LW:(2 * g + pr + 1) * LW] = pairs[pr].astype(CDT)

    return pl.pallas_call(
        body, grid=(T // tq,), name=f"attn_fwd_{l}",
        in_specs=[pl.BlockSpec(memory_space=pltpu.SMEM), _tile(tq, D), _tile(tq, KVW), _kv_halo(tq),
                  _tile(tq, KVW), _kv_halo(tq)],
        out_specs=_tile(tq, D), out_shape=jax.ShapeDtypeStruct((T, D), CDT), compiler_params=_cp(),
    )(sinks, q, kk, kk, vv, vv)


def _attn_bwd(q, kk, vv, o, do, sinks, l, j):
    T = q.shape[0]
    tq = min(TQ, T)
    nt = T // tq

    def body(sink_ref, q_ref, k_ref, kh_ref, v_ref, vh_ref, o_ref, do_ref,
             dq_ref, dk_ref, dkh_ref, dv_ref, dvh_ref, dsink_ref, dke, dve):
        _zero_first(dsink_ref)
        i = pl.program_id(0)
        ke = jnp.concatenate([kh_ref[...], k_ref[...]], axis=0)
        ve = jnp.concatenate([vh_ref[...], v_ref[...]], axis=0)
        dke[...] = jnp.zeros_like(dke)
        dve[...] = jnp.zeros_like(dve)
        lane = lax.broadcasted_iota(jnp.int32, (1, LW), 1)
        blk = lax.broadcasted_iota(jnp.int32, (GQ * WIN, 1), 0) // WIN
        dsink = jnp.zeros((1, LW), F32)
        for n in range(tq // WIN):
            rows = slice(n * WIN, (n + 1) * WIN)
            krows = slice(n * WIN, (n + 2) * WIN)
            mask = _band_mask((i == 0) & (n == 0))
            for g in range(NKV):
                cols = slice(g * LW, (g + 1) * LW)
                kg = ke[krows, cols]
                vg = ve[krows, cols]
                q4 = _stack_heads(q_ref, rows, g)
                do4 = _stack_heads(do_ref, rows, g)
                o4 = _stack_heads(o_ref, rows, g)
                p, ps = _softmax_sink(_dot_nt(q4, kg), mask, _sink_col(sink_ref, j, g))
                delta = jnp.sum(do4.astype(F32) * o4.astype(F32), axis=-1, keepdims=True)
                ds = (p * (_dot_nt(do4, vg) - delta)).astype(CDT)
                pc = p.astype(CDT)
                pairs = _unstack_heads(_dot(ds, kg))
                for pr in range(2):
                    dq_ref[rows, (2 * g + pr) * LW:(2 * g + pr + 1) * LW] = pairs[pr].astype(CDT)
                dke[krows, cols] += _dot_tn(ds, q4)
                dve[krows, cols] += _dot_tn(pc, do4)
                t = ps * delta
                for a in range(GQ):
                    dsink = dsink - jnp.where(lane == GQ * g + a, jnp.sum(jnp.where(blk == a, t, 0.0)), 0.0)
        dsink_ref[...] += dsink
        dkh_ref[...] = dke[:WIN, :]
        dk_ref[...] = dke[WIN:, :]
        dvh_ref[...] = dve[:WIN, :]
        dv_ref[...] = dve[WIN:, :]

    halo_out = pl.BlockSpec((WIN, KVW), lambda i: (i, 0))
    return pl.pallas_call(
        body, grid=(nt,), name=f"attn_bwd_{l}",
        in_specs=[pl.BlockSpec(memory_space=pltpu.SMEM), _tile(tq, D), _tile(tq, KVW), _kv_halo(tq),
                  _tile(tq, KVW), _kv_halo(tq), _tile(tq, D), _tile(tq, D)],
        out_specs=[_tile(tq, D), _tile(tq, KVW), halo_out, _tile(tq, KVW), halo_out, _acc(LW)],
        out_shape=[jax.ShapeDtypeStruct((T, D), CDT), jax.ShapeDtypeStruct((T, KVW), F32),
                   jax.ShapeDtypeStruct((nt * WIN, KVW), F32), jax.ShapeDtypeStruct((T, KVW), F32),
                   jax.ShapeDtypeStruct((nt * WIN, KVW), F32), jax.ShapeDtypeStruct((1, LW), F32)],
        scratch_shapes=[pltpu.VMEM((WIN + tq, KVW), F32), pltpu.VMEM((WIN + tq, KVW), F32)],
        compiler_params=_cp(),
    )(sinks, q, kk, kk, vv, vv, o, do)


def _chunks(tm):
    return pl.BlockSpec((NSH, tm, FS), lambda i: (0, i, 0))


def _ffn_fwd(h1, g_pre, wg, wu, wd, g_post, l):
    T = h1.shape[0]
    tm = min(TM, T)

    def body(h_ref, gpre_ref, wg_ref, wu_ref, wd_ref, gpost_ref, h2_ref, a_ref, b_ref, f_ref):
        x = h_ref[...]
        fn = _rms_fwd(x, gpre_ref[l:l + 1, :]).astype(CDT)
        f = jnp.zeros((tm, D), F32)
        for s in range(NSH):
            a = _dot(fn, wg_ref[s])
            b = _dot(fn, wu_ref[s])
            a_ref[s] = a.astype(CDT)
            b_ref[s] = b.astype(CDT)
            f = f + _dot((a * _sigmoid(a) * b).astype(CDT), wd_ref[s])
        f_ref[...] = f.astype(CDT)
        h2_ref[...] = x + _rms_fwd(f, gpost_ref[l:l + 1, :])

    return pl.pallas_call(
        body, grid=(T // tm,), name=f"ffn_fwd_{l}",
        in_specs=[_tile(tm, D), _res(g_pre.shape), _layer_w((D, FS), l), _layer_w((D, FS), l), _layer_w((FS, D), l),
                  _res(g_post.shape)],
        out_specs=[_tile(tm, D), _chunks(tm), _chunks(tm), _tile(tm, D)],
        out_shape=[jax.ShapeDtypeStruct((T, D), F32), jax.ShapeDtypeStruct((NSH, T, FS), CDT),
                   jax.ShapeDtypeStruct((NSH, T, FS), CDT), jax.ShapeDtypeStruct((T, D), CDT)],
        compiler_params=_cp(),
    )(h1, g_pre, wg, wu, wd, g_post)


def _ffn_bwd(dh2, h1, f, a, b, g_pre, wg, wu, wd, g_post, l):
    T = h1.shape[0]
    tm = min(TM_FFN_BWD, T)

    def body(dh_ref, h_ref, f_ref, a_ref, b_ref, gpre_ref, wg_ref, wu_ref, wd_ref, gpost_ref,
             dh1_ref, fn_ref, df_ref, act_ref, da_ref, db_ref, dgpre_ref, dgpost_ref):
        _zero_first(dgpre_ref, dgpost_ref)
        x = h_ref[...]
        gpre = gpre_ref[l:l + 1, :]
        fn_ref[...] = _rms_fwd(x, gpre).astype(CDT)
        dh2 = dh_ref[...]
        df, dg = _rms_bwd(dh2, f_ref[...].astype(F32), gpost_ref[l:l + 1, :])
        dgpost_ref[...] += dg
        dfc = df.astype(CDT)
        df_ref[...] = dfc
        dfn = jnp.zeros((tm, D), F32)
        for s in range(NSH):
            av = a_ref[s].astype(F32)
            bv = b_ref[s].astype(F32)
            sg = _sigmoid(av)
            silu = av * sg
            act_ref[s] = (silu * bv).astype(CDT)
            dact = _dot_nt(dfc, wd_ref[s])
            da = (dact * bv * (sg * (1.0 + av * (1.0 - sg)))).astype(CDT)
            db = (dact * silu).astype(CDT)
            da_ref[s] = da
            db_ref[s] = db
            dfn = dfn + _dot_nt(da, wg_ref[s]) + _dot_nt(db, wu_ref[s])
        dx, dg = _rms_bwd(dfn, x, gpre)
        dgpre_ref[...] += dg
        dh1_ref[...] = dh2 + dx

    return pl.pallas_call(
        body, grid=(T // tm,), name=f"ffn_bwd_{l}",
        in_specs=[_tile(tm, D), _tile(tm, D), _tile(tm, D), _chunks(tm), _chunks(tm), _res(g_pre.shape),
                  _layer_w((D, FS), l), _layer_w((D, FS), l), _layer_w((FS, D), l), _res(g_post.shape)],
        out_specs=[_tile(tm, D), _tile(tm, D), _tile(tm, D), _chunks(tm), _chunks(tm), _chunks(tm), _acc(D), _acc(D)],
        out_shape=[jax.ShapeDtypeStruct((T, D), F32), jax.ShapeDtypeStruct((T, D), CDT), jax.ShapeDtypeStruct((T, D), CDT),
                   jax.ShapeDtypeStruct((NSH, T, FS), CDT), jax.ShapeDtypeStruct((NSH, T, FS), CDT),
                   jax.ShapeDtypeStruct((NSH, T, FS), CDT), jax.ShapeDtypeStruct((1, D), F32),
                   jax.ShapeDtypeStruct((1, D), F32)],
        compiler_params=_cp(),
    )(dh2, h1, f, a, b, g_pre, wg, wu, wd, g_post)


def _ple_proj(p, wpp_ref):
    return jnp.concatenate([_dot(p, wpp_ref[s]) for s in range(NSH)], axis=1)


def _ple_fwd(h2, p, g_ple, wpg, wpp, l):
    T = h2.shape[0]
    tm = min(TM, T)

    def body(h_ref, p_ref, g_ref, wpg_ref, wpp_ref, o_ref):
        x = h_ref[...]
        pn = _rms_fwd(x, g_ref[l:l + 1, :]).astype(CDT)
        gate = _sigmoid(_proj_rows(pn, wpg_ref))
        o_ref[...] = x + _ple_proj(p_ref[...].astype(CDT), wpp_ref) * gate

    return pl.pallas_call(
        body, grid=(T // tm,), name=f"ple_fwd_{l}",
        in_specs=[_tile(tm, D), pl.BlockSpec((None, tm, PLE), lambda i: (l, i, 0)), _res(g_ple.shape),
                  _layer_w((D // NSH, D), l), _layer_w((PLE, D // NSH), l)],
        out_specs=_tile(tm, D), out_shape=jax.ShapeDtypeStruct((T, D), F32), compiler_params=_cp(),
    )(h2, p, g_ple, wpg, wpp)


def _ple_bwd(dh3, h2, p, g_ple, wpg, wpp, l):
    T = h2.shape[0]
    tm = min(TM, T)

    def body(dh_ref, h_ref, p_ref, g_ref, wpg_ref, wpp_ref, o_ref, pn_ref, dz_ref, dpp_ref, pc_ref, dg_ref):
        _zero_first(dg_ref)
        x = h_ref[...]
        g = g_ref[l:l + 1, :]
        pn = _rms_fwd(x, g).astype(CDT)
        pn_ref[...] = pn
        gate = _sigmoid(_proj_rows(pn, wpg_ref))
        pc = p_ref[...].astype(CDT)
        pc_ref[...] = pc
        pp = _ple_proj(pc, wpp_ref)
        dh3 = dh_ref[...]
        dpp_ref[...] = (dh3 * gate).astype(CDT)
        dz = (dh3 * pp * gate * (1.0 - gate)).astype(CDT)
        dz_ref[...] = dz
        dx, dg = _rms_bwd(_proj_rows_t(dz, wpg_ref), x, g)
        dg_ref[...] += dg
        o_ref[...] = dh3 + dx

    return pl.pallas_call(
        body, grid=(T // tm,), name=f"ple_bwd_{l}",
        in_specs=[_tile(tm, D), _tile(tm, D), pl.BlockSpec((None, tm, PLE), lambda i: (l, i, 0)), _res(g_ple.shape),
                  _layer_w((D // NSH, D), l), _layer_w((PLE, D // NSH), l)],
        out_specs=[_tile(tm, D), _tile(tm, D), _tile(tm, D), _tile(tm, D), _tile(tm, PLE), _acc(D)],
        out_shape=[jax.ShapeDtypeStruct((T, D), F32), jax.ShapeDtypeStruct((T, D), CDT), jax.ShapeDtypeStruct((T, D), CDT),
                   jax.ShapeDtypeStruct((T, D), CDT), jax.ShapeDtypeStruct((T, PLE), CDT), jax.ShapeDtypeStruct((1, D), F32)],
        compiler_params=_cp(),
    )(dh3, h2, p, g_ple, wpg, wpp)


def _loss_grad(y, tgt):
    T = y.shape[0]
    tm = min(TM, T)

    def body(y_ref, t_ref, dy_ref, loss_ref):
        _zero_first(loss_ref)
        err = y_ref[...] - t_ref[...]
        dy_ref[...] = err * (1.0 / D)
        lane = lax.broadcasted_iota(jnp.int32, (1, 128), 1)
        loss_ref[...] += jnp.where(lane == 0, (0.5 / D) * jnp.sum(err * err), 0.0)

    return pl.pallas_call(
        body, grid=(T // tm,), name="loss_grad", in_specs=[_tile(tm, D), _tile(tm, D)],
        out_specs=[_tile(tm, D), _acc(128)],
        out_shape=[jax.ShapeDtypeStruct((T, D), F32), jax.ShapeDtypeStruct((1, 128), F32)], compiler_params=_cp(),
    )(y, tgt)


def _mm_tn(x, dy, name, n_split=1):
    xb, yb = x.ndim == 3, dy.ndim == 3
    T, K = x.shape[-2:]
    N = dy.shape[-1] // n_split
    B = x.shape[0] if xb else dy.shape[0] if yb else n_split
    tt = min(TT, T)
    nt = T // tt

    def body(x_ref, dy_ref, o_ref, oc_ref):
        t = pl.program_id(1)

        @pl.when(t == 0)
        def _():
            o_ref[...] = jnp.zeros_like(o_ref)

        o_ref[...] += _dot_tn(x_ref[...].astype(CDT), dy_ref[...])

        @pl.when(t == nt - 1)
        def _():
            oc_ref[...] = o_ref[...].astype(CDT)

    x_spec = pl.BlockSpec((None, tt, K), lambda b, t: (b, t, 0)) if xb else pl.BlockSpec((tt, K), lambda b, t: (t, 0))
    if yb:
        y_spec = pl.BlockSpec((None, tt, N), lambda b, t: (b, t, 0))
    else:
        y_spec = pl.BlockSpec((tt, N), lambda b, t: (t, b if n_split > 1 else 0))
    o_spec = pl.BlockSpec((None, K, N), lambda b, t: (b, 0, 0))
    return pl.pallas_call(
        body, grid=(B, nt), name=name, in_specs=[x_spec, y_spec], out_specs=[o_spec, o_spec],
        out_shape=[jax.ShapeDtypeStruct((B, K, N), F32), jax.ShapeDtypeStruct((B, K, N), CDT)], compiler_params=_cp(2),
    )(x, dy)


def _fold_dup(dw2, name):
    def body(x_ref, o_ref, oc_ref):
        x = x_ref[...]
        y = jnp.concatenate([x[:, g * LW:g * LW + HD] + x[:, g * LW + HD:(g + 1) * LW] for g in range(NKV)], axis=1)
        o_ref[...] = y
        oc_ref[...] = y.astype(CDT)

    return pl.pallas_call(
        body, grid=(NSH,), name=name, in_specs=[_tile(D // NSH, KVW)], out_specs=[_tile(D // NSH, NKV * HD)] * 2,
        out_shape=[jax.ShapeDtypeStruct((D, NKV * HD), F32), jax.ShapeDtypeStruct((D, NKV * HD), CDT)],
        compiler_params=_cp(),
    )(dw2)


def _place():
    x, y, c = lax.axis_index("x"), lax.axis_index("y"), lax.axis_index("c")
    chips = [(1 - x, y), (x, 1 - y), (1 - x, 1 - y)]
    return x, y, c, chips


HBM = pl.BlockSpec(memory_space=pl.ANY)


def _all_gather(shards):
    n = len(shards)

    def body(*refs):
        ins, outs = refs[:n], refs[n:2 * n]
        send, recv, loc = refs[2 * n:]
        x, y, c, chips = _place()
        me = 2 * x + y
        local, sends = [], []
        for a in range(n):
            local.append(pltpu.make_async_copy(ins[a], outs[a].at[me], loc.at[a]))
            local[-1].start()
            for j, (px, py) in enumerate(chips):
                sends.append(pltpu.make_async_remote_copy(
                    src_ref=ins[a], dst_ref=outs[a].at[me], send_sem=send.at[3 * a + j], recv_sem=recv.at[3 * a + j],
                    device_id=(px, py, c), device_id_type=MESH))
                sends[-1].start()
        for a in range(n):
            for j, (px, py) in enumerate(chips):
                pltpu.make_async_remote_copy(
                    src_ref=ins[a], dst_ref=outs[a].at[2 * px + py], send_sem=send.at[3 * a + j],
                    recv_sem=recv.at[3 * a + j], device_id=(px, py, c), device_id_type=MESH).wait_recv()
        for cp in sends:
            cp.wait_send()
        for cp in local:
            cp.wait()

    return pl.pallas_call(
        body, name="all_gather_weights", in_specs=[HBM] * n, out_specs=[HBM] * n,
        out_shape=[jax.ShapeDtypeStruct((NSH,) + s.shape, s.dtype) for s in shards],
        scratch_shapes=[pltpu.SemaphoreType.DMA((3 * n,)), pltpu.SemaphoreType.DMA((3 * n,)), pltpu.SemaphoreType.DMA((n,))],
    )(*shards)


def _scatter_to_owners(grads):
    n = len(grads)

    def body(*refs):
        ins, outs = refs[:n], refs[n:2 * n]
        send, recv = refs[2 * n:]
        x, y, c, chips = _place()
        sends = []
        for a in range(n):
            for j, (px, py) in enumerate(chips):
                sends.append(pltpu.make_async_remote_copy(
                    src_ref=ins[a].at[2 * px + py], dst_ref=outs[a].at[j], send_sem=send.at[3 * a + j],
                    recv_sem=recv.at[3 * a + j], device_id=(px, py, c), device_id_type=MESH))
                sends[-1].start()
        for a in range(n):
            for j, (px, py) in enumerate(chips):
                pltpu.make_async_remote_copy(
                    src_ref=ins[a].at[2 * px + py], dst_ref=outs[a].at[j], send_sem=send.at[3 * a + j],
                    recv_sem=recv.at[3 * a + j], device_id=(px, py, c), device_id_type=MESH).wait_recv()
        for cp in sends:
            cp.wait_send()

    return pl.pallas_call(
        body, name="scatter_grads", in_specs=[HBM] * n, out_specs=[HBM] * n,
        out_shape=[jax.ShapeDtypeStruct((3,) + g.shape[1:], g.dtype) for g in grads],
        scratch_shapes=[pltpu.SemaphoreType.DMA((3 * n,)), pltpu.SemaphoreType.DMA((3 * n,))],
    )(*grads)


def _swap_with_sibling(parts):
    n = len(parts)

    def body(*refs):
        ins, outs = refs[:n], refs[n:2 * n]
        send, recv = refs[2 * n:]
        x, y, c, _ = _place()
        copies = [pltpu.make_async_remote_copy(
            src_ref=ins[a], dst_ref=outs[a], send_sem=send.at[a], recv_sem=recv.at[a],
            device_id=(x, y, 1 - c), device_id_type=MESH) for a in range(n)]
        for cp in copies:
            cp.start()
        for cp in copies:
            cp.wait_recv()
        for cp in copies:
            cp.wait_send()

    return pl.pallas_call(
        body, name="swap_partials", in_specs=[HBM] * n, out_specs=[HBM] * n,
        out_shape=[jax.ShapeDtypeStruct(p.shape, p.dtype) for p in parts],
        scratch_shapes=[pltpu.SemaphoreType.DMA((n,)), pltpu.SemaphoreType.DMA((n,))],
    )(*parts)


def _all_reduce_small(vec):
    R = vec.shape[0]
    ndev = 8

    def body(v_ref, o_ref, buf, send, recv):
        x, y, c, _ = _place()
        me = 4 * x + 2 * y + c
        buf[me] = v_ref[...]
        copies = []
        for k in range(1, ndev):
            peer = (x if not k & 4 else 1 - x, y if not k & 2 else 1 - y, c if not k & 1 else 1 - c)
            copies.append(pltpu.make_async_remote_copy(
                src_ref=v_ref, dst_ref=buf.at[me], send_sem=send.at[k - 1], recv_sem=recv.at[k - 1],
                device_id=peer, device_id_type=MESH))
            copies[-1].start()
        for k in range(1, ndev):
            peer = (x if not k & 4 else 1 - x, y if not k & 2 else 1 - y, c if not k & 1 else 1 - c)
            pltpu.make_async_remote_copy(
                src_ref=v_ref, dst_ref=buf.at[4 * peer[0] + 2 * peer[1] + peer[2]], send_sem=send.at[k - 1],
                recv_sem=recv.at[k - 1], device_id=peer, device_id_type=MESH).wait_recv()
        for cp in copies:
            cp.wait_send()
        tot = buf[0]
        for d in range(1, ndev):
            tot = tot + buf[d]
        o_ref[...] = tot

    return pl.pallas_call(
        body, name="all_reduce_small", in_specs=[pl.BlockSpec(memory_space=pltpu.VMEM)],
        out_specs=pl.BlockSpec(memory_space=pltpu.VMEM), out_shape=jax.ShapeDtypeStruct(vec.shape, F32),
        scratch_shapes=[pltpu.VMEM((ndev, R, D), F32), pltpu.SemaphoreType.DMA((ndev - 1,)),
                        pltpu.SemaphoreType.DMA((ndev - 1,))],
    )(vec)


def _row_tile(rows):
    for t in (512, 256, 128, 64, 32, 16, 8):
        if rows % t == 0:
            return t
    return rows


def _sum_own_and_received(own, got, me, name):
    _, R, C = own.shape
    tr = _row_tile(R)

    def body(me_ref, own_ref, got_ref, o_ref):
        o_ref[...] = ((own_ref[...] + got_ref[0].astype(F32)) + got_ref[1].astype(F32)) + got_ref[2].astype(F32)

    return pl.pallas_call(
        body, name=name,
        grid_spec=pltpu.PrefetchScalarGridSpec(
            num_scalar_prefetch=1, grid=(R // tr,),
            in_specs=[pl.BlockSpec((None, tr, C), lambda i, me_ref: (me_ref[0], i, 0)),
                      pl.BlockSpec((3, tr, C), lambda i, me_ref: (0, i, 0))],
            out_specs=pl.BlockSpec((tr, C), lambda i, me_ref: (i, 0))),
        out_shape=jax.ShapeDtypeStruct((R, C), F32), compiler_params=_cp(),
    )(me, own, got)


def _adamw(gs, w, m, v, name, layer=None):
    R, C = w.shape[-2:]
    tr = _row_tile(R)
    ng = len(gs)
    c1 = 1.0 - ADAM_B1 ** ADAM_STEP
    c2 = 1.0 - ADAM_B2 ** ADAM_STEP

    def body(*refs):
        w_ref, m_ref, v_ref, g_ref, d_ref, nm_ref, nv_ref = refs[ng:]
        g = refs[0][...]
        for r in refs[1:ng]:
            g = g + r[...]
        nm = ADAM_B1 * m_ref[...] + (1.0 - ADAM_B1) * g
        nv = ADAM_B2 * v_ref[...] + (1.0 - ADAM_B2) * (g * g)
        g_ref[...] = g
        nm_ref[...] = nm
        nv_ref[...] = nv
        d_ref[...] = -ADAM_LR * ((nm / c1) / (jnp.sqrt(nv / c2) + ADAM_EPS) + ADAM_WD * w_ref[...])

    spec = _tile(tr, C)
    wspec = spec if layer is None else pl.BlockSpec((None, tr, C), lambda i: (layer, i, 0))
    return pl.pallas_call(
        body, grid=(R // tr,), name=name, in_specs=[spec] * ng + [wspec] * 3, out_specs=[spec] * 4,
        out_shape=[jax.ShapeDtypeStruct((R, C), F32)] * 4, compiler_params=_cp(),
    )(*gs, w, m, v)


def _rope_tables(T):
    inv = 1.0 / (ROPE_THETA ** (jnp.arange(0, HD, 2, dtype=F32) / HD))
    ang = jnp.arange(T, dtype=F32)[:, None] * inv[None, :]
    c, s = jnp.cos(ang), jnp.sin(ang)
    return jnp.tile(jnp.concatenate([c, c], axis=1), (1, 2)), jnp.tile(jnp.concatenate([-s, s], axis=1), (1, 2))


def _dup_cols(w):
    return jnp.broadcast_to(w.reshape(D, NKV, 1, HD), (D, NKV, 2, HD)).reshape(D, KVW)


def kernel(x, p, mix_pre_g, mix_post_g, ffn_pre_g, ffn_post_g, pool_w, pool_scale, kv_norm_g, w_k, w_v, w_q, w_o, sinks, w_ff_gate, w_ff_up, w_ff_down, ple_norm_g, w_ple_gate, w_ple_proj, loss_target, m_mix_pre_g, m_mix_post_g, m_ffn_pre_g, m_ffn_post_g, m_pool_w, m_pool_scale, m_kv_norm_g, m_w_k, m_w_v, m_w_q, m_w_o, m_sinks, m_w_ff_gate, m_w_ff_up, m_w_ff_down, m_ple_norm_g, m_w_ple_gate, m_w_ple_proj, v_mix_pre_g, v_mix_post_g, v_ffn_pre_g, v_ffn_post_g, v_pool_w, v_pool_scale, v_kv_norm_g, v_w_k, v_w_v, v_w_q, v_w_o, v_sinks, v_w_ff_gate, v_w_ff_up, v_w_ff_down, v_ple_norm_g, v_w_ple_gate, v_w_ple_proj):
    depth = mix_pre_g.shape[0]
    n_pool = pool_w.shape[0]
    n_attn = w_q.shape[0]
    T = x.shape[1]
    h = x[0]
    p = p[:, 0]
    tgt = loss_target[0]
    me = (2 * lax.axis_index("x") + lax.axis_index("y")).astype(jnp.int32).reshape(1)

    sharded = [pool_w, w_k, w_v, w_q, w_o, w_ff_gate, w_ff_up, w_ff_down, w_ple_gate, w_ple_proj]
    gpw, gwk, gwv, wq, wo, wg, wu, wd, wpg, wpp = _all_gather([w.astype(CDT) for w in sharded])
    pw = jnp.transpose(gpw, (1, 2, 0, 3, 4)).reshape(n_pool, len(POOL_WINDOWS), PG, PG)
    wk2 = _dup_cols(gwk.reshape(D, NKV * HD))
    wv2 = _dup_cols(gwv.reshape(D, NKV * HD))
    kv_g = kv_norm_g.reshape(1, D)
    cos, sin = _rope_tables(T)

    pscale = _all_reduce_small(jnp.pad(
        lax.dynamic_update_slice(jnp.zeros((n_pool, D), F32), pool_scale, (0, me[0] * (D // NSH))),
        ((0, 8 - n_pool), (0, 0))))[:n_pool] * 0.5

    saved = []
    kk = vv = None
    for l in range(depth):
        s = {"h": h}
        if l < n_pool:
            h1 = _pool_fwd(h, mix_pre_g, pw, pscale, mix_post_g, l)
        else:
            j = l - n_pool
            s["q"] = _q_fwd(h, mix_pre_g, wq, cos, sin, l, j)
            s["o"] = _attn_fwd(s["q"], kk, vv, sinks, l, j)
            h1, s["m"] = _o_fwd(s["o"], h, wo, mix_post_g, l, j)
        s["h1"] = h1
        h2, s["a"], s["b"], s["f"] = _ffn_fwd(h1, ffn_pre_g, wg, wu, wd, ffn_post_g, l)
        s["h2"] = h2
        h = _ple_fwd(h2, p, ple_norm_g, wpg, wpp, l)
        if l == n_pool - 1:
            s["hkv"] = h
            kk, vv = _kv_fwd(h, kv_g, wk2, wv2, cos, sin)
        saved.append(s)

    dh, loss_row = _loss_grad(h, tgt)

    big = {}
    small = {k: [None] * depth for k in ("mix_pre", "mix_post", "ffn_pre", "ffn_post", "ple")}
    dpool_w = [None] * n_pool
    dpool_scale = [None] * n_pool
    dsinks = [None] * n_attn
    dks, dkhs, dvs, dvhs = [], [], [], []
    for l in reversed(range(depth)):
        s = saved[l]
        if l == n_pool - 1:
            dh, hk, dk0, dvc, dkv_g = _kv_bwd(dh, s["hkv"], dks, dkhs, dvs, dvhs, kv_g, wk2, wv2, cos, sin)
            big["w_k"] = [_fold_dup(_mm_tn(hk, dk0, "dw_k")[0][0], "fold_w_k")]
            big["w_v"] = [_fold_dup(_mm_tn(hk, dvc, "dw_v")[0][0], "fold_w_v")]
        dh, pn, dz, dpp, pc, small["ple"][l] = _ple_bwd(dh, s["h2"], p, ple_norm_g, wpg, wpp, l)
        big.setdefault("w_ple_gate", []).insert(0, _mm_tn(pn, dz, f"dw_ple_gate_{l}"))
        big.setdefault("w_ple_proj", []).insert(0, _mm_tn(pc, dpp, f"dw_ple_proj_{l}", n_split=NSH))
        dh, fn, df, act, da, db, small["ffn_pre"][l], small["ffn_post"][l] = _ffn_bwd(
            dh, s["h1"], s["f"], s["a"], s["b"], ffn_pre_g, wg, wu, wd, ffn_post_g, l)
        big.setdefault("w_ff_gate", []).insert(0, _mm_tn(fn, da, f"dw_ff_gate_{l}"))
        big.setdefault("w_ff_up", []).insert(0, _mm_tn(fn, db, f"dw_ff_up_{l}"))
        big.setdefault("w_ff_down", []).insert(0, _mm_tn(act, df, f"dw_ff_down_{l}"))
        if l < n_pool:
            dpool, dpool_w[l], dpool_scale[l], small["mix_post"][l] = _pool_bwd_a(
                dh, s["h"], mix_pre_g, pw, pscale, mix_post_g, l)
            dh, small["mix_pre"][l] = _pool_bwd_b(dpool, dh, s["h"], mix_pre_g, l)
        else:
            j = l - n_pool
            do, dm, small["mix_post"][l] = _o_bwd(dh, s["m"], wo, mix_post_g, l, j)
            big.setdefault("w_o", []).insert(0, _mm_tn(s["o"], dm, f"dw_o_{l}"))
            dq, dk, dkh, dv, dvh, dsinks[j] = _attn_bwd(s["q"], kk, vv, s["o"], do, sinks, l, j)
            dks.append(dk); dkhs.append(dkh); dvs.append(dv); dvhs.append(dvh)
            dh, hn, dq0, small["mix_pre"][l] = _q_bwd(dq, dh, s["h"], mix_pre_g, wq, cos, sin, l, j)
            big.setdefault("w_q", []).insert(0, _mm_tn(hn, dq0, f"dw_q_{l}"))
    grad_x = dh[None]

    rows = [jnp.concatenate(small[k], axis=0) for k in ("mix_pre", "mix_post", "ffn_pre", "ffn_post", "ple")]
    rows += [dkv_g, jnp.concatenate(dpool_scale, axis=0)]
    rows += [jnp.pad(jnp.concatenate(dsinks, axis=0), ((0, 0), (0, D - LW))), jnp.pad(loss_row, ((0, 0), (0, D - 128)))]
    vec = jnp.concatenate(rows, axis=0)
    nrow = vec.shape[0]
    vec = _all_reduce_small(jnp.pad(vec, ((0, -nrow % 8), (0, 0))))
    o = 0
    red = {}
    for k in ("mix_pre_g", "mix_post_g", "ffn_pre_g", "ffn_post_g", "ple_norm_g"):
        red[k] = vec[o:o + depth]
        o += depth
    red["kv_norm_g"] = vec[o:o + 1]
    red["pool_scale"] = lax.dynamic_slice(vec[o + 1:o + 1 + n_pool], (0, me[0] * (D // NSH)), (n_pool, D // NSH))
    o += 1 + n_pool
    red["sinks"] = vec[o:o + n_attn, :GQ * NKV]
    loss = vec[o + n_attn, 0]

    def by_shard(g):
        b, r, c = g.shape
        return g.reshape(NSH, b * r // NSH, c)

    names, own, sent = [], [], []
    for k in ("w_k", "w_v", "w_q", "w_o", "w_ff_gate", "w_ff_up", "w_ff_down", "w_ple_gate", "w_ple_proj"):
        for l, (g32, g16) in enumerate(big[k]):
            if g32.ndim == 2:
                g32, g16 = g32[None], g16[None]
            names.append((k, l))
            own.append(by_shard(g32))
            sent.append(by_shard(g16))
    for l in range(n_pool):
        g32 = jnp.transpose(dpool_w[l].reshape(len(POOL_WINDOWS), NSH, PG // NSH, PG), (1, 0, 2, 3))
        g32 = g32.reshape(NSH, len(POOL_WINDOWS) * PG // NSH, PG)
        names.append(("pool_w", l))
        own.append(g32)
        sent.append(g32.astype(CDT))
    got = _scatter_to_owners(sent)
    parts = [_sum_own_and_received(o_, g_, me, f"sum_{k}_{l}") for (k, l), o_, g_ in zip(names, own, got)]
    sib = _swap_with_sibling(parts)

    given = dict(
        mix_pre_g=(mix_pre_g, m_mix_pre_g, v_mix_pre_g), mix_post_g=(mix_post_g, m_mix_post_g, v_mix_post_g),
        ffn_pre_g=(ffn_pre_g, m_ffn_pre_g, v_ffn_pre_g), ffn_post_g=(ffn_post_g, m_ffn_post_g, v_ffn_post_g),
        pool_w=(pool_w, m_pool_w, v_pool_w), pool_scale=(pool_scale, m_pool_scale, v_pool_scale),
        kv_norm_g=(kv_norm_g, m_kv_norm_g, v_kv_norm_g), w_k=(w_k, m_w_k, v_w_k), w_v=(w_v, m_w_v, v_w_v),
        w_q=(w_q, m_w_q, v_w_q), w_o=(w_o, m_w_o, v_w_o), sinks=(sinks, m_sinks, v_sinks),
        w_ff_gate=(w_ff_gate, m_w_ff_gate, v_w_ff_gate), w_ff_up=(w_ff_up, m_w_ff_up, v_w_ff_up),
        w_ff_down=(w_ff_down, m_w_ff_down, v_w_ff_down), ple_norm_g=(ple_norm_g, m_ple_norm_g, v_ple_norm_g),
        w_ple_gate=(w_ple_gate, m_w_ple_gate, v_w_ple_gate), w_ple_proj=(w_ple_proj, m_w_ple_proj, v_w_ple_proj))
    results = {}
    for k, g in red.items():
        w, m, v = given[k]
        outs = _adamw([g.reshape(-1, g.shape[-1])], *(t.reshape(-1, t.shape[-1]) for t in (w, m, v)), f"adamw_{k}")
        results[k] = [t.reshape(w.shape) for t in outs]
    per_layer = {}
    for (k, l), a, b in zip(names, parts, sib):
        w, m, v = given[k]
        if k in ("w_k", "w_v"):
            outs = _adamw([a, b], w, m, v, f"adamw_{k}")
            per_layer[k] = [outs]
        else:
            outs = _adamw([a, b], *(t.reshape((t.shape[0],) + a.shape) for t in (w, m, v)), f"adamw_{k}_{l}", layer=l)
            per_layer.setdefault(k, []).append([t.reshape(w.shape[1:]) for t in outs])
    for k, layers in per_layer.items():
        if k in ("w_k", "w_v"):
            results[k] = layers[0]
        else:
            results[k] = [jnp.stack([lay[i] for lay in layers], axis=0) for i in range(4)]

    order = ["mix_pre_g", "mix_post_g", "ffn_pre_g", "ffn_post_g", "pool_w", "pool_scale", "kv_norm_g", "w_k", "w_v",
             "w_q", "w_o", "sinks", "w_ff_gate", "w_ff_up", "w_ff_down", "ple_norm_g", "w_ple_gate", "w_ple_proj"]
    return (loss, grad_x, *[results[k][0] for k in order], *[results[k][1] for k in order],
            *[results[k][2] for k in order], *[results[k][3] for k in order])
```

```python
import functools

import jax
import jax.numpy as jnp
from jax import lax
from jax.experimental import pallas as pl
from jax.experimental.pallas import tpu as pltpu

F32 = jnp.float32
CDT = jnp.bfloat16

D = 1024
PG = 256
POOL_WINDOWS = (2, 4, 8, 16)
HALO = 16
HD = 64
NKV = 4
WIN = 128
FS = 704
NSH = 4
PLE = 256
ROPE_THETA = 10000.0
EPS = 1e-6
NEG = -1e30
QSCALE = HD ** -0.5

ADAM_LR, ADAM_B1, ADAM_B2, ADAM_EPS, ADAM_WD, ADAM_STEP = 0.001, 0.9, 0.999, 1e-08, 0.01, 10

VMEM_LIMIT = 56 * 1024 * 1024
TM = 512
TM_FFN_BWD = 256
TQ = 256
TT = 1024
MESH = pl.DeviceIdType.MESH


def _cp(n_axes=1):
    return pltpu.CompilerParams(dimension_semantics=("arbitrary",) * n_axes, vmem_limit_bytes=VMEM_LIMIT)


def _dot(a, b):
    return jnp.dot(a, b, preferred_element_type=F32)


def _dot_nt(a, b):
    return lax.dot_general(a, b, (((1,), (1,)), ((), ())), preferred_element_type=F32)


def _dot_tn(a, b):
    return lax.dot_general(a, b, (((0,), (0,)), ((), ())), preferred_element_type=F32)


def _rms_fwd(x, g):
    r = lax.rsqrt(jnp.mean(x * x, axis=-1, keepdims=True) + EPS)
    return x * r * g


def _rms_bwd(dy, x, g):
    r = lax.rsqrt(jnp.mean(x * x, axis=-1, keepdims=True) + EPS)
    xh = x * r
    dg = jnp.sum(dy * xh, axis=0, keepdims=True)
    u = dy * g
    dx = r * (u - xh * jnp.mean(u * xh, axis=-1, keepdims=True))
    return dx, dg


def _sigmoid(a):
    return 1.0 / (1.0 + jnp.exp(-a))


def _swap_half(x):
    w = x.shape[1]
    lane = lax.broadcasted_iota(jnp.int32, x.shape, 1)
    return jnp.where((lane % HD) < HD // 2, pltpu.roll(x, w - HD // 2, axis=1), pltpu.roll(x, HD // 2, axis=1))


def _rope_fwd(x, cos, sin):
    return x * cos + _swap_half(x) * sin


def _rope_bwd(d, cos, sin):
    return d * cos + _swap_half(d * sin)


def _tile(tm, w):
    return pl.BlockSpec((tm, w), lambda i: (i, 0))


def _res(shape):
    return pl.BlockSpec(shape, lambda i: (0,) * len(shape), pipeline_mode=pl.Buffered(1))


def _layer_w(shape):
    return pl.BlockSpec((NSH,) + shape, lambda *_: (0, 0, 0), pipeline_mode=pl.Buffered(1))


def _acc(w):
    return pl.BlockSpec((1, w), lambda i: (0, 0))


def _zero_first(*refs):
    @pl.when(pl.program_id(0) == 0)
    def _():
        for r in refs:
            r[...] = jnp.zeros_like(r)


HBM = pl.BlockSpec(memory_space=pl.ANY)


def _place():
    x, y, c = lax.axis_index("x"), lax.axis_index("y"), lax.axis_index("c")
    chips = [(1 - x, y), (x, 1 - y), (1 - x, 1 - y)]
    return x, y, c, chips


class _Comm:
    def __init__(self, args, out_shape, sems, copies):
        self.args, self.out_shape, self.sems, self.copies = list(args), list(out_shape), list(sems), copies

    def start(self, cin, cout, sems):
        local, sends, _ = self.copies(cin, cout, sems)
        for cp in local + sends:
            cp.start()

    def wait(self, cin, cout, sems):
        local, sends, recvs = self.copies(cin, cout, sems)
        for cp in recvs:
            cp.wait_recv()
        for cp in sends:
            cp.wait_send()
        for cp in local:
            cp.wait()


def _gather_comm(srcs):
    n = len(srcs)
    shapes = [(a.shape if l is None else a.shape[1:]) for a, l in srcs]

    def copies(cin, cout, sems):
        send, recv, loc = sems
        x, y, c, chips = _place()
        me = 2 * x + y
        local, sends, recvs = [], [], []
        for a, (_, l) in enumerate(srcs):
            src = cin[a] if l is None else cin[a].at[l]
            local.append(pltpu.make_async_copy(src, cout[a].at[me], loc.at[a]))
            for j, (px, py) in enumerate(chips):
                kw = dict(src_ref=src, send_sem=send.at[3 * a + j], recv_sem=recv.at[3 * a + j],
                          device_id=(px, py, c), device_id_type=MESH)
                sends.append(pltpu.make_async_remote_copy(dst_ref=cout[a].at[me], **kw))
                recvs.append(pltpu.make_async_remote_copy(dst_ref=cout[a].at[2 * px + py], **kw))
        return local, sends, recvs

    return _Comm([a for a, _ in srcs], [jax.ShapeDtypeStruct((NSH,) + s, CDT) for s in shapes],
                 [pltpu.SemaphoreType.DMA((3 * n,)), pltpu.SemaphoreType.DMA((3 * n,)), pltpu.SemaphoreType.DMA((n,))],
                 copies)


def _scatter_comm(grads):
    n = len(grads)

    def copies(cin, cout, sems):
        send, recv = sems
        x, y, c, chips = _place()
        sends, recvs = [], []
        for a in range(n):
            for j, (px, py) in enumerate(chips):
                cp = pltpu.make_async_remote_copy(
                    src_ref=cin[a].at[2 * px + py], dst_ref=cout[a].at[j], send_sem=send.at[3 * a + j],
                    recv_sem=recv.at[3 * a + j], device_id=(px, py, c), device_id_type=MESH)
                sends.append(cp)
                recvs.append(cp)
        return [], sends, recvs

    return _Comm(grads, [jax.ShapeDtypeStruct((3,) + g.shape[1:], g.dtype) for g in grads],
                 [pltpu.SemaphoreType.DMA((3 * n,)), pltpu.SemaphoreType.DMA((3 * n,))], copies)


def _run_comm(comm, name):
    k_in, k_out = len(comm.args), len(comm.out_shape)

    def body(*refs):
        cin, cout, sems = refs[:k_in], refs[k_in:k_in + k_out], refs[k_in + k_out:]
        comm.start(cin, cout, sems)
        comm.wait(cin, cout, sems)

    return pl.pallas_call(body, name=name, in_specs=[HBM] * k_in, out_specs=[HBM] * k_out, out_shape=comm.out_shape,
                          scratch_shapes=comm.sems)(*comm.args)


def _pallas(body, *, name, grid, in_specs, out_specs, out_shape, args, scratch=(), comm=None):
    out_specs, out_shape, scratch = list(out_specs), list(out_shape), list(scratch)
    params = _cp(len(grid))
    if comm is None:
        return pl.pallas_call(body, grid=grid, name=name, in_specs=in_specs, out_specs=out_specs, out_shape=out_shape,
                              scratch_shapes=scratch, compiler_params=params)(*args), []
    n_in, n_out, n_sc = len(args), len(out_shape), len(scratch)
    k_in, k_out = len(comm.args), len(comm.out_shape)

    def carrying(*refs):
        ins, refs = refs[:n_in], refs[n_in:]
        cin, refs = refs[:k_in], refs[k_in:]
        outs, refs = refs[:n_out], refs[n_out:]
        cout, refs = refs[:k_out], refs[k_out:]
        sc, sems = refs[:n_sc], refs[n_sc:]
        first = functools.reduce(jnp.logical_and, [pl.program_id(d) == 0 for d in range(len(grid))])
        last = functools.reduce(jnp.logical_and, [pl.program_id(d) == g - 1 for d, g in enumerate(grid)])

        @pl.when(first)
        def _():
            comm.start(cin, cout, sems)

        body(*ins, *outs, *sc)

        @pl.when(last)
        def _():
            comm.wait(cin, cout, sems)

    res = pl.pallas_call(
        carrying, grid=grid, name=name, in_specs=list(in_specs) + [HBM] * k_in, out_specs=out_specs + [HBM] * k_out,
        out_shape=out_shape + comm.out_shape, scratch_shapes=scratch + comm.sems, compiler_params=params,
    )(*args, *comm.args)
    return res[:n_out], res[n_out:]


def _pool_normed_ext(h_ref, halo_ref, g, tm):
    xe = jnp.concatenate([halo_ref[...], h_ref[...]], axis=0)
    hn = _rms_fwd(xe, g)
    row = lax.broadcasted_iota(jnp.int32, (tm + HALO, 1), 0)
    return jnp.where((row >= HALO) | (pl.program_id(0) > 0), hn, 0.0)


def _pool_windows(hn_e, tm):
    t = lax.broadcasted_iota(jnp.int32, (tm, PG), 0) + pl.program_id(0) * tm
    outs = []
    for g, w in enumerate(POOL_WINDOWS):
        s = hn_e[:, g * PG:(g + 1) * PG]
        x = s[HALO:, :]
        k = 1
        while k < w:
            s = s + pltpu.roll(s, k, axis=0)
            k *= 2
        cnt = jnp.minimum(t + 1, w).astype(F32)
        outs.append(s[HALO:, :] / cnt - x)
    return jnp.concatenate(outs, axis=1)


def _pool_apply(pooled, w_ref):
    return jnp.concatenate([_dot(pooled[:, g * PG:(g + 1) * PG], w_ref[g]) for g in range(len(POOL_WINDOWS))], axis=1)


def _halo_prev(tm):
    return pl.BlockSpec((HALO, D), lambda i: (jnp.maximum(i * (tm // HALO) - 1, 0), 0))


def _pool_fwd(h, g_pre, pw, pscale, g_post, l):
    T = h.shape[0]
    tm = min(TM, T)

    def body(h_ref, halo_ref, gpre_ref, w_ref, sc_ref, gpost_ref, o_ref):
        hn_e = _pool_normed_ext(h_ref, halo_ref, gpre_ref[l:l + 1, :], tm)
        pooled = _pool_windows(hn_e, tm).astype(CDT)
        m = _pool_apply(pooled, w_ref) * sc_ref[l:l + 1, :]
        o_ref[...] = h_ref[...] + _rms_fwd(m, gpost_ref[l:l + 1, :])

    return pl.pallas_call(
        body, grid=(T // tm,), name=f"pool_fwd_{l}",
        in_specs=[_tile(tm, D), _halo_prev(tm), _res(g_pre.shape),
                  pl.BlockSpec((None,) + pw.shape[1:], lambda i: (l, 0, 0, 0), pipeline_mode=pl.Buffered(1)),
                  _res(pscale.shape), _res(g_post.shape)],
        out_specs=_tile(tm, D), out_shape=jax.ShapeDtypeStruct((T, D), F32), compiler_params=_cp(),
    )(h, h, g_pre, pw, pscale, g_post)


def _pool_bwd_a(dh1, h, g_pre, pw, pscale, g_post, l, comm=None):
    T = h.shape[0]
    tm = min(TM, T)
    ng = len(POOL_WINDOWS)

    def body(dh_ref, h_ref, halo_ref, gpre_ref, w_ref, sc_ref, gpost_ref, dp_ref, dw_ref, dsc_ref, dgpost_ref):
        _zero_first(dw_ref, dsc_ref, dgpost_ref)
        hn_e = _pool_normed_ext(h_ref, halo_ref, gpre_ref[l:l + 1, :], tm)
        pooled = _pool_windows(hn_e, tm).astype(CDT)
        y = _pool_apply(pooled, w_ref)
        sc = sc_ref[l:l + 1, :]
        dm, dg = _rms_bwd(dh_ref[...], y * sc, gpost_ref[l:l + 1, :])
        dgpost_ref[...] += dg
        dsc_ref[...] += jnp.sum(dm * y, axis=0, keepdims=True)
        dy = (dm * sc).astype(CDT)
        dps = []
        for g in range(ng):
            dyg = dy[:, g * PG:(g + 1) * PG]
            dw_ref[g] += _dot_tn(pooled[:, g * PG:(g + 1) * PG], dyg)
            dps.append(_dot_nt(dyg, w_ref[g]))
        dp_ref[...] = jnp.concatenate(dps, axis=1)

    return _pallas(
        body, grid=(T // tm,), name=f"pool_bwd_a_{l}",
        in_specs=[_tile(tm, D), _tile(tm, D), _halo_prev(tm), _res(g_pre.shape),
                  pl.BlockSpec((None,) + pw.shape[1:], lambda i: (l, 0, 0, 0), pipeline_mode=pl.Buffered(1)),
                  _res(pscale.shape), _res(g_post.shape)],
        out_specs=[_tile(tm, D), pl.BlockSpec((ng, PG, PG), lambda i: (0, 0, 0)), _acc(D), _acc(D)],
        out_shape=[jax.ShapeDtypeStruct((T, D), F32), jax.ShapeDtypeStruct((ng, PG, PG), F32),
                   jax.ShapeDtypeStruct((1, D), F32), jax.ShapeDtypeStruct((1, D), F32)],
        args=(dh1, h, h, g_pre, pw, pscale, g_post), comm=comm)


def _pool_bwd_b(dpool, dh1, h, g_pre, l, comm=None):
    T = h.shape[0]
    tm = min(TM, T)
    nt = T // tm

    def body(dp_ref, nxt_ref, dh_ref, h_ref, gpre_ref, o_ref, dgpre_ref):
        _zero_first(dgpre_ref)
        i = pl.program_id(0)
        dp = dp_ref[...]
        e_e = jnp.concatenate([dp, jnp.where(i < nt - 1, nxt_ref[...], 0.0)], axis=0)
        t = lax.broadcasted_iota(jnp.int32, (tm + HALO, PG), 0) + i * tm
        outs = []
        for g, w in enumerate(POOL_WINDOWS):
            s = e_e[:, g * PG:(g + 1) * PG] / jnp.minimum(t + 1, w).astype(F32)
            k = 1
            while k < w:
                s = s + pltpu.roll(s, tm + HALO - k, axis=0)
                k *= 2
            outs.append(s[:tm, :] - dp[:, g * PG:(g + 1) * PG])
        dx, dg = _rms_bwd(jnp.concatenate(outs, axis=1), h_ref[...], gpre_ref[l:l + 1, :])
        dgpre_ref[...] += dg
        o_ref[...] = dh_ref[...] + dx

    nxt = pl.BlockSpec((HALO, D), lambda i: (jnp.minimum((i + 1) * (tm // HALO), T // HALO - 1), 0))
    return _pallas(
        body, grid=(nt,), name=f"pool_bwd_b_{l}",
        in_specs=[_tile(tm, D), nxt, _tile(tm, D), _tile(tm, D), _res(g_pre.shape)],
        out_specs=[_tile(tm, D), _acc(D)],
        out_shape=[jax.ShapeDtypeStruct((T, D), F32), jax.ShapeDtypeStruct((1, D), F32)],
        args=(dpool, dpool, dh1, h, g_pre), comm=comm)


def _proj_rows(x, w_ref):
    k = w_ref.shape[1]
    out = _dot(x[:, :k], w_ref[0])
    for s in range(1, NSH):
        out = out + _dot(x[:, s * k:(s + 1) * k], w_ref[s])
    return out


def _proj_rows_t(dy, w_ref):
    return jnp.concatenate([_dot_nt(dy, w_ref[s]) for s in range(NSH)], axis=1)


def _rope_tiles(cos_ref, sin_ref, width):
    reps = width // cos_ref.shape[1]
    return jnp.tile(cos_ref[...], (1, reps)), jnp.tile(sin_ref[...], (1, reps))


def _q_fwd(h, g_pre, wq, cos, sin, l, j):
    T = h.shape[0]
    tm = min(TM, T)

    def body(h_ref, g_ref, w_ref, cos_ref, sin_ref, q_ref):
        hn = _rms_fwd(h_ref[...], g_ref[l:l + 1, :]).astype(CDT)
        c, s = _rope_tiles(cos_ref, sin_ref, D)
        q_ref[...] = (_rope_fwd(_proj_rows(hn, w_ref), c, s) * QSCALE).astype(CDT)

    return pl.pallas_call(
        body, grid=(T // tm,), name=f"q_fwd_{l}",
        in_specs=[_tile(tm, D), _res(g_pre.shape), _layer_w((D // NSH, D)), _tile(tm, 2 * HD), _tile(tm, 2 * HD)],
        out_specs=_tile(tm, D), out_shape=jax.ShapeDtypeStruct((T, D), CDT), compiler_params=_cp(),
    )(h, g_pre, wq, cos, sin)


def _q_bwd(dq, dh1, h, g_pre, wq, cos, sin, l, j):
    T = h.shape[0]
    tm = min(TM, T)

    def body(dq_ref, dh_ref, h_ref, g_ref, w_ref, cos_ref, sin_ref, o_ref, hn_ref, dq0_ref, dg_ref):
        _zero_first(dg_ref)
        g = g_ref[l:l + 1, :]
        x = h_ref[...]
        hn_ref[...] = _rms_fwd(x, g).astype(CDT)
        c, s = _rope_tiles(cos_ref, sin_ref, D)
        dq0 = _rope_bwd(dq_ref[...].astype(F32) * QSCALE, c, s).astype(CDT)
        dq0_ref[...] = dq0
        dx, dg = _rms_bwd(_proj_rows_t(dq0, w_ref), x, g)
        dg_ref[...] += dg
        o_ref[...] = dh_ref[...] + dx

    return pl.pallas_call(
        body, grid=(T // tm,), name=f"q_bwd_{l}",
        in_specs=[_tile(tm, D), _tile(tm, D), _tile(tm, D), _res(g_pre.shape), _layer_w((D // NSH, D)),
                  _tile(tm, 2 * HD), _tile(tm, 2 * HD)],
        out_specs=[_tile(tm, D), _tile(tm, D), _tile(tm, D), _acc(D)],
        out_shape=[jax.ShapeDtypeStruct((T, D), F32), jax.ShapeDtypeStruct((T, D), CDT),
                   jax.ShapeDtypeStruct((T, D), CDT), jax.ShapeDtypeStruct((1, D), F32)],
        compiler_params=_cp(),
    )(dq, dh1, h, g_pre, wq, cos, sin)


def _o_fwd(o, h, wo, g_post, l, j):
    T = h.shape[0]
    tm = min(TM, T)

    def body(o_ref, h_ref, w_ref, g_ref, h1_ref, m_ref):
        m = _proj_rows(o_ref[...], w_ref)
        m_ref[...] = m.astype(CDT)
        h1_ref[...] = h_ref[...] + _rms_fwd(m, g_ref[l:l + 1, :])

    return pl.pallas_call(
        body, grid=(T // tm,), name=f"o_fwd_{l}",
        in_specs=[_tile(tm, D), _tile(tm, D), _layer_w((D // NSH, D)), _res(g_post.shape)],
        out_specs=[_tile(tm, D), _tile(tm, D)],
        out_shape=[jax.ShapeDtypeStruct((T, D), F32), jax.ShapeDtypeStruct((T, D), CDT)], compiler_params=_cp(),
    )(o, h, wo, g_post)


def _o_bwd(dh1, m, wo, g_post, l, j):
    T = m.shape[0]
    tm = min(TM, T)

    def body(dh_ref, m_ref, w_ref, g_ref, do_ref, dm_ref, dg_ref):
        _zero_first(dg_ref)
        dm, dg = _rms_bwd(dh_ref[...], m_ref[...].astype(F32), g_ref[l:l + 1, :])
        dg_ref[...] += dg
        dmc = dm.astype(CDT)
        dm_ref[...] = dmc
        do_ref[...] = _proj_rows_t(dmc, w_ref).astype(CDT)

    return pl.pallas_call(
        body, grid=(T // tm,), name=f"o_bwd_{l}",
        in_specs=[_tile(tm, D), _tile(tm, D), _layer_w((D // NSH, D)), _res(g_post.shape)],
        out_specs=[_tile(tm, D), _tile(tm, D), _acc(D)],
        out_shape=[jax.ShapeDtypeStruct((T, D), CDT), jax.ShapeDtypeStruct((T, D), CDT),
                   jax.ShapeDtypeStruct((1, D), F32)],
        compiler_params=_cp(),
    )(dh1, m, wo, g_post)


KVW = 2 * NKV * HD


def _kv_fwd(h, g_kv, wk2, wv2, cos, sin):
    T = h.shape[0]
    tm = min(TM, T)

    def body(h_ref, g_ref, wk_ref, wv_ref, cos_ref, sin_ref, k_ref, v_ref):
        hk = _rms_fwd(h_ref[...], g_ref[...]).astype(CDT)
        c, s = _rope_tiles(cos_ref, sin_ref, KVW)
        k_ref[...] = _rope_fwd(_dot(hk, wk_ref[...]), c, s).astype(CDT)
        v_ref[...] = _dot(hk, wv_ref[...]).astype(CDT)

    return pl.pallas_call(
        body, grid=(T // tm,), name="kv_fwd",
        in_specs=[_tile(tm, D), _res(g_kv.shape), _res(wk2.shape), _res(wv2.shape), _tile(tm, 2 * HD), _tile(tm, 2 * HD)],
        out_specs=[_tile(tm, KVW), _tile(tm, KVW)],
        out_shape=[jax.ShapeDtypeStruct((T, KVW), CDT)] * 2, compiler_params=_cp(),
    )(h, g_kv, wk2, wv2, cos, sin)


def _kv_bwd(dh, h, dks, dkhs, dvs, dvhs, g_kv, wk2, wv2, cos, sin):
    T = h.shape[0]
    tq = min(TQ, T)
    nt = T // tq
    n = len(dks)

    def body(*refs):
        dh_ref, h_ref = refs[:2]
        main = refs[2:2 + 2 * n]
        halo = refs[2 + 2 * n:2 + 4 * n]
        g_ref, wk_ref, wv_ref, cos_ref, sin_ref, o_ref, hk_ref, dk0_ref, dv_ref, dg_ref = refs[2 + 4 * n:]
        _zero_first(dg_ref)
        i = pl.program_id(0)

        def total(mains, halos):
            d = mains[0][...]
            for r in mains[1:]:
                d = d + r[...]
            hl = halos[0][...]
            for r in halos[1:]:
                hl = hl + r[...]
            row = lax.broadcasted_iota(jnp.int32, (tq, 1), 0)
            if tq > WIN:
                hl = jnp.concatenate([jnp.zeros((tq - WIN, KVW), F32), hl], axis=0)
            return d + jnp.where((row >= tq - WIN) & (i < nt - 1), hl, 0.0)

        dk = total(main[:n], halo[:n])
        dv = total(main[n:], halo[n:])
        x = h_ref[...]
        g = g_ref[...]
        hk_ref[...] = _rms_fwd(x, g).astype(CDT)
        c, s = _rope_tiles(cos_ref, sin_ref, KVW)
        dk0 = _rope_bwd(dk, c, s).astype(CDT)
        dvc = dv.astype(CDT)
        dk0_ref[...] = dk0
        dv_ref[...] = dvc
        dx, dg = _rms_bwd(_dot_nt(dk0, wk_ref[...]) + _dot_nt(dvc, wv_ref[...]), x, g)
        dg_ref[...] += dg
        o_ref[...] = dh_ref[...] + dx

    nxt = pl.BlockSpec((WIN, KVW), lambda i: (jnp.minimum(i + 1, nt - 1), 0))
    return pl.pallas_call(
        body, grid=(nt,), name="kv_bwd",
        in_specs=[_tile(tq, D), _tile(tq, D)] + [_tile(tq, KVW)] * (2 * n) + [nxt] * (2 * n)
        + [_res(g_kv.shape), _res(wk2.shape), _res(wv2.shape), _tile(tq, 2 * HD), _tile(tq, 2 * HD)],
        out_specs=[_tile(tq, D), _tile(tq, D), _tile(tq, KVW), _tile(tq, KVW), _acc(D)],
        out_shape=[jax.ShapeDtypeStruct((T, D), F32), jax.ShapeDtypeStruct((T, D), CDT),
                   jax.ShapeDtypeStruct((T, KVW), CDT), jax.ShapeDtypeStruct((T, KVW), CDT),
                   jax.ShapeDtypeStruct((1, D), F32)],
        compiler_params=_cp(),
    )(dh, h, *dks, *dvs, *dkhs, *dvhs, g_kv, wk2, wv2, cos, sin)


GQ = 4
LW = 2 * HD


def _band_mask(first):
    r = lax.broadcasted_iota(jnp.int32, (GQ * WIN, 2 * WIN), 0) % WIN
    c = lax.broadcasted_iota(jnp.int32, (GQ * WIN, 2 * WIN), 1)
    return (c > r) & (c <= r + WIN) & ((c >= WIN) | jnp.logical_not(first))


def _stack_heads(ref, rows, g):
    lane = lax.broadcasted_iota(jnp.int32, (WIN, LW), 1)
    parts = []
    for pr in range(2):
        x = ref[rows, (2 * g + pr) * LW:(2 * g + pr + 1) * LW]
        parts += [jnp.where(lane < HD, x, jnp.zeros_like(x)), jnp.where(lane >= HD, x, jnp.zeros_like(x))]
    return jnp.concatenate(parts, axis=0)


def _unstack_heads(x4):
    lane = lax.broadcasted_iota(jnp.int32, (WIN, LW), 1)
    return [jnp.where(lane < HD, x4[(2 * pr) * WIN:(2 * pr + 1) * WIN], x4[(2 * pr + 1) * WIN:(2 * pr + 2) * WIN])
            for pr in range(2)]


def _sink_col(sink_ref, j, g):
    blk = lax.broadcasted_iota(jnp.int32, (GQ * WIN, 1), 0) // WIN
    col = jnp.zeros((GQ * WIN, 1), F32)
    for a in range(GQ):
        col = jnp.where(blk == a, sink_ref[j, GQ * g + a], col)
    return col


def _softmax_sink(s, mask, sink):
    s = jnp.where(mask, s, NEG)
    m = jnp.maximum(jnp.max(s, axis=-1, keepdims=True), sink)
    e = jnp.exp(s - m)
    es = jnp.exp(sink - m)
    l = jnp.sum(e, axis=-1, keepdims=True) + es
    return e / l, es / l


def _kv_halo(tq):
    return pl.BlockSpec((WIN, KVW), lambda i: (jnp.maximum(i * (tq // WIN) - 1, 0), 0))


def _attn_fwd(q, kk, vv, sinks, l, j):
    T = q.shape[0]
    tq = min(TQ, T)

    def body(sink_ref, q_ref, k_ref, kh_ref, v_ref, vh_ref, o_ref):
        i = pl.program_id(0)
        ke = jnp.concatenate([kh_ref[...], k_ref[...]], axis=0)
        ve = jnp.concatenate([vh_ref[...], v_ref[...]], axis=0)
        for n in range(tq // WIN):
            rows = slice(n * WIN, (n + 1) * WIN)
            mask = _band_mask((i == 0) & (n == 0))
            for g in range(NKV):
                kg = ke[n * WIN:(n + 2) * WIN, g * LW:(g + 1) * LW]
                vg = ve[n * WIN:(n + 2) * WIN, g * LW:(g + 1) * LW]
                p, _ = _softmax_sink(_dot_nt(_stack_heads(q_ref, rows, g), kg), mask, _sink_col(sink_ref, j, g))
                pairs = _unstack_heads(_dot(p.astype(CDT), vg))
                for pr in range(2):
                    o_ref[rows, (2 * g + pr) * LW:(2 * g + pr + 1) * LW] = pairs[pr].astype(CDT)

    return pl.pallas_call(
        body, grid=(T // tq,), name=f"attn_fwd_{l}",
        in_specs=[pl.BlockSpec(memory_space=pltpu.SMEM), _tile(tq, D), _tile(tq, KVW), _kv_halo(tq),
                  _tile(tq, KVW), _kv_halo(tq)],
        out_specs=_tile(tq, D), out_shape=jax.ShapeDtypeStruct((T, D), CDT), compiler_params=_cp(),
    )(sinks, q, kk, kk, vv, vv)


def _attn_bwd(q, kk, vv, o, do, sinks, l, j):
    T = q.shape[0]
    tq = min(TQ, T)
    nt = T // tq

    def body(sink_ref, q_ref, k_ref, kh_ref, v_ref, vh_ref, o_ref, do_ref,
             dq_ref, dk_ref, dkh_ref, dv_ref, dvh_ref, dsink_ref, dke, dve):
        _zero_first(dsink_ref)
        i = pl.program_id(0)
        ke = jnp.concatenate([kh_ref[...], k_ref[...]], axis=0)
        ve = jnp.concatenate([vh_ref[...], v_ref[...]], axis=0)
        dke[...] = jnp.zeros_like(dke)
        dve[...] = jnp.zeros_like(dve)
        lane = lax.broadcasted_iota(jnp.int32, (1, LW), 1)
        blk = lax.broadcasted_iota(jnp.int32, (GQ * WIN, 1), 0) // WIN
        dsink = jnp.zeros((1, LW), F32)
        for n in range(tq // WIN):
            rows = slice(n * WIN, (n + 1) * WIN)
            krows = slice(n * WIN, (n + 2) * WIN)
            mask = _band_mask((i == 0) & (n == 0))
            for g in range(NKV):
                cols = slice(g * LW, (g + 1) * LW)
                kg = ke[krows, cols]
                vg = ve[krows, cols]
                q4 = _stack_heads(q_ref, rows, g)
                do4 = _stack_heads(do_ref, rows, g)
                o4 = _stack_heads(o_ref, rows, g)
                p, ps = _softmax_sink(_dot_nt(q4, kg), mask, _sink_col(sink_ref, j, g))
                delta = jnp.sum(do4.astype(F32) * o4.astype(F32), axis=-1, keepdims=True)
                ds = (p * (_dot_nt(do4, vg) - delta)).astype(CDT)
                pc = p.astype(CDT)
                pairs = _unstack_heads(_dot(ds, kg))
                for pr in range(2):
                    dq_ref[rows, (2 * g + pr) * LW:(2 * g + pr + 1) * LW] = pairs[pr].astype(CDT)
                dke[krows, cols] += _dot_tn(ds, q4)
                dve[krows, cols] += _dot_tn(pc, do4)
                t = ps * delta
                for a in range(GQ):
                    dsink = dsink - jnp.where(lane == GQ * g + a, jnp.sum(jnp.where(blk == a, t, 0.0)), 0.0)
        dsink_ref[...] += dsink
        dkh_ref[...] = dke[:WIN, :]
        dk_ref[...] = dke[WIN:, :]
        dvh_ref[...] = dve[:WIN, :]
        dv_ref[...] = dve[WIN:, :]

    halo_out = pl.BlockSpec((WIN, KVW), lambda i: (i, 0))
    return pl.pallas_call(
        body, grid=(nt,), name=f"attn_bwd_{l}",
        in_specs=[pl.BlockSpec(memory_space=pltpu.SMEM), _tile(tq, D), _tile(tq, KVW), _kv_halo(tq),
                  _tile(tq, KVW), _kv_halo(tq), _tile(tq, D), _tile(tq, D)],
        out_specs=[_tile(tq, D), _tile(tq, KVW), halo_out, _tile(tq, KVW), halo_out, _acc(LW)],
        out_shape=[jax.ShapeDtypeStruct((T, D), CDT), jax.ShapeDtypeStruct((T, KVW), F32),
                   jax.ShapeDtypeStruct((nt * WIN, KVW), F32), jax.ShapeDtypeStruct((T, KVW), F32),
                   jax.ShapeDtypeStruct((nt * WIN, KVW), F32), jax.ShapeDtypeStruct((1, LW), F32)],
        scratch_shapes=[pltpu.VMEM((WIN + tq, KVW), F32), pltpu.VMEM((WIN + tq, KVW), F32)],
        compiler_params=_cp(),
    )(sinks, q, kk, kk, vv, vv, o, do)


def _chunks(tm):
    return pl.BlockSpec((NSH, tm, FS), lambda i: (0, i, 0))


def _ffn_fwd(h1, g_pre, wg, wu, wd, g_post, l, comm=None):
    T = h1.shape[0]
    tm = min(TM, T)

    def body(h_ref, gpre_ref, wg_ref, wu_ref, wd_ref, gpost_ref, h2_ref, a_ref, b_ref, f_ref):
        x = h_ref[...]
        fn = _rms_fwd(x, gpre_ref[l:l + 1, :]).astype(CDT)
        f = jnp.zeros((tm, D), F32)
        for s in range(NSH):
            a = _dot(fn, wg_ref[s])
            b = _dot(fn, wu_ref[s])
            a_ref[s] = a.astype(CDT)
            b_ref[s] = b.astype(CDT)
            f = f + _dot((a * _sigmoid(a) * b).astype(CDT), wd_ref[s])
        f_ref[...] = f.astype(CDT)
        h2_ref[...] = x + _rms_fwd(f, gpost_ref[l:l + 1, :])

    return _pallas(
        body, grid=(T // tm,), name=f"ffn_fwd_{l}",
        in_specs=[_tile(tm, D), _res(g_pre.shape), _layer_w((D, FS)), _layer_w((D, FS)), _layer_w((FS, D)),
                  _res(g_post.shape)],
        out_specs=[_tile(tm, D), _chunks(tm), _chunks(tm), _tile(tm, D)],
        out_shape=[jax.ShapeDtypeStruct((T, D), F32), jax.ShapeDtypeStruct((NSH, T, FS), CDT),
                   jax.ShapeDtypeStruct((NSH, T, FS), CDT), jax.ShapeDtypeStruct((T, D), CDT)],
        args=(h1, g_pre, wg, wu, wd, g_post), comm=comm)


def _ffn_bwd(dh2, h1, f, a, b, g_pre, wg, wu, wd, g_post, l, comm=None):
    T = h1.shape[0]
    tm = min(TM_FFN_BWD, T)

    def body(dh_ref, h_ref, f_ref, a_ref, b_ref, gpre_ref, wg_ref, wu_ref, wd_ref, gpost_ref,
             dh1_ref, fn_ref, df_ref, act_ref, da_ref, db_ref, dgpre_ref, dgpost_ref):
        _zero_first(dgpre_ref, dgpost_ref)
        x = h_ref[...]
        gpre = gpre_ref[l:l + 1, :]
        fn_ref[...] = _rms_fwd(x, gpre).astype(CDT)
        dh2 = dh_ref[...]
        df, dg = _rms_bwd(dh2, f_ref[...].astype(F32), gpost_ref[l:l + 1, :])
        dgpost_ref[...] += dg
        dfc = df.astype(CDT)
        df_ref[...] = dfc
        dfn = jnp.zeros((tm, D), F32)
        for s in range(NSH):
            av = a_ref[s].astype(F32)
            bv = b_ref[s].astype(F32)
            sg = _sigmoid(av)
            silu = av * sg
            act_ref[s] = (silu * bv).astype(CDT)
            dact = _dot_nt(dfc, wd_ref[s])
            da = (dact * bv * (sg * (1.0 + av * (1.0 - sg)))).astype(CDT)
            db = (dact * silu).astype(CDT)
            da_ref[s] = da
            db_ref[s] = db
            dfn = dfn + _dot_nt(da, wg_ref[s]) + _dot_nt(db, wu_ref[s])
        dx, dg = _rms_bwd(dfn, x, gpre)
        dgpre_ref[...] += dg
        dh1_ref[...] = dh2 + dx

    return _pallas(
        body, grid=(T // tm,), name=f"ffn_bwd_{l}",
        in_specs=[_tile(tm, D), _tile(tm, D), _tile(tm, D), _chunks(tm), _chunks(tm), _res(g_pre.shape),
                  _layer_w((D, FS)), _layer_w((D, FS)), _layer_w((FS, D)), _res(g_post.shape)],
        out_specs=[_tile(tm, D), _tile(tm, D), _tile(tm, D), _chunks(tm), _chunks(tm), _chunks(tm), _acc(D), _acc(D)],
        out_shape=[jax.ShapeDtypeStruct((T, D), F32), jax.ShapeDtypeStruct((T, D), CDT), jax.ShapeDtypeStruct((T, D), CDT),
                   jax.ShapeDtypeStruct((NSH, T, FS), CDT), jax.ShapeDtypeStruct((NSH, T, FS), CDT),
                   jax.ShapeDtypeStruct((NSH, T, FS), CDT), jax.ShapeDtypeStruct((1, D), F32),
                   jax.ShapeDtypeStruct((1, D), F32)],
        args=(dh2, h1, f, a, b, g_pre, wg, wu, wd, g_post), comm=comm)


def _ple_proj(p, wpp_ref):
    return jnp.concatenate([_dot(p, wpp_ref[s]) for s in range(NSH)], axis=1)


def _ple_fwd(h2, p, g_ple, wpg, wpp, l, comm=None):
    T = h2.shape[0]
    tm = min(TM, T)

    def body(h_ref, p_ref, g_ref, wpg_ref, wpp_ref, o_ref):
        x = h_ref[...]
        pn = _rms_fwd(x, g_ref[l:l + 1, :]).astype(CDT)
        gate = _sigmoid(_proj_rows(pn, wpg_ref))
        o_ref[...] = x + _ple_proj(p_ref[...].astype(CDT), wpp_ref) * gate

    return _pallas(
        body, grid=(T // tm,), name=f"ple_fwd_{l}",
        in_specs=[_tile(tm, D), pl.BlockSpec((None, tm, PLE), lambda i: (l, i, 0)), _res(g_ple.shape),
                  _layer_w((D // NSH, D)), _layer_w((PLE, D // NSH))],
        out_specs=[_tile(tm, D)], out_shape=[jax.ShapeDtypeStruct((T, D), F32)],
        args=(h2, p, g_ple, wpg, wpp), comm=comm)


def _ple_bwd(dh3, h2, p, g_ple, wpg, wpp, l):
    T = h2.shape[0]
    tm = min(TM, T)

    def body(dh_ref, h_ref, p_ref, g_ref, wpg_ref, wpp_ref, o_ref, pn_ref, dz_ref, dpp_ref, pc_ref, dg_ref):
        _zero_first(dg_ref)
        x = h_ref[...]
        g = g_ref[l:l + 1, :]
        pn = _rms_fwd(x, g).astype(CDT)
        pn_ref[...] = pn
        gate = _sigmoid(_proj_rows(pn, wpg_ref))
        pc = p_ref[...].astype(CDT)
        pc_ref[...] = pc
        pp = _ple_proj(pc, wpp_ref)
        dh3 = dh_ref[...]
        dpp_ref[...] = (dh3 * gate).astype(CDT)
        dz = (dh3 * pp * gate * (1.0 - gate)).astype(CDT)
        dz_ref[...] = dz
        dx, dg = _rms_bwd(_proj_rows_t(dz, wpg_ref), x, g)
        dg_ref[...] += dg
        o_ref[...] = dh3 + dx

    return pl.pallas_call(
        body, grid=(T // tm,), name=f"ple_bwd_{l}",
        in_specs=[_tile(tm, D), _tile(tm, D), pl.BlockSpec((None, tm, PLE), lambda i: (l, i, 0)), _res(g_ple.shape),
                  _layer_w((D // NSH, D)), _layer_w((PLE, D // NSH))],
        out_specs=[_tile(tm, D), _tile(tm, D), _tile(tm, D), _tile(tm, D), _tile(tm, PLE), _acc(D)],
        out_shape=[jax.ShapeDtypeStruct((T, D), F32), jax.ShapeDtypeStruct((T, D), CDT), jax.ShapeDtypeStruct((T, D), CDT),
                   jax.ShapeDtypeStruct((T, D), CDT), jax.ShapeDtypeStruct((T, PLE), CDT), jax.ShapeDtypeStruct((1, D), F32)],
        compiler_params=_cp(),
    )(dh3, h2, p, g_ple, wpg, wpp)


def _loss_grad(y, tgt):
    T = y.shape[0]
    tm = min(TM, T)

    def body(y_ref, t_ref, dy_ref, loss_ref):
        _zero_first(loss_ref)
        err = y_ref[...] - t_ref[...]
        dy_ref[...] = err * (1.0 / D)
        lane = lax.broadcasted_iota(jnp.int32, (1, 128), 1)
        loss_ref[...] += jnp.where(lane == 0, (0.5 / D) * jnp.sum(err * err), 0.0)

    return pl.pallas_call(
        body, grid=(T // tm,), name="loss_grad", in_specs=[_tile(tm, D), _tile(tm, D)],
        out_specs=[_tile(tm, D), _acc(128)],
        out_shape=[jax.ShapeDtypeStruct((T, D), F32), jax.ShapeDtypeStruct((1, 128), F32)], compiler_params=_cp(),
    )(y, tgt)


def _mm_tn(x, dy, name, n_split=1):
    xb, yb = x.ndim == 3, dy.ndim == 3
    T, K = x.shape[-2:]
    N = dy.shape[-1] // n_split
    B = x.shape[0] if xb else dy.shape[0] if yb else n_split
    tt = min(TT, T)
    nt = T // tt

    def body(x_ref, dy_ref, o_ref, oc_ref):
        t = pl.program_id(1)

        @pl.when(t == 0)
        def _():
            o_ref[...] = jnp.zeros_like(o_ref)

        o_ref[...] += _dot_tn(x_ref[...].astype(CDT), dy_ref[...])

        @pl.when(t == nt - 1)
        def _():
            oc_ref[...] = o_ref[...].astype(CDT)

    x_spec = pl.BlockSpec((None, tt, K), lambda b, t: (b, t, 0)) if xb else pl.BlockSpec((tt, K), lambda b, t: (t, 0))
    if yb:
        y_spec = pl.BlockSpec((None, tt, N), lambda b, t: (b, t, 0))
    else:
        y_spec = pl.BlockSpec((tt, N), lambda b, t: (t, b if n_split > 1 else 0))
    o_spec = pl.BlockSpec((None, K, N), lambda b, t: (b, 0, 0))
    return pl.pallas_call(
        body, grid=(B, nt), name=name, in_specs=[x_spec, y_spec], out_specs=[o_spec, o_spec],
        out_shape=[jax.ShapeDtypeStruct((B, K, N), F32), jax.ShapeDtypeStruct((B, K, N), CDT)], compiler_params=_cp(2),
    )(x, dy)


def _fold_dup(dw2, name):
    def body(x_ref, o_ref, oc_ref):
        x = x_ref[...]
        y = jnp.concatenate([x[:, g * LW:g * LW + HD] + x[:, g * LW + HD:(g + 1) * LW] for g in range(NKV)], axis=1)
        o_ref[...] = y
        oc_ref[...] = y.astype(CDT)

    return pl.pallas_call(
        body, grid=(NSH,), name=name, in_specs=[_tile(D // NSH, KVW)], out_specs=[_tile(D // NSH, NKV * HD)] * 2,
        out_shape=[jax.ShapeDtypeStruct((D, NKV * HD), F32), jax.ShapeDtypeStruct((D, NKV * HD), CDT)],
        compiler_params=_cp(),
    )(dw2)


def _swap_with_sibling(parts):
    n = len(parts)

    def body(*refs):
        ins, outs = refs[:n], refs[n:2 * n]
        send, recv = refs[2 * n:]
        x, y, c, _ = _place()
        copies = [pltpu.make_async_remote_copy(
            src_ref=ins[a], dst_ref=outs[a], send_sem=send.at[a], recv_sem=recv.at[a],
            device_id=(x, y, 1 - c), device_id_type=MESH) for a in range(n)]
        for cp in copies:
            cp.start()
        for cp in copies:
            cp.wait_recv()
        for cp in copies:
            cp.wait_send()

    return pl.pallas_call(
        body, name="swap_partials", in_specs=[HBM] * n, out_specs=[HBM] * n,
        out_shape=[jax.ShapeDtypeStruct(p.shape, p.dtype) for p in parts],
        scratch_shapes=[pltpu.SemaphoreType.DMA((n,)), pltpu.SemaphoreType.DMA((n,))],
    )(*parts)


def _all_reduce_small(vec):
    R = vec.shape[0]
    ndev = 8

    def body(v_ref, o_ref, buf, send, recv):
        x, y, c, _ = _place()
        me = 4 * x + 2 * y + c
        buf[me] = v_ref[...]
        copies = []
        for k in range(1, ndev):
            peer = (x if not k & 4 else 1 - x, y if not k & 2 else 1 - y, c if not k & 1 else 1 - c)
            copies.append(pltpu.make_async_remote_copy(
                src_ref=v_ref, dst_ref=buf.at[me], send_sem=send.at[k - 1], recv_sem=recv.at[k - 1],
                device_id=peer, device_id_type=MESH))
            copies[-1].start()
        for k in range(1, ndev):
            peer = (x if not k & 4 else 1 - x, y if not k & 2 else 1 - y, c if not k & 1 else 1 - c)
            pltpu.make_async_remote_copy(
                src_ref=v_ref, dst_ref=buf.at[4 * peer[0] + 2 * peer[1] + peer[2]], send_sem=send.at[k - 1],
                recv_sem=recv.at[k - 1], device_id=peer, device_id_type=MESH).wait_recv()
        for cp in copies:
            cp.wait_send()
        tot = buf[0]
        for d in range(1, ndev):
            tot = tot + buf[d]
        o_ref[...] = tot

    return pl.pallas_call(
        body, name="all_reduce_small", in_specs=[pl.BlockSpec(memory_space=pltpu.VMEM)],
        out_specs=pl.BlockSpec(memory_space=pltpu.VMEM), out_shape=jax.ShapeDtypeStruct(vec.shape, F32),
        scratch_shapes=[pltpu.VMEM((ndev, R, D), F32), pltpu.SemaphoreType.DMA((ndev - 1,)),
                        pltpu.SemaphoreType.DMA((ndev - 1,))],
    )(vec)


def _row_tile(rows):
    for t in (512, 256, 128, 64, 32, 16, 8):
        if rows % t == 0:
            return t
    return rows


def _sum_own_and_received(own, got, me, name):
    _, R, C = own.shape
    tr = _row_tile(R)

    def body(me_ref, own_ref, got_ref, o_ref):
        o_ref[...] = ((own_ref[...] + got_ref[0].astype(F32)) + got_ref[1].astype(F32)) + got_ref[2].astype(F32)

    return pl.pallas_call(
        body, name=name,
        grid_spec=pltpu.PrefetchScalarGridSpec(
            num_scalar_prefetch=1, grid=(R // tr,),
            in_specs=[pl.BlockSpec((None, tr, C), lambda i, me_ref: (me_ref[0], i, 0)),
                      pl.BlockSpec((3, tr, C), lambda i, me_ref: (0, i, 0))],
            out_specs=pl.BlockSpec((tr, C), lambda i, me_ref: (i, 0))),
        out_shape=jax.ShapeDtypeStruct((R, C), F32), compiler_params=_cp(),
    )(me, own, got)


def _adamw(gs, w, m, v, name):
    L, R, C = w.shape
    tr = _row_tile(R)
    ng = len(gs[0])
    c1 = 1.0 - ADAM_B1 ** ADAM_STEP
    c2 = 1.0 - ADAM_B2 ** ADAM_STEP

    def body(*refs):
        w_ref, m_ref, v_ref, g_ref, d_ref, nm_ref, nv_ref = refs[L * ng:]
        lay = pl.program_id(0)
        g = None
        for l in range(L):
            gl = refs[l * ng][...]
            for r in refs[l * ng + 1:(l + 1) * ng]:
                gl = gl + r[...]
            g = gl if g is None else jnp.where(lay == l, gl, g)
        nm = ADAM_B1 * m_ref[...] + (1.0 - ADAM_B1) * g
        nv = ADAM_B2 * v_ref[...] + (1.0 - ADAM_B2) * (g * g)
        g_ref[...] = g
        nm_ref[...] = nm
        nv_ref[...] = nv
        d_ref[...] = -ADAM_LR * ((nm / c1) / (jnp.sqrt(nv / c2) + ADAM_EPS) + ADAM_WD * w_ref[...])

    gspecs = [pl.BlockSpec((tr, C), lambda lay, i, l=l: (jnp.where(lay == l, i, 0), 0)) for l in range(L) for _ in range(ng)]
    spec = pl.BlockSpec((None, tr, C), lambda lay, i: (lay, i, 0))
    return pl.pallas_call(
        body, grid=(L, R // tr), name=name, in_specs=gspecs + [spec] * 3, out_specs=[spec] * 4,
        out_shape=[jax.ShapeDtypeStruct((L, R, C), F32)] * 4, compiler_params=_cp(2),
    )(*[g for gl in gs for g in gl], w, m, v)


def _rope_tables(T):
    inv = 1.0 / (ROPE_THETA ** (jnp.arange(0, HD, 2, dtype=F32) / HD))
    ang = jnp.arange(T, dtype=F32)[:, None] * inv[None, :]
    c, s = jnp.cos(ang), jnp.sin(ang)
    return jnp.tile(jnp.concatenate([c, c], axis=1), (1, 2)), jnp.tile(jnp.concatenate([-s, s], axis=1), (1, 2))


def _dup_cols(w):
    return jnp.broadcast_to(w.reshape(D, NKV, 1, HD), (D, NKV, 2, HD)).reshape(D, KVW)


def kernel(x, p, mix_pre_g, mix_post_g, ffn_pre_g, ffn_post_g, pool_w, pool_scale, kv_norm_g, w_k, w_v, w_q, w_o, sinks, w_ff_gate, w_ff_up, w_ff_down, ple_norm_g, w_ple_gate, w_ple_proj, loss_target, m_mix_pre_g, m_mix_post_g, m_ffn_pre_g, m_ffn_post_g, m_pool_w, m_pool_scale, m_kv_norm_g, m_w_k, m_w_v, m_w_q, m_w_o, m_sinks, m_w_ff_gate, m_w_ff_up, m_w_ff_down, m_ple_norm_g, m_w_ple_gate, m_w_ple_proj, v_mix_pre_g, v_mix_post_g, v_ffn_pre_g, v_ffn_post_g, v_pool_w, v_pool_scale, v_kv_norm_g, v_w_k, v_w_v, v_w_q, v_w_o, v_sinks, v_w_ff_gate, v_w_ff_up, v_w_ff_down, v_ple_norm_g, v_w_ple_gate, v_w_ple_proj):
    depth = mix_pre_g.shape[0]
    n_pool = pool_w.shape[0]
    n_attn = w_q.shape[0]
    T = x.shape[1]
    h = x[0]
    p = p[:, 0]
    tgt = loss_target[0]
    me = (2 * lax.axis_index("x") + lax.axis_index("y")).astype(jnp.int32).reshape(1)

    c_pool, c_wk, c_wv, c_wq, c_wo, c_wg, c_wu, c_wd, c_wpg, c_wpp = (
        w.astype(CDT) for w in (pool_w, w_k, w_v, w_q, w_o, w_ff_gate, w_ff_up, w_ff_down, w_ple_gate, w_ple_proj))

    def ffn_srcs(l):
        return [(c_wg, l), (c_wu, l), (c_wd, l)]

    def ple_srcs(l):
        return [(c_wpg, l), (c_wpp, l)]

    gpw, *w0 = _run_comm(_gather_comm([(c_pool, None)] + ffn_srcs(0) + ple_srcs(0)), "gather_layer_0")
    wff = {0: w0[:3]}
    wple = {0: w0[3:]}
    wattn = {}
    pw = jnp.transpose(gpw, (1, 2, 0, 3, 4)).reshape(n_pool, len(POOL_WINDOWS), PG, PG)
    kv_g = kv_norm_g.reshape(1, D)
    cos, sin = _rope_tables(T)

    pscale = _all_reduce_small(jnp.pad(
        lax.dynamic_update_slice(jnp.zeros((n_pool, D), F32), pool_scale, (0, me[0] * (D // NSH))),
        ((0, 8 - n_pool), (0, 0))))[:n_pool] * 0.5

    saved = []
    kk = vv = wk2 = wv2 = None
    for l in range(depth):
        s = {"h": h}
        if l < n_pool:
            h1 = _pool_fwd(h, mix_pre_g, pw, pscale, mix_post_g, l)
        else:
            j = l - n_pool
            wq, wo = wattn[l]
            s["q"] = _q_fwd(h, mix_pre_g, wq, cos, sin, l, j)
            s["o"] = _attn_fwd(s["q"], kk, vv, sinks, l, j)
            h1, s["m"] = _o_fwd(s["o"], h, wo, mix_post_g, l, j)
        s["h1"] = h1
        srcs = []
        if l + 1 < depth:
            srcs = ffn_srcs(l + 1)
            if l + 1 >= n_pool:
                srcs += [(c_wq, l + 1 - n_pool), (c_wo, l + 1 - n_pool)]
            if l + 1 == n_pool:
                srcs += [(c_wk, None), (c_wv, None)]
        (h2, s["a"], s["b"], s["f"]), got = _ffn_fwd(
            h1, ffn_pre_g, *wff[l], ffn_post_g, l, comm=_gather_comm(srcs) if srcs else None)
        if srcs:
            wff[l + 1] = got[:3]
            if l + 1 >= n_pool:
                wattn[l + 1] = got[3:5]
            if l + 1 == n_pool:
                wk2 = _dup_cols(got[5].reshape(D, NKV * HD))
                wv2 = _dup_cols(got[6].reshape(D, NKV * HD))
        s["h2"] = h2
        (h,), got = _ple_fwd(h2, p, ple_norm_g, *wple[l], l,
                             comm=_gather_comm(ple_srcs(l + 1)) if l + 1 < depth else None)
        if l + 1 < depth:
            wple[l + 1] = got
        if l == n_pool - 1:
            s["hkv"] = h
            kk, vv = _kv_fwd(h, kv_g, wk2, wv2, cos, sin)
        saved.append(s)

    dh, loss_row = _loss_grad(h, tgt)

    entries, pending = [], []

    def add_grad(k, l, g32, g16):
        def by_shard(g):
            return g.reshape(NSH, -1, g.shape[-1])
        e = dict(k=k, l=l, own=by_shard(g32), sent=by_shard(g16))
        entries.append(e)
        pending.append(e)

    def carry(n=None):
        take = pending[:n] if n else list(pending)
        del pending[:len(take)]
        return take, (_scatter_comm([e["sent"] for e in take]) if take else None)

    def landed(take, got):
        for e, g in zip(take, got):
            e["got"] = g

    small = {k: [None] * depth for k in ("mix_pre", "mix_post", "ffn_pre", "ffn_post", "ple")}
    dpool_scale = [None] * n_pool
    dsinks = [None] * n_attn
    dks, dkhs, dvs, dvhs = [], [], [], []
    for l in reversed(range(depth)):
        s = saved[l]
        if l == n_pool - 1:
            dh, hk, dk0, dvc, dkv_g = _kv_bwd(dh, s["hkv"], dks, dkhs, dvs, dvhs, kv_g, wk2, wv2, cos, sin)
            add_grad("w_k", None, *_fold_dup(_mm_tn(hk, dk0, "dw_k")[0][0], "fold_w_k"))
            add_grad("w_v", None, *_fold_dup(_mm_tn(hk, dvc, "dw_v")[0][0], "fold_w_v"))
        dh, pn, dz, dpp, pc, small["ple"][l] = _ple_bwd(dh, s["h2"], p, ple_norm_g, *wple[l], l)
        add_grad("w_ple_gate", l, *_mm_tn(pn, dz, f"dw_ple_gate_{l}"))
        add_grad("w_ple_proj", l, *_mm_tn(pc, dpp, f"dw_ple_proj_{l}", n_split=NSH))
        take, comm = carry()
        (dh, fn, df, act, da, db, small["ffn_pre"][l], small["ffn_post"][l]), got = _ffn_bwd(
            dh, s["h1"], s["f"], s["a"], s["b"], ffn_pre_g, *wff[l], ffn_post_g, l, comm=comm)
        landed(take, got)
        add_grad("w_ff_gate", l, *_mm_tn(fn, da, f"dw_ff_gate_{l}"))
        add_grad("w_ff_up", l, *_mm_tn(fn, db, f"dw_ff_up_{l}"))
        add_grad("w_ff_down", l, *_mm_tn(act, df, f"dw_ff_down_{l}"))
        if l < n_pool:
            take, comm = carry(2) if l == 0 else ([], None)
            (dpool, dpw, dpool_scale[l], small["mix_post"][l]), got = _pool_bwd_a(
                dh, s["h"], mix_pre_g, pw, pscale, mix_post_g, l, comm=comm)
            landed(take, got)
            dpw = jnp.transpose(dpw.reshape(len(POOL_WINDOWS), NSH, PG // NSH, PG), (1, 0, 2, 3))
            dpw = dpw.reshape(NSH, len(POOL_WINDOWS) * PG // NSH, PG)
            add_grad("pool_w", l, dpw, dpw.astype(CDT))
            take, comm = carry() if l == 0 else ([], None)
            (dh, small["mix_pre"][l]), got = _pool_bwd_b(dpool, dh, s["h"], mix_pre_g, l, comm=comm)
            landed(take, got)
        else:
            j = l - n_pool
            wq, wo = wattn[l]
            do, dm, small["mix_post"][l] = _o_bwd(dh, s["m"], wo, mix_post_g, l, j)
            add_grad("w_o", j, *_mm_tn(s["o"], dm, f"dw_o_{l}"))
            dq, dk, dkh, dv, dvh, dsinks[j] = _attn_bwd(s["q"], kk, vv, s["o"], do, sinks, l, j)
            dks.append(dk); dkhs.append(dkh); dvs.append(dv); dvhs.append(dvh)
            dh, hn, dq0, small["mix_pre"][l] = _q_bwd(dq, dh, s["h"], mix_pre_g, wq, cos, sin, l, j)
            add_grad("w_q", j, *_mm_tn(hn, dq0, f"dw_q_{l}"))
    if pending:
        take, comm = carry()
        landed(take, _run_comm(comm, "scatter_rest"))
    grad_x = dh[None]

    rows = [jnp.concatenate(small[k], axis=0) for k in ("mix_pre", "mix_post", "ffn_pre", "ffn_post", "ple")]
    rows += [dkv_g, jnp.concatenate(dpool_scale, axis=0)]
    rows += [jnp.pad(jnp.concatenate(dsinks, axis=0), ((0, 0), (0, D - LW))), jnp.pad(loss_row, ((0, 0), (0, D - 128)))]
    vec = jnp.concatenate(rows, axis=0)
    nrow = vec.shape[0]
    vec = _all_reduce_small(jnp.pad(vec, ((0, -nrow % 8), (0, 0))))
    o = 0
    red = {}
    for k in ("mix_pre_g", "mix_post_g", "ffn_pre_g", "ffn_post_g", "ple_norm_g"):
        red[k] = vec[o:o + depth]
        o += depth
    red["kv_norm_g"] = vec[o:o + 1]
    red["pool_scale"] = lax.dynamic_slice(vec[o + 1:o + 1 + n_pool], (0, me[0] * (D // NSH)), (n_pool, D // NSH))
    o += 1 + n_pool
    red["sinks"] = vec[o:o + n_attn, :GQ * NKV]
    loss = vec[o + n_attn, 0]

    parts = [_sum_own_and_received(e["own"], e["got"], me, f"sum_{e['k']}_{e['l']}") for e in entries]
    sib = _swap_with_sibling(parts)

    given = dict(
        mix_pre_g=(mix_pre_g, m_mix_pre_g, v_mix_pre_g), mix_post_g=(mix_post_g, m_mix_post_g, v_mix_post_g),
        ffn_pre_g=(ffn_pre_g, m_ffn_pre_g, v_ffn_pre_g), ffn_post_g=(ffn_post_g, m_ffn_post_g, v_ffn_post_g),
        pool_w=(pool_w, m_pool_w, v_pool_w), pool_scale=(pool_scale, m_pool_scale, v_pool_scale),
        kv_norm_g=(kv_norm_g, m_kv_norm_g, v_kv_norm_g), w_k=(w_k, m_w_k, v_w_k), w_v=(w_v, m_w_v, v_w_v),
        w_q=(w_q, m_w_q, v_w_q), w_o=(w_o, m_w_o, v_w_o), sinks=(sinks, m_sinks, v_sinks),
        w_ff_gate=(w_ff_gate, m_w_ff_gate, v_w_ff_gate), w_ff_up=(w_ff_up, m_w_ff_up, v_w_ff_up),
        w_ff_down=(w_ff_down, m_w_ff_down, v_w_ff_down), ple_norm_g=(ple_norm_g, m_ple_norm_g, v_ple_norm_g),
        w_ple_gate=(w_ple_gate, m_w_ple_gate, v_w_ple_gate), w_ple_proj=(w_ple_proj, m_w_ple_proj, v_w_ple_proj))
    results = {}
    for k, g in red.items():
        w, m, v = given[k]
        outs = _adamw([[g.reshape(-1, g.shape[-1])]], *(t.reshape(1, -1, t.shape[-1]) for t in (w, m, v)), f"adamw_{k}")
        results[k] = [t.reshape(w.shape) for t in outs]
    per_layer = {}
    for e, a, b in zip(entries, parts, sib):
        per_layer.setdefault(e["k"], {})[e["l"] or 0] = [a, b]
    for k, layers in per_layer.items():
        w, m, v = given[k]
        gs = [layers[l] for l in range(len(layers))]
        outs = _adamw(gs, *(t.reshape((len(gs),) + gs[0][0].shape) for t in (w, m, v)), f"adamw_{k}")
        results[k] = [t.reshape(w.shape) for t in outs]

    order = ["mix_pre_g", "mix_post_g", "ffn_pre_g", "ffn_post_g", "pool_w", "pool_scale", "kv_norm_g", "w_k", "w_v",
             "w_q", "w_o", "sinks", "w_ff_gate", "w_ff_up", "w_ff_down", "ple_norm_g", "w_ple_gate", "w_ple_proj"]
    return (loss, grad_x, *[results[k][0] for k in order], *[results[k][1] for k in order],
            *[results[k][2] for k in order], *[results[k][3] for k in order])
```

```python
import functools

import jax
import jax.numpy as jnp
from jax import lax
from jax.experimental import pallas as pl
from jax.experimental.pallas import tpu as pltpu

F32 = jnp.float32
CDT = jnp.bfloat16

D = 1024
PG = 256
POOL_WINDOWS = (2, 4, 8, 16)
HALO = 16
HD = 64
NKV = 4
WIN = 128
FS = 704
NSH = 4
PLE = 256
ROPE_THETA = 10000.0
EPS = 1e-6
NEG = -1e30
QSCALE = HD ** -0.5

ADAM_LR, ADAM_B1, ADAM_B2, ADAM_EPS, ADAM_WD, ADAM_STEP = 0.001, 0.9, 0.999, 1e-08, 0.01, 10

VMEM_LIMIT = 56 * 1024 * 1024
TM = 512
TM_FFN_BWD = 256
TQ = 256
TT = 1024
MESH = pl.DeviceIdType.MESH


def _cp(n_axes=1):
    return pltpu.CompilerParams(dimension_semantics=("arbitrary",) * n_axes, vmem_limit_bytes=VMEM_LIMIT)


def _dot(a, b):
    return jnp.dot(a, b, preferred_element_type=F32)


def _dot_nt(a, b):
    return lax.dot_general(a, b, (((1,), (1,)), ((), ())), preferred_element_type=F32)


def _dot_tn(a, b):
    return lax.dot_general(a, b, (((0,), (0,)), ((), ())), preferred_element_type=F32)


def _rms_fwd(x, g):
    r = lax.rsqrt(jnp.mean(x * x, axis=-1, keepdims=True) + EPS)
    return x * r * g


def _rms_bwd(dy, x, g):
    r = lax.rsqrt(jnp.mean(x * x, axis=-1, keepdims=True) + EPS)
    xh = x * r
    dg = jnp.sum(dy * xh, axis=0, keepdims=True)
    u = dy * g
    dx = r * (u - xh * jnp.mean(u * xh, axis=-1, keepdims=True))
    return dx, dg


def _sigmoid(a):
    return 1.0 / (1.0 + jnp.exp(-a))


def _swap_half(x):
    w = x.shape[1]
    lane = lax.broadcasted_iota(jnp.int32, x.shape, 1)
    return jnp.where((lane % HD) < HD // 2, pltpu.roll(x, w - HD // 2, axis=1), pltpu.roll(x, HD // 2, axis=1))


def _rope_fwd(x, cos, sin):
    return x * cos + _swap_half(x) * sin


def _rope_bwd(d, cos, sin):
    return d * cos + _swap_half(d * sin)


def _tile(tm, w):
    return pl.BlockSpec((tm, w), lambda i: (i, 0))


def _res(shape):
    return pl.BlockSpec(shape, lambda i: (0,) * len(shape), pipeline_mode=pl.Buffered(1))


def _layer_w(shape):
    return pl.BlockSpec((NSH,) + shape, lambda *_: (0, 0, 0), pipeline_mode=pl.Buffered(1))


def _acc(w):
    return pl.BlockSpec((1, w), lambda i: (0, 0))


def _zero_first(*refs):
    @pl.when(pl.program_id(0) == 0)
    def _():
        for r in refs:
            r[...] = jnp.zeros_like(r)


HBM = pl.BlockSpec(memory_space=pl.ANY)


def _place():
    x, y, c = lax.axis_index("x"), lax.axis_index("y"), lax.axis_index("c")
    chips = [(1 - x, y), (x, 1 - y), (1 - x, 1 - y)]
    return x, y, c, chips


class _Comm:
    def __init__(self, args, out_shape, sems, copies):
        self.args, self.out_shape, self.sems, self.copies = list(args), list(out_shape), list(sems), copies

    def start(self, cin, cout, sems):
        local, sends, _ = self.copies(cin, cout, sems)
        for cp in local + sends:
            cp.start()

    def wait(self, cin, cout, sems):
        local, sends, recvs = self.copies(cin, cout, sems)
        for cp in recvs:
            cp.wait_recv()
        for cp in sends:
            cp.wait_send()
        for cp in local:
            cp.wait()


def _gather_comm(srcs):
    n = len(srcs)
    shapes = [(a.shape if l is None else a.shape[1:]) for a, l in srcs]

    def copies(cin, cout, sems):
        send, recv, loc = sems
        x, y, c, chips = _place()
        me = 2 * x + y
        local, sends, recvs = [], [], []
        for a, (_, l) in enumerate(srcs):
            src = cin[a] if l is None else cin[a].at[l]
            local.append(pltpu.make_async_copy(src, cout[a].at[me], loc.at[a]))
            for j, (px, py) in enumerate(chips):
                kw = dict(src_ref=src, send_sem=send.at[3 * a + j], recv_sem=recv.at[3 * a + j],
                          device_id=(px, py, c), device_id_type=MESH)
                sends.append(pltpu.make_async_remote_copy(dst_ref=cout[a].at[me], **kw))
                recvs.append(pltpu.make_async_remote_copy(dst_ref=cout[a].at[2 * px + py], **kw))
        return local, sends, recvs

    return _Comm([a for a, _ in srcs], [jax.ShapeDtypeStruct((NSH,) + s, CDT) for s in shapes],
                 [pltpu.SemaphoreType.DMA((3 * n,)), pltpu.SemaphoreType.DMA((3 * n,)), pltpu.SemaphoreType.DMA((n,))],
                 copies)


def _scatter_comm(grads):
    n = len(grads)

    def copies(cin, cout, sems):
        send, recv = sems
        x, y, c, chips = _place()
        sends, recvs = [], []
        for a in range(n):
            for j, (px, py) in enumerate(chips):
                cp = pltpu.make_async_remote_copy(
                    src_ref=cin[a].at[2 * px + py], dst_ref=cout[a].at[j], send_sem=send.at[3 * a + j],
                    recv_sem=recv.at[3 * a + j], device_id=(px, py, c), device_id_type=MESH)
                sends.append(cp)
                recvs.append(cp)
        return [], sends, recvs

    return _Comm(grads, [jax.ShapeDtypeStruct((3,) + g.shape[1:], g.dtype) for g in grads],
                 [pltpu.SemaphoreType.DMA((3 * n,)), pltpu.SemaphoreType.DMA((3 * n,))], copies)


def _run_comm(comm, name):
    k_in, k_out = len(comm.args), len(comm.out_shape)

    def body(*refs):
        cin, cout, sems = refs[:k_in], refs[k_in:k_in + k_out], refs[k_in + k_out:]
        comm.start(cin, cout, sems)
        comm.wait(cin, cout, sems)

    return pl.pallas_call(body, name=name, in_specs=[HBM] * k_in, out_specs=[HBM] * k_out, out_shape=comm.out_shape,
                          scratch_shapes=comm.sems)(*comm.args)


def _pallas(body, *, name, grid, in_specs, out_specs, out_shape, args, scratch=(), comm=None):
    out_specs, out_shape, scratch = list(out_specs), list(out_shape), list(scratch)
    params = _cp(len(grid))
    if comm is None:
        return pl.pallas_call(body, grid=grid, name=name, in_specs=in_specs, out_specs=out_specs, out_shape=out_shape,
                              scratch_shapes=scratch, compiler_params=params)(*args), []
    n_in, n_out, n_sc = len(args), len(out_shape), len(scratch)
    k_in, k_out = len(comm.args), len(comm.out_shape)

    def carrying(*refs):
        ins, refs = refs[:n_in], refs[n_in:]
        cin, refs = refs[:k_in], refs[k_in:]
        outs, refs = refs[:n_out], refs[n_out:]
        cout, refs = refs[:k_out], refs[k_out:]
        sc, sems = refs[:n_sc], refs[n_sc:]
        first = functools.reduce(jnp.logical_and, [pl.program_id(d) == 0 for d in range(len(grid))])
        last = functools.reduce(jnp.logical_and, [pl.program_id(d) == g - 1 for d, g in enumerate(grid)])

        @pl.when(first)
        def _():
            comm.start(cin, cout, sems)

        body(*ins, *outs, *sc)

        @pl.when(last)
        def _():
            comm.wait(cin, cout, sems)

    res = pl.pallas_call(
        carrying, grid=grid, name=name, in_specs=list(in_specs) + [HBM] * k_in, out_specs=out_specs + [HBM] * k_out,
        out_shape=out_shape + comm.out_shape, scratch_shapes=scratch + comm.sems, compiler_params=params,
    )(*args, *comm.args)
    return res[:n_out], res[n_out:]


def _pool_normed_ext(h_ref, halo_ref, g, tm):
    xe = jnp.concatenate([halo_ref[...], h_ref[...]], axis=0)
    hn = _rms_fwd(xe, g)
    row = lax.broadcasted_iota(jnp.int32, (tm + HALO, 1), 0)
    return jnp.where((row >= HALO) | (pl.program_id(0) > 0), hn, 0.0)


def _pool_windows(hn_e, tm):
    t = lax.broadcasted_iota(jnp.int32, (tm, PG), 0) + pl.program_id(0) * tm
    outs = []
    for g, w in enumerate(POOL_WINDOWS):
        s = hn_e[:, g * PG:(g + 1) * PG]
        x = s[HALO:, :]
        k = 1
        while k < w:
            s = s + pltpu.roll(s, k, axis=0)
            k *= 2
        cnt = jnp.minimum(t + 1, w).astype(F32)
        outs.append(s[HALO:, :] / cnt - x)
    return jnp.concatenate(outs, axis=1)


def _pool_apply(pooled, w_ref):
    return jnp.concatenate([_dot(pooled[:, g * PG:(g + 1) * PG], w_ref[g]) for g in range(len(POOL_WINDOWS))], axis=1)


def _halo_prev(tm):
    return pl.BlockSpec((HALO, D), lambda i: (jnp.maximum(i * (tm // HALO) - 1, 0), 0))


def _pool_fwd(h, g_pre, pw, pscale, g_post, l, comm=None):
    T = h.shape[0]
    tm = min(TM, T)

    def body(h_ref, halo_ref, gpre_ref, w_ref, sc_ref, gpost_ref, o_ref):
        hn_e = _pool_normed_ext(h_ref, halo_ref, gpre_ref[l:l + 1, :], tm)
        pooled = _pool_windows(hn_e, tm).astype(CDT)
        m = _pool_apply(pooled, w_ref) * sc_ref[l:l + 1, :]
        o_ref[...] = h_ref[...] + _rms_fwd(m, gpost_ref[l:l + 1, :])

    return _pallas(
        body, grid=(T // tm,), name=f"pool_fwd_{l}",
        in_specs=[_tile(tm, D), _halo_prev(tm), _res(g_pre.shape),
                  pl.BlockSpec((None,) + pw.shape[1:], lambda i: (l, 0, 0, 0), pipeline_mode=pl.Buffered(1)),
                  _res(pscale.shape), _res(g_post.shape)],
        out_specs=[_tile(tm, D)], out_shape=[jax.ShapeDtypeStruct((T, D), F32)],
        args=(h, h, g_pre, pw, pscale, g_post), comm=comm)


def _pool_bwd_a(dh1, h, g_pre, pw, pscale, g_post, l, comm=None):
    T = h.shape[0]
    tm = min(TM, T)
    ng = len(POOL_WINDOWS)

    def body(dh_ref, h_ref, halo_ref, gpre_ref, w_ref, sc_ref, gpost_ref, dp_ref, dw_ref, dsc_ref, dgpost_ref):
        _zero_first(dw_ref, dsc_ref, dgpost_ref)
        hn_e = _pool_normed_ext(h_ref, halo_ref, gpre_ref[l:l + 1, :], tm)
        pooled = _pool_windows(hn_e, tm).astype(CDT)
        y = _pool_apply(pooled, w_ref)
        sc = sc_ref[l:l + 1, :]
        dm, dg = _rms_bwd(dh_ref[...], y * sc, gpost_ref[l:l + 1, :])
        dgpost_ref[...] += dg
        dsc_ref[...] += jnp.sum(dm * y, axis=0, keepdims=True)
        dy = (dm * sc).astype(CDT)
        dps = []
        for g in range(ng):
            dyg = dy[:, g * PG:(g + 1) * PG]
            dw_ref[g] += _dot_tn(pooled[:, g * PG:(g + 1) * PG], dyg)
            dps.append(_dot_nt(dyg, w_ref[g]))
        dp_ref[...] = jnp.concatenate(dps, axis=1)

    return _pallas(
        body, grid=(T // tm,), name=f"pool_bwd_a_{l}",
        in_specs=[_tile(tm, D), _tile(tm, D), _halo_prev(tm), _res(g_pre.shape),
                  pl.BlockSpec((None,) + pw.shape[1:], lambda i: (l, 0, 0, 0), pipeline_mode=pl.Buffered(1)),
                  _res(pscale.shape), _res(g_post.shape)],
        out_specs=[_tile(tm, D), pl.BlockSpec((ng, PG, PG), lambda i: (0, 0, 0)), _acc(D), _acc(D)],
        out_shape=[jax.ShapeDtypeStruct((T, D), F32), jax.ShapeDtypeStruct((ng, PG, PG), F32),
                   jax.ShapeDtypeStruct((1, D), F32), jax.ShapeDtypeStruct((1, D), F32)],
        args=(dh1, h, h, g_pre, pw, pscale, g_post), comm=comm)


def _pool_bwd_b(dpool, dh1, h, g_pre, l, comm=None):
    T = h.shape[0]
    tm = min(TM, T)
    nt = T // tm

    def body(dp_ref, nxt_ref, dh_ref, h_ref, gpre_ref, o_ref, dgpre_ref):
        _zero_first(dgpre_ref)
        i = pl.program_id(0)
        dp = dp_ref[...]
        e_e = jnp.concatenate([dp, jnp.where(i < nt - 1, nxt_ref[...], 0.0)], axis=0)
        t = lax.broadcasted_iota(jnp.int32, (tm + HALO, PG), 0) + i * tm
        outs = []
        for g, w in enumerate(POOL_WINDOWS):
            s = e_e[:, g * PG:(g + 1) * PG] / jnp.minimum(t + 1, w).astype(F32)
            k = 1
            while k < w:
                s = s + pltpu.roll(s, tm + HALO - k, axis=0)
                k *= 2
            outs.append(s[:tm, :] - dp[:, g * PG:(g + 1) * PG])
        dx, dg = _rms_bwd(jnp.concatenate(outs, axis=1), h_ref[...], gpre_ref[l:l + 1, :])
        dgpre_ref[...] += dg
        o_ref[...] = dh_ref[...] + dx

    nxt = pl.BlockSpec((HALO, D), lambda i: (jnp.minimum((i + 1) * (tm // HALO), T // HALO - 1), 0))
    return _pallas(
        body, grid=(nt,), name=f"pool_bwd_b_{l}",
        in_specs=[_tile(tm, D), nxt, _tile(tm, D), _tile(tm, D), _res(g_pre.shape)],
        out_specs=[_tile(tm, D), _acc(D)],
        out_shape=[jax.ShapeDtypeStruct((T, D), F32), jax.ShapeDtypeStruct((1, D), F32)],
        args=(dpool, dpool, dh1, h, g_pre), comm=comm)


def _proj_rows(x, w_ref):
    k = w_ref.shape[1]
    out = _dot(x[:, :k], w_ref[0])
    for s in range(1, NSH):
        out = out + _dot(x[:, s * k:(s + 1) * k], w_ref[s])
    return out


def _proj_rows_t(dy, w_ref):
    return jnp.concatenate([_dot_nt(dy, w_ref[s]) for s in range(NSH)], axis=1)


def _rope_tiles(cos_ref, sin_ref, width):
    reps = width // cos_ref.shape[1]
    return jnp.tile(cos_ref[...], (1, reps)), jnp.tile(sin_ref[...], (1, reps))


def _q_fwd(h, g_pre, wq, cos, sin, l, j):
    T = h.shape[0]
    tm = min(TM, T)

    def body(h_ref, g_ref, w_ref, cos_ref, sin_ref, q_ref):
        hn = _rms_fwd(h_ref[...], g_ref[l:l + 1, :]).astype(CDT)
        c, s = _rope_tiles(cos_ref, sin_ref, D)
        q_ref[...] = (_rope_fwd(_proj_rows(hn, w_ref), c, s) * QSCALE).astype(CDT)

    return pl.pallas_call(
        body, grid=(T // tm,), name=f"q_fwd_{l}",
        in_specs=[_tile(tm, D), _res(g_pre.shape), _layer_w((D // NSH, D)), _tile(tm, 2 * HD), _tile(tm, 2 * HD)],
        out_specs=_tile(tm, D), out_shape=jax.ShapeDtypeStruct((T, D), CDT), compiler_params=_cp(),
    )(h, g_pre, wq, cos, sin)


def _q_bwd(dq, dh1, h, g_pre, wq, cos, sin, l, j):
    T = h.shape[0]
    tm = min(TM, T)

    def body(dq_ref, dh_ref, h_ref, g_ref, w_ref, cos_ref, sin_ref, o_ref, hn_ref, dq0_ref, dg_ref):
        _zero_first(dg_ref)
        g = g_ref[l:l + 1, :]
        x = h_ref[...]
        hn_ref[...] = _rms_fwd(x, g).astype(CDT)
        c, s = _rope_tiles(cos_ref, sin_ref, D)
        dq0 = _rope_bwd(dq_ref[...].astype(F32) * QSCALE, c, s).astype(CDT)
        dq0_ref[...] = dq0
        dx, dg = _rms_bwd(_proj_rows_t(dq0, w_ref), x, g)
        dg_ref[...] += dg
        o_ref[...] = dh_ref[...] + dx

    return pl.pallas_call(
        body, grid=(T // tm,), name=f"q_bwd_{l}",
        in_specs=[_tile(tm, D), _tile(tm, D), _tile(tm, D), _res(g_pre.shape), _layer_w((D // NSH, D)),
                  _tile(tm, 2 * HD), _tile(tm, 2 * HD)],
        out_specs=[_tile(tm, D), _tile(tm, D), _tile(tm, D), _acc(D)],
        out_shape=[jax.ShapeDtypeStruct((T, D), F32), jax.ShapeDtypeStruct((T, D), CDT),
                   jax.ShapeDtypeStruct((T, D), CDT), jax.ShapeDtypeStruct((1, D), F32)],
        compiler_params=_cp(),
    )(dq, dh1, h, g_pre, wq, cos, sin)


def _o_fwd(o, h, wo, g_post, l, j):
    T = h.shape[0]
    tm = min(TM, T)

    def body(o_ref, h_ref, w_ref, g_ref, h1_ref, m_ref):
        m = _proj_rows(o_ref[...], w_ref)
        m_ref[...] = m.astype(CDT)
        h1_ref[...] = h_ref[...] + _rms_fwd(m, g_ref[l:l + 1, :])

    return pl.pallas_call(
        body, grid=(T // tm,), name=f"o_fwd_{l}",
        in_specs=[_tile(tm, D), _tile(tm, D), _layer_w((D // NSH, D)), _res(g_post.shape)],
        out_specs=[_tile(tm, D), _tile(tm, D)],
        out_shape=[jax.ShapeDtypeStruct((T, D), F32), jax.ShapeDtypeStruct((T, D), CDT)], compiler_params=_cp(),
    )(o, h, wo, g_post)


def _o_bwd(dh1, m, wo, g_post, l, j):
    T = m.shape[0]
    tm = min(TM, T)

    def body(dh_ref, m_ref, w_ref, g_ref, do_ref, dm_ref, dg_ref):
        _zero_first(dg_ref)
        dm, dg = _rms_bwd(dh_ref[...], m_ref[...].astype(F32), g_ref[l:l + 1, :])
        dg_ref[...] += dg
        dmc = dm.astype(CDT)
        dm_ref[...] = dmc
        do_ref[...] = _proj_rows_t(dmc, w_ref).astype(CDT)

    return pl.pallas_call(
        body, grid=(T // tm,), name=f"o_bwd_{l}",
        in_specs=[_tile(tm, D), _tile(tm, D), _layer_w((D // NSH, D)), _res(g_post.shape)],
        out_specs=[_tile(tm, D), _tile(tm, D), _acc(D)],
        out_shape=[jax.ShapeDtypeStruct((T, D), CDT), jax.ShapeDtypeStruct((T, D), CDT),
                   jax.ShapeDtypeStruct((1, D), F32)],
        compiler_params=_cp(),
    )(dh1, m, wo, g_post)


KVW = 2 * NKV * HD


def _kv_fwd(h, g_kv, wk2, wv2, cos, sin):
    T = h.shape[0]
    tm = min(TM, T)

    def body(h_ref, g_ref, wk_ref, wv_ref, cos_ref, sin_ref, k_ref, v_ref):
        hk = _rms_fwd(h_ref[...], g_ref[...]).astype(CDT)
        c, s = _rope_tiles(cos_ref, sin_ref, KVW)
        k_ref[...] = _rope_fwd(_dot(hk, wk_ref[...]), c, s).astype(CDT)
        v_ref[...] = _dot(hk, wv_ref[...]).astype(CDT)

    return pl.pallas_call(
        body, grid=(T // tm,), name="kv_fwd",
        in_specs=[_tile(tm, D), _res(g_kv.shape), _res(wk2.shape), _res(wv2.shape), _tile(tm, 2 * HD), _tile(tm, 2 * HD)],
        out_specs=[_tile(tm, KVW), _tile(tm, KVW)],
        out_shape=[jax.ShapeDtypeStruct((T, KVW), CDT)] * 2, compiler_params=_cp(),
    )(h, g_kv, wk2, wv2, cos, sin)


def _kv_bwd(dh, h, dks, dkhs, dvs, dvhs, g_kv, wk2, wv2, cos, sin):
    T = h.shape[0]
    tq = min(TQ, T)
    nt = T // tq
    n = len(dks)

    def body(*refs):
        dh_ref, h_ref = refs[:2]
        main = refs[2:2 + 2 * n]
        halo = refs[2 + 2 * n:2 + 4 * n]
        g_ref, wk_ref, wv_ref, cos_ref, sin_ref, o_ref, hk_ref, dk0_ref, dv_ref, dg_ref = refs[2 + 4 * n:]
        _zero_first(dg_ref)
        i = pl.program_id(0)

        def total(mains, halos):
            d = mains[0][...]
            for r in mains[1:]:
                d = d + r[...]
            hl = halos[0][...]
            for r in halos[1:]:
                hl = hl + r[...]
            row = lax.broadcasted_iota(jnp.int32, (tq, 1), 0)
            if tq > WIN:
                hl = jnp.concatenate([jnp.zeros((tq - WIN, KVW), F32), hl], axis=0)
            return d + jnp.where((row >= tq - WIN) & (i < nt - 1), hl, 0.0)

        dk = total(main[:n], halo[:n])
        dv = total(main[n:], halo[n:])
        x = h_ref[...]
        g = g_ref[...]
        hk_ref[...] = _rms_fwd(x, g).astype(CDT)
        c, s = _rope_tiles(cos_ref, sin_ref, KVW)
        dk0 = _rope_bwd(dk, c, s).astype(CDT)
        dvc = dv.astype(CDT)
        dk0_ref[...] = dk0
        dv_ref[...] = dvc
        dx, dg = _rms_bwd(_dot_nt(dk0, wk_ref[...]) + _dot_nt(dvc, wv_ref[...]), x, g)
        dg_ref[...] += dg
        o_ref[...] = dh_ref[...] + dx

    nxt = pl.BlockSpec((WIN, KVW), lambda i: (jnp.minimum(i + 1, nt - 1), 0))
    return pl.pallas_call(
        body, grid=(nt,), name="kv_bwd",
        in_specs=[_tile(tq, D), _tile(tq, D)] + [_tile(tq, KVW)] * (2 * n) + [nxt] * (2 * n)
        + [_res(g_kv.shape), _res(wk2.shape), _res(wv2.shape), _tile(tq, 2 * HD), _tile(tq, 2 * HD)],
        out_specs=[_tile(tq, D), _tile(tq, D), _tile(tq, KVW), _tile(tq, KVW), _acc(D)],
        out_shape=[jax.ShapeDtypeStruct((T, D), F32), jax.ShapeDtypeStruct((T, D), CDT),
                   jax.ShapeDtypeStruct((T, KVW), CDT), jax.ShapeDtypeStruct((T, KVW), CDT),
                   jax.ShapeDtypeStruct((1, D), F32)],
        compiler_params=_cp(),
    )(dh, h, *dks, *dvs, *dkhs, *dvhs, g_kv, wk2, wv2, cos, sin)


GQ = 4
LW = 2 * HD


def _from_prev():
    r = lax.broadcasted_iota(jnp.int32, (GQ * WIN, WIN), 0) % WIN
    j = lax.broadcasted_iota(jnp.int32, (GQ * WIN, WIN), 1)
    return j > r


def _fold(x2, prev):
    return jnp.where(prev, x2[:, :WIN], x2[:, WIN:])


def _unfold(x, prev):
    zero = jnp.zeros_like(x)
    return jnp.concatenate([jnp.where(prev, x, zero), jnp.where(prev, zero, x)], axis=1)


def _stack_heads(ref, rows, g):
    lane = lax.broadcasted_iota(jnp.int32, (WIN, LW), 1)
    parts = []
    for pr in range(2):
        x = ref[rows, (2 * g + pr) * LW:(2 * g + pr + 1) * LW]
        parts += [jnp.where(lane < HD, x, jnp.zeros_like(x)), jnp.where(lane >= HD, x, jnp.zeros_like(x))]
    return jnp.concatenate(parts, axis=0)


def _unstack_heads(x4):
    lane = lax.broadcasted_iota(jnp.int32, (WIN, LW), 1)
    return [jnp.where(lane < HD, x4[(2 * pr) * WIN:(2 * pr + 1) * WIN], x4[(2 * pr + 1) * WIN:(2 * pr + 2) * WIN])
            for pr in range(2)]


def _sink_col(sink_ref, j, g):
    blk = lax.broadcasted_iota(jnp.int32, (GQ * WIN, 1), 0) // WIN
    col = jnp.zeros((GQ * WIN, 1), F32)
    for a in range(GQ):
        col = jnp.where(blk == a, sink_ref[j, GQ * g + a], col)
    return col


def _softmax_sink(s, hidden, sink):
    if hidden is not None:
        s = jnp.where(hidden, NEG, s)
    m = jnp.maximum(jnp.max(s, axis=-1, keepdims=True), sink)
    e = jnp.exp(s - m)
    es = jnp.exp(sink - m)
    l = jnp.sum(e, axis=-1, keepdims=True) + es
    return e / l, es / l


def _kv_halo(tq):
    return pl.BlockSpec((WIN, KVW), lambda i: (jnp.maximum(i * (tq // WIN) - 1, 0), 0))


def _attn_fwd(q, kk, vv, sinks, l, j):
    T = q.shape[0]
    tq = min(TQ, T)

    def body(sink_ref, q_ref, k_ref, kh_ref, v_ref, vh_ref, o_ref):
        i = pl.program_id(0)
        ke = jnp.concatenate([kh_ref[...], k_ref[...]], axis=0)
        ve = jnp.concatenate([vh_ref[...], v_ref[...]], axis=0)
        prev = _from_prev()
        for n in range(tq // WIN):
            rows = slice(n * WIN, (n + 1) * WIN)
            hidden = prev & (i == 0) if n == 0 else None
            for g in range(NKV):
                kg = ke[n * WIN:(n + 2) * WIN, g * LW:(g + 1) * LW]
                vg = ve[n * WIN:(n + 2) * WIN, g * LW:(g + 1) * LW]
                s = _fold(_dot_nt(_stack_heads(q_ref, rows, g), kg), prev)
                p, _ = _softmax_sink(s, hidden, _sink_col(sink_ref, j, g))
                pairs = _unstack_heads(_dot(_unfold(p.astype(CDT), prev), vg))
                for pr in range(2):
                    o_ref[rows, (2 * g + pr) * LW:(2 * g + pr + 1) * LW] = pairs[pr].astype(CDT)

    return pl.pallas_call(
        body, grid=(T // tq,), name=f"attn_fwd_{l}",
        in_specs=[pl.BlockSpec(memory_space=pltpu.SMEM), _tile(tq, D), _tile(tq, KVW), _kv_halo(tq),
                  _tile(tq, KVW), _kv_halo(tq)],
        out_specs=_tile(tq, D), out_shape=jax.ShapeDtypeStruct((T, D), CDT), compiler_params=_cp(),
    )(sinks, q, kk, kk, vv, vv)


def _attn_bwd(q, kk, vv, o, do, sinks, l, j):
    T = q.shape[0]
    tq = min(TQ, T)
    nt = T // tq

    def body(sink_ref, q_ref, k_ref, kh_ref, v_ref, vh_ref, o_ref, do_ref,
             dq_ref, dk_ref, dkh_ref, dv_ref, dvh_ref, dsink_ref, dke, dve):
        _zero_first(dsink_ref)
        i = pl.program_id(0)
        ke = jnp.concatenate([kh_ref[...], k_ref[...]], axis=0)
        ve = jnp.concatenate([vh_ref[...], v_ref[...]], axis=0)
        dke[...] = jnp.zeros_like(dke)
        dve[...] = jnp.zeros_like(dve)
        lane = lax.broadcasted_iota(jnp.int32, (1, LW), 1)
        blk = lax.broadcasted_iota(jnp.int32, (GQ * WIN, 1), 0) // WIN
        dsink = jnp.zeros((1, LW), F32)
        prev = _from_prev()
        for n in range(tq // WIN):
            rows = slice(n * WIN, (n + 1) * WIN)
            krows = slice(n * WIN, (n + 2) * WIN)
            hidden = prev & (i == 0) if n == 0 else None
            for g in range(NKV):
                cols = slice(g * LW, (g + 1) * LW)
                kg = ke[krows, cols]
                vg = ve[krows, cols]
                q4 = _stack_heads(q_ref, rows, g)
                do4 = _stack_heads(do_ref, rows, g)
                o4 = _stack_heads(o_ref, rows, g)
                p, ps = _softmax_sink(_fold(_dot_nt(q4, kg), prev), hidden, _sink_col(sink_ref, j, g))
                delta = jnp.sum(do4.astype(F32) * o4.astype(F32), axis=-1, keepdims=True)
                ds = _unfold((p * (_fold(_dot_nt(do4, vg), prev) - delta)).astype(CDT), prev)
                pc = _unfold(p.astype(CDT), prev)
                pairs = _unstack_heads(_dot(ds, kg))
                for pr in range(2):
                    dq_ref[rows, (2 * g + pr) * LW:(2 * g + pr + 1) * LW] = pairs[pr].astype(CDT)
                dke[krows, cols] += _dot_tn(ds, q4)
                dve[krows, cols] += _dot_tn(pc, do4)
                t = ps * delta
                for a in range(GQ):
                    dsink = dsink - jnp.where(lane == GQ * g + a, jnp.sum(jnp.where(blk == a, t, 0.0)), 0.0)
        dsink_ref[...] += dsink
        dkh_ref[...] = dke[:WIN, :]
        dk_ref[...] = dke[WIN:, :]
        dvh_ref[...] = dve[:WIN, :]
        dv_ref[...] = dve[WIN:, :]

    halo_out = pl.BlockSpec((WIN, KVW), lambda i: (i, 0))
    return pl.pallas_call(
        body, grid=(nt,), name=f"attn_bwd_{l}",
        in_specs=[pl.BlockSpec(memory_space=pltpu.SMEM), _tile(tq, D), _tile(tq, KVW), _kv_halo(tq),
                  _tile(tq, KVW), _kv_halo(tq), _tile(tq, D), _tile(tq, D)],
        out_specs=[_tile(tq, D), _tile(tq, KVW), halo_out, _tile(tq, KVW), halo_out, _acc(LW)],
        out_shape=[jax.ShapeDtypeStruct((T, D), CDT), jax.ShapeDtypeStruct((T, KVW), F32),
                   jax.ShapeDtypeStruct((nt * WIN, KVW), F32), jax.ShapeDtypeStruct((T, KVW), F32),
                   jax.ShapeDtypeStruct((nt * WIN, KVW), F32), jax.ShapeDtypeStruct((1, LW), F32)],
        scratch_shapes=[pltpu.VMEM((WIN + tq, KVW), F32), pltpu.VMEM((WIN + tq, KVW), F32)],
        compiler_params=_cp(),
    )(sinks, q, kk, kk, vv, vv, o, do)


def _chunks(tm):
    return pl.BlockSpec((NSH, tm, FS), lambda i: (0, i, 0))


def _ffn_fwd(h1, g_pre, wg, wu, wd, g_post, l, comm=None):
    T = h1.shape[0]
    tm = min(TM, T)

    def body(h_ref, gpre_ref, wg_ref, wu_ref, wd_ref, gpost_ref, h2_ref, a_ref, b_ref, f_ref):
        x = h_ref[...]
        fn = _rms_fwd(x, gpre_ref[l:l + 1, :]).astype(CDT)
        f = jnp.zeros((tm, D), F32)
        for s in range(NSH):
            a = _dot_nt(fn, wg_ref[s])
            b = _dot_nt(fn, wu_ref[s])
            a_ref[s] = a.astype(CDT)
            b_ref[s] = b.astype(CDT)
            f = f + _dot((a * _sigmoid(a) * b).astype(CDT), wd_ref[s])
        f_ref[...] = f.astype(CDT)
        h2_ref[...] = x + _rms_fwd(f, gpost_ref[l:l + 1, :])

    return _pallas(
        body, grid=(T // tm,), name=f"ffn_fwd_{l}",
        in_specs=[_tile(tm, D), _res(g_pre.shape), _layer_w((FS, D)), _layer_w((FS, D)), _layer_w((FS, D)),
                  _res(g_post.shape)],
        out_specs=[_tile(tm, D), _chunks(tm), _chunks(tm), _tile(tm, D)],
        out_shape=[jax.ShapeDtypeStruct((T, D), F32), jax.ShapeDtypeStruct((NSH, T, FS), CDT),
                   jax.ShapeDtypeStruct((NSH, T, FS), CDT), jax.ShapeDtypeStruct((T, D), CDT)],
        args=(h1, g_pre, wg, wu, wd, g_post), comm=comm)


def _ffn_bwd(dh2, h1, f, a, b, g_pre, wg, wu, wd, g_post, l, comm=None):
    T = h1.shape[0]
    tm = min(TM_FFN_BWD, T)

    def body(dh_ref, h_ref, f_ref, a_ref, b_ref, gpre_ref, wg_ref, wu_ref, wd_ref, gpost_ref,
             dh1_ref, fn_ref, df_ref, act_ref, da_ref, db_ref, dgpre_ref, dgpost_ref):
        _zero_first(dgpre_ref, dgpost_ref)
        x = h_ref[...]
        gpre = gpre_ref[l:l + 1, :]
        fn_ref[...] = _rms_fwd(x, gpre).astype(CDT)
        dh2 = dh_ref[...]
        df, dg = _rms_bwd(dh2, f_ref[...].astype(F32), gpost_ref[l:l + 1, :])
        dgpost_ref[...] += dg
        dfc = df.astype(CDT)
        df_ref[...] = dfc
        dfn = jnp.zeros((tm, D), F32)
        for s in range(NSH):
            av = a_ref[s].astype(F32)
            bv = b_ref[s].astype(F32)
            sg = _sigmoid(av)
            silu = av * sg
            act_ref[s] = (silu * bv).astype(CDT)
            dact = _dot_nt(dfc, wd_ref[s])
            da = (dact * bv * (sg * (1.0 + av * (1.0 - sg)))).astype(CDT)
            db = (dact * silu).astype(CDT)
            da_ref[s] = da
            db_ref[s] = db
            dfn = dfn + _dot(da, wg_ref[s]) + _dot(db, wu_ref[s])
        dx, dg = _rms_bwd(dfn, x, gpre)
        dgpre_ref[...] += dg
        dh1_ref[...] = dh2 + dx

    return _pallas(
        body, grid=(T // tm,), name=f"ffn_bwd_{l}",
        in_specs=[_tile(tm, D), _tile(tm, D), _tile(tm, D), _chunks(tm), _chunks(tm), _res(g_pre.shape),
                  _layer_w((FS, D)), _layer_w((FS, D)), _layer_w((FS, D)), _res(g_post.shape)],
        out_specs=[_tile(tm, D), _tile(tm, D), _tile(tm, D), _chunks(tm), _chunks(tm), _chunks(tm), _acc(D), _acc(D)],
        out_shape=[jax.ShapeDtypeStruct((T, D), F32), jax.ShapeDtypeStruct((T, D), CDT), jax.ShapeDtypeStruct((T, D), CDT),
                   jax.ShapeDtypeStruct((NSH, T, FS), CDT), jax.ShapeDtypeStruct((NSH, T, FS), CDT),
                   jax.ShapeDtypeStruct((NSH, T, FS), CDT), jax.ShapeDtypeStruct((1, D), F32),
                   jax.ShapeDtypeStruct((1, D), F32)],
        args=(dh2, h1, f, a, b, g_pre, wg, wu, wd, g_post), comm=comm)


def _ple_proj(p, wpp_ref):
    return jnp.concatenate([_dot(p, wpp_ref[s]) for s in range(NSH)], axis=1)


def _ple_fwd(h2, p, g_ple, wpg, wpp, l, comm=None):
    T = h2.shape[0]
    tm = min(TM, T)

    def body(h_ref, p_ref, g_ref, wpg_ref, wpp_ref, o_ref):
        x = h_ref[...]
        pn = _rms_fwd(x, g_ref[l:l + 1, :]).astype(CDT)
        gate = _sigmoid(_proj_rows(pn, wpg_ref))
        o_ref[...] = x + _ple_proj(p_ref[...].astype(CDT), wpp_ref) * gate

    return _pallas(
        body, grid=(T // tm,), name=f"ple_fwd_{l}",
        in_specs=[_tile(tm, D), pl.BlockSpec((None, tm, PLE), lambda i: (l, i, 0)), _res(g_ple.shape),
                  _layer_w((D // NSH, D)), _layer_w((PLE, D // NSH))],
        out_specs=[_tile(tm, D)], out_shape=[jax.ShapeDtypeStruct((T, D), F32)],
        args=(h2, p, g_ple, wpg, wpp), comm=comm)


def _ple_bwd(dh3, h2, p, g_ple, wpg, wpp, l):
    T = h2.shape[0]
    tm = min(TM, T)

    def body(dh_ref, h_ref, p_ref, g_ref, wpg_ref, wpp_ref, o_ref, pn_ref, dz_ref, dpp_ref, pc_ref, dg_ref):
        _zero_first(dg_ref)
        x = h_ref[...]
        g = g_ref[l:l + 1, :]
        pn = _rms_fwd(x, g).astype(CDT)
        pn_ref[...] = pn
        gate = _sigmoid(_proj_rows(pn, wpg_ref))
        pc = p_ref[...].astype(CDT)
        pc_ref[...] = pc
        pp = _ple_proj(pc, wpp_ref)
        dh3 = dh_ref[...]
        dpp_ref[...] = (dh3 * gate).astype(CDT)
        dz = (dh3 * pp * gate * (1.0 - gate)).astype(CDT)
        dz_ref[...] = dz
        dx, dg = _rms_bwd(_proj_rows_t(dz, wpg_ref), x, g)
        dg_ref[...] += dg
        o_ref[...] = dh3 + dx

    return pl.pallas_call(
        body, grid=(T // tm,), name=f"ple_bwd_{l}",
        in_specs=[_tile(tm, D), _tile(tm, D), pl.BlockSpec((None, tm, PLE), lambda i: (l, i, 0)), _res(g_ple.shape),
                  _layer_w((D // NSH, D)), _layer_w((PLE, D // NSH))],
        out_specs=[_tile(tm, D), _tile(tm, D), _tile(tm, D), _tile(tm, D), _tile(tm, PLE), _acc(D)],
        out_shape=[jax.ShapeDtypeStruct((T, D), F32), jax.ShapeDtypeStruct((T, D), CDT), jax.ShapeDtypeStruct((T, D), CDT),
                   jax.ShapeDtypeStruct((T, D), CDT), jax.ShapeDtypeStruct((T, PLE), CDT), jax.ShapeDtypeStruct((1, D), F32)],
        compiler_params=_cp(),
    )(dh3, h2, p, g_ple, wpg, wpp)


def _loss_grad(y, tgt):
    T = y.shape[0]
    tm = min(TM, T)

    def body(y_ref, t_ref, dy_ref, loss_ref):
        _zero_first(loss_ref)
        err = y_ref[...] - t_ref[...]
        dy_ref[...] = err * (1.0 / D)
        lane = lax.broadcasted_iota(jnp.int32, (1, 128), 1)
        loss_ref[...] += jnp.where(lane == 0, (0.5 / D) * jnp.sum(err * err), 0.0)

    return pl.pallas_call(
        body, grid=(T // tm,), name="loss_grad", in_specs=[_tile(tm, D), _tile(tm, D)],
        out_specs=[_tile(tm, D), _acc(128)],
        out_shape=[jax.ShapeDtypeStruct((T, D), F32), jax.ShapeDtypeStruct((1, 128), F32)], compiler_params=_cp(),
    )(y, tgt)


def _mm_tn(x, dy, name, n_split=1, comm=None):
    xb, yb = x.ndim == 3, dy.ndim == 3
    T, K = x.shape[-2:]
    N = dy.shape[-1] // n_split
    B = x.shape[0] if xb else dy.shape[0] if yb else n_split
    tt = min(TT, T)
    nt = T // tt

    def body(x_ref, dy_ref, o_ref, oc_ref):
        t = pl.program_id(1)

        @pl.when(t == 0)
        def _():
            o_ref[...] = jnp.zeros_like(o_ref)

        o_ref[...] += _dot_tn(x_ref[...].astype(CDT), dy_ref[...])

        @pl.when(t == nt - 1)
        def _():
            oc_ref[...] = o_ref[...].astype(CDT)

    x_spec = pl.BlockSpec((None, tt, K), lambda b, t: (b, t, 0)) if xb else pl.BlockSpec((tt, K), lambda b, t: (t, 0))
    if yb:
        y_spec = pl.BlockSpec((None, tt, N), lambda b, t: (b, t, 0))
    else:
        y_spec = pl.BlockSpec((tt, N), lambda b, t: (t, b if n_split > 1 else 0))
    o_spec = pl.BlockSpec((None, K, N), lambda b, t: (b, 0, 0))
    res, got = _pallas(
        body, grid=(B, nt), name=name, in_specs=[x_spec, y_spec], out_specs=[o_spec, o_spec],
        out_shape=[jax.ShapeDtypeStruct((B, K, N), F32), jax.ShapeDtypeStruct((B, K, N), CDT)], args=(x, dy), comm=comm)
    return res if comm is None else (res, got)


def _fold_dup(dw2, name):
    def body(x_ref, o_ref, oc_ref):
        x = x_ref[...]
        y = jnp.concatenate([x[:, g * LW:g * LW + HD] + x[:, g * LW + HD:(g + 1) * LW] for g in range(NKV)], axis=1)
        o_ref[...] = y
        oc_ref[...] = y.astype(CDT)

    return pl.pallas_call(
        body, grid=(NSH,), name=name, in_specs=[_tile(D // NSH, KVW)], out_specs=[_tile(D // NSH, NKV * HD)] * 2,
        out_shape=[jax.ShapeDtypeStruct((D, NKV * HD), F32), jax.ShapeDtypeStruct((D, NKV * HD), CDT)],
        compiler_params=_cp(),
    )(dw2)


def _swap_with_sibling(parts):
    n = len(parts)

    def body(*refs):
        ins, outs = refs[:n], refs[n:2 * n]
        send, recv = refs[2 * n:]
        x, y, c, _ = _place()
        copies = [pltpu.make_async_remote_copy(
            src_ref=ins[a], dst_ref=outs[a], send_sem=send.at[a], recv_sem=recv.at[a],
            device_id=(x, y, 1 - c), device_id_type=MESH) for a in range(n)]
        for cp in copies:
            cp.start()
        for cp in copies:
            cp.wait_recv()
        for cp in copies:
            cp.wait_send()

    return pl.pallas_call(
        body, name="swap_partials", in_specs=[HBM] * n, out_specs=[HBM] * n,
        out_shape=[jax.ShapeDtypeStruct(p.shape, p.dtype) for p in parts],
        scratch_shapes=[pltpu.SemaphoreType.DMA((n,)), pltpu.SemaphoreType.DMA((n,))],
    )(*parts)


def _all_reduce_small(vec):
    R = vec.shape[0]
    ndev = 8

    def body(v_ref, o_ref, buf, send, recv):
        x, y, c, _ = _place()
        me = 4 * x + 2 * y + c
        buf[me] = v_ref[...]
        copies = []
        for k in range(1, ndev):
            peer = (x if not k & 4 else 1 - x, y if not k & 2 else 1 - y, c if not k & 1 else 1 - c)
            copies.append(pltpu.make_async_remote_copy(
                src_ref=v_ref, dst_ref=buf.at[me], send_sem=send.at[k - 1], recv_sem=recv.at[k - 1],
                device_id=peer, device_id_type=MESH))
            copies[-1].start()
        for k in range(1, ndev):
            peer = (x if not k & 4 else 1 - x, y if not k & 2 else 1 - y, c if not k & 1 else 1 - c)
            pltpu.make_async_remote_copy(
                src_ref=v_ref, dst_ref=buf.at[4 * peer[0] + 2 * peer[1] + peer[2]], send_sem=send.at[k - 1],
                recv_sem=recv.at[k - 1], device_id=peer, device_id_type=MESH).wait_recv()
        for cp in copies:
            cp.wait_send()
        tot = buf[0]
        for d in range(1, ndev):
            tot = tot + buf[d]
        o_ref[...] = tot

    return pl.pallas_call(
        body, name="all_reduce_small", in_specs=[pl.BlockSpec(memory_space=pltpu.VMEM)],
        out_specs=pl.BlockSpec(memory_space=pltpu.VMEM), out_shape=jax.ShapeDtypeStruct(vec.shape, F32),
        scratch_shapes=[pltpu.VMEM((ndev, R, D), F32), pltpu.SemaphoreType.DMA((ndev - 1,)),
                        pltpu.SemaphoreType.DMA((ndev - 1,))],
    )(vec)


def _row_tile(rows):
    for t in (512, 256, 128, 64, 32, 16, 8):
        if rows % t == 0:
            return t
    return rows


def _sum_own_and_received(own, got, me, name):
    _, R, C = own.shape
    tr = _row_tile(R)

    def body(me_ref, own_ref, got_ref, o_ref):
        o_ref[...] = ((own_ref[...] + got_ref[0].astype(F32)) + got_ref[1].astype(F32)) + got_ref[2].astype(F32)

    return pl.pallas_call(
        body, name=name,
        grid_spec=pltpu.PrefetchScalarGridSpec(
            num_scalar_prefetch=1, grid=(R // tr,),
            in_specs=[pl.BlockSpec((None, tr, C), lambda i, me_ref: (me_ref[0], i, 0)),
                      pl.BlockSpec((3, tr, C), lambda i, me_ref: (0, i, 0))],
            out_specs=pl.BlockSpec((tr, C), lambda i, me_ref: (i, 0))),
        out_shape=jax.ShapeDtypeStruct((R, C), F32), compiler_params=_cp(),
    )(me, own, got)


def _adamw(gs, w, m, v, name):
    L, R, C = w.shape
    tr = _row_tile(R)
    ng = len(gs[0])
    c1 = 1.0 - ADAM_B1 ** ADAM_STEP
    c2 = 1.0 - ADAM_B2 ** ADAM_STEP

    def body(*refs):
        w_ref, m_ref, v_ref, g_ref, d_ref, nm_ref, nv_ref = refs[L * ng:]
        lay = pl.program_id(0)
        g = None
        for l in range(L):
            gl = refs[l * ng][...]
            for r in refs[l * ng + 1:(l + 1) * ng]:
                gl = gl + r[...]
            g = gl if g is None else jnp.where(lay == l, gl, g)
        nm = ADAM_B1 * m_ref[...] + (1.0 - ADAM_B1) * g
        nv = ADAM_B2 * v_ref[...] + (1.0 - ADAM_B2) * (g * g)
        g_ref[...] = g
        nm_ref[...] = nm
        nv_ref[...] = nv
        d_ref[...] = -ADAM_LR * ((nm / c1) / (jnp.sqrt(nv / c2) + ADAM_EPS) + ADAM_WD * w_ref[...])

    gspecs = [pl.BlockSpec((tr, C), lambda lay, i, l=l: (jnp.where(lay == l, i, 0), 0)) for l in range(L) for _ in range(ng)]
    spec = pl.BlockSpec((None, tr, C), lambda lay, i: (lay, i, 0))
    return pl.pallas_call(
        body, grid=(L, R // tr), name=name, in_specs=gspecs + [spec] * 3, out_specs=[spec] * 4,
        out_shape=[jax.ShapeDtypeStruct((L, R, C), F32)] * 4, compiler_params=_cp(2),
    )(*[g for gl in gs for g in gl], w, m, v)


def _rope_tables(T):
    inv = 1.0 / (ROPE_THETA ** (jnp.arange(0, HD, 2, dtype=F32) / HD))
    ang = jnp.arange(T, dtype=F32)[:, None] * inv[None, :]
    c, s = jnp.cos(ang), jnp.sin(ang)
    return jnp.tile(jnp.concatenate([c, c], axis=1), (1, 2)), jnp.tile(jnp.concatenate([-s, s], axis=1), (1, 2))


def _dup_cols(w):
    return jnp.broadcast_to(w.reshape(D, NKV, 1, HD), (D, NKV, 2, HD)).reshape(D, KVW)


def kernel(x, p, mix_pre_g, mix_post_g, ffn_pre_g, ffn_post_g, pool_w, pool_scale, kv_norm_g, w_k, w_v, w_q, w_o, sinks, w_ff_gate, w_ff_up, w_ff_down, ple_norm_g, w_ple_gate, w_ple_proj, loss_target, m_mix_pre_g, m_mix_post_g, m_ffn_pre_g, m_ffn_post_g, m_pool_w, m_pool_scale, m_kv_norm_g, m_w_k, m_w_v, m_w_q, m_w_o, m_sinks, m_w_ff_gate, m_w_ff_up, m_w_ff_down, m_ple_norm_g, m_w_ple_gate, m_w_ple_proj, v_mix_pre_g, v_mix_post_g, v_ffn_pre_g, v_ffn_post_g, v_pool_w, v_pool_scale, v_kv_norm_g, v_w_k, v_w_v, v_w_q, v_w_o, v_sinks, v_w_ff_gate, v_w_ff_up, v_w_ff_down, v_ple_norm_g, v_w_ple_gate, v_w_ple_proj):
    depth = mix_pre_g.shape[0]
    n_pool = pool_w.shape[0]
    n_attn = w_q.shape[0]
    T = x.shape[1]
    h = x[0]
    p = p[:, 0]
    tgt = loss_target[0]
    me = (2 * lax.axis_index("x") + lax.axis_index("y")).astype(jnp.int32).reshape(1)

    def t12(a):
        return jnp.transpose(a, (0, 2, 1))

    ff_t = {k: tuple(t12(a) for a in v) for k, v in dict(
        w_ff_gate=(w_ff_gate, m_w_ff_gate, v_w_ff_gate), w_ff_up=(w_ff_up, m_w_ff_up, v_w_ff_up)).items()}
    c_pool, c_wk, c_wv, c_wq, c_wo, c_wg, c_wu, c_wd, c_wpg, c_wpp = (w.astype(CDT) for w in (
        pool_w, w_k, w_v, w_q, w_o, ff_t["w_ff_gate"][0], ff_t["w_ff_up"][0], w_ff_down, w_ple_gate, w_ple_proj))

    def ffn_srcs(l):
        return [(c_wg, l), (c_wu, l), (c_wd, l)]

    def ple_srcs(l):
        return [(c_wpg, l), (c_wpp, l)]

    (gpw,) = _run_comm(_gather_comm([(c_pool, None)]), "gather_pool_w")
    wff, wple, wattn = {}, {}, {}
    pw = jnp.transpose(gpw, (1, 2, 0, 3, 4)).reshape(n_pool, len(POOL_WINDOWS), PG, PG)
    kv_g = kv_norm_g.reshape(1, D)
    cos, sin = _rope_tables(T)

    pscale = _all_reduce_small(jnp.pad(
        lax.dynamic_update_slice(jnp.zeros((n_pool, D), F32), pool_scale, (0, me[0] * (D // NSH))),
        ((0, 8 - n_pool), (0, 0))))[:n_pool] * 0.5

    saved = []
    kk = vv = wk2 = wv2 = None
    for l in range(depth):
        s = {"h": h}
        if l < n_pool:
            (h1,), got = _pool_fwd(h, mix_pre_g, pw, pscale, mix_post_g, l,
                                   comm=_gather_comm(ffn_srcs(0) + ple_srcs(0)) if l == 0 else None)
            if l == 0:
                wff[0], wple[0] = got[:3], got[3:]
        else:
            j = l - n_pool
            wq, wo = wattn[l]
            s["q"] = _q_fwd(h, mix_pre_g, wq, cos, sin, l, j)
            s["o"] = _attn_fwd(s["q"], kk, vv, sinks, l, j)
            h1, s["m"] = _o_fwd(s["o"], h, wo, mix_post_g, l, j)
        s["h1"] = h1
        srcs = []
        if l + 1 < depth:
            srcs = ffn_srcs(l + 1)
            if l + 1 >= n_pool:
                srcs += [(c_wq, l + 1 - n_pool), (c_wo, l + 1 - n_pool)]
            if l + 1 == n_pool:
                srcs += [(c_wk, None), (c_wv, None)]
        (h2, s["a"], s["b"], s["f"]), got = _ffn_fwd(
            h1, ffn_pre_g, *wff[l], ffn_post_g, l, comm=_gather_comm(srcs) if srcs else None)
        if srcs:
            wff[l + 1] = got[:3]
            if l + 1 >= n_pool:
                wattn[l + 1] = got[3:5]
            if l + 1 == n_pool:
                wk2 = _dup_cols(got[5].reshape(D, NKV * HD))
                wv2 = _dup_cols(got[6].reshape(D, NKV * HD))
        s["h2"] = h2
        (h,), got = _ple_fwd(h2, p, ple_norm_g, *wple[l], l,
                             comm=_gather_comm(ple_srcs(l + 1)) if l + 1 < depth else None)
        if l + 1 < depth:
            wple[l + 1] = got
        if l == n_pool - 1:
            s["hkv"] = h
            kk, vv = _kv_fwd(h, kv_g, wk2, wv2, cos, sin)
        saved.append(s)

    dh, loss_row = _loss_grad(h, tgt)

    entries, pending = [], []

    def add_grad(k, l, g32, g16):
        def by_shard(g):
            return g.reshape(NSH, -1, g.shape[-1])
        e = dict(k=k, l=l, own=by_shard(g32), sent=by_shard(g16))
        entries.append(e)
        pending.append(e)

    def carry(n=None):
        take = pending[:n] if n else list(pending)
        del pending[:len(take)]
        return take, (_scatter_comm([e["sent"] for e in take]) if take else None)

    def landed(take, got):
        for e, g in zip(take, got):
            e["got"] = g

    small = {k: [None] * depth for k in ("mix_pre", "mix_post", "ffn_pre", "ffn_post", "ple")}
    dpool_scale = [None] * n_pool
    dsinks = [None] * n_attn
    dks, dkhs, dvs, dvhs = [], [], [], []
    for l in reversed(range(depth)):
        s = saved[l]
        if l == n_pool - 1:
            dh, hk, dk0, dvc, dkv_g = _kv_bwd(dh, s["hkv"], dks, dkhs, dvs, dvhs, kv_g, wk2, wv2, cos, sin)
            add_grad("w_k", None, *_fold_dup(_mm_tn(hk, dk0, "dw_k")[0][0], "fold_w_k"))
            add_grad("w_v", None, *_fold_dup(_mm_tn(hk, dvc, "dw_v")[0][0], "fold_w_v"))
        dh, pn, dz, dpp, pc, small["ple"][l] = _ple_bwd(dh, s["h2"], p, ple_norm_g, *wple[l], l)
        add_grad("w_ple_gate", l, *_mm_tn(pn, dz, f"dw_ple_gate_{l}"))
        add_grad("w_ple_proj", l, *_mm_tn(pc, dpp, f"dw_ple_proj_{l}", n_split=NSH))
        take, comm = carry()
        (dh, fn, df, act, da, db, small["ffn_pre"][l], small["ffn_post"][l]), got = _ffn_bwd(
            dh, s["h1"], s["f"], s["a"], s["b"], ffn_pre_g, *wff[l], ffn_post_g, l, comm=comm)
        landed(take, got)
        add_grad("w_ff_gate", l, *_mm_tn(da, fn, f"dw_ff_gate_{l}"))
        take, comm = carry() if l == 0 else ([], None)
        res = _mm_tn(db, fn, f"dw_ff_up_{l}", comm=comm)
        if comm is not None:
            res, got = res
            landed(take, got)
        add_grad("w_ff_up", l, *res)
        take, comm = carry() if l == 0 else ([], None)
        res = _mm_tn(act, df, f"dw_ff_down_{l}", comm=comm)
        if comm is not None:
            res, got = res
            landed(take, got)
        add_grad("w_ff_down", l, *res)
        if l < n_pool:
            take, comm = carry() if l == 0 else ([], None)
            (dpool, dpw, dpool_scale[l], small["mix_post"][l]), got = _pool_bwd_a(
                dh, s["h"], mix_pre_g, pw, pscale, mix_post_g, l, comm=comm)
            landed(take, got)
            dpw = jnp.transpose(dpw.reshape(len(POOL_WINDOWS), NSH, PG // NSH, PG), (1, 0, 2, 3))
            dpw = dpw.reshape(NSH, len(POOL_WINDOWS) * PG // NSH, PG)
            add_grad("pool_w", l, dpw, dpw.astype(CDT))
            (dh, small["mix_pre"][l]), _ = _pool_bwd_b(dpool, dh, s["h"], mix_pre_g, l)
        else:
            j = l - n_pool
            wq, wo = wattn[l]
            do, dm, small["mix_post"][l] = _o_bwd(dh, s["m"], wo, mix_post_g, l, j)
            add_grad("w_o", j, *_mm_tn(s["o"], dm, f"dw_o_{l}"))
            dq, dk, dkh, dv, dvh, dsinks[j] = _attn_bwd(s["q"], kk, vv, s["o"], do, sinks, l, j)
            dks.append(dk); dkhs.append(dkh); dvs.append(dv); dvhs.append(dvh)
            dh, hn, dq0, small["mix_pre"][l] = _q_bwd(dq, dh, s["h"], mix_pre_g, wq, cos, sin, l, j)
            add_grad("w_q", j, *_mm_tn(hn, dq0, f"dw_q_{l}"))
    if pending:
        take, comm = carry()
        landed(take, _run_comm(comm, "scatter_rest"))
    grad_x = dh[None]

    rows = [jnp.concatenate(small[k], axis=0) for k in ("mix_pre", "mix_post", "ffn_pre", "ffn_post", "ple")]
    rows += [dkv_g, jnp.concatenate(dpool_scale, axis=0)]
    rows += [jnp.pad(jnp.concatenate(dsinks, axis=0), ((0, 0), (0, D - LW))), jnp.pad(loss_row, ((0, 0), (0, D - 128)))]
    vec = jnp.concatenate(rows, axis=0)
    nrow = vec.shape[0]
    vec = _all_reduce_small(jnp.pad(vec, ((0, -nrow % 8), (0, 0))))
    o = 0
    red = {}
    for k in ("mix_pre_g", "mix_post_g", "ffn_pre_g", "ffn_post_g", "ple_norm_g"):
        red[k] = vec[o:o + depth]
        o += depth
    red["kv_norm_g"] = vec[o:o + 1]
    red["pool_scale"] = lax.dynamic_slice(vec[o + 1:o + 1 + n_pool], (0, me[0] * (D // NSH)), (n_pool, D // NSH))
    o += 1 + n_pool
    red["sinks"] = vec[o:o + n_attn, :GQ * NKV]
    loss = vec[o + n_attn, 0]

    parts = [_sum_own_and_received(e["own"], e["got"], me, f"sum_{e['k']}_{e['l']}") for e in entries]
    sib = _swap_with_sibling(parts)

    given = dict(
        mix_pre_g=(mix_pre_g, m_mix_pre_g, v_mix_pre_g), mix_post_g=(mix_post_g, m_mix_post_g, v_mix_post_g),
        ffn_pre_g=(ffn_pre_g, m_ffn_pre_g, v_ffn_pre_g), ffn_post_g=(ffn_post_g, m_ffn_post_g, v_ffn_post_g),
        pool_w=(pool_w, m_pool_w, v_pool_w), pool_scale=(pool_scale, m_pool_scale, v_pool_scale),
        kv_norm_g=(kv_norm_g, m_kv_norm_g, v_kv_norm_g), w_k=(w_k, m_w_k, v_w_k), w_v=(w_v, m_w_v, v_w_v),
        w_q=(w_q, m_w_q, v_w_q), w_o=(w_o, m_w_o, v_w_o), sinks=(sinks, m_sinks, v_sinks),
        w_ff_gate=ff_t["w_ff_gate"], w_ff_up=ff_t["w_ff_up"],
        w_ff_down=(w_ff_down, m_w_ff_down, v_w_ff_down), ple_norm_g=(ple_norm_g, m_ple_norm_g, v_ple_norm_g),
        w_ple_gate=(w_ple_gate, m_w_ple_gate, v_w_ple_gate), w_ple_proj=(w_ple_proj, m_w_ple_proj, v_w_ple_proj))
    results = {}
    for k, g in red.items():
        w, m, v = given[k]
        outs = _adamw([[g.reshape(-1, g.shape[-1])]], *(t.reshape(1, -1, t.shape[-1]) for t in (w, m, v)), f"adamw_{k}")
        results[k] = [t.reshape(w.shape) for t in outs]
    per_layer = {}
    for e, a, b in zip(entries, parts, sib):
        per_layer.setdefault(e["k"], {})[e["l"] or 0] = [a, b]
    for k, layers in per_layer.items():
        w, m, v = given[k]
        gs = [layers[l] for l in range(len(layers))]
        outs = _adamw(gs, *(t.reshape((len(gs),) + gs[0][0].shape) for t in (w, m, v)), f"adamw_{k}")
        results[k] = [t12(t) if k in ff_t else t.reshape(w.shape) for t in outs]

    order = ["mix_pre_g", "mix_post_g", "ffn_pre_g", "ffn_post_g", "pool_w", "pool_scale", "kv_norm_g", "w_k", "w_v",
             "w_q", "w_o", "sinks", "w_ff_gate", "w_ff_up", "w_ff_down", "ple_norm_g", "w_ple_gate", "w_ple_proj"]
    return (loss, grad_x, *[results[k][0] for k in order], *[results[k][1] for k in order],
            *[results[k][2] for k in order], *[results[k][3] for k in order])
```

```python
import functools

import jax
import jax.numpy as jnp
from jax import lax
from jax.experimental import pallas as pl
from jax.experimental.pallas import tpu as pltpu

F32 = jnp.float32
CDT = jnp.bfloat16

D = 1024
PG = 256
POOL_WINDOWS = (2, 4, 8, 16)
HALO = 16
HD = 64
NKV = 4
WIN = 128
FS = 704
NSH = 4
PLE = 256
ROPE_THETA = 10000.0
EPS = 1e-6
NEG = -1e30
QSCALE = HD ** -0.5

ADAM_LR, ADAM_B1, ADAM_B2, ADAM_EPS, ADAM_WD, ADAM_STEP = 0.001, 0.9, 0.999, 1e-08, 0.01, 10

VMEM_LIMIT = 56 * 1024 * 1024
TM = 1024
TM_FFN = 512
TM_FFN_BWD = 256
TQ = 512
TT = 4096
MESH = pl.DeviceIdType.MESH


def _cp(n_axes=1):
    return pltpu.CompilerParams(dimension_semantics=("arbitrary",) * n_axes, vmem_limit_bytes=VMEM_LIMIT)


def _dot(a, b):
    return jnp.dot(a, b, preferred_element_type=F32)


def _dot_nt(a, b):
    return lax.dot_general(a, b, (((1,), (1,)), ((), ())), preferred_element_type=F32)


def _dot_tn(a, b):
    return lax.dot_general(a, b, (((0,), (0,)), ((), ())), preferred_element_type=F32)


def _rms_fwd(x, g):
    r = lax.rsqrt(jnp.mean(x * x, axis=-1, keepdims=True) + EPS)
    return x * r * g


def _rms_bwd(dy, x, g):
    r = lax.rsqrt(jnp.mean(x * x, axis=-1, keepdims=True) + EPS)
    xh = x * r
    dg = jnp.sum(dy * xh, axis=0, keepdims=True)
    u = dy * g
    dx = r * (u - xh * jnp.mean(u * xh, axis=-1, keepdims=True))
    return dx, dg


def _sigmoid(a):
    return 1.0 / (1.0 + jnp.exp(-a))


def _swap_half(x):
    w = x.shape[1]
    lane = lax.broadcasted_iota(jnp.int32, x.shape, 1)
    return jnp.where((lane % HD) < HD // 2, pltpu.roll(x, w - HD // 2, axis=1), pltpu.roll(x, HD // 2, axis=1))


def _rope_fwd(x, cos, sin):
    return x * cos + _swap_half(x) * sin


def _rope_bwd(d, cos, sin):
    return d * cos + _swap_half(d * sin)


def _tile(tm, w):
    return pl.BlockSpec((tm, w), lambda i: (i, 0))


def _res(shape):
    return pl.BlockSpec(shape, lambda i: (0,) * len(shape), pipeline_mode=pl.Buffered(1))


def _layer_w(shape):
    return pl.BlockSpec((NSH,) + shape, lambda *_: (0, 0, 0), pipeline_mode=pl.Buffered(1))


def _acc(w):
    return pl.BlockSpec((1, w), lambda i: (0, 0))


def _zero_first(*refs):
    @pl.when(pl.program_id(0) == 0)
    def _():
        for r in refs:
            r[...] = jnp.zeros_like(r)


HBM = pl.BlockSpec(memory_space=pl.ANY)


def _place():
    x, y, c = lax.axis_index("x"), lax.axis_index("y"), lax.axis_index("c")
    chips = [(1 - x, y), (x, 1 - y), (1 - x, 1 - y)]
    return x, y, c, chips


class _Comm:
    def __init__(self, args, out_shape, sems, copies):
        self.args, self.out_shape, self.sems, self.copies = list(args), list(out_shape), list(sems), copies

    def start(self, cin, cout, sems):
        local, sends, _ = self.copies(cin, cout, sems)
        for cp in local + sends:
            cp.start()

    def wait(self, cin, cout, sems):
        local, sends, recvs = self.copies(cin, cout, sems)
        for cp in recvs:
            cp.wait_recv()
        for cp in sends:
            cp.wait_send()
        for cp in local:
            cp.wait()


def _gather_comm(srcs):
    n = len(srcs)
    shapes = [(a.shape if l is None else a.shape[1:]) for a, l in srcs]

    def copies(cin, cout, sems):
        send, recv, loc = sems
        x, y, c, chips = _place()
        me = 2 * x + y
        local, sends, recvs = [], [], []
        for a, (_, l) in enumerate(srcs):
            src = cin[a] if l is None else cin[a].at[l]
            local.append(pltpu.make_async_copy(src, cout[a].at[me], loc.at[a]))
            for j, (px, py) in enumerate(chips):
                kw = dict(src_ref=src, send_sem=send.at[3 * a + j], recv_sem=recv.at[3 * a + j],
                          device_id=(px, py, c), device_id_type=MESH)
                sends.append(pltpu.make_async_remote_copy(dst_ref=cout[a].at[me], **kw))
                recvs.append(pltpu.make_async_remote_copy(dst_ref=cout[a].at[2 * px + py], **kw))
        return local, sends, recvs

    return _Comm([a for a, _ in srcs], [jax.ShapeDtypeStruct((NSH,) + s, CDT) for s in shapes],
                 [pltpu.SemaphoreType.DMA((3 * n,)), pltpu.SemaphoreType.DMA((3 * n,)), pltpu.SemaphoreType.DMA((n,))],
                 copies)


def _scatter_comm(grads):
    n = len(grads)

    def copies(cin, cout, sems):
        send, recv = sems
        x, y, c, chips = _place()
        sends, recvs = [], []
        for a in range(n):
            for j, (px, py) in enumerate(chips):
                cp = pltpu.make_async_remote_copy(
                    src_ref=cin[a].at[2 * px + py], dst_ref=cout[a].at[j], send_sem=send.at[3 * a + j],
                    recv_sem=recv.at[3 * a + j], device_id=(px, py, c), device_id_type=MESH)
                sends.append(cp)
                recvs.append(cp)
        return [], sends, recvs

    return _Comm(grads, [jax.ShapeDtypeStruct((3,) + g.shape[1:], g.dtype) for g in grads],
                 [pltpu.SemaphoreType.DMA((3 * n,)), pltpu.SemaphoreType.DMA((3 * n,))], copies)


def _run_comm(comm, name):
    k_in, k_out = len(comm.args), len(comm.out_shape)

    def body(*refs):
        cin, cout, sems = refs[:k_in], refs[k_in:k_in + k_out], refs[k_in + k_out:]
        comm.start(cin, cout, sems)
        comm.wait(cin, cout, sems)

    return pl.pallas_call(body, name=name, in_specs=[HBM] * k_in, out_specs=[HBM] * k_out, out_shape=comm.out_shape,
                          scratch_shapes=comm.sems)(*comm.args)


def _pallas(body, *, name, grid, in_specs, out_specs, out_shape, args, scratch=(), comm=None):
    out_specs, out_shape, scratch = list(out_specs), list(out_shape), list(scratch)
    params = _cp(len(grid))
    if comm is None:
        return pl.pallas_call(body, grid=grid, name=name, in_specs=in_specs, out_specs=out_specs, out_shape=out_shape,
                              scratch_shapes=scratch, compiler_params=params)(*args), []
    n_in, n_out, n_sc = len(args), len(out_shape), len(scratch)
    k_in, k_out = len(comm.args), len(comm.out_shape)

    def carrying(*refs):
        ins, refs = refs[:n_in], refs[n_in:]
        cin, refs = refs[:k_in], refs[k_in:]
        outs, refs = refs[:n_out], refs[n_out:]
        cout, refs = refs[:k_out], refs[k_out:]
        sc, sems = refs[:n_sc], refs[n_sc:]
        first = functools.reduce(jnp.logical_and, [pl.program_id(d) == 0 for d in range(len(grid))])
        last = functools.reduce(jnp.logical_and, [pl.program_id(d) == g - 1 for d, g in enumerate(grid)])

        @pl.when(first)
        def _():
            comm.start(cin, cout, sems)

        body(*ins, *outs, *sc)

        @pl.when(last)
        def _():
            comm.wait(cin, cout, sems)

    res = pl.pallas_call(
        carrying, grid=grid, name=name, in_specs=list(in_specs) + [HBM] * k_in, out_specs=out_specs + [HBM] * k_out,
        out_shape=out_shape + comm.out_shape, scratch_shapes=scratch + comm.sems, compiler_params=params,
    )(*args, *comm.args)
    return res[:n_out], res[n_out:]


def _pool_normed_ext(h_ref, halo_ref, g, tm):
    xe = jnp.concatenate([halo_ref[...], h_ref[...]], axis=0)
    hn = _rms_fwd(xe, g)
    row = lax.broadcasted_iota(jnp.int32, (tm + HALO, 1), 0)
    return jnp.where((row >= HALO) | (pl.program_id(0) > 0), hn, 0.0)


def _pool_windows(hn_e, tm):
    t = lax.broadcasted_iota(jnp.int32, (tm, PG), 0) + pl.program_id(0) * tm
    outs = []
    for g, w in enumerate(POOL_WINDOWS):
        s = hn_e[:, g * PG:(g + 1) * PG]
        x = s[HALO:, :]
        k = 1
        while k < w:
            s = s + pltpu.roll(s, k, axis=0)
            k *= 2
        cnt = jnp.minimum(t + 1, w).astype(F32)
        outs.append(s[HALO:, :] / cnt - x)
    return jnp.concatenate(outs, axis=1)


def _pool_apply(pooled, w_ref):
    return jnp.concatenate([_dot(pooled[:, g * PG:(g + 1) * PG], w_ref[g]) for g in range(len(POOL_WINDOWS))], axis=1)


def _halo_prev(tm):
    return pl.BlockSpec((HALO, D), lambda i: (jnp.maximum(i * (tm // HALO) - 1, 0), 0))


def _pool_fwd(h, g_pre, pw, pscale, g_post, l, comm=None):
    T = h.shape[0]
    tm = min(TM, T)

    def body(h_ref, halo_ref, gpre_ref, w_ref, sc_ref, gpost_ref, o_ref):
        hn_e = _pool_normed_ext(h_ref, halo_ref, gpre_ref[l:l + 1, :], tm)
        pooled = _pool_windows(hn_e, tm).astype(CDT)
        m = _pool_apply(pooled, w_ref) * sc_ref[l:l + 1, :]
        o_ref[...] = h_ref[...] + _rms_fwd(m, gpost_ref[l:l + 1, :])

    return _pallas(
        body, grid=(T // tm,), name=f"pool_fwd_{l}",
        in_specs=[_tile(tm, D), _halo_prev(tm), _res(g_pre.shape),
                  pl.BlockSpec((None,) + pw.shape[1:], lambda i: (l, 0, 0, 0), pipeline_mode=pl.Buffered(1)),
                  _res(pscale.shape), _res(g_post.shape)],
        out_specs=[_tile(tm, D)], out_shape=[jax.ShapeDtypeStruct((T, D), F32)],
        args=(h, h, g_pre, pw, pscale, g_post), comm=comm)


def _pool_bwd_a(dh1, h, g_pre, pw, pscale, g_post, l, comm=None):
    T = h.shape[0]
    tm = min(TM, T)
    ng = len(POOL_WINDOWS)

    def body(dh_ref, h_ref, halo_ref, gpre_ref, w_ref, sc_ref, gpost_ref, dp_ref, dw_ref, dsc_ref, dgpost_ref):
        _zero_first(dw_ref, dsc_ref, dgpost_ref)
        hn_e = _pool_normed_ext(h_ref, halo_ref, gpre_ref[l:l + 1, :], tm)
        pooled = _pool_windows(hn_e, tm).astype(CDT)
        y = _pool_apply(pooled, w_ref)
        sc = sc_ref[l:l + 1, :]
        dm, dg = _rms_bwd(dh_ref[...], y * sc, gpost_ref[l:l + 1, :])
        dgpost_ref[...] += dg
        dsc_ref[...] += jnp.sum(dm * y, axis=0, keepdims=True)
        dy = (dm * sc).astype(CDT)
        dps = []
        for g in range(ng):
            dyg = dy[:, g * PG:(g + 1) * PG]
            dw_ref[g] += _dot_tn(pooled[:, g * PG:(g + 1) * PG], dyg)
            dps.append(_dot_nt(dyg, w_ref[g]))
        dp_ref[...] = jnp.concatenate(dps, axis=1)

    return _pallas(
        body, grid=(T // tm,), name=f"pool_bwd_a_{l}",
        in_specs=[_tile(tm, D), _tile(tm, D), _halo_prev(tm), _res(g_pre.shape),
                  pl.BlockSpec((None,) + pw.shape[1:], lambda i: (l, 0, 0, 0), pipeline_mode=pl.Buffered(1)),
                  _res(pscale.shape), _res(g_post.shape)],
        out_specs=[_tile(tm, D), pl.BlockSpec((ng, PG, PG), lambda i: (0, 0, 0)), _acc(D), _acc(D)],
        out_shape=[jax.ShapeDtypeStruct((T, D), F32), jax.ShapeDtypeStruct((ng, PG, PG), F32),
                   jax.ShapeDtypeStruct((1, D), F32), jax.ShapeDtypeStruct((1, D), F32)],
        args=(dh1, h, h, g_pre, pw, pscale, g_post), comm=comm)


def _pool_bwd_b(dpool, dh1, h, g_pre, l, comm=None):
    T = h.shape[0]
    tm = min(TM, T)
    nt = T // tm

    def body(dp_ref, nxt_ref, dh_ref, h_ref, gpre_ref, o_ref, dgpre_ref):
        _zero_first(dgpre_ref)
        i = pl.program_id(0)
        dp = dp_ref[...]
        e_e = jnp.concatenate([dp, jnp.where(i < nt - 1, nxt_ref[...], 0.0)], axis=0)
        t = lax.broadcasted_iota(jnp.int32, (tm + HALO, PG), 0) + i * tm
        outs = []
        for g, w in enumerate(POOL_WINDOWS):
            s = e_e[:, g * PG:(g + 1) * PG] / jnp.minimum(t + 1, w).astype(F32)
            k = 1
            while k < w:
                s = s + pltpu.roll(s, tm + HALO - k, axis=0)
                k *= 2
            outs.append(s[:tm, :] - dp[:, g * PG:(g + 1) * PG])
        dx, dg = _rms_bwd(jnp.concatenate(outs, axis=1), h_ref[...], gpre_ref[l:l + 1, :])
        dgpre_ref[...] += dg
        o_ref[...] = dh_ref[...] + dx

    nxt = pl.BlockSpec((HALO, D), lambda i: (jnp.minimum((i + 1) * (tm // HALO), T // HALO - 1), 0))
    return _pallas(
        body, grid=(nt,), name=f"pool_bwd_b_{l}",
        in_specs=[_tile(tm, D), nxt, _tile(tm, D), _tile(tm, D), _res(g_pre.shape)],
        out_specs=[_tile(tm, D), _acc(D)],
        out_shape=[jax.ShapeDtypeStruct((T, D), F32), jax.ShapeDtypeStruct((1, D), F32)],
        args=(dpool, dpool, dh1, h, g_pre), comm=comm)


def _proj_rows(x, w_ref):
    k = w_ref.shape[1]
    out = _dot(x[:, :k], w_ref[0])
    for s in range(1, NSH):
        out = out + _dot(x[:, s * k:(s + 1) * k], w_ref[s])
    return out


def _proj_rows_t(dy, w_ref):
    return jnp.concatenate([_dot_nt(dy, w_ref[s]) for s in range(NSH)], axis=1)


def _rope_tiles(cos_ref, sin_ref, width):
    reps = width // cos_ref.shape[1]
    return jnp.tile(cos_ref[...], (1, reps)), jnp.tile(sin_ref[...], (1, reps))


def _q_fwd(h, g_pre, wq, cos, sin, l, j):
    T = h.shape[0]
    tm = min(TM, T)

    def body(h_ref, g_ref, w_ref, cos_ref, sin_ref, q_ref):
        hn = _rms_fwd(h_ref[...], g_ref[l:l + 1, :]).astype(CDT)
        c, s = _rope_tiles(cos_ref, sin_ref, D)
        q_ref[...] = (_rope_fwd(_proj_rows(hn, w_ref), c, s) * QSCALE).astype(CDT)

    return pl.pallas_call(
        body, grid=(T // tm,), name=f"q_fwd_{l}",
        in_specs=[_tile(tm, D), _res(g_pre.shape), _layer_w((D // NSH, D)), _tile(tm, 2 * HD), _tile(tm, 2 * HD)],
        out_specs=_tile(tm, D), out_shape=jax.ShapeDtypeStruct((T, D), CDT), compiler_params=_cp(),
    )(h, g_pre, wq, cos, sin)


def _q_bwd(dq, dh1, h, g_pre, wq, cos, sin, l, j):
    T = h.shape[0]
    tm = min(TM, T)

    def body(dq_ref, dh_ref, h_ref, g_ref, w_ref, cos_ref, sin_ref, o_ref, hn_ref, dq0_ref, dg_ref):
        _zero_first(dg_ref)
        g = g_ref[l:l + 1, :]
        x = h_ref[...]
        hn_ref[...] = _rms_fwd(x, g).astype(CDT)
        c, s = _rope_tiles(cos_ref, sin_ref, D)
        dq0 = _rope_bwd(dq_ref[...].astype(F32) * QSCALE, c, s).astype(CDT)
        dq0_ref[...] = dq0
        dx, dg = _rms_bwd(_proj_rows_t(dq0, w_ref), x, g)
        dg_ref[...] += dg
        o_ref[...] = dh_ref[...] + dx

    return pl.pallas_call(
        body, grid=(T // tm,), name=f"q_bwd_{l}",
        in_specs=[_tile(tm, D), _tile(tm, D), _tile(tm, D), _res(g_pre.shape), _layer_w((D // NSH, D)),
                  _tile(tm, 2 * HD), _tile(tm, 2 * HD)],
        out_specs=[_tile(tm, D), _tile(tm, D), _tile(tm, D), _acc(D)],
        out_shape=[jax.ShapeDtypeStruct((T, D), F32), jax.ShapeDtypeStruct((T, D), CDT),
                   jax.ShapeDtypeStruct((T, D), CDT), jax.ShapeDtypeStruct((1, D), F32)],
        compiler_params=_cp(),
    )(dq, dh1, h, g_pre, wq, cos, sin)


def _o_fwd(o, h, wo, g_post, l, j):
    T = h.shape[0]
    tm = min(TM, T)

    def body(o_ref, h_ref, w_ref, g_ref, h1_ref, m_ref):
        m = _proj_rows(o_ref[...], w_ref)
        m_ref[...] = m.astype(CDT)
        h1_ref[...] = h_ref[...] + _rms_fwd(m, g_ref[l:l + 1, :])

    return pl.pallas_call(
        body, grid=(T // tm,), name=f"o_fwd_{l}",
        in_specs=[_tile(tm, D), _tile(tm, D), _layer_w((D // NSH, D)), _res(g_post.shape)],
        out_specs=[_tile(tm, D), _tile(tm, D)],
        out_shape=[jax.ShapeDtypeStruct((T, D), F32), jax.ShapeDtypeStruct((T, D), CDT)], compiler_params=_cp(),
    )(o, h, wo, g_post)


def _o_bwd(dh1, m, wo, g_post, l, j):
    T = m.shape[0]
    tm = min(TM, T)

    def body(dh_ref, m_ref, w_ref, g_ref, do_ref, dm_ref, dg_ref):
        _zero_first(dg_ref)
        dm, dg = _rms_bwd(dh_ref[...], m_ref[...].astype(F32), g_ref[l:l + 1, :])
        dg_ref[...] += dg
        dmc = dm.astype(CDT)
        dm_ref[...] = dmc
        do_ref[...] = _proj_rows_t(dmc, w_ref).astype(CDT)

    return pl.pallas_call(
        body, grid=(T // tm,), name=f"o_bwd_{l}",
        in_specs=[_tile(tm, D), _tile(tm, D), _layer_w((D // NSH, D)), _res(g_post.shape)],
        out_specs=[_tile(tm, D), _tile(tm, D), _acc(D)],
        out_shape=[jax.ShapeDtypeStruct((T, D), CDT), jax.ShapeDtypeStruct((T, D), CDT),
                   jax.ShapeDtypeStruct((1, D), F32)],
        compiler_params=_cp(),
    )(dh1, m, wo, g_post)


KVW = 2 * NKV * HD


def _kv_fwd(h, g_kv, wk2, wv2, cos, sin):
    T = h.shape[0]
    tm = min(TM, T)

    def body(h_ref, g_ref, wk_ref, wv_ref, cos_ref, sin_ref, k_ref, v_ref):
        hk = _rms_fwd(h_ref[...], g_ref[...]).astype(CDT)
        c, s = _rope_tiles(cos_ref, sin_ref, KVW)
        k_ref[...] = _rope_fwd(_dot(hk, wk_ref[...]), c, s).astype(CDT)
        v_ref[...] = _dot(hk, wv_ref[...]).astype(CDT)

    return pl.pallas_call(
        body, grid=(T // tm,), name="kv_fwd",
        in_specs=[_tile(tm, D), _res(g_kv.shape), _res(wk2.shape), _res(wv2.shape), _tile(tm, 2 * HD), _tile(tm, 2 * HD)],
        out_specs=[_tile(tm, KVW), _tile(tm, KVW)],
        out_shape=[jax.ShapeDtypeStruct((T, KVW), CDT)] * 2, compiler_params=_cp(),
    )(h, g_kv, wk2, wv2, cos, sin)


def _kv_bwd(dh, h, dks, dkhs, dvs, dvhs, g_kv, wk2, wv2, cos, sin):
    T = h.shape[0]
    tq = min(TQ, T)
    nt = T // tq
    n = len(dks)

    def body(*refs):
        dh_ref, h_ref = refs[:2]
        main = refs[2:2 + 2 * n]
        halo = refs[2 + 2 * n:2 + 4 * n]
        g_ref, wk_ref, wv_ref, cos_ref, sin_ref, o_ref, hk_ref, dk0_ref, dv_ref, dg_ref = refs[2 + 4 * n:]
        _zero_first(dg_ref)
        i = pl.program_id(0)

        def total(mains, halos):
            d = mains[0][...]
            for r in mains[1:]:
                d = d + r[...]
            hl = halos[0][...]
            for r in halos[1:]:
                hl = hl + r[...]
            row = lax.broadcasted_iota(jnp.int32, (tq, 1), 0)
            if tq > WIN:
                hl = jnp.concatenate([jnp.zeros((tq - WIN, KVW), F32), hl], axis=0)
            return d + jnp.where((row >= tq - WIN) & (i < nt - 1), hl, 0.0)

        dk = total(main[:n], halo[:n])
        dv = total(main[n:], halo[n:])
        x = h_ref[...]
        g = g_ref[...]
        hk_ref[...] = _rms_fwd(x, g).astype(CDT)
        c, s = _rope_tiles(cos_ref, sin_ref, KVW)
        dk0 = _rope_bwd(dk, c, s).astype(CDT)
        dvc = dv.astype(CDT)
        dk0_ref[...] = dk0
        dv_ref[...] = dvc
        dx, dg = _rms_bwd(_dot_nt(dk0, wk_ref[...]) + _dot_nt(dvc, wv_ref[...]), x, g)
        dg_ref[...] += dg
        o_ref[...] = dh_ref[...] + dx

    nxt = pl.BlockSpec((WIN, KVW), lambda i: (jnp.minimum(i + 1, nt - 1), 0))
    return pl.pallas_call(
        body, grid=(nt,), name="kv_bwd",
        in_specs=[_tile(tq, D), _tile(tq, D)] + [_tile(tq, KVW)] * (2 * n) + [nxt] * (2 * n)
        + [_res(g_kv.shape), _res(wk2.shape), _res(wv2.shape), _tile(tq, 2 * HD), _tile(tq, 2 * HD)],
        out_specs=[_tile(tq, D), _tile(tq, D), _tile(tq, KVW), _tile(tq, KVW), _acc(D)],
        out_shape=[jax.ShapeDtypeStruct((T, D), F32), jax.ShapeDtypeStruct((T, D), CDT),
                   jax.ShapeDtypeStruct((T, KVW), CDT), jax.ShapeDtypeStruct((T, KVW), CDT),
                   jax.ShapeDtypeStruct((1, D), F32)],
        compiler_params=_cp(),
    )(dh, h, *dks, *dvs, *dkhs, *dvhs, g_kv, wk2, wv2, cos, sin)


GQ = 4
LW = 2 * HD


def _from_prev():
    r = lax.broadcasted_iota(jnp.int32, (GQ * WIN, WIN), 0) % WIN
    j = lax.broadcasted_iota(jnp.int32, (GQ * WIN, WIN), 1)
    return j > r


def _fold(x2, prev):
    return jnp.where(prev, x2[:, :WIN], x2[:, WIN:])


def _unfold(x, prev):
    zero = jnp.zeros_like(x)
    return jnp.concatenate([jnp.where(prev, x, zero), jnp.where(prev, zero, x)], axis=1)


def _stack_heads(ref, rows, g):
    lane = lax.broadcasted_iota(jnp.int32, (WIN, LW), 1)
    parts = []
    for pr in range(2):
        x = ref[rows, (2 * g + pr) * LW:(2 * g + pr + 1) * LW]
        parts += [jnp.where(lane < HD, x, jnp.zeros_like(x)), jnp.where(lane >= HD, x, jnp.zeros_like(x))]
    return jnp.concatenate(parts, axis=0)


def _unstack_heads(x4):
    lane = lax.broadcasted_iota(jnp.int32, (WIN, LW), 1)
    return [jnp.where(lane < HD, x4[(2 * pr) * WIN:(2 * pr + 1) * WIN], x4[(2 * pr + 1) * WIN:(2 * pr + 2) * WIN])
            for pr in range(2)]


def _sink_col(sink_ref, j, g):
    blk = lax.broadcasted_iota(jnp.int32, (GQ * WIN, 1), 0) // WIN
    col = jnp.zeros((GQ * WIN, 1), F32)
    for a in range(GQ):
        col = jnp.where(blk == a, sink_ref[j, GQ * g + a], col)
    return col


def _softmax_sink(s, hidden, sink):
    if hidden is not None:
        s = jnp.where(hidden, NEG, s)
    m = jnp.maximum(jnp.max(s, axis=-1, keepdims=True), sink)
    e = jnp.exp(s - m)
    es = jnp.exp(sink - m)
    l = jnp.sum(e, axis=-1, keepdims=True) + es
    return e / l, es / l


def _kv_halo(tq):
    return pl.BlockSpec((WIN, KVW), lambda i: (jnp.maximum(i * (tq // WIN) - 1, 0), 0))


def _attn_fwd(q, kk, vv, sinks, l, j):
    T = q.shape[0]
    tq = min(TQ, T)

    def body(sink_ref, q_ref, k_ref, kh_ref, v_ref, vh_ref, o_ref):
        i = pl.program_id(0)
        ke = jnp.concatenate([kh_ref[...], k_ref[...]], axis=0)
        ve = jnp.concatenate([vh_ref[...], v_ref[...]], axis=0)
        prev = _from_prev()
        for n in range(tq // WIN):
            rows = slice(n * WIN, (n + 1) * WIN)
            hidden = prev & (i == 0) if n == 0 else None
            for g in range(NKV):
                kg = ke[n * WIN:(n + 2) * WIN, g * LW:(g + 1) * LW]
                vg = ve[n * WIN:(n + 2) * WIN, g * LW:(g + 1) * LW]
                s = _fold(_dot_nt(_stack_heads(q_ref, rows, g), kg), prev)
                p, _ = _softmax_sink(s, hidden, _sink_col(sink_ref, j, g))
                pairs = _unstack_heads(_dot(_unfold(p.astype(CDT), prev), vg))
                for pr in range(2):
                    o_ref[rows, (2 * g + pr) * LW:(2 * g + pr + 1) * LW] = pairs[pr].astype(CDT)

    return pl.pallas_call(
        body, grid=(T // tq,), name=f"attn_fwd_{l}",
        in_specs=[pl.BlockSpec(memory_space=pltpu.SMEM), _tile(tq, D), _tile(tq, KVW), _kv_halo(tq),
                  _tile(tq, KVW), _kv_halo(tq)],
        out_specs=_tile(tq, D), out_shape=jax.ShapeDtypeStruct((T, D), CDT), compiler_params=_cp(),
    )(sinks, q, kk, kk, vv, vv)


def _attn_bwd(q, kk, vv, o, do, sinks, l, j):
    T = q.shape[0]
    tq = min(TQ, T)
    nt = T // tq

    def body(sink_ref, q_ref, k_ref, kh_ref, v_ref, vh_ref, o_ref, do_ref,
             dq_ref, dk_ref, dkh_ref, dv_ref, dvh_ref, dsink_ref, dke, dve):
        _zero_first(dsink_ref)
        i = pl.program_id(0)
        ke = jnp.concatenate([kh_ref[...], k_ref[...]], axis=0)
        ve = jnp.concatenate([vh_ref[...], v_ref[...]], axis=0)
        dke[...] = jnp.zeros_like(dke)
        dve[...] = jnp.zeros_like(dve)
        lane = lax.broadcasted_iota(jnp.int32, (1, LW), 1)
        blk = lax.broadcasted_iota(jnp.int32, (GQ * WIN, 1), 0) // WIN
        dsink = jnp.zeros((1, LW), F32)
        prev = _from_prev()
        for n in range(tq // WIN):
            rows = slice(n * WIN, (n + 1) * WIN)
            krows = slice(n * WIN, (n + 2) * WIN)
            hidden = prev & (i == 0) if n == 0 else None
            for g in range(NKV):
                cols = slice(g * LW, (g + 1) * LW)
                kg = ke[krows, cols]
                vg = ve[krows, cols]
                q4 = _stack_heads(q_ref, rows, g)
                do4 = _stack_heads(do_ref, rows, g)
                o4 = _stack_heads(o_ref, rows, g)
                p, ps = _softmax_sink(_fold(_dot_nt(q4, kg), prev), hidden, _sink_col(sink_ref, j, g))
                delta = jnp.sum(do4.astype(F32) * o4.astype(F32), axis=-1, keepdims=True)
                ds = _unfold((p * (_fold(_dot_nt(do4, vg), prev) - delta)).astype(CDT), prev)
                pc = _unfold(p.astype(CDT), prev)
                pairs = _unstack_heads(_dot(ds, kg))
                for pr in range(2):
                    dq_ref[rows, (2 * g + pr) * LW:(2 * g + pr + 1) * LW] = pairs[pr].astype(CDT)
                dke[krows, cols] += _dot_tn(ds, q4)
                dve[krows, cols] += _dot_tn(pc, do4)
                t = ps * delta
                for a in range(GQ):
                    dsink = dsink - jnp.where(lane == GQ * g + a, jnp.sum(jnp.where(blk == a, t, 0.0)), 0.0)
        dsink_ref[...] += dsink
        dkh_ref[...] = dke[:WIN, :]
        dk_ref[...] = dke[WIN:, :]
        dvh_ref[...] = dve[:WIN, :]
        dv_ref[...] = dve[WIN:, :]

    halo_out = pl.BlockSpec((WIN, KVW), lambda i: (i, 0))
    return pl.pallas_call(
        body, grid=(nt,), name=f"attn_bwd_{l}",
        in_specs=[pl.BlockSpec(memory_space=pltpu.SMEM), _tile(tq, D), _tile(tq, KVW), _kv_halo(tq),
                  _tile(tq, KVW), _kv_halo(tq), _tile(tq, D), _tile(tq, D)],
        out_specs=[_tile(tq, D), _tile(tq, KVW), halo_out, _tile(tq, KVW), halo_out, _acc(LW)],
        out_shape=[jax.ShapeDtypeStruct((T, D), CDT), jax.ShapeDtypeStruct((T, KVW), F32),
                   jax.ShapeDtypeStruct((nt * WIN, KVW), F32), jax.ShapeDtypeStruct((T, KVW), F32),
                   jax.ShapeDtypeStruct((nt * WIN, KVW), F32), jax.ShapeDtypeStruct((1, LW), F32)],
        scratch_shapes=[pltpu.VMEM((WIN + tq, KVW), F32), pltpu.VMEM((WIN + tq, KVW), F32)],
        compiler_params=_cp(),
    )(sinks, q, kk, kk, vv, vv, o, do)


def _chunks(tm):
    return pl.BlockSpec((NSH, tm, FS), lambda i: (0, i, 0))


def _ffn_fwd(h1, g_pre, wg, wu, wd, g_post, l, comm=None):
    T = h1.shape[0]
    tm = min(TM_FFN, T)

    def body(h_ref, gpre_ref, wg_ref, wu_ref, wd_ref, gpost_ref, h2_ref, a_ref, b_ref, f_ref):
        x = h_ref[...]
        fn = _rms_fwd(x, gpre_ref[l:l + 1, :]).astype(CDT)
        f = jnp.zeros((tm, D), F32)
        for s in range(NSH):
            a = _dot_nt(fn, wg_ref[s])
            b = _dot_nt(fn, wu_ref[s])
            a_ref[s] = a.astype(CDT)
            b_ref[s] = b.astype(CDT)
            f = f + _dot((a * _sigmoid(a) * b).astype(CDT), wd_ref[s])
        f_ref[...] = f.astype(CDT)
        h2_ref[...] = x + _rms_fwd(f, gpost_ref[l:l + 1, :])

    return _pallas(
        body, grid=(T // tm,), name=f"ffn_fwd_{l}",
        in_specs=[_tile(tm, D), _res(g_pre.shape), _layer_w((FS, D)), _layer_w((FS, D)), _layer_w((FS, D)),
                  _res(g_post.shape)],
        out_specs=[_tile(tm, D), _chunks(tm), _chunks(tm), _tile(tm, D)],
        out_shape=[jax.ShapeDtypeStruct((T, D), F32), jax.ShapeDtypeStruct((NSH, T, FS), CDT),
                   jax.ShapeDtypeStruct((NSH, T, FS), CDT), jax.ShapeDtypeStruct((T, D), CDT)],
        args=(h1, g_pre, wg, wu, wd, g_post), comm=comm)


def _ffn_bwd(dh2, h1, f, a, b, g_pre, wg, wu, wd, g_post, l, comm=None):
    T = h1.shape[0]
    tm = min(TM_FFN_BWD, T)

    def body(dh_ref, h_ref, f_ref, a_ref, b_ref, gpre_ref, wg_ref, wu_ref, wd_ref, gpost_ref,
             dh1_ref, fn_ref, df_ref, act_ref, da_ref, db_ref, dgpre_ref, dgpost_ref):
        _zero_first(dgpre_ref, dgpost_ref)
        x = h_ref[...]
        gpre = gpre_ref[l:l + 1, :]
        fn_ref[...] = _rms_fwd(x, gpre).astype(CDT)
        dh2 = dh_ref[...]
        df, dg = _rms_bwd(dh2, f_ref[...].astype(F32), gpost_ref[l:l + 1, :])
        dgpost_ref[...] += dg
        dfc = df.astype(CDT)
        df_ref[...] = dfc
        dfn = jnp.zeros((tm, D), F32)
        for s in range(NSH):
            av = a_ref[s].astype(F32)
            bv = b_ref[s].astype(F32)
            sg = _sigmoid(av)
            silu = av * sg
            act_ref[s] = (silu * bv).astype(CDT)
            dact = _dot_nt(dfc, wd_ref[s])
            da = (dact * bv * (sg * (1.0 + av * (1.0 - sg)))).astype(CDT)
            db = (dact * silu).astype(CDT)
            da_ref[s] = da
            db_ref[s] = db
            dfn = dfn + _dot(da, wg_ref[s]) + _dot(db, wu_ref[s])
        dx, dg = _rms_bwd(dfn, x, gpre)
        dgpre_ref[...] += dg
        dh1_ref[...] = dh2 + dx

    return _pallas(
        body, grid=(T // tm,), name=f"ffn_bwd_{l}",
        in_specs=[_tile(tm, D), _tile(tm, D), _tile(tm, D), _chunks(tm), _chunks(tm), _res(g_pre.shape),
                  _layer_w((FS, D)), _layer_w((FS, D)), _layer_w((FS, D)), _res(g_post.shape)],
        out_specs=[_tile(tm, D), _tile(tm, D), _tile(tm, D), _chunks(tm), _chunks(tm), _chunks(tm), _acc(D), _acc(D)],
        out_shape=[jax.ShapeDtypeStruct((T, D), F32), jax.ShapeDtypeStruct((T, D), CDT), jax.ShapeDtypeStruct((T, D), CDT),
                   jax.ShapeDtypeStruct((NSH, T, FS), CDT), jax.ShapeDtypeStruct((NSH, T, FS), CDT),
                   jax.ShapeDtypeStruct((NSH, T, FS), CDT), jax.ShapeDtypeStruct((1, D), F32),
                   jax.ShapeDtypeStruct((1, D), F32)],
        args=(dh2, h1, f, a, b, g_pre, wg, wu, wd, g_post), comm=comm)


def _ple_proj(p, wpp_ref):
    return jnp.concatenate([_dot(p, wpp_ref[s]) for s in range(NSH)], axis=1)


def _ple_fwd(h2, p, g_ple, wpg, wpp, l, comm=None):
    T = h2.shape[0]
    tm = min(TM, T)

    def body(h_ref, p_ref, g_ref, wpg_ref, wpp_ref, o_ref):
        x = h_ref[...]
        pn = _rms_fwd(x, g_ref[l:l + 1, :]).astype(CDT)
        gate = _sigmoid(_proj_rows(pn, wpg_ref))
        o_ref[...] = x + _ple_proj(p_ref[...].astype(CDT), wpp_ref) * gate

    return _pallas(
        body, grid=(T // tm,), name=f"ple_fwd_{l}",
        in_specs=[_tile(tm, D), pl.BlockSpec((None, tm, PLE), lambda i: (l, i, 0)), _res(g_ple.shape),
                  _layer_w((D // NSH, D)), _layer_w((PLE, D // NSH))],
        out_specs=[_tile(tm, D)], out_shape=[jax.ShapeDtypeStruct((T, D), F32)],
        args=(h2, p, g_ple, wpg, wpp), comm=comm)


def _ple_bwd(dh3, h2, p, g_ple, wpg, wpp, l):
    T = h2.shape[0]
    tm = min(TM, T)

    def body(dh_ref, h_ref, p_ref, g_ref, wpg_ref, wpp_ref, o_ref, pn_ref, dz_ref, dpp_ref, pc_ref, dg_ref):
        _zero_first(dg_ref)
        x = h_ref[...]
        g = g_ref[l:l + 1, :]
        pn = _rms_fwd(x, g).astype(CDT)
        pn_ref[...] = pn
        gate = _sigmoid(_proj_rows(pn, wpg_ref))
        pc = p_ref[...].astype(CDT)
        pc_ref[...] = pc
        pp = _ple_proj(pc, wpp_ref)
        dh3 = dh_ref[...]
        dpp_ref[...] = (dh3 * gate).astype(CDT)
        dz = (dh3 * pp * gate * (1.0 - gate)).astype(CDT)
        dz_ref[...] = dz
        dx, dg = _rms_bwd(_proj_rows_t(dz, wpg_ref), x, g)
        dg_ref[...] += dg
        o_ref[...] = dh3 + dx

    return pl.pallas_call(
        body, grid=(T // tm,), name=f"ple_bwd_{l}",
        in_specs=[_tile(tm, D), _tile(tm, D), pl.BlockSpec((None, tm, PLE), lambda i: (l, i, 0)), _res(g_ple.shape),
                  _layer_w((D // NSH, D)), _layer_w((PLE, D // NSH))],
        out_specs=[_tile(tm, D), _tile(tm, D), _tile(tm, D), _tile(tm, D), _tile(tm, PLE), _acc(D)],
        out_shape=[jax.ShapeDtypeStruct((T, D), F32), jax.ShapeDtypeStruct((T, D), CDT), jax.ShapeDtypeStruct((T, D), CDT),
                   jax.ShapeDtypeStruct((T, D), CDT), jax.ShapeDtypeStruct((T, PLE), CDT), jax.ShapeDtypeStruct((1, D), F32)],
        compiler_params=_cp(),
    )(dh3, h2, p, g_ple, wpg, wpp)


def _loss_grad(y, tgt):
    T = y.shape[0]
    tm = min(TM, T)

    def body(y_ref, t_ref, dy_ref, loss_ref):
        _zero_first(loss_ref)
        err = y_ref[...] - t_ref[...]
        dy_ref[...] = err * (1.0 / D)
        lane = lax.broadcasted_iota(jnp.int32, (1, 128), 1)
        loss_ref[...] += jnp.where(lane == 0, (0.5 / D) * jnp.sum(err * err), 0.0)

    return pl.pallas_call(
        body, grid=(T // tm,), name="loss_grad", in_specs=[_tile(tm, D), _tile(tm, D)],
        out_specs=[_tile(tm, D), _acc(128)],
        out_shape=[jax.ShapeDtypeStruct((T, D), F32), jax.ShapeDtypeStruct((1, 128), F32)], compiler_params=_cp(),
    )(y, tgt)


def _mm_tn(x, dy, name, n_split=1, comm=None):
    xb, yb = x.ndim == 3, dy.ndim == 3
    T, K = x.shape[-2:]
    N = dy.shape[-1] // n_split
    B = x.shape[0] if xb else dy.shape[0] if yb else n_split
    tt = min(TT, T)
    nt = T // tt

    def body(x_ref, dy_ref, o_ref, oc_ref):
        t = pl.program_id(1)

        @pl.when(t == 0)
        def _():
            o_ref[...] = jnp.zeros_like(o_ref)

        o_ref[...] += _dot_tn(x_ref[...].astype(CDT), dy_ref[...])

        @pl.when(t == nt - 1)
        def _():
            oc_ref[...] = o_ref[...].astype(CDT)

    x_spec = pl.BlockSpec((None, tt, K), lambda b, t: (b, t, 0)) if xb else pl.BlockSpec((tt, K), lambda b, t: (t, 0))
    if yb:
        y_spec = pl.BlockSpec((None, tt, N), lambda b, t: (b, t, 0))
    else:
        y_spec = pl.BlockSpec((tt, N), lambda b, t: (t, b if n_split > 1 else 0))
    o_spec = pl.BlockSpec((None, K, N), lambda b, t: (b, 0, 0))
    res, got = _pallas(
        body, grid=(B, nt), name=name, in_specs=[x_spec, y_spec], out_specs=[o_spec, o_spec],
        out_shape=[jax.ShapeDtypeStruct((B, K, N), F32), jax.ShapeDtypeStruct((B, K, N), CDT)], args=(x, dy), comm=comm)
    return res if comm is None else (res, got)


def _fold_dup(dw2, name):
    def body(x_ref, o_ref, oc_ref):
        x = x_ref[...]
        y = jnp.concatenate([x[:, g * LW:g * LW + HD] + x[:, g * LW + HD:(g + 1) * LW] for g in range(NKV)], axis=1)
        o_ref[...] = y
        oc_ref[...] = y.astype(CDT)

    return pl.pallas_call(
        body, grid=(NSH,), name=name, in_specs=[_tile(D // NSH, KVW)], out_specs=[_tile(D // NSH, NKV * HD)] * 2,
        out_shape=[jax.ShapeDtypeStruct((D, NKV * HD), F32), jax.ShapeDtypeStruct((D, NKV * HD), CDT)],
        compiler_params=_cp(),
    )(dw2)


def _swap_with_sibling(parts):
    n = len(parts)

    def body(*refs):
        ins, outs = refs[:n], refs[n:2 * n]
        send, recv = refs[2 * n:]
        x, y, c, _ = _place()
        copies = [pltpu.make_async_remote_copy(
            src_ref=ins[a], dst_ref=outs[a], send_sem=send.at[a], recv_sem=recv.at[a],
            device_id=(x, y, 1 - c), device_id_type=MESH) for a in range(n)]
        for cp in copies:
            cp.start()
        for cp in copies:
            cp.wait_recv()
        for cp in copies:
            cp.wait_send()

    return pl.pallas_call(
        body, name="swap_partials", in_specs=[HBM] * n, out_specs=[HBM] * n,
        out_shape=[jax.ShapeDtypeStruct(p.shape, p.dtype) for p in parts],
        scratch_shapes=[pltpu.SemaphoreType.DMA((n,)), pltpu.SemaphoreType.DMA((n,))],
    )(*parts)


def _all_reduce_small(vec):
    R = vec.shape[0]
    ndev = 8

    def body(v_ref, o_ref, buf, send, recv):
        x, y, c, _ = _place()
        me = 4 * x + 2 * y + c
        buf[me] = v_ref[...]
        copies = []
        for k in range(1, ndev):
            peer = (x if not k & 4 else 1 - x, y if not k & 2 else 1 - y, c if not k & 1 else 1 - c)
            copies.append(pltpu.make_async_remote_copy(
                src_ref=v_ref, dst_ref=buf.at[me], send_sem=send.at[k - 1], recv_sem=recv.at[k - 1],
                device_id=peer, device_id_type=MESH))
            copies[-1].start()
        for k in range(1, ndev):
            peer = (x if not k & 4 else 1 - x, y if not k & 2 else 1 - y, c if not k & 1 else 1 - c)
            pltpu.make_async_remote_copy(
                src_ref=v_ref, dst_ref=buf.at[4 * peer[0] + 2 * peer[1] + peer[2]], send_sem=send.at[k - 1],
                recv_sem=recv.at[k - 1], device_id=peer, device_id_type=MESH).wait_recv()
        for cp in copies:
            cp.wait_send()
        tot = buf[0]
        for d in range(1, ndev):
            tot = tot + buf[d]
        o_ref[...] = tot

    return pl.pallas_call(
        body, name="all_reduce_small", in_specs=[pl.BlockSpec(memory_space=pltpu.VMEM)],
        out_specs=pl.BlockSpec(memory_space=pltpu.VMEM), out_shape=jax.ShapeDtypeStruct(vec.shape, F32),
        scratch_shapes=[pltpu.VMEM((ndev, R, D), F32), pltpu.SemaphoreType.DMA((ndev - 1,)),
                        pltpu.SemaphoreType.DMA((ndev - 1,))],
    )(vec)


def _row_tile(rows):
    for t in (512, 256, 128, 64, 32, 16, 8):
        if rows % t == 0:
            return t
    return rows


def _sum_own_and_received(own, got, me, name):
    _, R, C = own.shape
    tr = _row_tile(R)

    def body(me_ref, own_ref, got_ref, o_ref):
        o_ref[...] = ((own_ref[...] + got_ref[0].astype(F32)) + got_ref[1].astype(F32)) + got_ref[2].astype(F32)

    return pl.pallas_call(
        body, name=name,
        grid_spec=pltpu.PrefetchScalarGridSpec(
            num_scalar_prefetch=1, grid=(R // tr,),
            in_specs=[pl.BlockSpec((None, tr, C), lambda i, me_ref: (me_ref[0], i, 0)),
                      pl.BlockSpec((3, tr, C), lambda i, me_ref: (0, i, 0))],
            out_specs=pl.BlockSpec((tr, C), lambda i, me_ref: (i, 0))),
        out_shape=jax.ShapeDtypeStruct((R, C), F32), compiler_params=_cp(),
    )(me, own, got)


def _adamw(gs, w, m, v, name):
    L, R, C = w.shape
    tr = _row_tile(R)
    ng = len(gs[0])
    c1 = 1.0 - ADAM_B1 ** ADAM_STEP
    c2 = 1.0 - ADAM_B2 ** ADAM_STEP

    def body(*refs):
        w_ref, m_ref, v_ref, g_ref, d_ref, nm_ref, nv_ref = refs[L * ng:]
        lay = pl.program_id(0)
        g = None
        for l in range(L):
            gl = refs[l * ng][...]
            for r in refs[l * ng + 1:(l + 1) * ng]:
                gl = gl + r[...]
            g = gl if g is None else jnp.where(lay == l, gl, g)
        nm = ADAM_B1 * m_ref[...] + (1.0 - ADAM_B1) * g
        nv = ADAM_B2 * v_ref[...] + (1.0 - ADAM_B2) * (g * g)
        g_ref[...] = g
        nm_ref[...] = nm
        nv_ref[...] = nv
        d_ref[...] = -ADAM_LR * ((nm / c1) / (jnp.sqrt(nv / c2) + ADAM_EPS) + ADAM_WD * w_ref[...])

    gspecs = [pl.BlockSpec((tr, C), lambda lay, i, l=l: (jnp.where(lay == l, i, 0), 0)) for l in range(L) for _ in range(ng)]
    spec = pl.BlockSpec((None, tr, C), lambda lay, i: (lay, i, 0))
    return pl.pallas_call(
        body, grid=(L, R // tr), name=name, in_specs=gspecs + [spec] * 3, out_specs=[spec] * 4,
        out_shape=[jax.ShapeDtypeStruct((L, R, C), F32)] * 4, compiler_params=_cp(2),
    )(*[g for gl in gs for g in gl], w, m, v)


def _rope_tables(T):
    inv = 1.0 / (ROPE_THETA ** (jnp.arange(0, HD, 2, dtype=F32) / HD))
    ang = jnp.arange(T, dtype=F32)[:, None] * inv[None, :]
    c, s = jnp.cos(ang), jnp.sin(ang)
    return jnp.tile(jnp.concatenate([c, c], axis=1), (1, 2)), jnp.tile(jnp.concatenate([-s, s], axis=1), (1, 2))


def _dup_cols(w):
    return jnp.broadcast_to(w.reshape(D, NKV, 1, HD), (D, NKV, 2, HD)).reshape(D, KVW)


def kernel(x, p, mix_pre_g, mix_post_g, ffn_pre_g, ffn_post_g, pool_w, pool_scale, kv_norm_g, w_k, w_v, w_q, w_o, sinks, w_ff_gate, w_ff_up, w_ff_down, ple_norm_g, w_ple_gate, w_ple_proj, loss_target, m_mix_pre_g, m_mix_post_g, m_ffn_pre_g, m_ffn_post_g, m_pool_w, m_pool_scale, m_kv_norm_g, m_w_k, m_w_v, m_w_q, m_w_o, m_sinks, m_w_ff_gate, m_w_ff_up, m_w_ff_down, m_ple_norm_g, m_w_ple_gate, m_w_ple_proj, v_mix_pre_g, v_mix_post_g, v_ffn_pre_g, v_ffn_post_g, v_pool_w, v_pool_scale, v_kv_norm_g, v_w_k, v_w_v, v_w_q, v_w_o, v_sinks, v_w_ff_gate, v_w_ff_up, v_w_ff_down, v_ple_norm_g, v_w_ple_gate, v_w_ple_proj):
    depth = mix_pre_g.shape[0]
    n_pool = pool_w.shape[0]
    n_attn = w_q.shape[0]
    T = x.shape[1]
    h = x[0]
    p = p[:, 0]
    tgt = loss_target[0]
    me = (2 * lax.axis_index("x") + lax.axis_index("y")).astype(jnp.int32).reshape(1)

    def t12(a):
        return jnp.transpose(a, (0, 2, 1))

    ff_t = {k: tuple(t12(a) for a in v) for k, v in dict(
        w_ff_gate=(w_ff_gate, m_w_ff_gate, v_w_ff_gate), w_ff_up=(w_ff_up, m_w_ff_up, v_w_ff_up)).items()}
    c_pool, c_wk, c_wv, c_wq, c_wo, c_wg, c_wu, c_wd, c_wpg, c_wpp = (w.astype(CDT) for w in (
        pool_w, w_k, w_v, w_q, w_o, ff_t["w_ff_gate"][0], ff_t["w_ff_up"][0], w_ff_down, w_ple_gate, w_ple_proj))

    def ffn_srcs(l):
        return [(c_wg, l), (c_wu, l), (c_wd, l)]

    def ple_srcs(l):
        return [(c_wpg, l), (c_wpp, l)]

    (gpw,) = _run_comm(_gather_comm([(c_pool, None)]), "gather_pool_w")
    wff, wple, wattn = {}, {}, {}
    pw = jnp.transpose(gpw, (1, 2, 0, 3, 4)).reshape(n_pool, len(POOL_WINDOWS), PG, PG)
    kv_g = kv_norm_g.reshape(1, D)
    cos, sin = _rope_tables(T)

    pscale = _all_reduce_small(jnp.pad(
        lax.dynamic_update_slice(jnp.zeros((n_pool, D), F32), pool_scale, (0, me[0] * (D // NSH))),
        ((0, 8 - n_pool), (0, 0))))[:n_pool] * 0.5

    saved = []
    kk = vv = wk2 = wv2 = None
    for l in range(depth):
        s = {"h": h}
        if l < n_pool:
            (h1,), got = _pool_fwd(h, mix_pre_g, pw, pscale, mix_post_g, l,
                                   comm=_gather_comm(ffn_srcs(0)) if l == 0 else None)
            if l == 0:
                wff[0] = got
        else:
            j = l - n_pool
            wq, wo = wattn[l]
            s["q"] = _q_fwd(h, mix_pre_g, wq, cos, sin, l, j)
            s["o"] = _attn_fwd(s["q"], kk, vv, sinks, l, j)
            h1, s["m"] = _o_fwd(s["o"], h, wo, mix_post_g, l, j)
        s["h1"] = h1
        srcs = []
        if l + 1 < depth:
            srcs = ffn_srcs(l + 1)
            if l + 1 >= n_pool:
                srcs += [(c_wq, l + 1 - n_pool), (c_wo, l + 1 - n_pool)]
            if l + 1 == n_pool:
                srcs += [(c_wk, None), (c_wv, None)]
        if l == 0:
            srcs += ple_srcs(0)
        (h2, s["a"], s["b"], s["f"]), got = _ffn_fwd(
            h1, ffn_pre_g, *wff[l], ffn_post_g, l, comm=_gather_comm(srcs) if srcs else None)
        if l == 0:
            wple[0] = got[-2:]
        if l + 1 < depth:
            wff[l + 1] = got[:3]
            if l + 1 >= n_pool:
                wattn[l + 1] = got[3:5]
            if l + 1 == n_pool:
                wk2 = _dup_cols(got[5].reshape(D, NKV * HD))
                wv2 = _dup_cols(got[6].reshape(D, NKV * HD))
        s["h2"] = h2
        (h,), got = _ple_fwd(h2, p, ple_norm_g, *wple[l], l,
                             comm=_gather_comm(ple_srcs(l + 1)) if l + 1 < depth else None)
        if l + 1 < depth:
            wple[l + 1] = got
        if l == n_pool - 1:
            s["hkv"] = h
            kk, vv = _kv_fwd(h, kv_g, wk2, wv2, cos, sin)
        saved.append(s)

    dh, loss_row = _loss_grad(h, tgt)

    entries, pending = [], []

    def add_grad(k, l, g32, g16):
        def by_shard(g):
            return g.reshape(NSH, -1, g.shape[-1])
        e = dict(k=k, l=l, own=by_shard(g32), sent=by_shard(g16))
        entries.append(e)
        pending.append(e)

    def carry(n=None):
        take = pending[:n] if n else list(pending)
        del pending[:len(take)]
        return take, (_scatter_comm([e["sent"] for e in take]) if take else None)

    def landed(take, got):
        for e, g in zip(take, got):
            e["got"] = g

    small = {k: [None] * depth for k in ("mix_pre", "mix_post", "ffn_pre", "ffn_post", "ple")}
    dpool_scale = [None] * n_pool
    dsinks = [None] * n_attn
    dks, dkhs, dvs, dvhs = [], [], [], []
    for l in reversed(range(depth)):
        s = saved[l]
        if l == n_pool - 1:
            dh, hk, dk0, dvc, dkv_g = _kv_bwd(dh, s["hkv"], dks, dkhs, dvs, dvhs, kv_g, wk2, wv2, cos, sin)
            add_grad("w_k", None, *_fold_dup(_mm_tn(hk, dk0, "dw_k")[0][0], "fold_w_k"))
            add_grad("w_v", None, *_fold_dup(_mm_tn(hk, dvc, "dw_v")[0][0], "fold_w_v"))
        dh, pn, dz, dpp, pc, small["ple"][l] = _ple_bwd(dh, s["h2"], p, ple_norm_g, *wple[l], l)
        add_grad("w_ple_gate", l, *_mm_tn(pn, dz, f"dw_ple_gate_{l}"))
        add_grad("w_ple_proj", l, *_mm_tn(pc, dpp, f"dw_ple_proj_{l}", n_split=NSH))
        take, comm = carry()
        (dh, fn, df, act, da, db, small["ffn_pre"][l], small["ffn_post"][l]), got = _ffn_bwd(
            dh, s["h1"], s["f"], s["a"], s["b"], ffn_pre_g, *wff[l], ffn_post_g, l, comm=comm)
        landed(take, got)
        add_grad("w_ff_gate", l, *_mm_tn(da, fn, f"dw_ff_gate_{l}"))
        take, comm = carry() if l == 0 else ([], None)
        res = _mm_tn(db, fn, f"dw_ff_up_{l}", comm=comm)
        if comm is not None:
            res, got = res
            landed(take, got)
        add_grad("w_ff_up", l, *res)
        take, comm = carry() if l == 0 else ([], None)
        res = _mm_tn(act, df, f"dw_ff_down_{l}", comm=comm)
        if comm is not None:
            res, got = res
            landed(take, got)
        add_grad("w_ff_down", l, *res)
        if l < n_pool:
            take, comm = carry() if l == 0 else ([], None)
            (dpool, dpw, dpool_scale[l], small["mix_post"][l]), got = _pool_bwd_a(
                dh, s["h"], mix_pre_g, pw, pscale, mix_post_g, l, comm=comm)
            landed(take, got)
            dpw = jnp.transpose(dpw.reshape(len(POOL_WINDOWS), NSH, PG // NSH, PG), (1, 0, 2, 3))
            dpw = dpw.reshape(NSH, len(POOL_WINDOWS) * PG // NSH, PG)
            add_grad("pool_w", l, dpw, dpw.astype(CDT))
            (dh, small["mix_pre"][l]), _ = _pool_bwd_b(dpool, dh, s["h"], mix_pre_g, l)
        else:
            j = l - n_pool
            wq, wo = wattn[l]
            do, dm, small["mix_post"][l] = _o_bwd(dh, s["m"], wo, mix_post_g, l, j)
            add_grad("w_o", j, *_mm_tn(s["o"], dm, f"dw_o_{l}"))
            dq, dk, dkh, dv, dvh, dsinks[j] = _attn_bwd(s["q"], kk, vv, s["o"], do, sinks, l, j)
            dks.append(dk); dkhs.append(dkh); dvs.append(dv); dvhs.append(dvh)
            dh, hn, dq0, small["mix_pre"][l] = _q_bwd(dq, dh, s["h"], mix_pre_g, wq, cos, sin, l, j)
            add_grad("w_q", j, *_mm_tn(hn, dq0, f"dw_q_{l}"))
    if pending:
        take, comm = carry()
        landed(take, _run_comm(comm, "scatter_rest"))
    grad_x = dh[None]

    rows = [jnp.concatenate(small[k], axis=0) for k in ("mix_pre", "mix_post", "ffn_pre", "ffn_post", "ple")]
    rows += [dkv_g, jnp.concatenate(dpool_scale, axis=0)]
    rows += [jnp.pad(jnp.concatenate(dsinks, axis=0), ((0, 0), (0, D - LW))), jnp.pad(loss_row, ((0, 0), (0, D - 128)))]
    vec = jnp.concatenate(rows, axis=0)
    nrow = vec.shape[0]
    vec = _all_reduce_small(jnp.pad(vec, ((0, -nrow % 8), (0, 0))))
    o = 0
    red = {}
    for k in ("mix_pre_g", "mix_post_g", "ffn_pre_g", "ffn_post_g", "ple_norm_g"):
        red[k] = vec[o:o + depth]
        o += depth
    red["kv_norm_g"] = vec[o:o + 1]
    red["pool_scale"] = lax.dynamic_slice(vec[o + 1:o + 1 + n_pool], (0, me[0] * (D // NSH)), (n_pool, D // NSH))
    o += 1 + n_pool
    red["sinks"] = vec[o:o + n_attn, :GQ * NKV]
    loss = vec[o + n_attn, 0]

    parts = [_sum_own_and_received(e["own"], e["got"], me, f"sum_{e['k']}_{e['l']}") for e in entries]
    sib = _swap_with_sibling(parts)

    given = dict(
        mix_pre_g=(mix_pre_g, m_mix_pre_g, v_mix_pre_g), mix_post_g=(mix_post_g, m_mix_post_g, v_mix_post_g),
        ffn_pre_g=(ffn_pre_g, m_ffn_pre_g, v_ffn_pre_g), ffn_post_g=(ffn_post_g, m_ffn_post_g, v_ffn_post_g),
        pool_w=(pool_w, m_pool_w, v_pool_w), pool_scale=(pool_scale, m_pool_scale, v_pool_scale),
        kv_norm_g=(kv_norm_g, m_kv_norm_g, v_kv_norm_g), w_k=(w_k, m_w_k, v_w_k), w_v=(w_v, m_w_v, v_w_v),
        w_q=(w_q, m_w_q, v_w_q), w_o=(w_o, m_w_o, v_w_o), sinks=(sinks, m_sinks, v_sinks),
        w_ff_gate=ff_t["w_ff_gate"], w_ff_up=ff_t["w_ff_up"],
        w_ff_down=(w_ff_down, m_w_ff_down, v_w_ff_down), ple_norm_g=(ple_norm_g, m_ple_norm_g, v_ple_norm_g),
        w_ple_gate=(w_ple_gate, m_w_ple_gate, v_w_ple_gate), w_ple_proj=(w_ple_proj, m_w_ple_proj, v_w_ple_proj))
    results = {}
    for k, g in red.items():
        w, m, v = given[k]
        outs = _adamw([[g.reshape(-1, g.shape[-1])]], *(t.reshape(1, -1, t.shape[-1]) for t in (w, m, v)), f"adamw_{k}")
        results[k] = [t.reshape(w.shape) for t in outs]
    per_layer = {}
    for e, a, b in zip(entries, parts, sib):
        per_layer.setdefault(e["k"], {})[e["l"] or 0] = [a, b]
    for k, layers in per_layer.items():
        w, m, v = given[k]
        gs = [layers[l] for l in range(len(layers))]
        outs = _adamw(gs, *(t.reshape((len(gs),) + gs[0][0].shape) for t in (w, m, v)), f"adamw_{k}")
        results[k] = [t12(t) if k in ff_t else t.reshape(w.shape) for t in outs]

    order = ["mix_pre_g", "mix_post_g", "ffn_pre_g", "ffn_post_g", "pool_w", "pool_scale", "kv_norm_g", "w_k", "w_v",
             "w_q", "w_o", "sinks", "w_ff_gate", "w_ff_up", "w_ff_down", "ple_norm_g", "w_ple_gate", "w_ple_proj"]
    return (loss, grad_x, *[results[k][0] for k in order], *[results[k][1] for k in order],
            *[results[k][2] for k in order], *[results[k][3] for k in order])
```

```python
import functools

import jax
import jax.numpy as jnp
from jax import lax
from jax.experimental import pallas as pl
from jax.experimental.pallas import tpu as pltpu

F32 = jnp.float32
CDT = jnp.bfloat16

D = 1024
PG = 256
POOL_WINDOWS = (2, 4, 8, 16)
HALO = 16
HD = 64
NKV = 4
WIN = 128
FS = 704
NSH = 4
PLE = 256
ROPE_THETA = 10000.0
EPS = 1e-6
NEG = -1e30
QSCALE = HD ** -0.5

ADAM_LR, ADAM_B1, ADAM_B2, ADAM_EPS, ADAM_WD, ADAM_STEP = 0.001, 0.9, 0.999, 1e-08, 0.01, 10

VMEM_LIMIT = 56 * 1024 * 1024
TM = 1024
TM_FFN = 512
TM_FFN_BWD = 256
TQ = 512
TT = 4096
MESH = pl.DeviceIdType.MESH


def _cp(n_axes=1):
    return pltpu.CompilerParams(dimension_semantics=("arbitrary",) * n_axes, vmem_limit_bytes=VMEM_LIMIT)


def _dot(a, b):
    return jnp.dot(a, b, preferred_element_type=F32)


def _dot_nt(a, b):
    return lax.dot_general(a, b, (((1,), (1,)), ((), ())), preferred_element_type=F32)


def _dot_tn(a, b):
    return lax.dot_general(a, b, (((0,), (0,)), ((), ())), preferred_element_type=F32)


def _rms_fwd(x, g):
    r = lax.rsqrt(jnp.mean(x * x, axis=-1, keepdims=True) + EPS)
    return x * r * g


def _rms_bwd(dy, x, g):
    r = lax.rsqrt(jnp.mean(x * x, axis=-1, keepdims=True) + EPS)
    xh = x * r
    dg = jnp.sum(dy * xh, axis=0, keepdims=True)
    u = dy * g
    dx = r * (u - xh * jnp.mean(u * xh, axis=-1, keepdims=True))
    return dx, dg


def _sigmoid(a):
    return 1.0 / (1.0 + jnp.exp(-a))


def _swap_half(x):
    w = x.shape[1]
    lane = lax.broadcasted_iota(jnp.int32, x.shape, 1)
    return jnp.where((lane % HD) < HD // 2, pltpu.roll(x, w - HD // 2, axis=1), pltpu.roll(x, HD // 2, axis=1))


def _rope_fwd(x, cos, sin):
    return x * cos + _swap_half(x) * sin


def _rope_bwd(d, cos, sin):
    return d * cos + _swap_half(d * sin)


def _tile(tm, w):
    return pl.BlockSpec((tm, w), lambda i: (i, 0))


def _res(shape):
    return pl.BlockSpec(shape, lambda i: (0,) * len(shape), pipeline_mode=pl.Buffered(1))


def _layer_w(shape):
    return pl.BlockSpec((NSH,) + shape, lambda *_: (0, 0, 0), pipeline_mode=pl.Buffered(1))


def _acc(w):
    return pl.BlockSpec((1, w), lambda i: (0, 0))


def _zero_first(*refs):
    @pl.when(pl.program_id(0) == 0)
    def _():
        for r in refs:
            r[...] = jnp.zeros_like(r)


HBM = pl.BlockSpec(memory_space=pl.ANY)


def _place():
    x, y, c = lax.axis_index("x"), lax.axis_index("y"), lax.axis_index("c")
    chips = [(1 - x, y), (x, 1 - y), (1 - x, 1 - y)]
    return x, y, c, chips


class _Comm:
    def __init__(self, args, out_shape, sems, copies):
        self.args, self.out_shape, self.sems, self.copies = list(args), list(out_shape), list(sems), copies

    def start(self, cin, cout, sems):
        local, sends, _ = self.copies(cin, cout, sems)
        for cp in local + sends:
            cp.start()

    def wait(self, cin, cout, sems):
        local, sends, recvs = self.copies(cin, cout, sems)
        for cp in recvs:
            cp.wait_recv()
        for cp in sends:
            cp.wait_send()
        for cp in local:
            cp.wait()


def _gather_comm(srcs):
    n = len(srcs)
    shapes = [(a.shape if l is None else a.shape[1:]) for a, l in srcs]

    def copies(cin, cout, sems):
        send, recv, loc = sems
        x, y, c, chips = _place()
        me = 2 * x + y
        local, sends, recvs = [], [], []
        for a, (_, l) in enumerate(srcs):
            src = cin[a] if l is None else cin[a].at[l]
            local.append(pltpu.make_async_copy(src, cout[a].at[me], loc.at[a]))
            for j, (px, py) in enumerate(chips):
                kw = dict(src_ref=src, send_sem=send.at[3 * a + j], recv_sem=recv.at[3 * a + j],
                          device_id=(px, py, c), device_id_type=MESH)
                sends.append(pltpu.make_async_remote_copy(dst_ref=cout[a].at[me], **kw))
                recvs.append(pltpu.make_async_remote_copy(dst_ref=cout[a].at[2 * px + py], **kw))
        return local, sends, recvs

    return _Comm([a for a, _ in srcs], [jax.ShapeDtypeStruct((NSH,) + s, CDT) for s in shapes],
                 [pltpu.SemaphoreType.DMA((3 * n,)), pltpu.SemaphoreType.DMA((3 * n,)), pltpu.SemaphoreType.DMA((n,))],
                 copies)


def _scatter_comm(grads):
    n = len(grads)

    def copies(cin, cout, sems):
        send, recv = sems
        x, y, c, chips = _place()
        sends, recvs = [], []
        for a in range(n):
            for j, (px, py) in enumerate(chips):
                cp = pltpu.make_async_remote_copy(
                    src_ref=cin[a].at[2 * px + py], dst_ref=cout[a].at[j], send_sem=send.at[3 * a + j],
                    recv_sem=recv.at[3 * a + j], device_id=(px, py, c), device_id_type=MESH)
                sends.append(cp)
                recvs.append(cp)
        return [], sends, recvs

    return _Comm(grads, [jax.ShapeDtypeStruct((3,) + g.shape[1:], g.dtype) for g in grads],
                 [pltpu.SemaphoreType.DMA((3 * n,)), pltpu.SemaphoreType.DMA((3 * n,))], copies)


def _swap_comm(parts):
    n = len(parts)

    def copies(cin, cout, sems):
        send, recv = sems
        x, y, c, _ = _place()
        cps = [pltpu.make_async_remote_copy(
            src_ref=cin[a], dst_ref=cout[a], send_sem=send.at[a], recv_sem=recv.at[a],
            device_id=(x, y, 1 - c), device_id_type=MESH) for a in range(n)]
        return [], cps, cps

    return _Comm(parts, [jax.ShapeDtypeStruct(p.shape, p.dtype) for p in parts],
                 [pltpu.SemaphoreType.DMA((n,)), pltpu.SemaphoreType.DMA((n,))], copies)


def _join_comms(comms):
    comms = [c for c in comms if c is not None]
    if len(comms) <= 1:
        return comms[0] if comms else None

    def copies(cin, cout, sems):
        out = ([], [], [])
        i = o = k = 0
        for c in comms:
            part = c.copies(cin[i:i + len(c.args)], cout[o:o + len(c.out_shape)], sems[k:k + len(c.sems)])
            for acc, cps in zip(out, part):
                acc.extend(cps)
            i, o, k = i + len(c.args), o + len(c.out_shape), k + len(c.sems)
        return out

    return _Comm([a for c in comms for a in c.args], [s for c in comms for s in c.out_shape],
                 [s for c in comms for s in c.sems], copies)


def _run_comm(comm, name):
    k_in, k_out = len(comm.args), len(comm.out_shape)

    def body(*refs):
        cin, cout, sems = refs[:k_in], refs[k_in:k_in + k_out], refs[k_in + k_out:]
        comm.start(cin, cout, sems)
        comm.wait(cin, cout, sems)

    return pl.pallas_call(body, name=name, in_specs=[HBM] * k_in, out_specs=[HBM] * k_out, out_shape=comm.out_shape,
                          scratch_shapes=comm.sems)(*comm.args)


def _pallas(body, *, name, grid, in_specs, out_specs, out_shape, args, scratch=(), comm=None):
    out_specs, out_shape, scratch = list(out_specs), list(out_shape), list(scratch)
    params = _cp(len(grid))
    if comm is None:
        return pl.pallas_call(body, grid=grid, name=name, in_specs=in_specs, out_specs=out_specs, out_shape=out_shape,
                              scratch_shapes=scratch, compiler_params=params)(*args), []
    n_in, n_out, n_sc = len(args), len(out_shape), len(scratch)
    k_in, k_out = len(comm.args), len(comm.out_shape)

    def carrying(*refs):
        ins, refs = refs[:n_in], refs[n_in:]
        cin, refs = refs[:k_in], refs[k_in:]
        outs, refs = refs[:n_out], refs[n_out:]
        cout, refs = refs[:k_out], refs[k_out:]
        sc, sems = refs[:n_sc], refs[n_sc:]
        first = functools.reduce(jnp.logical_and, [pl.program_id(d) == 0 for d in range(len(grid))])
        last = functools.reduce(jnp.logical_and, [pl.program_id(d) == g - 1 for d, g in enumerate(grid)])

        @pl.when(first)
        def _():
            comm.start(cin, cout, sems)

        body(*ins, *outs, *sc)

        @pl.when(last)
        def _():
            comm.wait(cin, cout, sems)

    res = pl.pallas_call(
        carrying, grid=grid, name=name, in_specs=list(in_specs) + [HBM] * k_in, out_specs=out_specs + [HBM] * k_out,
        out_shape=out_shape + comm.out_shape, scratch_shapes=scratch + comm.sems, compiler_params=params,
    )(*args, *comm.args)
    return res[:n_out], res[n_out:]


def _pool_normed_ext(h_ref, halo_ref, g, tm):
    xe = jnp.concatenate([halo_ref[...], h_ref[...]], axis=0)
    hn = _rms_fwd(xe, g)
    row = lax.broadcasted_iota(jnp.int32, (tm + HALO, 1), 0)
    return jnp.where((row >= HALO) | (pl.program_id(0) > 0), hn, 0.0)


def _pool_windows(hn_e, tm):
    t = lax.broadcasted_iota(jnp.int32, (tm, PG), 0) + pl.program_id(0) * tm
    outs = []
    for g, w in enumerate(POOL_WINDOWS):
        s = hn_e[:, g * PG:(g + 1) * PG]
        x = s[HALO:, :]
        k = 1
        while k < w:
            s = s + pltpu.roll(s, k, axis=0)
            k *= 2
        cnt = jnp.minimum(t + 1, w).astype(F32)
        outs.append(s[HALO:, :] / cnt - x)
    return jnp.concatenate(outs, axis=1)


def _pool_apply(pooled, w_ref):
    return jnp.concatenate([_dot(pooled[:, g * PG:(g + 1) * PG], w_ref[g]) for g in range(len(POOL_WINDOWS))], axis=1)


def _halo_prev(tm):
    return pl.BlockSpec((HALO, D), lambda i: (jnp.maximum(i * (tm // HALO) - 1, 0), 0))


def _pool_fwd(h, g_pre, pw, pscale, g_post, l, comm=None):
    T = h.shape[0]
    tm = min(TM, T)

    def body(h_ref, halo_ref, gpre_ref, w_ref, sc_ref, gpost_ref, o_ref):
        hn_e = _pool_normed_ext(h_ref, halo_ref, gpre_ref[l:l + 1, :], tm)
        pooled = _pool_windows(hn_e, tm).astype(CDT)
        m = _pool_apply(pooled, w_ref) * sc_ref[l:l + 1, :]
        o_ref[...] = h_ref[...] + _rms_fwd(m, gpost_ref[l:l + 1, :])

    return _pallas(
        body, grid=(T // tm,), name=f"pool_fwd_{l}",
        in_specs=[_tile(tm, D), _halo_prev(tm), _res(g_pre.shape),
                  pl.BlockSpec((None,) + pw.shape[1:], lambda i: (l, 0, 0, 0), pipeline_mode=pl.Buffered(1)),
                  _res(pscale.shape), _res(g_post.shape)],
        out_specs=[_tile(tm, D)], out_shape=[jax.ShapeDtypeStruct((T, D), F32)],
        args=(h, h, g_pre, pw, pscale, g_post), comm=comm)


def _pool_bwd_a(dh1, h, g_pre, pw, pscale, g_post, l, comm=None):
    T = h.shape[0]
    tm = min(TM, T)
    ng = len(POOL_WINDOWS)

    def body(dh_ref, h_ref, halo_ref, gpre_ref, w_ref, sc_ref, gpost_ref, dp_ref, dw_ref, dsc_ref, dgpost_ref):
        _zero_first(dw_ref, dsc_ref, dgpost_ref)
        hn_e = _pool_normed_ext(h_ref, halo_ref, gpre_ref[l:l + 1, :], tm)
        pooled = _pool_windows(hn_e, tm).astype(CDT)
        y = _pool_apply(pooled, w_ref)
        sc = sc_ref[l:l + 1, :]
        dm, dg = _rms_bwd(dh_ref[...], y * sc, gpost_ref[l:l + 1, :])
        dgpost_ref[...] += dg
        dsc_ref[...] += jnp.sum(dm * y, axis=0, keepdims=True)
        dy = (dm * sc).astype(CDT)
        dps = []
        for g in range(ng):
            dyg = dy[:, g * PG:(g + 1) * PG]
            dw_ref[g] += _dot_tn(pooled[:, g * PG:(g + 1) * PG], dyg)
            dps.append(_dot_nt(dyg, w_ref[g]))
        dp_ref[...] = jnp.concatenate(dps, axis=1)

    return _pallas(
        body, grid=(T // tm,), name=f"pool_bwd_a_{l}",
        in_specs=[_tile(tm, D), _tile(tm, D), _halo_prev(tm), _res(g_pre.shape),
                  pl.BlockSpec((None,) + pw.shape[1:], lambda i: (l, 0, 0, 0), pipeline_mode=pl.Buffered(1)),
                  _res(pscale.shape), _res(g_post.shape)],
        out_specs=[_tile(tm, D), pl.BlockSpec((ng, PG, PG), lambda i: (0, 0, 0)), _acc(D), _acc(D)],
        out_shape=[jax.ShapeDtypeStruct((T, D), F32), jax.ShapeDtypeStruct((ng, PG, PG), F32),
                   jax.ShapeDtypeStruct((1, D), F32), jax.ShapeDtypeStruct((1, D), F32)],
        args=(dh1, h, h, g_pre, pw, pscale, g_post), comm=comm)


def _pool_bwd_b(dpool, dh1, h, g_pre, l, comm=None):
    T = h.shape[0]
    tm = min(TM, T)
    nt = T // tm

    def body(dp_ref, nxt_ref, dh_ref, h_ref, gpre_ref, o_ref, dgpre_ref):
        _zero_first(dgpre_ref)
        i = pl.program_id(0)
        dp = dp_ref[...]
        e_e = jnp.concatenate([dp, jnp.where(i < nt - 1, nxt_ref[...], 0.0)], axis=0)
        t = lax.broadcasted_iota(jnp.int32, (tm + HALO, PG), 0) + i * tm
        outs = []
        for g, w in enumerate(POOL_WINDOWS):
            s = e_e[:, g * PG:(g + 1) * PG] / jnp.minimum(t + 1, w).astype(F32)
            k = 1
            while k < w:
                s = s + pltpu.roll(s, tm + HALO - k, axis=0)
                k *= 2
            outs.append(s[:tm, :] - dp[:, g * PG:(g + 1) * PG])
        dx, dg = _rms_bwd(jnp.concatenate(outs, axis=1), h_ref[...], gpre_ref[l:l + 1, :])
        dgpre_ref[...] += dg
        o_ref[...] = dh_ref[...] + dx

    nxt = pl.BlockSpec((HALO, D), lambda i: (jnp.minimum((i + 1) * (tm // HALO), T // HALO - 1), 0))
    return _pallas(
        body, grid=(nt,), name=f"pool_bwd_b_{l}",
        in_specs=[_tile(tm, D), nxt, _tile(tm, D), _tile(tm, D), _res(g_pre.shape)],
        out_specs=[_tile(tm, D), _acc(D)],
        out_shape=[jax.ShapeDtypeStruct((T, D), F32), jax.ShapeDtypeStruct((1, D), F32)],
        args=(dpool, dpool, dh1, h, g_pre), comm=comm)


def _proj_rows(x, w_ref):
    k = w_ref.shape[1]
    out = _dot(x[:, :k], w_ref[0])
    for s in range(1, NSH):
        out = out + _dot(x[:, s * k:(s + 1) * k], w_ref[s])
    return out


def _proj_rows_t(dy, w_ref):
    return jnp.concatenate([_dot_nt(dy, w_ref[s]) for s in range(NSH)], axis=1)


def _rope_tiles(cos_ref, sin_ref, width):
    reps = width // cos_ref.shape[1]
    return jnp.tile(cos_ref[...], (1, reps)), jnp.tile(sin_ref[...], (1, reps))


def _q_fwd(h, g_pre, wq, cos, sin, l, j):
    T = h.shape[0]
    tm = min(TM, T)

    def body(h_ref, g_ref, w_ref, cos_ref, sin_ref, q_ref):
        hn = _rms_fwd(h_ref[...], g_ref[l:l + 1, :]).astype(CDT)
        c, s = _rope_tiles(cos_ref, sin_ref, D)
        q_ref[...] = (_rope_fwd(_proj_rows(hn, w_ref), c, s) * QSCALE).astype(CDT)

    return pl.pallas_call(
        body, grid=(T // tm,), name=f"q_fwd_{l}",
        in_specs=[_tile(tm, D), _res(g_pre.shape), _layer_w((D // NSH, D)), _tile(tm, 2 * HD), _tile(tm, 2 * HD)],
        out_specs=_tile(tm, D), out_shape=jax.ShapeDtypeStruct((T, D), CDT), compiler_params=_cp(),
    )(h, g_pre, wq, cos, sin)


def _q_bwd(dq, dh1, h, g_pre, wq, cos, sin, l, j):
    T = h.shape[0]
    tm = min(TM, T)

    def body(dq_ref, dh_ref, h_ref, g_ref, w_ref, cos_ref, sin_ref, o_ref, hn_ref, dq0_ref, dg_ref):
        _zero_first(dg_ref)
        g = g_ref[l:l + 1, :]
        x = h_ref[...]
        hn_ref[...] = _rms_fwd(x, g).astype(CDT)
        c, s = _rope_tiles(cos_ref, sin_ref, D)
        dq0 = _rope_bwd(dq_ref[...].astype(F32) * QSCALE, c, s).astype(CDT)
        dq0_ref[...] = dq0
        dx, dg = _rms_bwd(_proj_rows_t(dq0, w_ref), x, g)
        dg_ref[...] += dg
        o_ref[...] = dh_ref[...] + dx

    return pl.pallas_call(
        body, grid=(T // tm,), name=f"q_bwd_{l}",
        in_specs=[_tile(tm, D), _tile(tm, D), _tile(tm, D), _res(g_pre.shape), _layer_w((D // NSH, D)),
                  _tile(tm, 2 * HD), _tile(tm, 2 * HD)],
        out_specs=[_tile(tm, D), _tile(tm, D), _tile(tm, D), _acc(D)],
        out_shape=[jax.ShapeDtypeStruct((T, D), F32), jax.ShapeDtypeStruct((T, D), CDT),
                   jax.ShapeDtypeStruct((T, D), CDT), jax.ShapeDtypeStruct((1, D), F32)],
        compiler_params=_cp(),
    )(dq, dh1, h, g_pre, wq, cos, sin)


def _o_fwd(o, h, wo, g_post, l, j):
    T = h.shape[0]
    tm = min(TM, T)

    def body(o_ref, h_ref, w_ref, g_ref, h1_ref, m_ref):
        m = _proj_rows(o_ref[...], w_ref)
        m_ref[...] = m.astype(CDT)
        h1_ref[...] = h_ref[...] + _rms_fwd(m, g_ref[l:l + 1, :])

    return pl.pallas_call(
        body, grid=(T // tm,), name=f"o_fwd_{l}",
        in_specs=[_tile(tm, D), _tile(tm, D), _layer_w((D // NSH, D)), _res(g_post.shape)],
        out_specs=[_tile(tm, D), _tile(tm, D)],
        out_shape=[jax.ShapeDtypeStruct((T, D), F32), jax.ShapeDtypeStruct((T, D), CDT)], compiler_params=_cp(),
    )(o, h, wo, g_post)


def _o_bwd(dh1, m, wo, g_post, l, j):
    T = m.shape[0]
    tm = min(TM, T)

    def body(dh_ref, m_ref, w_ref, g_ref, do_ref, dm_ref, dg_ref):
        _zero_first(dg_ref)
        dm, dg = _rms_bwd(dh_ref[...], m_ref[...].astype(F32), g_ref[l:l + 1, :])
        dg_ref[...] += dg
        dmc = dm.astype(CDT)
        dm_ref[...] = dmc
        do_ref[...] = _proj_rows_t(dmc, w_ref).astype(CDT)

    return pl.pallas_call(
        body, grid=(T // tm,), name=f"o_bwd_{l}",
        in_specs=[_tile(tm, D), _tile(tm, D), _layer_w((D // NSH, D)), _res(g_post.shape)],
        out_specs=[_tile(tm, D), _tile(tm, D), _acc(D)],
        out_shape=[jax.ShapeDtypeStruct((T, D), CDT), jax.ShapeDtypeStruct((T, D), CDT),
                   jax.ShapeDtypeStruct((1, D), F32)],
        compiler_params=_cp(),
    )(dh1, m, wo, g_post)


KVW = 2 * NKV * HD


def _kv_fwd(h, g_kv, wk2, wv2, cos, sin):
    T = h.shape[0]
    tm = min(TM, T)

    def body(h_ref, g_ref, wk_ref, wv_ref, cos_ref, sin_ref, k_ref, v_ref):
        hk = _rms_fwd(h_ref[...], g_ref[...]).astype(CDT)
        c, s = _rope_tiles(cos_ref, sin_ref, KVW)
        k_ref[...] = _rope_fwd(_dot(hk, wk_ref[...]), c, s).astype(CDT)
        v_ref[...] = _dot(hk, wv_ref[...]).astype(CDT)

    return pl.pallas_call(
        body, grid=(T // tm,), name="kv_fwd",
        in_specs=[_tile(tm, D), _res(g_kv.shape), _res(wk2.shape), _res(wv2.shape), _tile(tm, 2 * HD), _tile(tm, 2 * HD)],
        out_specs=[_tile(tm, KVW), _tile(tm, KVW)],
        out_shape=[jax.ShapeDtypeStruct((T, KVW), CDT)] * 2, compiler_params=_cp(),
    )(h, g_kv, wk2, wv2, cos, sin)


def _kv_bwd(dh, h, dks, dkhs, dvs, dvhs, g_kv, wk2, wv2, cos, sin):
    T = h.shape[0]
    tq = min(TQ, T)
    nt = T // tq
    n = len(dks)

    def body(*refs):
        dh_ref, h_ref = refs[:2]
        main = refs[2:2 + 2 * n]
        halo = refs[2 + 2 * n:2 + 4 * n]
        g_ref, wk_ref, wv_ref, cos_ref, sin_ref, o_ref, hk_ref, dk0_ref, dv_ref, dg_ref = refs[2 + 4 * n:]
        _zero_first(dg_ref)
        i = pl.program_id(0)

        def total(mains, halos):
            d = mains[0][...]
            for r in mains[1:]:
                d = d + r[...]
            hl = halos[0][...]
            for r in halos[1:]:
                hl = hl + r[...]
            row = lax.broadcasted_iota(jnp.int32, (tq, 1), 0)
            if tq > WIN:
                hl = jnp.concatenate([jnp.zeros((tq - WIN, KVW), F32), hl], axis=0)
            return d + jnp.where((row >= tq - WIN) & (i < nt - 1), hl, 0.0)

        dk = total(main[:n], halo[:n])
        dv = total(main[n:], halo[n:])
        x = h_ref[...]
        g = g_ref[...]
        hk_ref[...] = _rms_fwd(x, g).astype(CDT)
        c, s = _rope_tiles(cos_ref, sin_ref, KVW)
        dk0 = _rope_bwd(dk, c, s).astype(CDT)
        dvc = dv.astype(CDT)
        dk0_ref[...] = dk0
        dv_ref[...] = dvc
        dx, dg = _rms_bwd(_dot_nt(dk0, wk_ref[...]) + _dot_nt(dvc, wv_ref[...]), x, g)
        dg_ref[...] += dg
        o_ref[...] = dh_ref[...] + dx

    nxt = pl.BlockSpec((WIN, KVW), lambda i: (jnp.minimum(i + 1, nt - 1), 0))
    return pl.pallas_call(
        body, grid=(nt,), name="kv_bwd",
        in_specs=[_tile(tq, D), _tile(tq, D)] + [_tile(tq, KVW)] * (2 * n) + [nxt] * (2 * n)
        + [_res(g_kv.shape), _res(wk2.shape), _res(wv2.shape), _tile(tq, 2 * HD), _tile(tq, 2 * HD)],
        out_specs=[_tile(tq, D), _tile(tq, D), _tile(tq, KVW), _tile(tq, KVW), _acc(D)],
        out_shape=[jax.ShapeDtypeStruct((T, D), F32), jax.ShapeDtypeStruct((T, D), CDT),
                   jax.ShapeDtypeStruct((T, KVW), CDT), jax.ShapeDtypeStruct((T, KVW), CDT),
                   jax.ShapeDtypeStruct((1, D), F32)],
        compiler_params=_cp(),
    )(dh, h, *dks, *dvs, *dkhs, *dvhs, g_kv, wk2, wv2, cos, sin)


GQ = 4
LW = 2 * HD


def _from_prev():
    r = lax.broadcasted_iota(jnp.int32, (GQ * WIN, WIN), 0) % WIN
    j = lax.broadcasted_iota(jnp.int32, (GQ * WIN, WIN), 1)
    return j > r


def _fold(x2, prev):
    return jnp.where(prev, x2[:, :WIN], x2[:, WIN:])


def _unfold(x, prev):
    zero = jnp.zeros_like(x)
    return jnp.concatenate([jnp.where(prev, x, zero), jnp.where(prev, zero, x)], axis=1)


def _stack_heads(ref, rows, g):
    lane = lax.broadcasted_iota(jnp.int32, (WIN, LW), 1)
    parts = []
    for pr in range(2):
        x = ref[rows, (2 * g + pr) * LW:(2 * g + pr + 1) * LW]
        parts += [jnp.where(lane < HD, x, jnp.zeros_like(x)), jnp.where(lane >= HD, x, jnp.zeros_like(x))]
    return jnp.concatenate(parts, axis=0)


def _unstack_heads(x4):
    lane = lax.broadcasted_iota(jnp.int32, (WIN, LW), 1)
    return [jnp.where(lane < HD, x4[(2 * pr) * WIN:(2 * pr + 1) * WIN], x4[(2 * pr + 1) * WIN:(2 * pr + 2) * WIN])
            for pr in range(2)]


def _sink_col(sink_ref, j, g):
    blk = lax.broadcasted_iota(jnp.int32, (GQ * WIN, 1), 0) // WIN
    col = jnp.zeros((GQ * WIN, 1), F32)
    for a in range(GQ):
        col = jnp.where(blk == a, sink_ref[j, GQ * g + a], col)
    return col


def _softmax_sink(s, hidden, sink):
    if hidden is not None:
        s = jnp.where(hidden, NEG, s)
    m = jnp.maximum(jnp.max(s, axis=-1, keepdims=True), sink)
    e = jnp.exp(s - m)
    es = jnp.exp(sink - m)
    l = jnp.sum(e, axis=-1, keepdims=True) + es
    return e / l, es / l


def _kv_halo(tq):
    return pl.BlockSpec((WIN, KVW), lambda i: (jnp.maximum(i * (tq // WIN) - 1, 0), 0))


def _attn_fwd(q, kk, vv, sinks, l, j):
    T = q.shape[0]
    tq = min(TQ, T)

    def body(sink_ref, q_ref, k_ref, kh_ref, v_ref, vh_ref, o_ref):
        i = pl.program_id(0)
        ke = jnp.concatenate([kh_ref[...], k_ref[...]], axis=0)
        ve = jnp.concatenate([vh_ref[...], v_ref[...]], axis=0)
        prev = _from_prev()
        for n in range(tq // WIN):
            rows = slice(n * WIN, (n + 1) * WIN)
            hidden = prev & (i == 0) if n == 0 else None
            for g in range(NKV):
                kg = ke[n * WIN:(n + 2) * WIN, g * LW:(g + 1) * LW]
                vg = ve[n * WIN:(n + 2) * WIN, g * LW:(g + 1) * LW]
                s = _fold(_dot_nt(_stack_heads(q_ref, rows, g), kg), prev)
                p, _ = _softmax_sink(s, hidden, _sink_col(sink_ref, j, g))
                pairs = _unstack_heads(_dot(_unfold(p.astype(CDT), prev), vg))
                for pr in range(2):
                    o_ref[rows, (2 * g + pr) * LW:(2 * g + pr + 1) * LW] = pairs[pr].astype(CDT)

    return pl.pallas_call(
        body, grid=(T // tq,), name=f"attn_fwd_{l}",
        in_specs=[pl.BlockSpec(memory_space=pltpu.SMEM), _tile(tq, D), _tile(tq, KVW), _kv_halo(tq),
                  _tile(tq, KVW), _kv_halo(tq)],
        out_specs=_tile(tq, D), out_shape=jax.ShapeDtypeStruct((T, D), CDT), compiler_params=_cp(),
    )(sinks, q, kk, kk, vv, vv)


def _attn_bwd(q, kk, vv, o, do, sinks, l, j):
    T = q.shape[0]
    tq = min(TQ, T)
    nt = T // tq

    def body(sink_ref, q_ref, k_ref, kh_ref, v_ref, vh_ref, o_ref, do_ref,
             dq_ref, dk_ref, dkh_ref, dv_ref, dvh_ref, dsink_ref, dke, dve):
        _zero_first(dsink_ref)
        i = pl.program_id(0)
        ke = jnp.concatenate([kh_ref[...], k_ref[...]], axis=0)
        ve = jnp.concatenate([vh_ref[...], v_ref[...]], axis=0)
        dke[...] = jnp.zeros_like(dke)
        dve[...] = jnp.zeros_like(dve)
        lane = lax.broadcasted_iota(jnp.int32, (1, LW), 1)
        blk = lax.broadcasted_iota(jnp.int32, (GQ * WIN, 1), 0) // WIN
        dsink = jnp.zeros((1, LW), F32)
        prev = _from_prev()
        for n in range(tq // WIN):
            rows = slice(n * WIN, (n + 1) * WIN)
            krows = slice(n * WIN, (n + 2) * WIN)
            hidden = prev & (i == 0) if n == 0 else None
            for g in range(NKV):
                cols = slice(g * LW, (g + 1) * LW)
                kg = ke[krows, cols]
                vg = ve[krows, cols]
                q4 = _stack_heads(q_ref, rows, g)
                do4 = _stack_heads(do_ref, rows, g)
                o4 = _stack_heads(o_ref, rows, g)
                p, ps = _softmax_sink(_fold(_dot_nt(q4, kg), prev), hidden, _sink_col(sink_ref, j, g))
                delta = jnp.sum(do4.astype(F32) * o4.astype(F32), axis=-1, keepdims=True)
                ds = _unfold((p * (_fold(_dot_nt(do4, vg), prev) - delta)).astype(CDT), prev)
                pc = _unfold(p.astype(CDT), prev)
                pairs = _unstack_heads(_dot(ds, kg))
                for pr in range(2):
                    dq_ref[rows, (2 * g + pr) * LW:(2 * g + pr + 1) * LW] = pairs[pr].astype(CDT)
                dke[krows, cols] += _dot_tn(ds, q4)
                dve[krows, cols] += _dot_tn(pc, do4)
                t = ps * delta
                for a in range(GQ):
                    dsink = dsink - jnp.where(lane == GQ * g + a, jnp.sum(jnp.where(blk == a, t, 0.0)), 0.0)
        dsink_ref[...] += dsink
        dkh_ref[...] = dke[:WIN, :]
        dk_ref[...] = dke[WIN:, :]
        dvh_ref[...] = dve[:WIN, :]
        dv_ref[...] = dve[WIN:, :]

    halo_out = pl.BlockSpec((WIN, KVW), lambda i: (i, 0))
    return pl.pallas_call(
        body, grid=(nt,), name=f"attn_bwd_{l}",
        in_specs=[pl.BlockSpec(memory_space=pltpu.SMEM), _tile(tq, D), _tile(tq, KVW), _kv_halo(tq),
                  _tile(tq, KVW), _kv_halo(tq), _tile(tq, D), _tile(tq, D)],
        out_specs=[_tile(tq, D), _tile(tq, KVW), halo_out, _tile(tq, KVW), halo_out, _acc(LW)],
        out_shape=[jax.ShapeDtypeStruct((T, D), CDT), jax.ShapeDtypeStruct((T, KVW), F32),
                   jax.ShapeDtypeStruct((nt * WIN, KVW), F32), jax.ShapeDtypeStruct((T, KVW), F32),
                   jax.ShapeDtypeStruct((nt * WIN, KVW), F32), jax.ShapeDtypeStruct((1, LW), F32)],
        scratch_shapes=[pltpu.VMEM((WIN + tq, KVW), F32), pltpu.VMEM((WIN + tq, KVW), F32)],
        compiler_params=_cp(),
    )(sinks, q, kk, kk, vv, vv, o, do)


def _chunks(tm):
    return pl.BlockSpec((NSH, tm, FS), lambda i: (0, i, 0))


def _ffn_fwd(h1, g_pre, wg, wu, wd, g_post, l, comm=None):
    T = h1.shape[0]
    tm = min(TM_FFN, T)

    def body(h_ref, gpre_ref, wg_ref, wu_ref, wd_ref, gpost_ref, h2_ref, a_ref, b_ref, f_ref):
        x = h_ref[...]
        fn = _rms_fwd(x, gpre_ref[l:l + 1, :]).astype(CDT)
        f = jnp.zeros((tm, D), F32)
        for s in range(NSH):
            a = _dot_nt(fn, wg_ref[s])
            b = _dot_nt(fn, wu_ref[s])
            a_ref[s] = a.astype(CDT)
            b_ref[s] = b.astype(CDT)
            f = f + _dot((a * _sigmoid(a) * b).astype(CDT), wd_ref[s])
        f_ref[...] = f.astype(CDT)
        h2_ref[...] = x + _rms_fwd(f, gpost_ref[l:l + 1, :])

    return _pallas(
        body, grid=(T // tm,), name=f"ffn_fwd_{l}",
        in_specs=[_tile(tm, D), _res(g_pre.shape), _layer_w((FS, D)), _layer_w((FS, D)), _layer_w((FS, D)),
                  _res(g_post.shape)],
        out_specs=[_tile(tm, D), _chunks(tm), _chunks(tm), _tile(tm, D)],
        out_shape=[jax.ShapeDtypeStruct((T, D), F32), jax.ShapeDtypeStruct((NSH, T, FS), CDT),
                   jax.ShapeDtypeStruct((NSH, T, FS), CDT), jax.ShapeDtypeStruct((T, D), CDT)],
        args=(h1, g_pre, wg, wu, wd, g_post), comm=comm)


def _ffn_bwd(dh2, h1, f, a, b, g_pre, wg, wu, wd, g_post, l, comm=None):
    T = h1.shape[0]
    tm = min(TM_FFN_BWD, T)

    def body(dh_ref, h_ref, f_ref, a_ref, b_ref, gpre_ref, wg_ref, wu_ref, wd_ref, gpost_ref,
             dh1_ref, fn_ref, df_ref, act_ref, da_ref, db_ref, dgpre_ref, dgpost_ref):
        _zero_first(dgpre_ref, dgpost_ref)
        x = h_ref[...]
        gpre = gpre_ref[l:l + 1, :]
        fn_ref[...] = _rms_fwd(x, gpre).astype(CDT)
        dh2 = dh_ref[...]
        df, dg = _rms_bwd(dh2, f_ref[...].astype(F32), gpost_ref[l:l + 1, :])
        dgpost_ref[...] += dg
        dfc = df.astype(CDT)
        df_ref[...] = dfc
        dfn = jnp.zeros((tm, D), F32)
        for s in range(NSH):
            av = a_ref[s].astype(F32)
            bv = b_ref[s].astype(F32)
            sg = _sigmoid(av)
            silu = av * sg
            act_ref[s] = (silu * bv).astype(CDT)
            dact = _dot_nt(dfc, wd_ref[s])
            da = (dact * bv * (sg * (1.0 + av * (1.0 - sg)))).astype(CDT)
            db = (dact * silu).astype(CDT)
            da_ref[s] = da
            db_ref[s] = db
            dfn = dfn + _dot(da, wg_ref[s]) + _dot(db, wu_ref[s])
        dx, dg = _rms_bwd(dfn, x, gpre)
        dgpre_ref[...] += dg
        dh1_ref[...] = dh2 + dx

    return _pallas(
        body, grid=(T // tm,), name=f"ffn_bwd_{l}",
        in_specs=[_tile(tm, D), _tile(tm, D), _tile(tm, D), _chunks(tm), _chunks(tm), _res(g_pre.shape),
                  _layer_w((FS, D)), _layer_w((FS, D)), _layer_w((FS, D)), _res(g_post.shape)],
        out_specs=[_tile(tm, D), _tile(tm, D), _tile(tm, D), _chunks(tm), _chunks(tm), _chunks(tm), _acc(D), _acc(D)],
        out_shape=[jax.ShapeDtypeStruct((T, D), F32), jax.ShapeDtypeStruct((T, D), CDT), jax.ShapeDtypeStruct((T, D), CDT),
                   jax.ShapeDtypeStruct((NSH, T, FS), CDT), jax.ShapeDtypeStruct((NSH, T, FS), CDT),
                   jax.ShapeDtypeStruct((NSH, T, FS), CDT), jax.ShapeDtypeStruct((1, D), F32),
                   jax.ShapeDtypeStruct((1, D), F32)],
        args=(dh2, h1, f, a, b, g_pre, wg, wu, wd, g_post), comm=comm)


def _ple_proj(p, wpp_ref):
    return jnp.concatenate([_dot(p, wpp_ref[s]) for s in range(NSH)], axis=1)


def _ple_fwd(h2, p, g_ple, wpg, wpp, l, comm=None):
    T = h2.shape[0]
    tm = min(TM, T)

    def body(h_ref, p_ref, g_ref, wpg_ref, wpp_ref, o_ref):
        x = h_ref[...]
        pn = _rms_fwd(x, g_ref[l:l + 1, :]).astype(CDT)
        gate = _sigmoid(_proj_rows(pn, wpg_ref))
        o_ref[...] = x + _ple_proj(p_ref[...].astype(CDT), wpp_ref) * gate

    return _pallas(
        body, grid=(T // tm,), name=f"ple_fwd_{l}",
        in_specs=[_tile(tm, D), pl.BlockSpec((None, tm, PLE), lambda i: (l, i, 0)), _res(g_ple.shape),
                  _layer_w((D // NSH, D)), _layer_w((PLE, D // NSH))],
        out_specs=[_tile(tm, D)], out_shape=[jax.ShapeDtypeStruct((T, D), F32)],
        args=(h2, p, g_ple, wpg, wpp), comm=comm)


def _ple_bwd(dh3, h2, p, g_ple, wpg, wpp, l):
    T = h2.shape[0]
    tm = min(TM, T)

    def body(dh_ref, h_ref, p_ref, g_ref, wpg_ref, wpp_ref, o_ref, pn_ref, dz_ref, dpp_ref, pc_ref, dg_ref):
        _zero_first(dg_ref)
        x = h_ref[...]
        g = g_ref[l:l + 1, :]
        pn = _rms_fwd(x, g).astype(CDT)
        pn_ref[...] = pn
        gate = _sigmoid(_proj_rows(pn, wpg_ref))
        pc = p_ref[...].astype(CDT)
        pc_ref[...] = pc
        pp = _ple_proj(pc, wpp_ref)
        dh3 = dh_ref[...]
        dpp_ref[...] = (dh3 * gate).astype(CDT)
        dz = (dh3 * pp * gate * (1.0 - gate)).astype(CDT)
        dz_ref[...] = dz
        dx, dg = _rms_bwd(_proj_rows_t(dz, wpg_ref), x, g)
        dg_ref[...] += dg
        o_ref[...] = dh3 + dx

    return pl.pallas_call(
        body, grid=(T // tm,), name=f"ple_bwd_{l}",
        in_specs=[_tile(tm, D), _tile(tm, D), pl.BlockSpec((None, tm, PLE), lambda i: (l, i, 0)), _res(g_ple.shape),
                  _layer_w((D // NSH, D)), _layer_w((PLE, D // NSH))],
        out_specs=[_tile(tm, D), _tile(tm, D), _tile(tm, D), _tile(tm, D), _tile(tm, PLE), _acc(D)],
        out_shape=[jax.ShapeDtypeStruct((T, D), F32), jax.ShapeDtypeStruct((T, D), CDT), jax.ShapeDtypeStruct((T, D), CDT),
                   jax.ShapeDtypeStruct((T, D), CDT), jax.ShapeDtypeStruct((T, PLE), CDT), jax.ShapeDtypeStruct((1, D), F32)],
        compiler_params=_cp(),
    )(dh3, h2, p, g_ple, wpg, wpp)


def _loss_grad(y, tgt):
    T = y.shape[0]
    tm = min(TM, T)

    def body(y_ref, t_ref, dy_ref, loss_ref):
        _zero_first(loss_ref)
        err = y_ref[...] - t_ref[...]
        dy_ref[...] = err * (1.0 / D)
        lane = lax.broadcasted_iota(jnp.int32, (1, 128), 1)
        loss_ref[...] += jnp.where(lane == 0, (0.5 / D) * jnp.sum(err * err), 0.0)

    return pl.pallas_call(
        body, grid=(T // tm,), name="loss_grad", in_specs=[_tile(tm, D), _tile(tm, D)],
        out_specs=[_tile(tm, D), _acc(128)],
        out_shape=[jax.ShapeDtypeStruct((T, D), F32), jax.ShapeDtypeStruct((1, 128), F32)], compiler_params=_cp(),
    )(y, tgt)


def _mm_tn(x, dy, name, n_split=1, comm=None):
    xb, yb = x.ndim == 3, dy.ndim == 3
    T, K = x.shape[-2:]
    N = dy.shape[-1] // n_split
    B = x.shape[0] if xb else dy.shape[0] if yb else n_split
    tt = min(TT, T)
    nt = T // tt

    def body(x_ref, dy_ref, o_ref, oc_ref):
        t = pl.program_id(1)

        @pl.when(t == 0)
        def _():
            o_ref[...] = jnp.zeros_like(o_ref)

        o_ref[...] += _dot_tn(x_ref[...].astype(CDT), dy_ref[...])

        @pl.when(t == nt - 1)
        def _():
            oc_ref[...] = o_ref[...].astype(CDT)

    x_spec = pl.BlockSpec((None, tt, K), lambda b, t: (b, t, 0)) if xb else pl.BlockSpec((tt, K), lambda b, t: (t, 0))
    if yb:
        y_spec = pl.BlockSpec((None, tt, N), lambda b, t: (b, t, 0))
    else:
        y_spec = pl.BlockSpec((tt, N), lambda b, t: (t, b if n_split > 1 else 0))
    o_spec = pl.BlockSpec((None, K, N), lambda b, t: (b, 0, 0))
    res, got = _pallas(
        body, grid=(B, nt), name=name, in_specs=[x_spec, y_spec], out_specs=[o_spec, o_spec],
        out_shape=[jax.ShapeDtypeStruct((B, K, N), F32), jax.ShapeDtypeStruct((B, K, N), CDT)], args=(x, dy), comm=comm)
    return res if comm is None else (res, got)


def _fold_dup(dw2, name):
    def body(x_ref, o_ref, oc_ref):
        x = x_ref[...]
        y = jnp.concatenate([x[:, g * LW:g * LW + HD] + x[:, g * LW + HD:(g + 1) * LW] for g in range(NKV)], axis=1)
        o_ref[...] = y
        oc_ref[...] = y.astype(CDT)

    return pl.pallas_call(
        body, grid=(NSH,), name=name, in_specs=[_tile(D // NSH, KVW)], out_specs=[_tile(D // NSH, NKV * HD)] * 2,
        out_shape=[jax.ShapeDtypeStruct((D, NKV * HD), F32), jax.ShapeDtypeStruct((D, NKV * HD), CDT)],
        compiler_params=_cp(),
    )(dw2)


def _all_reduce_small(vec):
    R = vec.shape[0]
    ndev = 8

    def body(v_ref, o_ref, buf, send, recv):
        x, y, c, _ = _place()
        me = 4 * x + 2 * y + c
        buf[me] = v_ref[...]
        copies = []
        for k in range(1, ndev):
            peer = (x if not k & 4 else 1 - x, y if not k & 2 else 1 - y, c if not k & 1 else 1 - c)
            copies.append(pltpu.make_async_remote_copy(
                src_ref=v_ref, dst_ref=buf.at[me], send_sem=send.at[k - 1], recv_sem=recv.at[k - 1],
                device_id=peer, device_id_type=MESH))
            copies[-1].start()
        for k in range(1, ndev):
            peer = (x if not k & 4 else 1 - x, y if not k & 2 else 1 - y, c if not k & 1 else 1 - c)
            pltpu.make_async_remote_copy(
                src_ref=v_ref, dst_ref=buf.at[4 * peer[0] + 2 * peer[1] + peer[2]], send_sem=send.at[k - 1],
                recv_sem=recv.at[k - 1], device_id=peer, device_id_type=MESH).wait_recv()
        for cp in copies:
            cp.wait_send()
        tot = buf[0]
        for d in range(1, ndev):
            tot = tot + buf[d]
        o_ref[...] = tot

    return pl.pallas_call(
        body, name="all_reduce_small", in_specs=[pl.BlockSpec(memory_space=pltpu.VMEM)],
        out_specs=pl.BlockSpec(memory_space=pltpu.VMEM), out_shape=jax.ShapeDtypeStruct(vec.shape, F32),
        scratch_shapes=[pltpu.VMEM((ndev, R, D), F32), pltpu.SemaphoreType.DMA((ndev - 1,)),
                        pltpu.SemaphoreType.DMA((ndev - 1,))],
    )(vec)


def _row_tile(rows):
    for t in range(min(rows, 512) // 8 * 8, 7, -8):
        if rows % t == 0:
            return t
    return rows


def _sum_own_and_received(own, got, me, name):
    _, R, C = own.shape
    tr = _row_tile(R)

    def body(me_ref, own_ref, got_ref, o_ref):
        o_ref[...] = ((own_ref[...] + got_ref[0].astype(F32)) + got_ref[1].astype(F32)) + got_ref[2].astype(F32)

    return pl.pallas_call(
        body, name=name,
        grid_spec=pltpu.PrefetchScalarGridSpec(
            num_scalar_prefetch=1, grid=(R // tr,),
            in_specs=[pl.BlockSpec((None, tr, C), lambda i, me_ref: (me_ref[0], i, 0)),
                      pl.BlockSpec((3, tr, C), lambda i, me_ref: (0, i, 0))],
            out_specs=pl.BlockSpec((tr, C), lambda i, me_ref: (i, 0))),
        out_shape=jax.ShapeDtypeStruct((R, C), F32), compiler_params=_cp(),
    )(me, own, got)


def _adamw(gs, w, m, v, name):
    L, R, C = w.shape
    tr = _row_tile(R)
    ng = len(gs[0])
    c1 = 1.0 - ADAM_B1 ** ADAM_STEP
    c2 = 1.0 - ADAM_B2 ** ADAM_STEP

    def body(*refs):
        w_ref, m_ref, v_ref, g_ref, d_ref, nm_ref, nv_ref = refs[L * ng:]
        lay = pl.program_id(0)
        g = None
        for l in range(L):
            gl = refs[l * ng][...]
            for r in refs[l * ng + 1:(l + 1) * ng]:
                gl = gl + r[...]
            g = gl if g is None else jnp.where(lay == l, gl, g)
        nm = ADAM_B1 * m_ref[...] + (1.0 - ADAM_B1) * g
        nv = ADAM_B2 * v_ref[...] + (1.0 - ADAM_B2) * (g * g)
        g_ref[...] = g
        nm_ref[...] = nm
        nv_ref[...] = nv
        d_ref[...] = -ADAM_LR * ((nm / c1) / (jnp.sqrt(nv / c2) + ADAM_EPS) + ADAM_WD * w_ref[...])

    gspecs = [pl.BlockSpec((tr, C), lambda lay, i, l=l: (jnp.where(lay == l, i, 0), 0)) for l in range(L) for _ in range(ng)]
    spec = pl.BlockSpec((None, tr, C), lambda lay, i: (lay, i, 0))
    return pl.pallas_call(
        body, grid=(L, R // tr), name=name, in_specs=gspecs + [spec] * 3, out_specs=[spec] * 4,
        out_shape=[jax.ShapeDtypeStruct((L, R, C), F32)] * 4, compiler_params=_cp(2),
    )(*[g for gl in gs for g in gl], w, m, v)


def _rope_tables(T):
    inv = 1.0 / (ROPE_THETA ** (jnp.arange(0, HD, 2, dtype=F32) / HD))
    ang = jnp.arange(T, dtype=F32)[:, None] * inv[None, :]
    c, s = jnp.cos(ang), jnp.sin(ang)
    return jnp.tile(jnp.concatenate([c, c], axis=1), (1, 2)), jnp.tile(jnp.concatenate([-s, s], axis=1), (1, 2))


def _dup_cols(w):
    return jnp.broadcast_to(w.reshape(D, NKV, 1, HD), (D, NKV, 2, HD)).reshape(D, KVW)


def kernel(x, p, mix_pre_g, mix_post_g, ffn_pre_g, ffn_post_g, pool_w, pool_scale, kv_norm_g, w_k, w_v, w_q, w_o, sinks, w_ff_gate, w_ff_up, w_ff_down, ple_norm_g, w_ple_gate, w_ple_proj, loss_target, m_mix_pre_g, m_mix_post_g, m_ffn_pre_g, m_ffn_post_g, m_pool_w, m_pool_scale, m_kv_norm_g, m_w_k, m_w_v, m_w_q, m_w_o, m_sinks, m_w_ff_gate, m_w_ff_up, m_w_ff_down, m_ple_norm_g, m_w_ple_gate, m_w_ple_proj, v_mix_pre_g, v_mix_post_g, v_ffn_pre_g, v_ffn_post_g, v_pool_w, v_pool_scale, v_kv_norm_g, v_w_k, v_w_v, v_w_q, v_w_o, v_sinks, v_w_ff_gate, v_w_ff_up, v_w_ff_down, v_ple_norm_g, v_w_ple_gate, v_w_ple_proj):
    depth = mix_pre_g.shape[0]
    n_pool = pool_w.shape[0]
    n_attn = w_q.shape[0]
    T = x.shape[1]
    h = x[0]
    p = p[:, 0]
    tgt = loss_target[0]
    me = (2 * lax.axis_index("x") + lax.axis_index("y")).astype(jnp.int32).reshape(1)

    def t12(a):
        return jnp.transpose(a, (0, 2, 1))

    ff_t = {k: tuple(t12(a) for a in v) for k, v in dict(
        w_ff_gate=(w_ff_gate, m_w_ff_gate, v_w_ff_gate), w_ff_up=(w_ff_up, m_w_ff_up, v_w_ff_up)).items()}
    c_pool, c_wk, c_wv, c_wq, c_wo, c_wg, c_wu, c_wd, c_wpg, c_wpp = (w.astype(CDT) for w in (
        pool_w, w_k, w_v, w_q, w_o, ff_t["w_ff_gate"][0], ff_t["w_ff_up"][0], w_ff_down, w_ple_gate, w_ple_proj))

    def ffn_srcs(l):
        return [(c_wg, l), (c_wu, l), (c_wd, l)]

    def ple_srcs(l):
        return [(c_wpg, l), (c_wpp, l)]

    (gpw,) = _run_comm(_gather_comm([(c_pool, None)]), "gather_pool_w")
    wff, wple, wattn = {}, {}, {}
    pw = jnp.transpose(gpw, (1, 2, 0, 3, 4)).reshape(n_pool, len(POOL_WINDOWS), PG, PG)
    kv_g = kv_norm_g.reshape(1, D)
    cos, sin = _rope_tables(T)

    pscale = _all_reduce_small(jnp.pad(
        lax.dynamic_update_slice(jnp.zeros((n_pool, D), F32), pool_scale, (0, me[0] * (D // NSH))),
        ((0, 8 - n_pool), (0, 0))))[:n_pool] * 0.5

    saved = []
    kk = vv = wk2 = wv2 = None
    for l in range(depth):
        s = {"h": h}
        if l < n_pool:
            (h1,), got = _pool_fwd(h, mix_pre_g, pw, pscale, mix_post_g, l,
                                   comm=_gather_comm(ffn_srcs(0)) if l == 0 else None)
            if l == 0:
                wff[0] = got
        else:
            j = l - n_pool
            wq, wo = wattn[l]
            s["q"] = _q_fwd(h, mix_pre_g, wq, cos, sin, l, j)
            s["o"] = _attn_fwd(s["q"], kk, vv, sinks, l, j)
            h1, s["m"] = _o_fwd(s["o"], h, wo, mix_post_g, l, j)
        s["h1"] = h1
        srcs = []
        if l + 1 < depth:
            srcs = ffn_srcs(l + 1)
            if l + 1 >= n_pool:
                srcs += [(c_wq, l + 1 - n_pool), (c_wo, l + 1 - n_pool)]
            if l + 1 == n_pool:
                srcs += [(c_wk, None), (c_wv, None)]
        if l == 0:
            srcs += ple_srcs(0)
        (h2, s["a"], s["b"], s["f"]), got = _ffn_fwd(
            h1, ffn_pre_g, *wff[l], ffn_post_g, l, comm=_gather_comm(srcs) if srcs else None)
        if l == 0:
            wple[0] = got[-2:]
        if l + 1 < depth:
            wff[l + 1] = got[:3]
            if l + 1 >= n_pool:
                wattn[l + 1] = got[3:5]
            if l + 1 == n_pool:
                wk2 = _dup_cols(got[5].reshape(D, NKV * HD))
                wv2 = _dup_cols(got[6].reshape(D, NKV * HD))
        s["h2"] = h2
        (h,), got = _ple_fwd(h2, p, ple_norm_g, *wple[l], l,
                             comm=_gather_comm(ple_srcs(l + 1)) if l + 1 < depth else None)
        if l + 1 < depth:
            wple[l + 1] = got
        if l == n_pool - 1:
            s["hkv"] = h
            kk, vv = _kv_fwd(h, kv_g, wk2, wv2, cos, sin)
        saved.append(s)

    dh, loss_row = _loss_grad(h, tgt)

    entries, pending, to_swap = [], [], []

    def add_grad(k, l, g32, g16):
        def by_shard(g):
            return g.reshape(NSH, -1, g.shape[-1])
        e = dict(k=k, l=l, own=by_shard(g32), sent=by_shard(g16))
        entries.append(e)
        pending.append(e)

    def carry():
        take, swap = list(pending), list(to_swap)
        del pending[:], to_swap[:]
        return (take, swap), _join_comms([_scatter_comm([e["sent"] for e in take]) if take else None,
                                          _swap_comm([e["part"] for e in swap]) if swap else None])

    def landed(taken, got):
        take, swap = taken
        for e, g in zip(take, got[:len(take)]):
            e["part"] = _sum_own_and_received(e["own"], g, me, f"sum_{e['k']}_{e['l']}")
            to_swap.append(e)
        for e, g in zip(swap, got[len(take):]):
            e["sib"] = g

    small = {k: [None] * depth for k in ("mix_pre", "mix_post", "ffn_pre", "ffn_post", "ple")}
    dpool_scale = [None] * n_pool
    dsinks = [None] * n_attn
    dks, dkhs, dvs, dvhs = [], [], [], []
    for l in reversed(range(depth)):
        s = saved[l]
        if l == n_pool - 1:
            dh, hk, dk0, dvc, dkv_g = _kv_bwd(dh, s["hkv"], dks, dkhs, dvs, dvhs, kv_g, wk2, wv2, cos, sin)
            add_grad("w_k", None, *_fold_dup(_mm_tn(hk, dk0, "dw_k")[0][0], "fold_w_k"))
            add_grad("w_v", None, *_fold_dup(_mm_tn(hk, dvc, "dw_v")[0][0], "fold_w_v"))
        dh, pn, dz, dpp, pc, small["ple"][l] = _ple_bwd(dh, s["h2"], p, ple_norm_g, *wple[l], l)
        add_grad("w_ple_gate", l, *_mm_tn(pn, dz, f"dw_ple_gate_{l}"))
        add_grad("w_ple_proj", l, *_mm_tn(pc, dpp, f"dw_ple_proj_{l}", n_split=NSH))
        take, comm = carry()
        (dh, fn, df, act, da, db, small["ffn_pre"][l], small["ffn_post"][l]), got = _ffn_bwd(
            dh, s["h1"], s["f"], s["a"], s["b"], ffn_pre_g, *wff[l], ffn_post_g, l, comm=comm)
        landed(take, got)
        add_grad("w_ff_gate", l, *_mm_tn(da, fn, f"dw_ff_gate_{l}"))
        take, comm = carry() if l == 0 else (([], []), None)
        res = _mm_tn(db, fn, f"dw_ff_up_{l}", comm=comm)
        if comm is not None:
            res, got = res
            landed(take, got)
        add_grad("w_ff_up", l, *res)
        take, comm = carry() if l == 0 else (([], []), None)
        res = _mm_tn(act, df, f"dw_ff_down_{l}", comm=comm)
        if comm is not None:
            res, got = res
            landed(take, got)
        add_grad("w_ff_down", l, *res)
        if l < n_pool:
            take, comm = carry() if l == 0 else (([], []), None)
            (dpool, dpw, dpool_scale[l], small["mix_post"][l]), got = _pool_bwd_a(
                dh, s["h"], mix_pre_g, pw, pscale, mix_post_g, l, comm=comm)
            landed(take, got)
            dpw = jnp.transpose(dpw.reshape(len(POOL_WINDOWS), NSH, PG // NSH, PG), (1, 0, 2, 3))
            dpw = dpw.reshape(NSH, len(POOL_WINDOWS) * PG // NSH, PG)
            add_grad("pool_w", l, dpw, dpw.astype(CDT))
            (dh, small["mix_pre"][l]), _ = _pool_bwd_b(dpool, dh, s["h"], mix_pre_g, l)
        else:
            j = l - n_pool
            wq, wo = wattn[l]
            do, dm, small["mix_post"][l] = _o_bwd(dh, s["m"], wo, mix_post_g, l, j)
            add_grad("w_o", j, *_mm_tn(s["o"], dm, f"dw_o_{l}"))
            dq, dk, dkh, dv, dvh, dsinks[j] = _attn_bwd(s["q"], kk, vv, s["o"], do, sinks, l, j)
            dks.append(dk); dkhs.append(dkh); dvs.append(dv); dvhs.append(dvh)
            dh, hn, dq0, small["mix_pre"][l] = _q_bwd(dq, dh, s["h"], mix_pre_g, wq, cos, sin, l, j)
            add_grad("w_q", j, *_mm_tn(hn, dq0, f"dw_q_{l}"))
    rest = 0
    while pending or to_swap:
        take, comm = carry()
        landed(take, _run_comm(comm, f"exchange_rest_{rest}"))
        rest += 1
    grad_x = dh[None]

    rows = [jnp.concatenate(small[k], axis=0) for k in ("mix_pre", "mix_post", "ffn_pre", "ffn_post", "ple")]
    rows += [dkv_g, jnp.concatenate(dpool_scale, axis=0)]
    rows += [jnp.pad(jnp.concatenate(dsinks, axis=0), ((0, 0), (0, D - LW))), jnp.pad(loss_row, ((0, 0), (0, D - 128)))]
    vec = jnp.concatenate(rows, axis=0)
    nrow = vec.shape[0]
    vec = _all_reduce_small(jnp.pad(vec, ((0, -nrow % 8), (0, 0))))
    o = 0
    red = {}
    for k in ("mix_pre_g", "mix_post_g", "ffn_pre_g", "ffn_post_g", "ple_norm_g"):
        red[k] = vec[o:o + depth]
        o += depth
    red["kv_norm_g"] = vec[o:o + 1]
    red["pool_scale"] = lax.dynamic_slice(vec[o + 1:o + 1 + n_pool], (0, me[0] * (D // NSH)), (n_pool, D // NSH))
    o += 1 + n_pool
    red["sinks"] = vec[o:o + n_attn, :GQ * NKV]
    loss = vec[o + n_attn, 0]


    given = dict(
        mix_pre_g=(mix_pre_g, m_mix_pre_g, v_mix_pre_g), mix_post_g=(mix_post_g, m_mix_post_g, v_mix_post_g),
        ffn_pre_g=(ffn_pre_g, m_ffn_pre_g, v_ffn_pre_g), ffn_post_g=(ffn_post_g, m_ffn_post_g, v_ffn_post_g),
        pool_w=(pool_w, m_pool_w, v_pool_w), pool_scale=(pool_scale, m_pool_scale, v_pool_scale),
        kv_norm_g=(kv_norm_g, m_kv_norm_g, v_kv_norm_g), w_k=(w_k, m_w_k, v_w_k), w_v=(w_v, m_w_v, v_w_v),
        w_q=(w_q, m_w_q, v_w_q), w_o=(w_o, m_w_o, v_w_o), sinks=(sinks, m_sinks, v_sinks),
        w_ff_gate=ff_t["w_ff_gate"], w_ff_up=ff_t["w_ff_up"],
        w_ff_down=(w_ff_down, m_w_ff_down, v_w_ff_down), ple_norm_g=(ple_norm_g, m_ple_norm_g, v_ple_norm_g),
        w_ple_gate=(w_ple_gate, m_w_ple_gate, v_w_ple_gate), w_ple_proj=(w_ple_proj, m_w_ple_proj, v_w_ple_proj))
    results = {}
    for k, g in red.items():
        w, m, v = given[k]
        outs = _adamw([[g.reshape(-1, g.shape[-1])]], *(t.reshape(1, -1, t.shape[-1]) for t in (w, m, v)), f"adamw_{k}")
        results[k] = [t.reshape(w.shape) for t in outs]
    per_layer = {}
    for e in entries:
        per_layer.setdefault(e["k"], {})[e["l"] or 0] = [e["part"], e["sib"]]
    for k, layers in per_layer.items():
        w, m, v = given[k]
        gs = [layers[l] for l in range(len(layers))]
        outs = _adamw(gs, *(t.reshape((len(gs),) + gs[0][0].shape) for t in (w, m, v)), f"adamw_{k}")
        results[k] = [t12(t) if k in ff_t else t.reshape(w.shape) for t in outs]

    order = ["mix_pre_g", "mix_post_g", "ffn_pre_g", "ffn_post_g", "pool_w", "pool_scale", "kv_norm_g", "w_k", "w_v",
             "w_q", "w_o", "sinks", "w_ff_gate", "w_ff_up", "w_ff_down", "ple_norm_g", "w_ple_gate", "w_ple_proj"]
    return (loss, grad_x, *[results[k][0] for k in order], *[results[k][1] for k in order],
            *[results[k][2] for k in order], *[results[k][3] for k in order])
```

```python
import functools

import jax
import jax.numpy as jnp
from jax import lax
from jax.experimental import pallas as pl
from jax.experimental.pallas import tpu as pltpu

F32 = jnp.float32
CDT = jnp.bfloat16

D = 1024
PG = 256
POOL_WINDOWS = (2, 4, 8, 16)
HALO = 16
HD = 64
NKV = 4
WIN = 128
FS = 704
NSH = 4
PLE = 256
ROPE_THETA = 10000.0
EPS = 1e-6
NEG = -1e30
QSCALE = HD ** -0.5

ADAM_LR, ADAM_B1, ADAM_B2, ADAM_EPS, ADAM_WD, ADAM_STEP = 0.001, 0.9, 0.999, 1e-08, 0.01, 10

VMEM_LIMIT = 56 * 1024 * 1024
TM = 1024
TM_FFN = 512
TM_FFN_BWD = 256
TQ = 512
TT = 4096
MESH = pl.DeviceIdType.MESH


def _cp(n_axes=1):
    return pltpu.CompilerParams(dimension_semantics=("arbitrary",) * n_axes, vmem_limit_bytes=VMEM_LIMIT)


def _dot(a, b):
    return jnp.dot(a, b, preferred_element_type=F32)


def _dot_nt(a, b):
    return lax.dot_general(a, b, (((1,), (1,)), ((), ())), preferred_element_type=F32)


def _dot_tn(a, b):
    return lax.dot_general(a, b, (((0,), (0,)), ((), ())), preferred_element_type=F32)


def _rms_fwd(x, g):
    r = lax.rsqrt(jnp.mean(x * x, axis=-1, keepdims=True) + EPS)
    return x * r * g


def _rms_bwd(dy, x, g):
    r = lax.rsqrt(jnp.mean(x * x, axis=-1, keepdims=True) + EPS)
    xh = x * r
    dg = jnp.sum(dy * xh, axis=0, keepdims=True)
    u = dy * g
    dx = r * (u - xh * jnp.mean(u * xh, axis=-1, keepdims=True))
    return dx, dg


def _sigmoid(a):
    return 1.0 / (1.0 + jnp.exp(-a))


def _swap_half(x):
    w = x.shape[1]
    lane = lax.broadcasted_iota(jnp.int32, x.shape, 1)
    return jnp.where((lane % HD) < HD // 2, pltpu.roll(x, w - HD // 2, axis=1), pltpu.roll(x, HD // 2, axis=1))


def _rope_fwd(x, cos, sin):
    return x * cos + _swap_half(x) * sin


def _rope_bwd(d, cos, sin):
    return d * cos + _swap_half(d * sin)


def _tile(tm, w):
    return pl.BlockSpec((tm, w), lambda i: (i, 0))


def _res(shape):
    return pl.BlockSpec(shape, lambda i: (0,) * len(shape), pipeline_mode=pl.Buffered(1))


def _layer_w(shape):
    return pl.BlockSpec((NSH,) + shape, lambda *_: (0, 0, 0), pipeline_mode=pl.Buffered(1))


def _acc(w):
    return pl.BlockSpec((1, w), lambda i: (0, 0))


def _zero_first(*refs):
    @pl.when(pl.program_id(0) == 0)
    def _():
        for r in refs:
            r[...] = jnp.zeros_like(r)


HBM = pl.BlockSpec(memory_space=pl.ANY)


def _place():
    x, y, c = lax.axis_index("x"), lax.axis_index("y"), lax.axis_index("c")
    chips = [(1 - x, y), (x, 1 - y), (1 - x, 1 - y)]
    return x, y, c, chips


class _Comm:
    def __init__(self, args, out_shape, sems, copies):
        self.args, self.out_shape, self.sems, self.copies = list(args), list(out_shape), list(sems), copies

    def start(self, cin, cout, sems):
        local, sends, _ = self.copies(cin, cout, sems)
        for cp in local + sends:
            cp.start()

    def wait(self, cin, cout, sems):
        local, sends, recvs = self.copies(cin, cout, sems)
        for cp in recvs:
            cp.wait_recv()
        for cp in sends:
            cp.wait_send()
        for cp in local:
            cp.wait()


def _gather_comm(srcs):
    n = len(srcs)
    shapes = [(a.shape if l is None else a.shape[1:]) for a, l in srcs]

    def copies(cin, cout, sems):
        send, recv, loc = sems
        x, y, c, chips = _place()
        me = 2 * x + y
        local, sends, recvs = [], [], []
        for a, (_, l) in enumerate(srcs):
            src = cin[a] if l is None else cin[a].at[l]
            local.append(pltpu.make_async_copy(src, cout[a].at[me], loc.at[a]))
            for j, (px, py) in enumerate(chips):
                kw = dict(src_ref=src, send_sem=send.at[3 * a + j], recv_sem=recv.at[3 * a + j],
                          device_id=(px, py, c), device_id_type=MESH)
                sends.append(pltpu.make_async_remote_copy(dst_ref=cout[a].at[me], **kw))
                recvs.append(pltpu.make_async_remote_copy(dst_ref=cout[a].at[2 * px + py], **kw))
        return local, sends, recvs

    return _Comm([a for a, _ in srcs], [jax.ShapeDtypeStruct((NSH,) + s, CDT) for s in shapes],
                 [pltpu.SemaphoreType.DMA((3 * n,)), pltpu.SemaphoreType.DMA((3 * n,)), pltpu.SemaphoreType.DMA((n,))],
                 copies)


class _GatherHalves(_Comm):
    def __init__(self, srcs):
        n = len(srcs)
        self.srcs = srcs
        super().__init__([a for a, _ in srcs], [jax.ShapeDtypeStruct((NSH,) + a.shape[1:], CDT) for a, _ in srcs],
                         [pltpu.SemaphoreType.DMA((3 * n,))] * 4 + [pltpu.SemaphoreType.DMA((n,))], None)

    def _copies(self, cin, cout, sems):
        isend, irecv, dsend, drecv, loc = sems
        x, y, c, chips = _place()
        me = 2 * x + y
        local, ici_out, ici_in, d2d_out, d2d_in = [], [], [], [], []
        for a, (arr, l) in enumerate(self.srcs):
            hr = arr.shape[1] // 2
            mine = pl.ds(pl.multiple_of(c * hr, hr), hr)
            other = pl.ds(pl.multiple_of((1 - c) * hr, hr), hr)
            src = cin[a].at[l]
            local.append(pltpu.make_async_copy(src, cout[a].at[me], loc.at[a]))
            for j, (px, py) in enumerate(chips):
                k, sj = 3 * a + j, 2 * px + py
                ici = dict(src_ref=src.at[mine], send_sem=isend.at[k], recv_sem=irecv.at[k],
                           device_id=(px, py, c), device_id_type=MESH)
                ici_out.append(pltpu.make_async_remote_copy(dst_ref=cout[a].at[me, mine], **ici))
                ici_in.append(pltpu.make_async_remote_copy(dst_ref=cout[a].at[sj, mine], **ici))
                d2d = dict(src_ref=cout[a].at[sj, mine], send_sem=dsend.at[k], recv_sem=drecv.at[k],
                           device_id=(x, y, 1 - c), device_id_type=MESH)
                d2d_out.append(pltpu.make_async_remote_copy(dst_ref=cout[a].at[sj, mine], **d2d))
                d2d_in.append(pltpu.make_async_remote_copy(dst_ref=cout[a].at[sj, other], **d2d))
        return local, ici_out, ici_in, d2d_out, d2d_in

    def start(self, cin, cout, sems):
        local, ici_out, _, _, _ = self._copies(cin, cout, sems)
        for cp in local + ici_out:
            cp.start()

    def wait(self, cin, cout, sems):
        local, ici_out, ici_in, d2d_out, d2d_in = self._copies(cin, cout, sems)
        for got, fwd in zip(ici_in, d2d_out):
            got.wait_recv()
            fwd.start()
        for cp in d2d_in:
            cp.wait_recv()
        for cp in d2d_out + ici_out:
            cp.wait_send()
        for cp in local:
            cp.wait()


def _scatter_comm(grads):
    n = len(grads)

    def copies(cin, cout, sems):
        send, recv = sems
        x, y, c, chips = _place()
        sends, recvs = [], []
        for a in range(n):
            for j, (px, py) in enumerate(chips):
                cp = pltpu.make_async_remote_copy(
                    src_ref=cin[a].at[2 * px + py], dst_ref=cout[a].at[j], send_sem=send.at[3 * a + j],
                    recv_sem=recv.at[3 * a + j], device_id=(px, py, c), device_id_type=MESH)
                sends.append(cp)
                recvs.append(cp)
        return [], sends, recvs

    return _Comm(grads, [jax.ShapeDtypeStruct((3,) + g.shape[1:], g.dtype) for g in grads],
                 [pltpu.SemaphoreType.DMA((3 * n,)), pltpu.SemaphoreType.DMA((3 * n,))], copies)


def _swap_comm(parts):
    n = len(parts)

    def copies(cin, cout, sems):
        send, recv = sems
        x, y, c, _ = _place()
        cps = [pltpu.make_async_remote_copy(
            src_ref=cin[a], dst_ref=cout[a], send_sem=send.at[a], recv_sem=recv.at[a],
            device_id=(x, y, 1 - c), device_id_type=MESH) for a in range(n)]
        return [], cps, cps

    return _Comm(parts, [jax.ShapeDtypeStruct(p.shape, p.dtype) for p in parts],
                 [pltpu.SemaphoreType.DMA((n,)), pltpu.SemaphoreType.DMA((n,))], copies)


def _join_comms(comms):
    comms = [c for c in comms if c is not None]
    if len(comms) <= 1:
        return comms[0] if comms else None

    def copies(cin, cout, sems):
        out = ([], [], [])
        i = o = k = 0
        for c in comms:
            part = c.copies(cin[i:i + len(c.args)], cout[o:o + len(c.out_shape)], sems[k:k + len(c.sems)])
            for acc, cps in zip(out, part):
                acc.extend(cps)
            i, o, k = i + len(c.args), o + len(c.out_shape), k + len(c.sems)
        return out

    return _Comm([a for c in comms for a in c.args], [s for c in comms for s in c.out_shape],
                 [s for c in comms for s in c.sems], copies)


def _run_comm(comm, name):
    k_in, k_out = len(comm.args), len(comm.out_shape)

    def body(*refs):
        cin, cout, sems = refs[:k_in], refs[k_in:k_in + k_out], refs[k_in + k_out:]
        comm.start(cin, cout, sems)
        comm.wait(cin, cout, sems)

    return pl.pallas_call(body, name=name, in_specs=[HBM] * k_in, out_specs=[HBM] * k_out, out_shape=comm.out_shape,
                          scratch_shapes=comm.sems)(*comm.args)


def _pallas(body, *, name, grid, in_specs, out_specs, out_shape, args, scratch=(), comm=None):
    out_specs, out_shape, scratch = list(out_specs), list(out_shape), list(scratch)
    params = _cp(len(grid))
    if comm is None:
        return pl.pallas_call(body, grid=grid, name=name, in_specs=in_specs, out_specs=out_specs, out_shape=out_shape,
                              scratch_shapes=scratch, compiler_params=params)(*args), []
    n_in, n_out, n_sc = len(args), len(out_shape), len(scratch)
    k_in, k_out = len(comm.args), len(comm.out_shape)

    def carrying(*refs):
        ins, refs = refs[:n_in], refs[n_in:]
        cin, refs = refs[:k_in], refs[k_in:]
        outs, refs = refs[:n_out], refs[n_out:]
        cout, refs = refs[:k_out], refs[k_out:]
        sc, sems = refs[:n_sc], refs[n_sc:]
        first = functools.reduce(jnp.logical_and, [pl.program_id(d) == 0 for d in range(len(grid))])
        last = functools.reduce(jnp.logical_and, [pl.program_id(d) == g - 1 for d, g in enumerate(grid)])

        @pl.when(first)
        def _():
            comm.start(cin, cout, sems)

        body(*ins, *outs, *sc)

        @pl.when(last)
        def _():
            comm.wait(cin, cout, sems)

    res = pl.pallas_call(
        carrying, grid=grid, name=name, in_specs=list(in_specs) + [HBM] * k_in, out_specs=out_specs + [HBM] * k_out,
        out_shape=out_shape + comm.out_shape, scratch_shapes=scratch + comm.sems, compiler_params=params,
    )(*args, *comm.args)
    return res[:n_out], res[n_out:]


def _pool_normed_ext(h_ref, halo_ref, g, tm):
    xe = jnp.concatenate([halo_ref[...], h_ref[...]], axis=0)
    hn = _rms_fwd(xe, g)
    row = lax.broadcasted_iota(jnp.int32, (tm + HALO, 1), 0)
    return jnp.where((row >= HALO) | (pl.program_id(0) > 0), hn, 0.0)


def _pool_windows(hn_e, tm):
    t = lax.broadcasted_iota(jnp.int32, (tm, PG), 0) + pl.program_id(0) * tm
    outs = []
    for g, w in enumerate(POOL_WINDOWS):
        s = hn_e[:, g * PG:(g + 1) * PG]
        x = s[HALO:, :]
        k = 1
        while k < w:
            s = s + pltpu.roll(s, k, axis=0)
            k *= 2
        cnt = jnp.minimum(t + 1, w).astype(F32)
        outs.append(s[HALO:, :] / cnt - x)
    return jnp.concatenate(outs, axis=1)


def _pool_apply(pooled, w_ref):
    return jnp.concatenate([_dot(pooled[:, g * PG:(g + 1) * PG], w_ref[g]) for g in range(len(POOL_WINDOWS))], axis=1)


def _halo_prev(tm):
    return pl.BlockSpec((HALO, D), lambda i: (jnp.maximum(i * (tm // HALO) - 1, 0), 0))


def _pool_fwd(h, g_pre, pw, pscale, g_post, l, comm=None):
    T = h.shape[0]
    tm = min(TM, T)

    def body(h_ref, halo_ref, gpre_ref, w_ref, sc_ref, gpost_ref, o_ref):
        hn_e = _pool_normed_ext(h_ref, halo_ref, gpre_ref[l:l + 1, :], tm)
        pooled = _pool_windows(hn_e, tm).astype(CDT)
        m = _pool_apply(pooled, w_ref) * sc_ref[l:l + 1, :]
        o_ref[...] = h_ref[...] + _rms_fwd(m, gpost_ref[l:l + 1, :])

    return _pallas(
        body, grid=(T // tm,), name=f"pool_fwd_{l}",
        in_specs=[_tile(tm, D), _halo_prev(tm), _res(g_pre.shape),
                  pl.BlockSpec((None,) + pw.shape[1:], lambda i: (l, 0, 0, 0), pipeline_mode=pl.Buffered(1)),
                  _res(pscale.shape), _res(g_post.shape)],
        out_specs=[_tile(tm, D)], out_shape=[jax.ShapeDtypeStruct((T, D), F32)],
        args=(h, h, g_pre, pw, pscale, g_post), comm=comm)


def _pool_bwd_a(dh1, h, g_pre, pw, pscale, g_post, l, comm=None):
    T = h.shape[0]
    tm = min(TM, T)
    ng = len(POOL_WINDOWS)

    def body(dh_ref, h_ref, halo_ref, gpre_ref, w_ref, sc_ref, gpost_ref, dp_ref, dw_ref, dsc_ref, dgpost_ref):
        _zero_first(dw_ref, dsc_ref, dgpost_ref)
        hn_e = _pool_normed_ext(h_ref, halo_ref, gpre_ref[l:l + 1, :], tm)
        pooled = _pool_windows(hn_e, tm).astype(CDT)
        y = _pool_apply(pooled, w_ref)
        sc = sc_ref[l:l + 1, :]
        dm, dg = _rms_bwd(dh_ref[...], y * sc, gpost_ref[l:l + 1, :])
        dgpost_ref[...] += dg
        dsc_ref[...] += jnp.sum(dm * y, axis=0, keepdims=True)
        dy = (dm * sc).astype(CDT)
        dps = []
        for g in range(ng):
            dyg = dy[:, g * PG:(g + 1) * PG]
            dw_ref[g] += _dot_tn(pooled[:, g * PG:(g + 1) * PG], dyg)
            dps.append(_dot_nt(dyg, w_ref[g]))
        dp_ref[...] = jnp.concatenate(dps, axis=1)

    return _pallas(
        body, grid=(T // tm,), name=f"pool_bwd_a_{l}",
        in_specs=[_tile(tm, D), _tile(tm, D), _halo_prev(tm), _res(g_pre.shape),
                  pl.BlockSpec((None,) + pw.shape[1:], lambda i: (l, 0, 0, 0), pipeline_mode=pl.Buffered(1)),
                  _res(pscale.shape), _res(g_post.shape)],
        out_specs=[_tile(tm, D), pl.BlockSpec((ng, PG, PG), lambda i: (0, 0, 0)), _acc(D), _acc(D)],
        out_shape=[jax.ShapeDtypeStruct((T, D), F32), jax.ShapeDtypeStruct((ng, PG, PG), F32),
                   jax.ShapeDtypeStruct((1, D), F32), jax.ShapeDtypeStruct((1, D), F32)],
        args=(dh1, h, h, g_pre, pw, pscale, g_post), comm=comm)


def _pool_bwd_b(dpool, dh1, h, g_pre, l, comm=None):
    T = h.shape[0]
    tm = min(TM, T)
    nt = T // tm

    def body(dp_ref, nxt_ref, dh_ref, h_ref, gpre_ref, o_ref, dgpre_ref):
        _zero_first(dgpre_ref)
        i = pl.program_id(0)
        dp = dp_ref[...]
        e_e = jnp.concatenate([dp, jnp.where(i < nt - 1, nxt_ref[...], 0.0)], axis=0)
        t = lax.broadcasted_iota(jnp.int32, (tm + HALO, PG), 0) + i * tm
        outs = []
        for g, w in enumerate(POOL_WINDOWS):
            s = e_e[:, g * PG:(g + 1) * PG] / jnp.minimum(t + 1, w).astype(F32)
            k = 1
            while k < w:
                s = s + pltpu.roll(s, tm + HALO - k, axis=0)
                k *= 2
            outs.append(s[:tm, :] - dp[:, g * PG:(g + 1) * PG])
        dx, dg = _rms_bwd(jnp.concatenate(outs, axis=1), h_ref[...], gpre_ref[l:l + 1, :])
        dgpre_ref[...] += dg
        o_ref[...] = dh_ref[...] + dx

    nxt = pl.BlockSpec((HALO, D), lambda i: (jnp.minimum((i + 1) * (tm // HALO), T // HALO - 1), 0))
    return _pallas(
        body, grid=(nt,), name=f"pool_bwd_b_{l}",
        in_specs=[_tile(tm, D), nxt, _tile(tm, D), _tile(tm, D), _res(g_pre.shape)],
        out_specs=[_tile(tm, D), _acc(D)],
        out_shape=[jax.ShapeDtypeStruct((T, D), F32), jax.ShapeDtypeStruct((1, D), F32)],
        args=(dpool, dpool, dh1, h, g_pre), comm=comm)


def _proj_rows(x, w_ref):
    k = w_ref.shape[1]
    out = _dot(x[:, :k], w_ref[0])
    for s in range(1, NSH):
        out = out + _dot(x[:, s * k:(s + 1) * k], w_ref[s])
    return out


def _proj_rows_t(dy, w_ref):
    return jnp.concatenate([_dot_nt(dy, w_ref[s]) for s in range(NSH)], axis=1)


def _rope_tiles(cos_ref, sin_ref, width):
    reps = width // cos_ref.shape[1]
    return jnp.tile(cos_ref[...], (1, reps)), jnp.tile(sin_ref[...], (1, reps))


def _q_fwd(h, g_pre, wq, cos, sin, l, j):
    T = h.shape[0]
    tm = min(TM, T)

    def body(h_ref, g_ref, w_ref, cos_ref, sin_ref, q_ref):
        hn = _rms_fwd(h_ref[...], g_ref[l:l + 1, :]).astype(CDT)
        c, s = _rope_tiles(cos_ref, sin_ref, D)
        q_ref[...] = (_rope_fwd(_proj_rows(hn, w_ref), c, s) * QSCALE).astype(CDT)

    return pl.pallas_call(
        body, grid=(T // tm,), name=f"q_fwd_{l}",
        in_specs=[_tile(tm, D), _res(g_pre.shape), _layer_w((D // NSH, D)), _tile(tm, 2 * HD), _tile(tm, 2 * HD)],
        out_specs=_tile(tm, D), out_shape=jax.ShapeDtypeStruct((T, D), CDT), compiler_params=_cp(),
    )(h, g_pre, wq, cos, sin)


def _q_bwd(dq, dh1, h, g_pre, wq, cos, sin, l, j):
    T = h.shape[0]
    tm = min(TM, T)

    def body(dq_ref, dh_ref, h_ref, g_ref, w_ref, cos_ref, sin_ref, o_ref, hn_ref, dq0_ref, dg_ref):
        _zero_first(dg_ref)
        g = g_ref[l:l + 1, :]
        x = h_ref[...]
        hn_ref[...] = _rms_fwd(x, g).astype(CDT)
        c, s = _rope_tiles(cos_ref, sin_ref, D)
        dq0 = _rope_bwd(dq_ref[...].astype(F32) * QSCALE, c, s).astype(CDT)
        dq0_ref[...] = dq0
        dx, dg = _rms_bwd(_proj_rows_t(dq0, w_ref), x, g)
        dg_ref[...] += dg
        o_ref[...] = dh_ref[...] + dx

    return pl.pallas_call(
        body, grid=(T // tm,), name=f"q_bwd_{l}",
        in_specs=[_tile(tm, D), _tile(tm, D), _tile(tm, D), _res(g_pre.shape), _layer_w((D // NSH, D)),
                  _tile(tm, 2 * HD), _tile(tm, 2 * HD)],
        out_specs=[_tile(tm, D), _tile(tm, D), _tile(tm, D), _acc(D)],
        out_shape=[jax.ShapeDtypeStruct((T, D), F32), jax.ShapeDtypeStruct((T, D), CDT),
                   jax.ShapeDtypeStruct((T, D), CDT), jax.ShapeDtypeStruct((1, D), F32)],
        compiler_params=_cp(),
    )(dq, dh1, h, g_pre, wq, cos, sin)


def _o_fwd(o, h, wo, g_post, l, j):
    T = h.shape[0]
    tm = min(TM, T)

    def body(o_ref, h_ref, w_ref, g_ref, h1_ref, m_ref):
        m = _proj_rows(o_ref[...], w_ref)
        m_ref[...] = m.astype(CDT)
        h1_ref[...] = h_ref[...] + _rms_fwd(m, g_ref[l:l + 1, :])

    return pl.pallas_call(
        body, grid=(T // tm,), name=f"o_fwd_{l}",
        in_specs=[_tile(tm, D), _tile(tm, D), _layer_w((D // NSH, D)), _res(g_post.shape)],
        out_specs=[_tile(tm, D), _tile(tm, D)],
        out_shape=[jax.ShapeDtypeStruct((T, D), F32), jax.ShapeDtypeStruct((T, D), CDT)], compiler_params=_cp(),
    )(o, h, wo, g_post)


def _o_bwd(dh1, m, wo, g_post, l, j):
    T = m.shape[0]
    tm = min(TM, T)

    def body(dh_ref, m_ref, w_ref, g_ref, do_ref, dm_ref, dg_ref):
        _zero_first(dg_ref)
        dm, dg = _rms_bwd(dh_ref[...], m_ref[...].astype(F32), g_ref[l:l + 1, :])
        dg_ref[...] += dg
        dmc = dm.astype(CDT)
        dm_ref[...] = dmc
        do_ref[...] = _proj_rows_t(dmc, w_ref).astype(CDT)

    return pl.pallas_call(
        body, grid=(T // tm,), name=f"o_bwd_{l}",
        in_specs=[_tile(tm, D), _tile(tm, D), _layer_w((D // NSH, D)), _res(g_post.shape)],
        out_specs=[_tile(tm, D), _tile(tm, D), _acc(D)],
        out_shape=[jax.ShapeDtypeStruct((T, D), CDT), jax.ShapeDtypeStruct((T, D), CDT),
                   jax.ShapeDtypeStruct((1, D), F32)],
        compiler_params=_cp(),
    )(dh1, m, wo, g_post)


KVW = 2 * NKV * HD


def _kv_fwd(h, g_kv, wk2, wv2, cos, sin):
    T = h.shape[0]
    tm = min(TM, T)

    def body(h_ref, g_ref, wk_ref, wv_ref, cos_ref, sin_ref, k_ref, v_ref):
        hk = _rms_fwd(h_ref[...], g_ref[...]).astype(CDT)
        c, s = _rope_tiles(cos_ref, sin_ref, KVW)
        k_ref[...] = _rope_fwd(_dot(hk, wk_ref[...]), c, s).astype(CDT)
        v_ref[...] = _dot(hk, wv_ref[...]).astype(CDT)

    return pl.pallas_call(
        body, grid=(T // tm,), name="kv_fwd",
        in_specs=[_tile(tm, D), _res(g_kv.shape), _res(wk2.shape), _res(wv2.shape), _tile(tm, 2 * HD), _tile(tm, 2 * HD)],
        out_specs=[_tile(tm, KVW), _tile(tm, KVW)],
        out_shape=[jax.ShapeDtypeStruct((T, KVW), CDT)] * 2, compiler_params=_cp(),
    )(h, g_kv, wk2, wv2, cos, sin)


def _kv_bwd(dh, h, dks, dkhs, dvs, dvhs, g_kv, wk2, wv2, cos, sin):
    T = h.shape[0]
    tq = min(TQ, T)
    nt = T // tq
    n = len(dks)

    def body(*refs):
        dh_ref, h_ref = refs[:2]
        main = refs[2:2 + 2 * n]
        halo = refs[2 + 2 * n:2 + 4 * n]
        g_ref, wk_ref, wv_ref, cos_ref, sin_ref, o_ref, hk_ref, dk0_ref, dv_ref, dg_ref = refs[2 + 4 * n:]
        _zero_first(dg_ref)
        i = pl.program_id(0)

        def total(mains, halos):
            d = mains[0][...]
            for r in mains[1:]:
                d = d + r[...]
            hl = halos[0][...]
            for r in halos[1:]:
                hl = hl + r[...]
            row = lax.broadcasted_iota(jnp.int32, (tq, 1), 0)
            if tq > WIN:
                hl = jnp.concatenate([jnp.zeros((tq - WIN, KVW), F32), hl], axis=0)
            return d + jnp.where((row >= tq - WIN) & (i < nt - 1), hl, 0.0)

        dk = total(main[:n], halo[:n])
        dv = total(main[n:], halo[n:])
        x = h_ref[...]
        g = g_ref[...]
        hk_ref[...] = _rms_fwd(x, g).astype(CDT)
        c, s = _rope_tiles(cos_ref, sin_ref, KVW)
        dk0 = _rope_bwd(dk, c, s).astype(CDT)
        dvc = dv.astype(CDT)
        dk0_ref[...] = dk0
        dv_ref[...] = dvc
        dx, dg = _rms_bwd(_dot_nt(dk0, wk_ref[...]) + _dot_nt(dvc, wv_ref[...]), x, g)
        dg_ref[...] += dg
        o_ref[...] = dh_ref[...] + dx

    nxt = pl.BlockSpec((WIN, KVW), lambda i: (jnp.minimum(i + 1, nt - 1), 0))
    return pl.pallas_call(
        body, grid=(nt,), name="kv_bwd",
        in_specs=[_tile(tq, D), _tile(tq, D)] + [_tile(tq, KVW)] * (2 * n) + [nxt] * (2 * n)
        + [_res(g_kv.shape), _res(wk2.shape), _res(wv2.shape), _tile(tq, 2 * HD), _tile(tq, 2 * HD)],
        out_specs=[_tile(tq, D), _tile(tq, D), _tile(tq, KVW), _tile(tq, KVW), _acc(D)],
        out_shape=[jax.ShapeDtypeStruct((T, D), F32), jax.ShapeDtypeStruct((T, D), CDT),
                   jax.ShapeDtypeStruct((T, KVW), CDT), jax.ShapeDtypeStruct((T, KVW), CDT),
                   jax.ShapeDtypeStruct((1, D), F32)],
        compiler_params=_cp(),
    )(dh, h, *dks, *dvs, *dkhs, *dvhs, g_kv, wk2, wv2, cos, sin)


GQ = 4
LW = 2 * HD


def _from_prev():
    r = lax.broadcasted_iota(jnp.int32, (GQ * WIN, WIN), 0) % WIN
    j = lax.broadcasted_iota(jnp.int32, (GQ * WIN, WIN), 1)
    return j > r


def _fold(x2, prev):
    return jnp.where(prev, x2[:, :WIN], x2[:, WIN:])


def _unfold(x, prev):
    zero = jnp.zeros_like(x)
    return jnp.concatenate([jnp.where(prev, x, zero), jnp.where(prev, zero, x)], axis=1)


def _stack_heads(ref, rows, g):
    lane = lax.broadcasted_iota(jnp.int32, (WIN, LW), 1)
    parts = []
    for pr in range(2):
        x = ref[rows, (2 * g + pr) * LW:(2 * g + pr + 1) * LW]
        parts += [jnp.where(lane < HD, x, jnp.zeros_like(x)), jnp.where(lane >= HD, x, jnp.zeros_like(x))]
    return jnp.concatenate(parts, axis=0)


def _unstack_heads(x4):
    lane = lax.broadcasted_iota(jnp.int32, (WIN, LW), 1)
    return [jnp.where(lane < HD, x4[(2 * pr) * WIN:(2 * pr + 1) * WIN], x4[(2 * pr + 1) * WIN:(2 * pr + 2) * WIN])
            for pr in range(2)]


def _sink_col(sink_ref, j, g):
    blk = lax.broadcasted_iota(jnp.int32, (GQ * WIN, 1), 0) // WIN
    col = jnp.zeros((GQ * WIN, 1), F32)
    for a in range(GQ):
        col = jnp.where(blk == a, sink_ref[j, GQ * g + a], col)
    return col


def _softmax_sink(s, hidden, sink):
    if hidden is not None:
        s = jnp.where(hidden, NEG, s)
    m = jnp.maximum(jnp.max(s, axis=-1, keepdims=True), sink)
    e = jnp.exp(s - m)
    es = jnp.exp(sink - m)
    l = jnp.sum(e, axis=-1, keepdims=True) + es
    return e / l, es / l


def _kv_halo(tq):
    return pl.BlockSpec((WIN, KVW), lambda i: (jnp.maximum(i * (tq // WIN) - 1, 0), 0))


def _attn_fwd(q, kk, vv, sinks, l, j):
    T = q.shape[0]
    tq = min(TQ, T)

    def body(sink_ref, q_ref, k_ref, kh_ref, v_ref, vh_ref, o_ref):
        i = pl.program_id(0)
        ke = jnp.concatenate([kh_ref[...], k_ref[...]], axis=0)
        ve = jnp.concatenate([vh_ref[...], v_ref[...]], axis=0)
        prev = _from_prev()
        for n in range(tq // WIN):
            rows = slice(n * WIN, (n + 1) * WIN)
            hidden = prev & (i == 0) if n == 0 else None
            for g in range(NKV):
                kg = ke[n * WIN:(n + 2) * WIN, g * LW:(g + 1) * LW]
                vg = ve[n * WIN:(n + 2) * WIN, g * LW:(g + 1) * LW]
                s = _fold(_dot_nt(_stack_heads(q_ref, rows, g), kg), prev)
                p, _ = _softmax_sink(s, hidden, _sink_col(sink_ref, j, g))
                pairs = _unstack_heads(_dot(_unfold(p.astype(CDT), prev), vg))
                for pr in range(2):
                    o_ref[rows, (2 * g + pr) * LW:(2 * g + pr + 1) * LW] = pairs[pr].astype(CDT)

    return pl.pallas_call(
        body, grid=(T // tq,), name=f"attn_fwd_{l}",
        in_specs=[pl.BlockSpec(memory_space=pltpu.SMEM), _tile(tq, D), _tile(tq, KVW), _kv_halo(tq),
                  _tile(tq, KVW), _kv_halo(tq)],
        out_specs=_tile(tq, D), out_shape=jax.ShapeDtypeStruct((T, D), CDT), compiler_params=_cp(),
    )(sinks, q, kk, kk, vv, vv)


def _attn_bwd(q, kk, vv, o, do, sinks, l, j):
    T = q.shape[0]
    tq = min(TQ, T)
    nt = T // tq

    def body(sink_ref, q_ref, k_ref, kh_ref, v_ref, vh_ref, o_ref, do_ref,
             dq_ref, dk_ref, dkh_ref, dv_ref, dvh_ref, dsink_ref, dke, dve):
        _zero_first(dsink_ref)
        i = pl.program_id(0)
        ke = jnp.concatenate([kh_ref[...], k_ref[...]], axis=0)
        ve = jnp.concatenate([vh_ref[...], v_ref[...]], axis=0)
        dke[...] = jnp.zeros_like(dke)
        dve[...] = jnp.zeros_like(dve)
        lane = lax.broadcasted_iota(jnp.int32, (1, LW), 1)
        blk = lax.broadcasted_iota(jnp.int32, (GQ * WIN, 1), 0) // WIN
        dsink = jnp.zeros((1, LW), F32)
        prev = _from_prev()
        for n in range(tq // WIN):
            rows = slice(n * WIN, (n + 1) * WIN)
            krows = slice(n * WIN, (n + 2) * WIN)
            hidden = prev & (i == 0) if n == 0 else None
            for g in range(NKV):
                cols = slice(g * LW, (g + 1) * LW)
                kg = ke[krows, cols]
                vg = ve[krows, cols]
                q4 = _stack_heads(q_ref, rows, g)
                do4 = _stack_heads(do_ref, rows, g)
                o4 = _stack_heads(o_ref, rows, g)
                p, ps = _softmax_sink(_fold(_dot_nt(q4, kg), prev), hidden, _sink_col(sink_ref, j, g))
                delta = jnp.sum(do4.astype(F32) * o4.astype(F32), axis=-1, keepdims=True)
                ds = _unfold((p * (_fold(_dot_nt(do4, vg), prev) - delta)).astype(CDT), prev)
                pc = _unfold(p.astype(CDT), prev)
                pairs = _unstack_heads(_dot(ds, kg))
                for pr in range(2):
                    dq_ref[rows, (2 * g + pr) * LW:(2 * g + pr + 1) * LW] = pairs[pr].astype(CDT)
                dke[krows, cols] += _dot_tn(ds, q4)
                dve[krows, cols] += _dot_tn(pc, do4)
                t = ps * delta
                for a in range(GQ):
                    dsink = dsink - jnp.where(lane == GQ * g + a, jnp.sum(jnp.where(blk == a, t, 0.0)), 0.0)
        dsink_ref[...] += dsink
        dkh_ref[...] = dke[:WIN, :]
        dk_ref[...] = dke[WIN:, :]
        dvh_ref[...] = dve[:WIN, :]
        dv_ref[...] = dve[WIN:, :]

    halo_out = pl.BlockSpec((WIN, KVW), lambda i: (i, 0))
    return pl.pallas_call(
        body, grid=(nt,), name=f"attn_bwd_{l}",
        in_specs=[pl.BlockSpec(memory_space=pltpu.SMEM), _tile(tq, D), _tile(tq, KVW), _kv_halo(tq),
                  _tile(tq, KVW), _kv_halo(tq), _tile(tq, D), _tile(tq, D)],
        out_specs=[_tile(tq, D), _tile(tq, KVW), halo_out, _tile(tq, KVW), halo_out, _acc(LW)],
        out_shape=[jax.ShapeDtypeStruct((T, D), CDT), jax.ShapeDtypeStruct((T, KVW), F32),
                   jax.ShapeDtypeStruct((nt * WIN, KVW), F32), jax.ShapeDtypeStruct((T, KVW), F32),
                   jax.ShapeDtypeStruct((nt * WIN, KVW), F32), jax.ShapeDtypeStruct((1, LW), F32)],
        scratch_shapes=[pltpu.VMEM((WIN + tq, KVW), F32), pltpu.VMEM((WIN + tq, KVW), F32)],
        compiler_params=_cp(),
    )(sinks, q, kk, kk, vv, vv, o, do)


def _chunks(tm):
    return pl.BlockSpec((NSH, tm, FS), lambda i: (0, i, 0))


def _ffn_fwd(h1, g_pre, wg, wu, wd, g_post, l, comm=None):
    T = h1.shape[0]
    tm = min(TM_FFN, T)

    def body(h_ref, gpre_ref, wg_ref, wu_ref, wd_ref, gpost_ref, h2_ref, a_ref, b_ref, f_ref):
        x = h_ref[...]
        fn = _rms_fwd(x, gpre_ref[l:l + 1, :]).astype(CDT)
        f = jnp.zeros((tm, D), F32)
        for s in range(NSH):
            a = _dot_nt(fn, wg_ref[s])
            b = _dot_nt(fn, wu_ref[s])
            a_ref[s] = a.astype(CDT)
            b_ref[s] = b.astype(CDT)
            f = f + _dot((a * _sigmoid(a) * b).astype(CDT), wd_ref[s])
        f_ref[...] = f.astype(CDT)
        h2_ref[...] = x + _rms_fwd(f, gpost_ref[l:l + 1, :])

    return _pallas(
        body, grid=(T // tm,), name=f"ffn_fwd_{l}",
        in_specs=[_tile(tm, D), _res(g_pre.shape), _layer_w((FS, D)), _layer_w((FS, D)), _layer_w((FS, D)),
                  _res(g_post.shape)],
        out_specs=[_tile(tm, D), _chunks(tm), _chunks(tm), _tile(tm, D)],
        out_shape=[jax.ShapeDtypeStruct((T, D), F32), jax.ShapeDtypeStruct((NSH, T, FS), CDT),
                   jax.ShapeDtypeStruct((NSH, T, FS), CDT), jax.ShapeDtypeStruct((T, D), CDT)],
        args=(h1, g_pre, wg, wu, wd, g_post), comm=comm)


def _ffn_bwd(dh2, h1, f, a, b, g_pre, wg, wu, wd, g_post, l, comm=None):
    T = h1.shape[0]
    tm = min(TM_FFN_BWD, T)

    def body(dh_ref, h_ref, f_ref, a_ref, b_ref, gpre_ref, wg_ref, wu_ref, wd_ref, gpost_ref,
             dh1_ref, fn_ref, df_ref, act_ref, da_ref, db_ref, dgpre_ref, dgpost_ref):
        _zero_first(dgpre_ref, dgpost_ref)
        x = h_ref[...]
        gpre = gpre_ref[l:l + 1, :]
        fn_ref[...] = _rms_fwd(x, gpre).astype(CDT)
        dh2 = dh_ref[...]
        df, dg = _rms_bwd(dh2, f_ref[...].astype(F32), gpost_ref[l:l + 1, :])
        dgpost_ref[...] += dg
        dfc = df.astype(CDT)
        df_ref[...] = dfc
        dfn = jnp.zeros((tm, D), F32)
        for s in range(NSH):
            av = a_ref[s].astype(F32)
            bv = b_ref[s].astype(F32)
            sg = _sigmoid(av)
            silu = av * sg
            act_ref[s] = (silu * bv).astype(CDT)
            dact = _dot_nt(dfc, wd_ref[s])
            da = (dact * bv * (sg * (1.0 + av * (1.0 - sg)))).astype(CDT)
            db = (dact * silu).astype(CDT)
            da_ref[s] = da
            db_ref[s] = db
            dfn = dfn + _dot(da, wg_ref[s]) + _dot(db, wu_ref[s])
        dx, dg = _rms_bwd(dfn, x, gpre)
        dgpre_ref[...] += dg
        dh1_ref[...] = dh2 + dx

    return _pallas(
        body, grid=(T // tm,), name=f"ffn_bwd_{l}",
        in_specs=[_tile(tm, D), _tile(tm, D), _tile(tm, D), _chunks(tm), _chunks(tm), _res(g_pre.shape),
                  _layer_w((FS, D)), _layer_w((FS, D)), _layer_w((FS, D)), _res(g_post.shape)],
        out_specs=[_tile(tm, D), _tile(tm, D), _tile(tm, D), _chunks(tm), _chunks(tm), _chunks(tm), _acc(D), _acc(D)],
        out_shape=[jax.ShapeDtypeStruct((T, D), F32), jax.ShapeDtypeStruct((T, D), CDT), jax.ShapeDtypeStruct((T, D), CDT),
                   jax.ShapeDtypeStruct((NSH, T, FS), CDT), jax.ShapeDtypeStruct((NSH, T, FS), CDT),
                   jax.ShapeDtypeStruct((NSH, T, FS), CDT), jax.ShapeDtypeStruct((1, D), F32),
                   jax.ShapeDtypeStruct((1, D), F32)],
        args=(dh2, h1, f, a, b, g_pre, wg, wu, wd, g_post), comm=comm)


def _ple_proj(p, wpp_ref):
    return jnp.concatenate([_dot(p, wpp_ref[s]) for s in range(NSH)], axis=1)


def _ple_fwd(h2, p, g_ple, wpg, wpp, l, comm=None):
    T = h2.shape[0]
    tm = min(TM, T)

    def body(h_ref, p_ref, g_ref, wpg_ref, wpp_ref, o_ref):
        x = h_ref[...]
        pn = _rms_fwd(x, g_ref[l:l + 1, :]).astype(CDT)
        gate = _sigmoid(_proj_rows(pn, wpg_ref))
        o_ref[...] = x + _ple_proj(p_ref[...].astype(CDT), wpp_ref) * gate

    return _pallas(
        body, grid=(T // tm,), name=f"ple_fwd_{l}",
        in_specs=[_tile(tm, D), pl.BlockSpec((None, tm, PLE), lambda i: (l, i, 0)), _res(g_ple.shape),
                  _layer_w((D // NSH, D)), _layer_w((PLE, D // NSH))],
        out_specs=[_tile(tm, D)], out_shape=[jax.ShapeDtypeStruct((T, D), F32)],
        args=(h2, p, g_ple, wpg, wpp), comm=comm)


def _ple_bwd(dh3, h2, p, g_ple, wpg, wpp, l):
    T = h2.shape[0]
    tm = min(TM, T)

    def body(dh_ref, h_ref, p_ref, g_ref, wpg_ref, wpp_ref, o_ref, pn_ref, dz_ref, dpp_ref, pc_ref, dg_ref):
        _zero_first(dg_ref)
        x = h_ref[...]
        g = g_ref[l:l + 1, :]
        pn = _rms_fwd(x, g).astype(CDT)
        pn_ref[...] = pn
        gate = _sigmoid(_proj_rows(pn, wpg_ref))
        pc = p_ref[...].astype(CDT)
        pc_ref[...] = pc
        pp = _ple_proj(pc, wpp_ref)
        dh3 = dh_ref[...]
        dpp_ref[...] = (dh3 * gate).astype(CDT)
        dz = (dh3 * pp * gate * (1.0 - gate)).astype(CDT)
        dz_ref[...] = dz
        dx, dg = _rms_bwd(_proj_rows_t(dz, wpg_ref), x, g)
        dg_ref[...] += dg
        o_ref[...] = dh3 + dx

    return pl.pallas_call(
        body, grid=(T // tm,), name=f"ple_bwd_{l}",
        in_specs=[_tile(tm, D), _tile(tm, D), pl.BlockSpec((None, tm, PLE), lambda i: (l, i, 0)), _res(g_ple.shape),
                  _layer_w((D // NSH, D)), _layer_w((PLE, D // NSH))],
        out_specs=[_tile(tm, D), _tile(tm, D), _tile(tm, D), _tile(tm, D), _tile(tm, PLE), _acc(D)],
        out_shape=[jax.ShapeDtypeStruct((T, D), F32), jax.ShapeDtypeStruct((T, D), CDT), jax.ShapeDtypeStruct((T, D), CDT),
                   jax.ShapeDtypeStruct((T, D), CDT), jax.ShapeDtypeStruct((T, PLE), CDT), jax.ShapeDtypeStruct((1, D), F32)],
        compiler_params=_cp(),
    )(dh3, h2, p, g_ple, wpg, wpp)


def _loss_grad(y, tgt):
    T = y.shape[0]
    tm = min(TM, T)

    def body(y_ref, t_ref, dy_ref, loss_ref):
        _zero_first(loss_ref)
        err = y_ref[...] - t_ref[...]
        dy_ref[...] = err * (1.0 / D)
        lane = lax.broadcasted_iota(jnp.int32, (1, 128), 1)
        loss_ref[...] += jnp.where(lane == 0, (0.5 / D) * jnp.sum(err * err), 0.0)

    return pl.pallas_call(
        body, grid=(T // tm,), name="loss_grad", in_specs=[_tile(tm, D), _tile(tm, D)],
        out_specs=[_tile(tm, D), _acc(128)],
        out_shape=[jax.ShapeDtypeStruct((T, D), F32), jax.ShapeDtypeStruct((1, 128), F32)], compiler_params=_cp(),
    )(y, tgt)


def _mm_tn(x, dy, name, n_split=1, comm=None):
    xb, yb = x.ndim == 3, dy.ndim == 3
    T, K = x.shape[-2:]
    N = dy.shape[-1] // n_split
    B = x.shape[0] if xb else dy.shape[0] if yb else n_split
    tt = min(TT, T)
    nt = T // tt

    def body(x_ref, dy_ref, o_ref, oc_ref):
        t = pl.program_id(1)

        @pl.when(t == 0)
        def _():
            o_ref[...] = jnp.zeros_like(o_ref)

        o_ref[...] += _dot_tn(x_ref[...].astype(CDT), dy_ref[...])

        @pl.when(t == nt - 1)
        def _():
            oc_ref[...] = o_ref[...].astype(CDT)

    x_spec = pl.BlockSpec((None, tt, K), lambda b, t: (b, t, 0)) if xb else pl.BlockSpec((tt, K), lambda b, t: (t, 0))
    if yb:
        y_spec = pl.BlockSpec((None, tt, N), lambda b, t: (b, t, 0))
    else:
        y_spec = pl.BlockSpec((tt, N), lambda b, t: (t, b if n_split > 1 else 0))
    o_spec = pl.BlockSpec((None, K, N), lambda b, t: (b, 0, 0))
    res, got = _pallas(
        body, grid=(B, nt), name=name, in_specs=[x_spec, y_spec], out_specs=[o_spec, o_spec],
        out_shape=[jax.ShapeDtypeStruct((B, K, N), F32), jax.ShapeDtypeStruct((B, K, N), CDT)], args=(x, dy), comm=comm)
    return res if comm is None else (res, got)


def _fold_dup(dw2, name):
    def body(x_ref, o_ref, oc_ref):
        x = x_ref[...]
        y = jnp.concatenate([x[:, g * LW:g * LW + HD] + x[:, g * LW + HD:(g + 1) * LW] for g in range(NKV)], axis=1)
        o_ref[...] = y
        oc_ref[...] = y.astype(CDT)

    return pl.pallas_call(
        body, grid=(NSH,), name=name, in_specs=[_tile(D // NSH, KVW)], out_specs=[_tile(D // NSH, NKV * HD)] * 2,
        out_shape=[jax.ShapeDtypeStruct((D, NKV * HD), F32), jax.ShapeDtypeStruct((D, NKV * HD), CDT)],
        compiler_params=_cp(),
    )(dw2)


def _all_reduce_small(vec):
    R = vec.shape[0]
    ndev = 8

    def body(v_ref, o_ref, buf, send, recv):
        x, y, c, _ = _place()
        me = 4 * x + 2 * y + c
        buf[me] = v_ref[...]
        copies = []
        for k in range(1, ndev):
            peer = (x if not k & 4 else 1 - x, y if not k & 2 else 1 - y, c if not k & 1 else 1 - c)
            copies.append(pltpu.make_async_remote_copy(
                src_ref=v_ref, dst_ref=buf.at[me], send_sem=send.at[k - 1], recv_sem=recv.at[k - 1],
                device_id=peer, device_id_type=MESH))
            copies[-1].start()
        for k in range(1, ndev):
            peer = (x if not k & 4 else 1 - x, y if not k & 2 else 1 - y, c if not k & 1 else 1 - c)
            pltpu.make_async_remote_copy(
                src_ref=v_ref, dst_ref=buf.at[4 * peer[0] + 2 * peer[1] + peer[2]], send_sem=send.at[k - 1],
                recv_sem=recv.at[k - 1], device_id=peer, device_id_type=MESH).wait_recv()
        for cp in copies:
            cp.wait_send()
        tot = buf[0]
        for d in range(1, ndev):
            tot = tot + buf[d]
        o_ref[...] = tot

    return pl.pallas_call(
        body, name="all_reduce_small", in_specs=[pl.BlockSpec(memory_space=pltpu.VMEM)],
        out_specs=pl.BlockSpec(memory_space=pltpu.VMEM), out_shape=jax.ShapeDtypeStruct(vec.shape, F32),
        scratch_shapes=[pltpu.VMEM((ndev, R, D), F32), pltpu.SemaphoreType.DMA((ndev - 1,)),
                        pltpu.SemaphoreType.DMA((ndev - 1,))],
    )(vec)


def _row_tile(rows):
    for t in range(min(rows, 512) // 8 * 8, 7, -8):
        if rows % t == 0:
            return t
    return rows


def _sum_own_and_received(own, got, me, name):
    _, R, C = own.shape
    tr = _row_tile(R)

    def body(me_ref, own_ref, got_ref, o_ref):
        o_ref[...] = ((own_ref[...] + got_ref[0].astype(F32)) + got_ref[1].astype(F32)) + got_ref[2].astype(F32)

    return pl.pallas_call(
        body, name=name,
        grid_spec=pltpu.PrefetchScalarGridSpec(
            num_scalar_prefetch=1, grid=(R // tr,),
            in_specs=[pl.BlockSpec((None, tr, C), lambda i, me_ref: (me_ref[0], i, 0)),
                      pl.BlockSpec((3, tr, C), lambda i, me_ref: (0, i, 0))],
            out_specs=pl.BlockSpec((tr, C), lambda i, me_ref: (i, 0))),
        out_shape=jax.ShapeDtypeStruct((R, C), F32), compiler_params=_cp(),
    )(me, own, got)


def _adamw(gs, w, m, v, name):
    L, R, C = w.shape
    tr = _row_tile(R)
    ng = len(gs[0])
    c1 = 1.0 - ADAM_B1 ** ADAM_STEP
    c2 = 1.0 - ADAM_B2 ** ADAM_STEP

    def body(*refs):
        w_ref, m_ref, v_ref, g_ref, d_ref, nm_ref, nv_ref = refs[L * ng:]
        lay = pl.program_id(0)
        g = None
        for l in range(L):
            gl = refs[l * ng][...]
            for r in refs[l * ng + 1:(l + 1) * ng]:
                gl = gl + r[...]
            g = gl if g is None else jnp.where(lay == l, gl, g)
        nm = ADAM_B1 * m_ref[...] + (1.0 - ADAM_B1) * g
        nv = ADAM_B2 * v_ref[...] + (1.0 - ADAM_B2) * (g * g)
        g_ref[...] = g
        nm_ref[...] = nm
        nv_ref[...] = nv
        d_ref[...] = -ADAM_LR * ((nm / c1) / (jnp.sqrt(nv / c2) + ADAM_EPS) + ADAM_WD * w_ref[...])

    gspecs = [pl.BlockSpec((tr, C), lambda lay, i, l=l: (jnp.where(lay == l, i, 0), 0)) for l in range(L) for _ in range(ng)]
    spec = pl.BlockSpec((None, tr, C), lambda lay, i: (lay, i, 0))
    return pl.pallas_call(
        body, grid=(L, R // tr), name=name, in_specs=gspecs + [spec] * 3, out_specs=[spec] * 4,
        out_shape=[jax.ShapeDtypeStruct((L, R, C), F32)] * 4, compiler_params=_cp(2),
    )(*[g for gl in gs for g in gl], w, m, v)


def _rope_tables(T):
    inv = 1.0 / (ROPE_THETA ** (jnp.arange(0, HD, 2, dtype=F32) / HD))
    ang = jnp.arange(T, dtype=F32)[:, None] * inv[None, :]
    c, s = jnp.cos(ang), jnp.sin(ang)
    return jnp.tile(jnp.concatenate([c, c], axis=1), (1, 2)), jnp.tile(jnp.concatenate([-s, s], axis=1), (1, 2))


def _dup_cols(w):
    return jnp.broadcast_to(w.reshape(D, NKV, 1, HD), (D, NKV, 2, HD)).reshape(D, KVW)


def kernel(x, p, mix_pre_g, mix_post_g, ffn_pre_g, ffn_post_g, pool_w, pool_scale, kv_norm_g, w_k, w_v, w_q, w_o, sinks, w_ff_gate, w_ff_up, w_ff_down, ple_norm_g, w_ple_gate, w_ple_proj, loss_target, m_mix_pre_g, m_mix_post_g, m_ffn_pre_g, m_ffn_post_g, m_pool_w, m_pool_scale, m_kv_norm_g, m_w_k, m_w_v, m_w_q, m_w_o, m_sinks, m_w_ff_gate, m_w_ff_up, m_w_ff_down, m_ple_norm_g, m_w_ple_gate, m_w_ple_proj, v_mix_pre_g, v_mix_post_g, v_ffn_pre_g, v_ffn_post_g, v_pool_w, v_pool_scale, v_kv_norm_g, v_w_k, v_w_v, v_w_q, v_w_o, v_sinks, v_w_ff_gate, v_w_ff_up, v_w_ff_down, v_ple_norm_g, v_w_ple_gate, v_w_ple_proj):
    depth = mix_pre_g.shape[0]
    n_pool = pool_w.shape[0]
    n_attn = w_q.shape[0]
    T = x.shape[1]
    h = x[0]
    p = p[:, 0]
    tgt = loss_target[0]
    me = (2 * lax.axis_index("x") + lax.axis_index("y")).astype(jnp.int32).reshape(1)

    def t12(a):
        return jnp.transpose(a, (0, 2, 1))

    ff_t = {k: tuple(t12(a) for a in v) for k, v in dict(
        w_ff_gate=(w_ff_gate, m_w_ff_gate, v_w_ff_gate), w_ff_up=(w_ff_up, m_w_ff_up, v_w_ff_up)).items()}
    c_pool, c_wk, c_wv, c_wq, c_wo, c_wg, c_wu, c_wd, c_wpg, c_wpp = (w.astype(CDT) for w in (
        pool_w, w_k, w_v, w_q, w_o, ff_t["w_ff_gate"][0], ff_t["w_ff_up"][0], w_ff_down, w_ple_gate, w_ple_proj))

    def ffn_srcs(l):
        return [(c_wg, l), (c_wu, l), (c_wd, l)]

    def ple_srcs(l):
        return [(c_wpg, l), (c_wpp, l)]

    (gpw,) = _run_comm(_gather_comm([(c_pool, None)]), "gather_pool_w")
    wff, wple, wattn = {}, {}, {}
    pw = jnp.transpose(gpw, (1, 2, 0, 3, 4)).reshape(n_pool, len(POOL_WINDOWS), PG, PG)
    kv_g = kv_norm_g.reshape(1, D)
    cos, sin = _rope_tables(T)

    pscale = _all_reduce_small(jnp.pad(
        lax.dynamic_update_slice(jnp.zeros((n_pool, D), F32), pool_scale, (0, me[0] * (D // NSH))),
        ((0, 8 - n_pool), (0, 0))))[:n_pool] * 0.5

    saved = []
    kk = vv = wk2 = wv2 = None
    for l in range(depth):
        s = {"h": h}
        if l < n_pool:
            (h1,), got = _pool_fwd(h, mix_pre_g, pw, pscale, mix_post_g, l,
                                   comm=_GatherHalves(ffn_srcs(0)) if l == 0 else None)
            if l == 0:
                wff[0] = got
        else:
            j = l - n_pool
            wq, wo = wattn[l]
            s["q"] = _q_fwd(h, mix_pre_g, wq, cos, sin, l, j)
            s["o"] = _attn_fwd(s["q"], kk, vv, sinks, l, j)
            h1, s["m"] = _o_fwd(s["o"], h, wo, mix_post_g, l, j)
        s["h1"] = h1
        srcs = []
        if l + 1 < depth:
            srcs = ffn_srcs(l + 1)
            if l + 1 >= n_pool:
                srcs += [(c_wq, l + 1 - n_pool), (c_wo, l + 1 - n_pool)]
            if l + 1 == n_pool:
                srcs += [(c_wk, None), (c_wv, None)]
        if l == 0:
            srcs += ple_srcs(0)
        (h2, s["a"], s["b"], s["f"]), got = _ffn_fwd(
            h1, ffn_pre_g, *wff[l], ffn_post_g, l, comm=_gather_comm(srcs) if srcs else None)
        if l == 0:
            wple[0] = got[-2:]
        if l + 1 < depth:
            wff[l + 1] = got[:3]
            if l + 1 >= n_pool:
                wattn[l + 1] = got[3:5]
            if l + 1 == n_pool:
                wk2 = _dup_cols(got[5].reshape(D, NKV * HD))
                wv2 = _dup_cols(got[6].reshape(D, NKV * HD))
        s["h2"] = h2
        (h,), got = _ple_fwd(h2, p, ple_norm_g, *wple[l], l,
                             comm=_gather_comm(ple_srcs(l + 1)) if l + 1 < depth else None)
        if l + 1 < depth:
            wple[l + 1] = got
        if l == n_pool - 1:
            s["hkv"] = h
            kk, vv = _kv_fwd(h, kv_g, wk2, wv2, cos, sin)
        saved.append(s)

    dh, loss_row = _loss_grad(h, tgt)

    entries, pending, to_swap = [], [], []

    def add_grad(k, l, g32, g16):
        def by_shard(g):
            return g.reshape(NSH, -1, g.shape[-1])
        e = dict(k=k, l=l, own=by_shard(g32), sent=by_shard(g16))
        entries.append(e)
        pending.append(e)

    def carry():
        take, swap = list(pending), list(to_swap)
        del pending[:], to_swap[:]
        return (take, swap), _join_comms([_scatter_comm([e["sent"] for e in take]) if take else None,
                                          _swap_comm([e["part"] for e in swap]) if swap else None])

    def landed(taken, got):
        take, swap = taken
        for e, g in zip(take, got[:len(take)]):
            e["part"] = _sum_own_and_received(e["own"], g, me, f"sum_{e['k']}_{e['l']}")
            to_swap.append(e)
        for e, g in zip(swap, got[len(take):]):
            e["sib"] = g

    small = {k: [None] * depth for k in ("mix_pre", "mix_post", "ffn_pre", "ffn_post", "ple")}
    dpool_scale = [None] * n_pool
    dsinks = [None] * n_attn
    dks, dkhs, dvs, dvhs = [], [], [], []
    for l in reversed(range(depth)):
        s = saved[l]
        if l == n_pool - 1:
            dh, hk, dk0, dvc, dkv_g = _kv_bwd(dh, s["hkv"], dks, dkhs, dvs, dvhs, kv_g, wk2, wv2, cos, sin)
            add_grad("w_k", None, *_fold_dup(_mm_tn(hk, dk0, "dw_k")[0][0], "fold_w_k"))
            add_grad("w_v", None, *_fold_dup(_mm_tn(hk, dvc, "dw_v")[0][0], "fold_w_v"))
        dh, pn, dz, dpp, pc, small["ple"][l] = _ple_bwd(dh, s["h2"], p, ple_norm_g, *wple[l], l)
        add_grad("w_ple_gate", l, *_mm_tn(pn, dz, f"dw_ple_gate_{l}"))
        add_grad("w_ple_proj", l, *_mm_tn(pc, dpp, f"dw_ple_proj_{l}", n_split=NSH))
        take, comm = carry()
        (dh, fn, df, act, da, db, small["ffn_pre"][l], small["ffn_post"][l]), got = _ffn_bwd(
            dh, s["h1"], s["f"], s["a"], s["b"], ffn_pre_g, *wff[l], ffn_post_g, l, comm=comm)
        landed(take, got)
        add_grad("w_ff_gate", l, *_mm_tn(da, fn, f"dw_ff_gate_{l}"))
        take, comm = carry() if l == 0 else (([], []), None)
        res = _mm_tn(db, fn, f"dw_ff_up_{l}", comm=comm)
        if comm is not None:
            res, got = res
            landed(take, got)
        add_grad("w_ff_up", l, *res)
        take, comm = carry() if l == 0 else (([], []), None)
        res = _mm_tn(act, df, f"dw_ff_down_{l}", comm=comm)
        if comm is not None:
            res, got = res
            landed(take, got)
        add_grad("w_ff_down", l, *res)
        if l < n_pool:
            take, comm = carry() if l == 0 else (([], []), None)
            (dpool, dpw, dpool_scale[l], small["mix_post"][l]), got = _pool_bwd_a(
                dh, s["h"], mix_pre_g, pw, pscale, mix_post_g, l, comm=comm)
            landed(take, got)
            dpw = jnp.transpose(dpw.reshape(len(POOL_WINDOWS), NSH, PG // NSH, PG), (1, 0, 2, 3))
            dpw = dpw.reshape(NSH, len(POOL_WINDOWS) * PG // NSH, PG)
            add_grad("pool_w", l, dpw, dpw.astype(CDT))
            (dh, small["mix_pre"][l]), _ = _pool_bwd_b(dpool, dh, s["h"], mix_pre_g, l)
        else:
            j = l - n_pool
            wq, wo = wattn[l]
            do, dm, small["mix_post"][l] = _o_bwd(dh, s["m"], wo, mix_post_g, l, j)
            add_grad("w_o", j, *_mm_tn(s["o"], dm, f"dw_o_{l}"))
            dq, dk, dkh, dv, dvh, dsinks[j] = _attn_bwd(s["q"], kk, vv, s["o"], do, sinks, l, j)
            dks.append(dk); dkhs.append(dkh); dvs.append(dv); dvhs.append(dvh)
            dh, hn, dq0, small["mix_pre"][l] = _q_bwd(dq, dh, s["h"], mix_pre_g, wq, cos, sin, l, j)
            add_grad("w_q", j, *_mm_tn(hn, dq0, f"dw_q_{l}"))
    rest = 0
    while pending or to_swap:
        take, comm = carry()
        landed(take, _run_comm(comm, f"exchange_rest_{rest}"))
        rest += 1
    grad_x = dh[None]

    rows = [jnp.concatenate(small[k], axis=0) for k in ("mix_pre", "mix_post", "ffn_pre", "ffn_post", "ple")]
    rows += [dkv_g, jnp.concatenate(dpool_scale, axis=0)]
    rows += [jnp.pad(jnp.concatenate(dsinks, axis=0), ((0, 0), (0, D - LW))), jnp.pad(loss_row, ((0, 0), (0, D - 128)))]
    vec = jnp.concatenate(rows, axis=0)
    nrow = vec.shape[0]
    vec = _all_reduce_small(jnp.pad(vec, ((0, -nrow % 8), (0, 0))))
    o = 0
    red = {}
    for k in ("mix_pre_g", "mix_post_g", "ffn_pre_g", "ffn_post_g", "ple_norm_g"):
        red[k] = vec[o:o + depth]
        o += depth
    red["kv_norm_g"] = vec[o:o + 1]
    red["pool_scale"] = lax.dynamic_slice(vec[o + 1:o + 1 + n_pool], (0, me[0] * (D // NSH)), (n_pool, D // NSH))
    o += 1 + n_pool
    red["sinks"] = vec[o:o + n_attn, :GQ * NKV]
    loss = vec[o + n_attn, 0]


    given = dict(
        mix_pre_g=(mix_pre_g, m_mix_pre_g, v_mix_pre_g), mix_post_g=(mix_post_g, m_mix_post_g, v_mix_post_g),
        ffn_pre_g=(ffn_pre_g, m_ffn_pre_g, v_ffn_pre_g), ffn_post_g=(ffn_post_g, m_ffn_post_g, v_ffn_post_g),
        pool_w=(pool_w, m_pool_w, v_pool_w), pool_scale=(pool_scale, m_pool_scale, v_pool_scale),
        kv_norm_g=(kv_norm_g, m_kv_norm_g, v_kv_norm_g), w_k=(w_k, m_w_k, v_w_k), w_v=(w_v, m_w_v, v_w_v),
        w_q=(w_q, m_w_q, v_w_q), w_o=(w_o, m_w_o, v_w_o), sinks=(sinks, m_sinks, v_sinks),
        w_ff_gate=ff_t["w_ff_gate"], w_ff_up=ff_t["w_ff_up"],
        w_ff_down=(w_ff_down, m_w_ff_down, v_w_ff_down), ple_norm_g=(ple_norm_g, m_ple_norm_g, v_ple_norm_g),
        w_ple_gate=(w_ple_gate, m_w_ple_gate, v_w_ple_gate), w_ple_proj=(w_ple_proj, m_w_ple_proj, v_w_ple_proj))
    results = {}
    for k, g in red.items():
        w, m, v = given[k]
        outs = _adamw([[g.reshape(-1, g.shape[-1])]], *(t.reshape(1, -1, t.shape[-1]) for t in (w, m, v)), f"adamw_{k}")
        results[k] = [t.reshape(w.shape) for t in outs]
    per_layer = {}
    for e in entries:
        per_layer.setdefault(e["k"], {})[e["l"] or 0] = [e["part"], e["sib"]]
    for k, layers in per_layer.items():
        w, m, v = given[k]
        gs = [layers[l] for l in range(len(layers))]
        outs = _adamw(gs, *(t.reshape((len(gs),) + gs[0][0].shape) for t in (w, m, v)), f"adamw_{k}")
        results[k] = [t12(t) if k in ff_t else t.reshape(w.shape) for t in outs]

    order = ["mix_pre_g", "mix_post_g", "ffn_pre_g", "ffn_post_g", "pool_w", "pool_scale", "kv_norm_g", "w_k", "w_v",
             "w_q", "w_o", "sinks", "w_ff_gate", "w_ff_up", "w_ff_down", "ple_norm_g", "w_ple_gate", "w_ple_proj"]
    return (loss, grad_x, *[results[k][0] for k in order], *[results[k][1] for k in order],
            *[results[k][2] for k in order], *[results[k][3] for k in order])
```

```python
import functools

import jax
import jax.numpy as jnp
from jax import lax
from jax.experimental import pallas as pl
from jax.experimental.pallas import tpu as pltpu

F32 = jnp.float32
CDT = jnp.bfloat16

D = 1024
PG = 256
POOL_WINDOWS = (2, 4, 8, 16)
HALO = 16
HD = 64
NKV = 4
WIN = 128
FS = 704
NSH = 4
PLE = 256
ROPE_THETA = 10000.0
EPS = 1e-6
NEG = -1e30
QSCALE = HD ** -0.5

ADAM_LR, ADAM_B1, ADAM_B2, ADAM_EPS, ADAM_WD, ADAM_STEP = 0.001, 0.9, 0.999, 1e-08, 0.01, 10

VMEM_LIMIT = 56 * 1024 * 1024
TM = 1024
TM_FFN = 512
TM_FFN_BWD = 256
TQ = 512
TT = 4096
MESH = pl.DeviceIdType.MESH


def _cp(n_axes=1):
    return pltpu.CompilerParams(dimension_semantics=("arbitrary",) * n_axes, vmem_limit_bytes=VMEM_LIMIT)


def _dot(a, b):
    return jnp.dot(a, b, preferred_element_type=F32)


def _dot_nt(a, b):
    return lax.dot_general(a, b, (((1,), (1,)), ((), ())), preferred_element_type=F32)


def _dot_tn(a, b):
    return lax.dot_general(a, b, (((0,), (0,)), ((), ())), preferred_element_type=F32)


def _rms_fwd(x, g):
    r = lax.rsqrt(jnp.mean(x * x, axis=-1, keepdims=True) + EPS)
    return x * r * g


def _rms_bwd(dy, x, g):
    r = lax.rsqrt(jnp.mean(x * x, axis=-1, keepdims=True) + EPS)
    xh = x * r
    dg = jnp.sum(dy * xh, axis=0, keepdims=True)
    u = dy * g
    dx = r * (u - xh * jnp.mean(u * xh, axis=-1, keepdims=True))
    return dx, dg


def _sigmoid(a):
    return 1.0 / (1.0 + jnp.exp(-a))


def _swap_half(x):
    w = x.shape[1]
    lane = lax.broadcasted_iota(jnp.int32, x.shape, 1)
    return jnp.where((lane % HD) < HD // 2, pltpu.roll(x, w - HD // 2, axis=1), pltpu.roll(x, HD // 2, axis=1))


def _rope_fwd(x, cos, sin):
    return x * cos + _swap_half(x) * sin


def _rope_bwd(d, cos, sin):
    return d * cos + _swap_half(d * sin)


def _tile(tm, w):
    return pl.BlockSpec((tm, w), lambda i: (i, 0))


def _res(shape):
    return pl.BlockSpec(shape, lambda i: (0,) * len(shape), pipeline_mode=pl.Buffered(1))


def _layer_w(shape):
    return pl.BlockSpec((NSH,) + shape, lambda *_: (0, 0, 0), pipeline_mode=pl.Buffered(1))


def _acc(w):
    return pl.BlockSpec((1, w), lambda i: (0, 0))


def _zero_first(*refs):
    @pl.when(pl.program_id(0) == 0)
    def _():
        for r in refs:
            r[...] = jnp.zeros_like(r)


HBM = pl.BlockSpec(memory_space=pl.ANY)


def _place():
    x, y, c = lax.axis_index("x"), lax.axis_index("y"), lax.axis_index("c")
    chips = [(1 - x, y), (x, 1 - y), (1 - x, 1 - y)]
    return x, y, c, chips


class _Comm:
    def __init__(self, args, out_shape, sems, copies):
        self.args, self.out_shape, self.sems, self.copies = list(args), list(out_shape), list(sems), copies

    def start(self, cin, cout, sems):
        local, sends, _ = self.copies(cin, cout, sems)
        for cp in local + sends:
            cp.start()

    def wait(self, cin, cout, sems):
        local, sends, recvs = self.copies(cin, cout, sems)
        for cp in recvs:
            cp.wait_recv()
        for cp in sends:
            cp.wait_send()
        for cp in local:
            cp.wait()


def _gather_comm(srcs):
    n = len(srcs)
    shapes = [(a.shape if l is None else a.shape[1:]) for a, l in srcs]

    def copies(cin, cout, sems):
        send, recv, loc = sems
        x, y, c, chips = _place()
        me = 2 * x + y
        local, sends, recvs = [], [], []
        for a, (_, l) in enumerate(srcs):
            src = cin[a] if l is None else cin[a].at[l]
            local.append(pltpu.make_async_copy(src, cout[a].at[me], loc.at[a]))
            for j, (px, py) in enumerate(chips):
                kw = dict(src_ref=src, send_sem=send.at[3 * a + j], recv_sem=recv.at[3 * a + j],
                          device_id=(px, py, c), device_id_type=MESH)
                sends.append(pltpu.make_async_remote_copy(dst_ref=cout[a].at[me], **kw))
                recvs.append(pltpu.make_async_remote_copy(dst_ref=cout[a].at[2 * px + py], **kw))
        return local, sends, recvs

    return _Comm([a for a, _ in srcs], [jax.ShapeDtypeStruct((NSH,) + s, CDT) for s in shapes],
                 [pltpu.SemaphoreType.DMA((3 * n,)), pltpu.SemaphoreType.DMA((3 * n,)), pltpu.SemaphoreType.DMA((n,))],
                 copies)


class _GatherHalves(_Comm):
    def __init__(self, srcs):
        n = len(srcs)
        self.srcs = srcs
        super().__init__([a for a, _ in srcs], [jax.ShapeDtypeStruct((NSH,) + a.shape[1:], CDT) for a, _ in srcs],
                         [pltpu.SemaphoreType.DMA((3 * n,))] * 4 + [pltpu.SemaphoreType.DMA((n,))], None)

    def _copies(self, cin, cout, sems):
        isend, irecv, dsend, drecv, loc = sems
        x, y, c, chips = _place()
        me = 2 * x + y
        local, ici_out, ici_in, d2d_out, d2d_in = [], [], [], [], []
        for a, (arr, l) in enumerate(self.srcs):
            hr = arr.shape[1] // 2
            mine = pl.ds(pl.multiple_of(c * hr, hr), hr)
            other = pl.ds(pl.multiple_of((1 - c) * hr, hr), hr)
            src = cin[a].at[l]
            local.append(pltpu.make_async_copy(src, cout[a].at[me], loc.at[a]))
            for j, (px, py) in enumerate(chips):
                k, sj = 3 * a + j, 2 * px + py
                ici = dict(src_ref=src.at[mine], send_sem=isend.at[k], recv_sem=irecv.at[k],
                           device_id=(px, py, c), device_id_type=MESH)
                ici_out.append(pltpu.make_async_remote_copy(dst_ref=cout[a].at[me, mine], **ici))
                ici_in.append(pltpu.make_async_remote_copy(dst_ref=cout[a].at[sj, mine], **ici))
                d2d = dict(src_ref=cout[a].at[sj, mine], send_sem=dsend.at[k], recv_sem=drecv.at[k],
                           device_id=(x, y, 1 - c), device_id_type=MESH)
                d2d_out.append(pltpu.make_async_remote_copy(dst_ref=cout[a].at[sj, mine], **d2d))
                d2d_in.append(pltpu.make_async_remote_copy(dst_ref=cout[a].at[sj, other], **d2d))
        return local, ici_out, ici_in, d2d_out, d2d_in

    def start(self, cin, cout, sems):
        local, ici_out, _, _, _ = self._copies(cin, cout, sems)
        for cp in local + ici_out:
            cp.start()

    def wait(self, cin, cout, sems):
        local, ici_out, ici_in, d2d_out, d2d_in = self._copies(cin, cout, sems)
        for got, fwd in zip(ici_in, d2d_out):
            got.wait_recv()
            fwd.start()
        for cp in d2d_in:
            cp.wait_recv()
        for cp in d2d_out + ici_out:
            cp.wait_send()
        for cp in local:
            cp.wait()


def _scatter_comm(grads):
    n = len(grads)

    def copies(cin, cout, sems):
        send, recv = sems
        x, y, c, chips = _place()
        sends, recvs = [], []
        for a in range(n):
            for j, (px, py) in enumerate(chips):
                cp = pltpu.make_async_remote_copy(
                    src_ref=cin[a].at[2 * px + py], dst_ref=cout[a].at[j], send_sem=send.at[3 * a + j],
                    recv_sem=recv.at[3 * a + j], device_id=(px, py, c), device_id_type=MESH)
                sends.append(cp)
                recvs.append(cp)
        return [], sends, recvs

    return _Comm(grads, [jax.ShapeDtypeStruct((3,) + g.shape[1:], g.dtype) for g in grads],
                 [pltpu.SemaphoreType.DMA((3 * n,)), pltpu.SemaphoreType.DMA((3 * n,))], copies)


def _swap_comm(parts):
    n = len(parts)

    def copies(cin, cout, sems):
        send, recv = sems
        x, y, c, _ = _place()
        cps = [pltpu.make_async_remote_copy(
            src_ref=cin[a], dst_ref=cout[a], send_sem=send.at[a], recv_sem=recv.at[a],
            device_id=(x, y, 1 - c), device_id_type=MESH) for a in range(n)]
        return [], cps, cps

    return _Comm(parts, [jax.ShapeDtypeStruct(p.shape, p.dtype) for p in parts],
                 [pltpu.SemaphoreType.DMA((n,)), pltpu.SemaphoreType.DMA((n,))], copies)


def _join_comms(comms):
    comms = [c for c in comms if c is not None]
    if len(comms) <= 1:
        return comms[0] if comms else None

    def copies(cin, cout, sems):
        out = ([], [], [])
        i = o = k = 0
        for c in comms:
            part = c.copies(cin[i:i + len(c.args)], cout[o:o + len(c.out_shape)], sems[k:k + len(c.sems)])
            for acc, cps in zip(out, part):
                acc.extend(cps)
            i, o, k = i + len(c.args), o + len(c.out_shape), k + len(c.sems)
        return out

    return _Comm([a for c in comms for a in c.args], [s for c in comms for s in c.out_shape],
                 [s for c in comms for s in c.sems], copies)


def _run_comm(comm, name):
    k_in, k_out = len(comm.args), len(comm.out_shape)

    def body(*refs):
        cin, cout, sems = refs[:k_in], refs[k_in:k_in + k_out], refs[k_in + k_out:]
        comm.start(cin, cout, sems)
        comm.wait(cin, cout, sems)

    return pl.pallas_call(body, name=name, in_specs=[HBM] * k_in, out_specs=[HBM] * k_out, out_shape=comm.out_shape,
                          scratch_shapes=comm.sems)(*comm.args)


def _pallas(body, *, name, grid, in_specs, out_specs, out_shape, args, scratch=(), comm=None):
    out_specs, out_shape, scratch = list(out_specs), list(out_shape), list(scratch)
    params = _cp(len(grid))
    if comm is None:
        return pl.pallas_call(body, grid=grid, name=name, in_specs=in_specs, out_specs=out_specs, out_shape=out_shape,
                              scratch_shapes=scratch, compiler_params=params)(*args), []
    n_in, n_out, n_sc = len(args), len(out_shape), len(scratch)
    k_in, k_out = len(comm.args), len(comm.out_shape)

    def carrying(*refs):
        ins, refs = refs[:n_in], refs[n_in:]
        cin, refs = refs[:k_in], refs[k_in:]
        outs, refs = refs[:n_out], refs[n_out:]
        cout, refs = refs[:k_out], refs[k_out:]
        sc, sems = refs[:n_sc], refs[n_sc:]
        first = functools.reduce(jnp.logical_and, [pl.program_id(d) == 0 for d in range(len(grid))])
        last = functools.reduce(jnp.logical_and, [pl.program_id(d) == g - 1 for d, g in enumerate(grid)])

        @pl.when(first)
        def _():
            comm.start(cin, cout, sems)

        body(*ins, *outs, *sc)

        @pl.when(last)
        def _():
            comm.wait(cin, cout, sems)

    res = pl.pallas_call(
        carrying, grid=grid, name=name, in_specs=list(in_specs) + [HBM] * k_in, out_specs=out_specs + [HBM] * k_out,
        out_shape=out_shape + comm.out_shape, scratch_shapes=scratch + comm.sems, compiler_params=params,
    )(*args, *comm.args)
    return res[:n_out], res[n_out:]


def _pool_normed_ext(h_ref, halo_ref, g, tm):
    xe = jnp.concatenate([halo_ref[...], h_ref[...]], axis=0)
    hn = _rms_fwd(xe, g)
    row = lax.broadcasted_iota(jnp.int32, (tm + HALO, 1), 0)
    return jnp.where((row >= HALO) | (pl.program_id(0) > 0), hn, 0.0)


WBLK = 128


def _window_sums(xe, tm, w, trailing):
    blk = min(WBLK, tm)
    r = lax.broadcasted_iota(jnp.int32, (blk, blk + HALO), 0)
    j = lax.broadcasted_iota(jnp.int32, (blk, blk + HALO), 1)
    d = r + HALO - j if trailing else j - r
    band = ((d >= 1) & (d < w)).astype(CDT)
    xc = xe.astype(CDT)
    return jnp.concatenate([_dot(band, xc[b * blk:(b + 1) * blk + HALO, :]) for b in range(tm // blk)], axis=0)


def _pool_windows(hn_e, tm):
    t = lax.broadcasted_iota(jnp.int32, (tm, PG), 0) + pl.program_id(0) * tm
    outs = []
    for g, w in enumerate(POOL_WINDOWS):
        xe = hn_e[:, g * PG:(g + 1) * PG]
        cnt = jnp.minimum(t + 1, w).astype(F32)
        x = xe[HALO:, :]
        outs.append((_window_sums(xe, tm, w, True) + x) / cnt - x)
    return jnp.concatenate(outs, axis=1)


def _pool_apply(pooled, w_ref):
    return jnp.concatenate([_dot(pooled[:, g * PG:(g + 1) * PG], w_ref[g]) for g in range(len(POOL_WINDOWS))], axis=1)


def _halo_prev(tm):
    return pl.BlockSpec((HALO, D), lambda i: (jnp.maximum(i * (tm // HALO) - 1, 0), 0))


def _pool_fwd(h, g_pre, pw, pscale, g_post, l, comm=None):
    T = h.shape[0]
    tm = min(TM, T)

    def body(h_ref, halo_ref, gpre_ref, w_ref, sc_ref, gpost_ref, o_ref):
        hn_e = _pool_normed_ext(h_ref, halo_ref, gpre_ref[l:l + 1, :], tm)
        pooled = _pool_windows(hn_e, tm).astype(CDT)
        m = _pool_apply(pooled, w_ref) * sc_ref[l:l + 1, :]
        o_ref[...] = h_ref[...] + _rms_fwd(m, gpost_ref[l:l + 1, :])

    return _pallas(
        body, grid=(T // tm,), name=f"pool_fwd_{l}",
        in_specs=[_tile(tm, D), _halo_prev(tm), _res(g_pre.shape),
                  pl.BlockSpec((None,) + pw.shape[1:], lambda i: (l, 0, 0, 0), pipeline_mode=pl.Buffered(1)),
                  _res(pscale.shape), _res(g_post.shape)],
        out_specs=[_tile(tm, D)], out_shape=[jax.ShapeDtypeStruct((T, D), F32)],
        args=(h, h, g_pre, pw, pscale, g_post), comm=comm)


def _pool_bwd_a(dh1, h, g_pre, pw, pscale, g_post, l, comm=None):
    T = h.shape[0]
    tm = min(TM, T)
    ng = len(POOL_WINDOWS)

    def body(dh_ref, h_ref, halo_ref, gpre_ref, w_ref, sc_ref, gpost_ref, dp_ref, dw_ref, dsc_ref, dgpost_ref):
        _zero_first(dw_ref, dsc_ref, dgpost_ref)
        hn_e = _pool_normed_ext(h_ref, halo_ref, gpre_ref[l:l + 1, :], tm)
        pooled = _pool_windows(hn_e, tm).astype(CDT)
        y = _pool_apply(pooled, w_ref)
        sc = sc_ref[l:l + 1, :]
        dm, dg = _rms_bwd(dh_ref[...], y * sc, gpost_ref[l:l + 1, :])
        dgpost_ref[...] += dg
        dsc_ref[...] += jnp.sum(dm * y, axis=0, keepdims=True)
        dy = (dm * sc).astype(CDT)
        dps = []
        for g in range(ng):
            dyg = dy[:, g * PG:(g + 1) * PG]
            dw_ref[g] += _dot_tn(pooled[:, g * PG:(g + 1) * PG], dyg)
            dps.append(_dot_nt(dyg, w_ref[g]))
        dp_ref[...] = jnp.concatenate(dps, axis=1)

    return _pallas(
        body, grid=(T // tm,), name=f"pool_bwd_a_{l}",
        in_specs=[_tile(tm, D), _tile(tm, D), _halo_prev(tm), _res(g_pre.shape),
                  pl.BlockSpec((None,) + pw.shape[1:], lambda i: (l, 0, 0, 0), pipeline_mode=pl.Buffered(1)),
                  _res(pscale.shape), _res(g_post.shape)],
        out_specs=[_tile(tm, D), pl.BlockSpec((ng, PG, PG), lambda i: (0, 0, 0)), _acc(D), _acc(D)],
        out_shape=[jax.ShapeDtypeStruct((T, D), F32), jax.ShapeDtypeStruct((ng, PG, PG), F32),
                   jax.ShapeDtypeStruct((1, D), F32), jax.ShapeDtypeStruct((1, D), F32)],
        args=(dh1, h, h, g_pre, pw, pscale, g_post), comm=comm)


def _pool_bwd_b(dpool, dh1, h, g_pre, l, comm=None):
    T = h.shape[0]
    tm = min(TM, T)
    nt = T // tm

    def body(dp_ref, nxt_ref, dh_ref, h_ref, gpre_ref, o_ref, dgpre_ref):
        _zero_first(dgpre_ref)
        i = pl.program_id(0)
        dp = dp_ref[...]
        e_e = jnp.concatenate([dp, jnp.where(i < nt - 1, nxt_ref[...], 0.0)], axis=0)
        t = lax.broadcasted_iota(jnp.int32, (tm + HALO, PG), 0) + i * tm
        outs = []
        for g, w in enumerate(POOL_WINDOWS):
            e = e_e[:, g * PG:(g + 1) * PG] / jnp.minimum(t + 1, w).astype(F32)
            outs.append(_window_sums(e, tm, w, False) + (e[:tm, :] - dp[:, g * PG:(g + 1) * PG]))
        dx, dg = _rms_bwd(jnp.concatenate(outs, axis=1), h_ref[...], gpre_ref[l:l + 1, :])
        dgpre_ref[...] += dg
        o_ref[...] = dh_ref[...] + dx

    nxt = pl.BlockSpec((HALO, D), lambda i: (jnp.minimum((i + 1) * (tm // HALO), T // HALO - 1), 0))
    return _pallas(
        body, grid=(nt,), name=f"pool_bwd_b_{l}",
        in_specs=[_tile(tm, D), nxt, _tile(tm, D), _tile(tm, D), _res(g_pre.shape)],
        out_specs=[_tile(tm, D), _acc(D)],
        out_shape=[jax.ShapeDtypeStruct((T, D), F32), jax.ShapeDtypeStruct((1, D), F32)],
        args=(dpool, dpool, dh1, h, g_pre), comm=comm)


def _proj_rows(x, w_ref):
    k = w_ref.shape[1]
    out = _dot(x[:, :k], w_ref[0])
    for s in range(1, NSH):
        out = out + _dot(x[:, s * k:(s + 1) * k], w_ref[s])
    return out


def _proj_rows_t(dy, w_ref):
    return jnp.concatenate([_dot_nt(dy, w_ref[s]) for s in range(NSH)], axis=1)


def _rope_tiles(cos_ref, sin_ref, width):
    reps = width // cos_ref.shape[1]
    return jnp.tile(cos_ref[...], (1, reps)), jnp.tile(sin_ref[...], (1, reps))


def _q_fwd(h, g_pre, wq, cos, sin, l, j):
    T = h.shape[0]
    tm = min(TM, T)

    def body(h_ref, g_ref, w_ref, cos_ref, sin_ref, q_ref):
        hn = _rms_fwd(h_ref[...], g_ref[l:l + 1, :]).astype(CDT)
        c, s = _rope_tiles(cos_ref, sin_ref, D)
        q_ref[...] = (_rope_fwd(_proj_rows(hn, w_ref), c, s) * QSCALE).astype(CDT)

    return pl.pallas_call(
        body, grid=(T // tm,), name=f"q_fwd_{l}",
        in_specs=[_tile(tm, D), _res(g_pre.shape), _layer_w((D // NSH, D)), _tile(tm, 2 * HD), _tile(tm, 2 * HD)],
        out_specs=_tile(tm, D), out_shape=jax.ShapeDtypeStruct((T, D), CDT), compiler_params=_cp(),
    )(h, g_pre, wq, cos, sin)


def _q_bwd(dq, dh1, h, g_pre, wq, cos, sin, l, j):
    T = h.shape[0]
    tm = min(TM, T)

    def body(dq_ref, dh_ref, h_ref, g_ref, w_ref, cos_ref, sin_ref, o_ref, hn_ref, dq0_ref, dg_ref):
        _zero_first(dg_ref)
        g = g_ref[l:l + 1, :]
        x = h_ref[...]
        hn_ref[...] = _rms_fwd(x, g).astype(CDT)
        c, s = _rope_tiles(cos_ref, sin_ref, D)
        dq0 = _rope_bwd(dq_ref[...].astype(F32) * QSCALE, c, s).astype(CDT)
        dq0_ref[...] = dq0
        dx, dg = _rms_bwd(_proj_rows_t(dq0, w_ref), x, g)
        dg_ref[...] += dg
        o_ref[...] = dh_ref[...] + dx

    return pl.pallas_call(
        body, grid=(T // tm,), name=f"q_bwd_{l}",
        in_specs=[_tile(tm, D), _tile(tm, D), _tile(tm, D), _res(g_pre.shape), _layer_w((D // NSH, D)),
                  _tile(tm, 2 * HD), _tile(tm, 2 * HD)],
        out_specs=[_tile(tm, D), _tile(tm, D), _tile(tm, D), _acc(D)],
        out_shape=[jax.ShapeDtypeStruct((T, D), F32), jax.ShapeDtypeStruct((T, D), CDT),
                   jax.ShapeDtypeStruct((T, D), CDT), jax.ShapeDtypeStruct((1, D), F32)],
        compiler_params=_cp(),
    )(dq, dh1, h, g_pre, wq, cos, sin)


def _o_fwd(o, h, wo, g_post, l, j):
    T = h.shape[0]
    tm = min(TM, T)

    def body(o_ref, h_ref, w_ref, g_ref, h1_ref, m_ref):
        m = _proj_rows(o_ref[...], w_ref)
        m_ref[...] = m.astype(CDT)
        h1_ref[...] = h_ref[...] + _rms_fwd(m, g_ref[l:l + 1, :])

    return pl.pallas_call(
        body, grid=(T // tm,), name=f"o_fwd_{l}",
        in_specs=[_tile(tm, D), _tile(tm, D), _layer_w((D // NSH, D)), _res(g_post.shape)],
        out_specs=[_tile(tm, D), _tile(tm, D)],
        out_shape=[jax.ShapeDtypeStruct((T, D), F32), jax.ShapeDtypeStruct((T, D), CDT)], compiler_params=_cp(),
    )(o, h, wo, g_post)


def _o_bwd(dh1, m, wo, g_post, l, j):
    T = m.shape[0]
    tm = min(TM, T)

    def body(dh_ref, m_ref, w_ref, g_ref, do_ref, dm_ref, dg_ref):
        _zero_first(dg_ref)
        dm, dg = _rms_bwd(dh_ref[...], m_ref[...].astype(F32), g_ref[l:l + 1, :])
        dg_ref[...] += dg
        dmc = dm.astype(CDT)
        dm_ref[...] = dmc
        do_ref[...] = _proj_rows_t(dmc, w_ref).astype(CDT)

    return pl.pallas_call(
        body, grid=(T // tm,), name=f"o_bwd_{l}",
        in_specs=[_tile(tm, D), _tile(tm, D), _layer_w((D // NSH, D)), _res(g_post.shape)],
        out_specs=[_tile(tm, D), _tile(tm, D), _acc(D)],
        out_shape=[jax.ShapeDtypeStruct((T, D), CDT), jax.ShapeDtypeStruct((T, D), CDT),
                   jax.ShapeDtypeStruct((1, D), F32)],
        compiler_params=_cp(),
    )(dh1, m, wo, g_post)


KVW = 2 * NKV * HD


def _kv_fwd(h, g_kv, wk2, wv2, cos, sin):
    T = h.shape[0]
    tm = min(TM, T)

    def body(h_ref, g_ref, wk_ref, wv_ref, cos_ref, sin_ref, k_ref, v_ref):
        hk = _rms_fwd(h_ref[...], g_ref[...]).astype(CDT)
        c, s = _rope_tiles(cos_ref, sin_ref, KVW)
        k_ref[...] = _rope_fwd(_dot(hk, wk_ref[...]), c, s).astype(CDT)
        v_ref[...] = _dot(hk, wv_ref[...]).astype(CDT)

    return pl.pallas_call(
        body, grid=(T // tm,), name="kv_fwd",
        in_specs=[_tile(tm, D), _res(g_kv.shape), _res(wk2.shape), _res(wv2.shape), _tile(tm, 2 * HD), _tile(tm, 2 * HD)],
        out_specs=[_tile(tm, KVW), _tile(tm, KVW)],
        out_shape=[jax.ShapeDtypeStruct((T, KVW), CDT)] * 2, compiler_params=_cp(),
    )(h, g_kv, wk2, wv2, cos, sin)


def _kv_bwd(dh, h, dks, dkhs, dvs, dvhs, g_kv, wk2, wv2, cos, sin):
    T = h.shape[0]
    tq = min(TQ, T)
    nt = T // tq
    n = len(dks)

    def body(*refs):
        dh_ref, h_ref = refs[:2]
        main = refs[2:2 + 2 * n]
        halo = refs[2 + 2 * n:2 + 4 * n]
        g_ref, wk_ref, wv_ref, cos_ref, sin_ref, o_ref, hk_ref, dk0_ref, dv_ref, dg_ref = refs[2 + 4 * n:]
        _zero_first(dg_ref)
        i = pl.program_id(0)

        def total(mains, halos):
            d = mains[0][...]
            for r in mains[1:]:
                d = d + r[...]
            hl = halos[0][...]
            for r in halos[1:]:
                hl = hl + r[...]
            row = lax.broadcasted_iota(jnp.int32, (tq, 1), 0)
            if tq > WIN:
                hl = jnp.concatenate([jnp.zeros((tq - WIN, KVW), F32), hl], axis=0)
            return d + jnp.where((row >= tq - WIN) & (i < nt - 1), hl, 0.0)

        dk = total(main[:n], halo[:n])
        dv = total(main[n:], halo[n:])
        x = h_ref[...]
        g = g_ref[...]
        hk_ref[...] = _rms_fwd(x, g).astype(CDT)
        c, s = _rope_tiles(cos_ref, sin_ref, KVW)
        dk0 = _rope_bwd(dk, c, s).astype(CDT)
        dvc = dv.astype(CDT)
        dk0_ref[...] = dk0
        dv_ref[...] = dvc
        dx, dg = _rms_bwd(_dot_nt(dk0, wk_ref[...]) + _dot_nt(dvc, wv_ref[...]), x, g)
        dg_ref[...] += dg
        o_ref[...] = dh_ref[...] + dx

    nxt = pl.BlockSpec((WIN, KVW), lambda i: (jnp.minimum(i + 1, nt - 1), 0))
    return pl.pallas_call(
        body, grid=(nt,), name="kv_bwd",
        in_specs=[_tile(tq, D), _tile(tq, D)] + [_tile(tq, KVW)] * (2 * n) + [nxt] * (2 * n)
        + [_res(g_kv.shape), _res(wk2.shape), _res(wv2.shape), _tile(tq, 2 * HD), _tile(tq, 2 * HD)],
        out_specs=[_tile(tq, D), _tile(tq, D), _tile(tq, KVW), _tile(tq, KVW), _acc(D)],
        out_shape=[jax.ShapeDtypeStruct((T, D), F32), jax.ShapeDtypeStruct((T, D), CDT),
                   jax.ShapeDtypeStruct((T, KVW), CDT), jax.ShapeDtypeStruct((T, KVW), CDT),
                   jax.ShapeDtypeStruct((1, D), F32)],
        compiler_params=_cp(),
    )(dh, h, *dks, *dvs, *dkhs, *dvhs, g_kv, wk2, wv2, cos, sin)


GQ = 4
LW = 2 * HD


def _from_prev():
    r = lax.broadcasted_iota(jnp.int32, (GQ * WIN, WIN), 0) % WIN
    j = lax.broadcasted_iota(jnp.int32, (GQ * WIN, WIN), 1)
    return j > r


def _fold(x2, prev):
    return jnp.where(prev, x2[:, :WIN], x2[:, WIN:])


def _unfold(x, prev):
    zero = jnp.zeros_like(x)
    return jnp.concatenate([jnp.where(prev, x, zero), jnp.where(prev, zero, x)], axis=1)


def _stack_heads(ref, rows, g):
    lane = lax.broadcasted_iota(jnp.int32, (WIN, LW), 1)
    parts = []
    for pr in range(2):
        x = ref[rows, (2 * g + pr) * LW:(2 * g + pr + 1) * LW]
        parts += [jnp.where(lane < HD, x, jnp.zeros_like(x)), jnp.where(lane >= HD, x, jnp.zeros_like(x))]
    return jnp.concatenate(parts, axis=0)


def _unstack_heads(x4):
    lane = lax.broadcasted_iota(jnp.int32, (WIN, LW), 1)
    return [jnp.where(lane < HD, x4[(2 * pr) * WIN:(2 * pr + 1) * WIN], x4[(2 * pr + 1) * WIN:(2 * pr + 2) * WIN])
            for pr in range(2)]


def _sink_col(sink_ref, j, g):
    blk = lax.broadcasted_iota(jnp.int32, (GQ * WIN, 1), 0) // WIN
    col = jnp.zeros((GQ * WIN, 1), F32)
    for a in range(GQ):
        col = jnp.where(blk == a, sink_ref[j, GQ * g + a], col)
    return col


def _softmax_sink(s, hidden, sink):
    if hidden is not None:
        s = jnp.where(hidden, NEG, s)
    m = jnp.maximum(jnp.max(s, axis=-1, keepdims=True), sink)
    e = jnp.exp(s - m)
    es = jnp.exp(sink - m)
    l = jnp.sum(e, axis=-1, keepdims=True) + es
    return e / l, es / l


def _kv_halo(tq):
    return pl.BlockSpec((WIN, KVW), lambda i: (jnp.maximum(i * (tq // WIN) - 1, 0), 0))


def _attn_fwd(q, kk, vv, sinks, l, j):
    T = q.shape[0]
    tq = min(TQ, T)

    def body(sink_ref, q_ref, k_ref, kh_ref, v_ref, vh_ref, o_ref):
        i = pl.program_id(0)
        ke = jnp.concatenate([kh_ref[...], k_ref[...]], axis=0)
        ve = jnp.concatenate([vh_ref[...], v_ref[...]], axis=0)
        prev = _from_prev()
        for n in range(tq // WIN):
            rows = slice(n * WIN, (n + 1) * WIN)
            hidden = prev & (i == 0) if n == 0 else None
            for g in range(NKV):
                kg = ke[n * WIN:(n + 2) * WIN, g * LW:(g + 1) * LW]
                vg = ve[n * WIN:(n + 2) * WIN, g * LW:(g + 1) * LW]
                s = _fold(_dot_nt(_stack_heads(q_ref, rows, g), kg), prev)
                p, _ = _softmax_sink(s, hidden, _sink_col(sink_ref, j, g))
                pairs = _unstack_heads(_dot(_unfold(p.astype(CDT), prev), vg))
                for pr in range(2):
                    o_ref[rows, (2 * g + pr) * LW:(2 * g + pr + 1) * LW] = pairs[pr].astype(CDT)

    return pl.pallas_call(
        body, grid=(T // tq,), name=f"attn_fwd_{l}",
        in_specs=[pl.BlockSpec(memory_space=pltpu.SMEM), _tile(tq, D), _tile(tq, KVW), _kv_halo(tq),
                  _tile(tq, KVW), _kv_halo(tq)],
        out_specs=_tile(tq, D), out_shape=jax.ShapeDtypeStruct((T, D), CDT), compiler_params=_cp(),
    )(sinks, q, kk, kk, vv, vv)


def _attn_bwd(q, kk, vv, o, do, sinks, l, j):
    T = q.shape[0]
    tq = min(TQ, T)
    nt = T // tq

    def body(sink_ref, q_ref, k_ref, kh_ref, v_ref, vh_ref, o_ref, do_ref,
             dq_ref, dk_ref, dkh_ref, dv_ref, dvh_ref, dsink_ref, dke, dve):
        _zero_first(dsink_ref)
        i = pl.program_id(0)
        ke = jnp.concatenate([kh_ref[...], k_ref[...]], axis=0)
        ve = jnp.concatenate([vh_ref[...], v_ref[...]], axis=0)
        dke[...] = jnp.zeros_like(dke)
        dve[...] = jnp.zeros_like(dve)
        lane = lax.broadcasted_iota(jnp.int32, (1, LW), 1)
        blk = lax.broadcasted_iota(jnp.int32, (GQ * WIN, 1), 0) // WIN
        dsink = jnp.zeros((1, LW), F32)
        prev = _from_prev()
        for n in range(tq // WIN):
            rows = slice(n * WIN, (n + 1) * WIN)
            krows = slice(n * WIN, (n + 2) * WIN)
            hidden = prev & (i == 0) if n == 0 else None
            for g in range(NKV):
                cols = slice(g * LW, (g + 1) * LW)
                kg = ke[krows, cols]
                vg = ve[krows, cols]
                q4 = _stack_heads(q_ref, rows, g)
                do4 = _stack_heads(do_ref, rows, g)
                o4 = _stack_heads(o_ref, rows, g)
                p, ps = _softmax_sink(_fold(_dot_nt(q4, kg), prev), hidden, _sink_col(sink_ref, j, g))
                delta = jnp.sum(do4.astype(F32) * o4.astype(F32), axis=-1, keepdims=True)
                ds = _unfold((p * (_fold(_dot_nt(do4, vg), prev) - delta)).astype(CDT), prev)
                pc = _unfold(p.astype(CDT), prev)
                pairs = _unstack_heads(_dot(ds, kg))
                for pr in range(2):
                    dq_ref[rows, (2 * g + pr) * LW:(2 * g + pr + 1) * LW] = pairs[pr].astype(CDT)
                dke[krows, cols] += _dot_tn(ds, q4)
                dve[krows, cols] += _dot_tn(pc, do4)
                t = ps * delta
                for a in range(GQ):
                    dsink = dsink - jnp.where(lane == GQ * g + a, jnp.sum(jnp.where(blk == a, t, 0.0)), 0.0)
        dsink_ref[...] += dsink
        dkh_ref[...] = dke[:WIN, :]
        dk_ref[...] = dke[WIN:, :]
        dvh_ref[...] = dve[:WIN, :]
        dv_ref[...] = dve[WIN:, :]

    halo_out = pl.BlockSpec((WIN, KVW), lambda i: (i, 0))
    return pl.pallas_call(
        body, grid=(nt,), name=f"attn_bwd_{l}",
        in_specs=[pl.BlockSpec(memory_space=pltpu.SMEM), _tile(tq, D), _tile(tq, KVW), _kv_halo(tq),
                  _tile(tq, KVW), _kv_halo(tq), _tile(tq, D), _tile(tq, D)],
        out_specs=[_tile(tq, D), _tile(tq, KVW), halo_out, _tile(tq, KVW), halo_out, _acc(LW)],
        out_shape=[jax.ShapeDtypeStruct((T, D), CDT), jax.ShapeDtypeStruct((T, KVW), F32),
                   jax.ShapeDtypeStruct((nt * WIN, KVW), F32), jax.ShapeDtypeStruct((T, KVW), F32),
                   jax.ShapeDtypeStruct((nt * WIN, KVW), F32), jax.ShapeDtypeStruct((1, LW), F32)],
        scratch_shapes=[pltpu.VMEM((WIN + tq, KVW), F32), pltpu.VMEM((WIN + tq, KVW), F32)],
        compiler_params=_cp(),
    )(sinks, q, kk, kk, vv, vv, o, do)


def _chunks(tm):
    return pl.BlockSpec((NSH, tm, FS), lambda i: (0, i, 0))


def _ffn_fwd(h1, g_pre, wg, wu, wd, g_post, l, comm=None):
    T = h1.shape[0]
    tm = min(TM_FFN, T)

    def body(h_ref, gpre_ref, wg_ref, wu_ref, wd_ref, gpost_ref, h2_ref, a_ref, b_ref, f_ref):
        x = h_ref[...]
        fn = _rms_fwd(x, gpre_ref[l:l + 1, :]).astype(CDT)
        f = jnp.zeros((tm, D), F32)
        for s in range(NSH):
            a = _dot_nt(fn, wg_ref[s])
            b = _dot_nt(fn, wu_ref[s])
            a_ref[s] = a.astype(CDT)
            b_ref[s] = b.astype(CDT)
            f = f + _dot((a * _sigmoid(a) * b).astype(CDT), wd_ref[s])
        f_ref[...] = f.astype(CDT)
        h2_ref[...] = x + _rms_fwd(f, gpost_ref[l:l + 1, :])

    return _pallas(
        body, grid=(T // tm,), name=f"ffn_fwd_{l}",
        in_specs=[_tile(tm, D), _res(g_pre.shape), _layer_w((FS, D)), _layer_w((FS, D)), _layer_w((FS, D)),
                  _res(g_post.shape)],
        out_specs=[_tile(tm, D), _chunks(tm), _chunks(tm), _tile(tm, D)],
        out_shape=[jax.ShapeDtypeStruct((T, D), F32), jax.ShapeDtypeStruct((NSH, T, FS), CDT),
                   jax.ShapeDtypeStruct((NSH, T, FS), CDT), jax.ShapeDtypeStruct((T, D), CDT)],
        args=(h1, g_pre, wg, wu, wd, g_post), comm=comm)


def _ffn_bwd(dh2, h1, f, a, b, g_pre, wg, wu, wd, g_post, l, comm=None):
    T = h1.shape[0]
    tm = min(TM_FFN_BWD, T)

    def body(dh_ref, h_ref, f_ref, a_ref, b_ref, gpre_ref, wg_ref, wu_ref, wd_ref, gpost_ref,
             dh1_ref, fn_ref, df_ref, act_ref, da_ref, db_ref, dgpre_ref, dgpost_ref):
        _zero_first(dgpre_ref, dgpost_ref)
        x = h_ref[...]
        gpre = gpre_ref[l:l + 1, :]
        fn_ref[...] = _rms_fwd(x, gpre).astype(CDT)
        dh2 = dh_ref[...]
        df, dg = _rms_bwd(dh2, f_ref[...].astype(F32), gpost_ref[l:l + 1, :])
        dgpost_ref[...] += dg
        dfc = df.astype(CDT)
        df_ref[...] = dfc
        dfn = jnp.zeros((tm, D), F32)
        for s in range(NSH):
            av = a_ref[s].astype(F32)
            bv = b_ref[s].astype(F32)
            sg = _sigmoid(av)
            silu = av * sg
            act_ref[s] = (silu * bv).astype(CDT)
            dact = _dot_nt(dfc, wd_ref[s])
            da = (dact * bv * (sg * (1.0 + av * (1.0 - sg)))).astype(CDT)
            db = (dact * silu).astype(CDT)
            da_ref[s] = da
            db_ref[s] = db
            dfn = dfn + _dot(da, wg_ref[s]) + _dot(db, wu_ref[s])
        dx, dg = _rms_bwd(dfn, x, gpre)
        dgpre_ref[...] += dg
        dh1_ref[...] = dh2 + dx

    return _pallas(
        body, grid=(T // tm,), name=f"ffn_bwd_{l}",
        in_specs=[_tile(tm, D), _tile(tm, D), _tile(tm, D), _chunks(tm), _chunks(tm), _res(g_pre.shape),
                  _layer_w((FS, D)), _layer_w((FS, D)), _layer_w((FS, D)), _res(g_post.shape)],
        out_specs=[_tile(tm, D), _tile(tm, D), _tile(tm, D), _chunks(tm), _chunks(tm), _chunks(tm), _acc(D), _acc(D)],
        out_shape=[jax.ShapeDtypeStruct((T, D), F32), jax.ShapeDtypeStruct((T, D), CDT), jax.ShapeDtypeStruct((T, D), CDT),
                   jax.ShapeDtypeStruct((NSH, T, FS), CDT), jax.ShapeDtypeStruct((NSH, T, FS), CDT),
                   jax.ShapeDtypeStruct((NSH, T, FS), CDT), jax.ShapeDtypeStruct((1, D), F32),
                   jax.ShapeDtypeStruct((1, D), F32)],
        args=(dh2, h1, f, a, b, g_pre, wg, wu, wd, g_post), comm=comm)


def _ple_proj(p, wpp_ref):
    return jnp.concatenate([_dot(p, wpp_ref[s]) for s in range(NSH)], axis=1)


def _ple_fwd(h2, p, g_ple, wpg, wpp, l, comm=None):
    T = h2.shape[0]
    tm = min(TM, T)

    def body(h_ref, p_ref, g_ref, wpg_ref, wpp_ref, o_ref):
        x = h_ref[...]
        pn = _rms_fwd(x, g_ref[l:l + 1, :]).astype(CDT)
        gate = _sigmoid(_proj_rows(pn, wpg_ref))
        o_ref[...] = x + _ple_proj(p_ref[...].astype(CDT), wpp_ref) * gate

    return _pallas(
        body, grid=(T // tm,), name=f"ple_fwd_{l}",
        in_specs=[_tile(tm, D), pl.BlockSpec((None, tm, PLE), lambda i: (l, i, 0)), _res(g_ple.shape),
                  _layer_w((D // NSH, D)), _layer_w((PLE, D // NSH))],
        out_specs=[_tile(tm, D)], out_shape=[jax.ShapeDtypeStruct((T, D), F32)],
        args=(h2, p, g_ple, wpg, wpp), comm=comm)


def _ple_bwd(dh3, h2, p, g_ple, wpg, wpp, l):
    T = h2.shape[0]
    tm = min(TM, T)

    def body(dh_ref, h_ref, p_ref, g_ref, wpg_ref, wpp_ref, o_ref, pn_ref, dz_ref, dpp_ref, pc_ref, dg_ref):
        _zero_first(dg_ref)
        x = h_ref[...]
        g = g_ref[l:l + 1, :]
        pn = _rms_fwd(x, g).astype(CDT)
        pn_ref[...] = pn
        gate = _sigmoid(_proj_rows(pn, wpg_ref))
        pc = p_ref[...].astype(CDT)
        pc_ref[...] = pc
        pp = _ple_proj(pc, wpp_ref)
        dh3 = dh_ref[...]
        dpp_ref[...] = (dh3 * gate).astype(CDT)
        dz = (dh3 * pp * gate * (1.0 - gate)).astype(CDT)
        dz_ref[...] = dz
        dx, dg = _rms_bwd(_proj_rows_t(dz, wpg_ref), x, g)
        dg_ref[...] += dg
        o_ref[...] = dh3 + dx

    return pl.pallas_call(
        body, grid=(T // tm,), name=f"ple_bwd_{l}",
        in_specs=[_tile(tm, D), _tile(tm, D), pl.BlockSpec((None, tm, PLE), lambda i: (l, i, 0)), _res(g_ple.shape),
                  _layer_w((D // NSH, D)), _layer_w((PLE, D // NSH))],
        out_specs=[_tile(tm, D), _tile(tm, D), _tile(tm, D), _tile(tm, D), _tile(tm, PLE), _acc(D)],
        out_shape=[jax.ShapeDtypeStruct((T, D), F32), jax.ShapeDtypeStruct((T, D), CDT), jax.ShapeDtypeStruct((T, D), CDT),
                   jax.ShapeDtypeStruct((T, D), CDT), jax.ShapeDtypeStruct((T, PLE), CDT), jax.ShapeDtypeStruct((1, D), F32)],
        compiler_params=_cp(),
    )(dh3, h2, p, g_ple, wpg, wpp)


def _loss_grad(y, tgt):
    T = y.shape[0]
    tm = min(TM, T)

    def body(y_ref, t_ref, dy_ref, loss_ref):
        _zero_first(loss_ref)
        err = y_ref[...] - t_ref[...]
        dy_ref[...] = err * (1.0 / D)
        lane = lax.broadcasted_iota(jnp.int32, (1, 128), 1)
        loss_ref[...] += jnp.where(lane == 0, (0.5 / D) * jnp.sum(err * err), 0.0)

    return pl.pallas_call(
        body, grid=(T // tm,), name="loss_grad", in_specs=[_tile(tm, D), _tile(tm, D)],
        out_specs=[_tile(tm, D), _acc(128)],
        out_shape=[jax.ShapeDtypeStruct((T, D), F32), jax.ShapeDtypeStruct((1, 128), F32)], compiler_params=_cp(),
    )(y, tgt)


def _mm_tn(x, dy, name, n_split=1, comm=None):
    xb, yb = x.ndim == 3, dy.ndim == 3
    T, K = x.shape[-2:]
    N = dy.shape[-1] // n_split
    B = x.shape[0] if xb else dy.shape[0] if yb else n_split
    tt = min(TT, T)
    nt = T // tt

    def body(x_ref, dy_ref, o_ref, oc_ref):
        t = pl.program_id(1)

        @pl.when(t == 0)
        def _():
            o_ref[...] = jnp.zeros_like(o_ref)

        o_ref[...] += _dot_tn(x_ref[...].astype(CDT), dy_ref[...])

        @pl.when(t == nt - 1)
        def _():
            oc_ref[...] = o_ref[...].astype(CDT)

    x_spec = pl.BlockSpec((None, tt, K), lambda b, t: (b, t, 0)) if xb else pl.BlockSpec((tt, K), lambda b, t: (t, 0))
    if yb:
        y_spec = pl.BlockSpec((None, tt, N), lambda b, t: (b, t, 0))
    else:
        y_spec = pl.BlockSpec((tt, N), lambda b, t: (t, b if n_split > 1 else 0))
    o_spec = pl.BlockSpec((None, K, N), lambda b, t: (b, 0, 0))
    res, got = _pallas(
        body, grid=(B, nt), name=name, in_specs=[x_spec, y_spec], out_specs=[o_spec, o_spec],
        out_shape=[jax.ShapeDtypeStruct((B, K, N), F32), jax.ShapeDtypeStruct((B, K, N), CDT)], args=(x, dy), comm=comm)
    return res if comm is None else (res, got)


def _fold_dup(dw2, name):
    def body(x_ref, o_ref, oc_ref):
        x = x_ref[...]
        y = jnp.concatenate([x[:, g * LW:g * LW + HD] + x[:, g * LW + HD:(g + 1) * LW] for g in range(NKV)], axis=1)
        o_ref[...] = y
        oc_ref[...] = y.astype(CDT)

    return pl.pallas_call(
        body, grid=(NSH,), name=name, in_specs=[_tile(D // NSH, KVW)], out_specs=[_tile(D // NSH, NKV * HD)] * 2,
        out_shape=[jax.ShapeDtypeStruct((D, NKV * HD), F32), jax.ShapeDtypeStruct((D, NKV * HD), CDT)],
        compiler_params=_cp(),
    )(dw2)


def _all_reduce_small(vec):
    R = vec.shape[0]
    ndev = 8

    def body(v_ref, o_ref, buf, send, recv):
        x, y, c, _ = _place()
        me = 4 * x + 2 * y + c
        buf[me] = v_ref[...]
        copies = []
        for k in range(1, ndev):
            peer = (x if not k & 4 else 1 - x, y if not k & 2 else 1 - y, c if not k & 1 else 1 - c)
            copies.append(pltpu.make_async_remote_copy(
                src_ref=v_ref, dst_ref=buf.at[me], send_sem=send.at[k - 1], recv_sem=recv.at[k - 1],
                device_id=peer, device_id_type=MESH))
            copies[-1].start()
        for k in range(1, ndev):
            peer = (x if not k & 4 else 1 - x, y if not k & 2 else 1 - y, c if not k & 1 else 1 - c)
            pltpu.make_async_remote_copy(
                src_ref=v_ref, dst_ref=buf.at[4 * peer[0] + 2 * peer[1] + peer[2]], send_sem=send.at[k - 1],
                recv_sem=recv.at[k - 1], device_id=peer, device_id_type=MESH).wait_recv()
        for cp in copies:
            cp.wait_send()
        tot = buf[0]
        for d in range(1, ndev):
            tot = tot + buf[d]
        o_ref[...] = tot

    return pl.pallas_call(
        body, name="all_reduce_small", in_specs=[pl.BlockSpec(memory_space=pltpu.VMEM)],
        out_specs=pl.BlockSpec(memory_space=pltpu.VMEM), out_shape=jax.ShapeDtypeStruct(vec.shape, F32),
        scratch_shapes=[pltpu.VMEM((ndev, R, D), F32), pltpu.SemaphoreType.DMA((ndev - 1,)),
                        pltpu.SemaphoreType.DMA((ndev - 1,))],
    )(vec)


def _row_tile(rows):
    for t in range(min(rows, 512) // 8 * 8, 7, -8):
        if rows % t == 0:
            return t
    return rows


def _sum_own_and_received(own, got, me, name):
    _, R, C = own.shape
    tr = _row_tile(R)

    def body(me_ref, own_ref, got_ref, o_ref):
        o_ref[...] = ((own_ref[...] + got_ref[0].astype(F32)) + got_ref[1].astype(F32)) + got_ref[2].astype(F32)

    return pl.pallas_call(
        body, name=name,
        grid_spec=pltpu.PrefetchScalarGridSpec(
            num_scalar_prefetch=1, grid=(R // tr,),
            in_specs=[pl.BlockSpec((None, tr, C), lambda i, me_ref: (me_ref[0], i, 0)),
                      pl.BlockSpec((3, tr, C), lambda i, me_ref: (0, i, 0))],
            out_specs=pl.BlockSpec((tr, C), lambda i, me_ref: (i, 0))),
        out_shape=jax.ShapeDtypeStruct((R, C), F32), compiler_params=_cp(),
    )(me, own, got)


def _adamw(gs, w, m, v, name):
    L, R, C = w.shape
    tr = _row_tile(R)
    ng = len(gs[0])
    c1 = 1.0 - ADAM_B1 ** ADAM_STEP
    c2 = 1.0 - ADAM_B2 ** ADAM_STEP

    def body(*refs):
        w_ref, m_ref, v_ref, g_ref, d_ref, nm_ref, nv_ref = refs[L * ng:]
        lay = pl.program_id(0)
        g = None
        for l in range(L):
            gl = refs[l * ng][...]
            for r in refs[l * ng + 1:(l + 1) * ng]:
                gl = gl + r[...]
            g = gl if g is None else jnp.where(lay == l, gl, g)
        nm = ADAM_B1 * m_ref[...] + (1.0 - ADAM_B1) * g
        nv = ADAM_B2 * v_ref[...] + (1.0 - ADAM_B2) * (g * g)
        g_ref[...] = g
        nm_ref[...] = nm
        nv_ref[...] = nv
        d_ref[...] = -ADAM_LR * ((nm / c1) / (jnp.sqrt(nv / c2) + ADAM_EPS) + ADAM_WD * w_ref[...])

    gspecs = [pl.BlockSpec((tr, C), lambda lay, i, l=l: (jnp.where(lay == l, i, 0), 0)) for l in range(L) for _ in range(ng)]
    spec = pl.BlockSpec((None, tr, C), lambda lay, i: (lay, i, 0))
    return pl.pallas_call(
        body, grid=(L, R // tr), name=name, in_specs=gspecs + [spec] * 3, out_specs=[spec] * 4,
        out_shape=[jax.ShapeDtypeStruct((L, R, C), F32)] * 4, compiler_params=_cp(2),
    )(*[g for gl in gs for g in gl], w, m, v)


def _rope_tables(T):
    inv = 1.0 / (ROPE_THETA ** (jnp.arange(0, HD, 2, dtype=F32) / HD))
    ang = jnp.arange(T, dtype=F32)[:, None] * inv[None, :]
    c, s = jnp.cos(ang), jnp.sin(ang)
    return jnp.tile(jnp.concatenate([c, c], axis=1), (1, 2)), jnp.tile(jnp.concatenate([-s, s], axis=1), (1, 2))


def _dup_cols(w):
    return jnp.broadcast_to(w.reshape(D, NKV, 1, HD), (D, NKV, 2, HD)).reshape(D, KVW)


def kernel(x, p, mix_pre_g, mix_post_g, ffn_pre_g, ffn_post_g, pool_w, pool_scale, kv_norm_g, w_k, w_v, w_q, w_o, sinks, w_ff_gate, w_ff_up, w_ff_down, ple_norm_g, w_ple_gate, w_ple_proj, loss_target, m_mix_pre_g, m_mix_post_g, m_ffn_pre_g, m_ffn_post_g, m_pool_w, m_pool_scale, m_kv_norm_g, m_w_k, m_w_v, m_w_q, m_w_o, m_sinks, m_w_ff_gate, m_w_ff_up, m_w_ff_down, m_ple_norm_g, m_w_ple_gate, m_w_ple_proj, v_mix_pre_g, v_mix_post_g, v_ffn_pre_g, v_ffn_post_g, v_pool_w, v_pool_scale, v_kv_norm_g, v_w_k, v_w_v, v_w_q, v_w_o, v_sinks, v_w_ff_gate, v_w_ff_up, v_w_ff_down, v_ple_norm_g, v_w_ple_gate, v_w_ple_proj):
    depth = mix_pre_g.shape[0]
    n_pool = pool_w.shape[0]
    n_attn = w_q.shape[0]
    T = x.shape[1]
    h = x[0]
    p = p[:, 0]
    tgt = loss_target[0]
    me = (2 * lax.axis_index("x") + lax.axis_index("y")).astype(jnp.int32).reshape(1)

    def t12(a):
        return jnp.transpose(a, (0, 2, 1))

    ff_t = {k: tuple(t12(a) for a in v) for k, v in dict(
        w_ff_gate=(w_ff_gate, m_w_ff_gate, v_w_ff_gate), w_ff_up=(w_ff_up, m_w_ff_up, v_w_ff_up)).items()}
    c_pool, c_wk, c_wv, c_wq, c_wo, c_wg, c_wu, c_wd, c_wpg, c_wpp = (w.astype(CDT) for w in (
        pool_w, w_k, w_v, w_q, w_o, ff_t["w_ff_gate"][0], ff_t["w_ff_up"][0], w_ff_down, w_ple_gate, w_ple_proj))

    def ffn_srcs(l):
        return [(c_wg, l), (c_wu, l), (c_wd, l)]

    def ple_srcs(l):
        return [(c_wpg, l), (c_wpp, l)]

    (gpw,) = _run_comm(_gather_comm([(c_pool, None)]), "gather_pool_w")
    wff, wple, wattn = {}, {}, {}
    pw = jnp.transpose(gpw, (1, 2, 0, 3, 4)).reshape(n_pool, len(POOL_WINDOWS), PG, PG)
    kv_g = kv_norm_g.reshape(1, D)
    cos, sin = _rope_tables(T)

    pscale = _all_reduce_small(jnp.pad(
        lax.dynamic_update_slice(jnp.zeros((n_pool, D), F32), pool_scale, (0, me[0] * (D // NSH))),
        ((0, 8 - n_pool), (0, 0))))[:n_pool] * 0.5

    saved = []
    kk = vv = wk2 = wv2 = None
    for l in range(depth):
        s = {"h": h}
        if l < n_pool:
            (h1,), got = _pool_fwd(h, mix_pre_g, pw, pscale, mix_post_g, l,
                                   comm=_GatherHalves(ffn_srcs(0)) if l == 0 else None)
            if l == 0:
                wff[0] = got
        else:
            j = l - n_pool
            wq, wo = wattn[l]
            s["q"] = _q_fwd(h, mix_pre_g, wq, cos, sin, l, j)
            s["o"] = _attn_fwd(s["q"], kk, vv, sinks, l, j)
            h1, s["m"] = _o_fwd(s["o"], h, wo, mix_post_g, l, j)
        s["h1"] = h1
        srcs = []
        if l + 1 < depth:
            srcs = ffn_srcs(l + 1)
            if l + 1 >= n_pool:
                srcs += [(c_wq, l + 1 - n_pool), (c_wo, l + 1 - n_pool)]
            if l + 1 == n_pool:
                srcs += [(c_wk, None), (c_wv, None)]
        if l == 0:
            srcs += ple_srcs(0)
        (h2, s["a"], s["b"], s["f"]), got = _ffn_fwd(
            h1, ffn_pre_g, *wff[l], ffn_post_g, l, comm=_gather_comm(srcs) if srcs else None)
        if l == 0:
            wple[0] = got[-2:]
        if l + 1 < depth:
            wff[l + 1] = got[:3]
            if l + 1 >= n_pool:
                wattn[l + 1] = got[3:5]
            if l + 1 == n_pool:
                wk2 = _dup_cols(got[5].reshape(D, NKV * HD))
                wv2 = _dup_cols(got[6].reshape(D, NKV * HD))
        s["h2"] = h2
        (h,), got = _ple_fwd(h2, p, ple_norm_g, *wple[l], l,
                             comm=_gather_comm(ple_srcs(l + 1)) if l + 1 < depth else None)
        if l + 1 < depth:
            wple[l + 1] = got
        if l == n_pool - 1:
            s["hkv"] = h
            kk, vv = _kv_fwd(h, kv_g, wk2, wv2, cos, sin)
        saved.append(s)

    dh, loss_row = _loss_grad(h, tgt)

    entries, pending, to_swap = [], [], []

    def add_grad(k, l, g32, g16):
        def by_shard(g):
            return g.reshape(NSH, -1, g.shape[-1])
        e = dict(k=k, l=l, own=by_shard(g32), sent=by_shard(g16))
        entries.append(e)
        pending.append(e)

    def carry():
        take, swap = list(pending), list(to_swap)
        del pending[:], to_swap[:]
        return (take, swap), _join_comms([_scatter_comm([e["sent"] for e in take]) if take else None,
                                          _swap_comm([e["part"] for e in swap]) if swap else None])

    def landed(taken, got):
        take, swap = taken
        for e, g in zip(take, got[:len(take)]):
            e["part"] = _sum_own_and_received(e["own"], g, me, f"sum_{e['k']}_{e['l']}")
            to_swap.append(e)
        for e, g in zip(swap, got[len(take):]):
            e["sib"] = g

    small = {k: [None] * depth for k in ("mix_pre", "mix_post", "ffn_pre", "ffn_post", "ple")}
    dpool_scale = [None] * n_pool
    dsinks = [None] * n_attn
    dks, dkhs, dvs, dvhs = [], [], [], []
    for l in reversed(range(depth)):
        s = saved[l]
        if l == n_pool - 1:
            dh, hk, dk0, dvc, dkv_g = _kv_bwd(dh, s["hkv"], dks, dkhs, dvs, dvhs, kv_g, wk2, wv2, cos, sin)
            add_grad("w_k", None, *_fold_dup(_mm_tn(hk, dk0, "dw_k")[0][0], "fold_w_k"))
            add_grad("w_v", None, *_fold_dup(_mm_tn(hk, dvc, "dw_v")[0][0], "fold_w_v"))
        dh, pn, dz, dpp, pc, small["ple"][l] = _ple_bwd(dh, s["h2"], p, ple_norm_g, *wple[l], l)
        add_grad("w_ple_gate", l, *_mm_tn(pn, dz, f"dw_ple_gate_{l}"))
        add_grad("w_ple_proj", l, *_mm_tn(pc, dpp, f"dw_ple_proj_{l}", n_split=NSH))
        take, comm = carry()
        (dh, fn, df, act, da, db, small["ffn_pre"][l], small["ffn_post"][l]), got = _ffn_bwd(
            dh, s["h1"], s["f"], s["a"], s["b"], ffn_pre_g, *wff[l], ffn_post_g, l, comm=comm)
        landed(take, got)
        add_grad("w_ff_gate", l, *_mm_tn(da, fn, f"dw_ff_gate_{l}"))
        take, comm = carry() if l == 0 else (([], []), None)
        res = _mm_tn(db, fn, f"dw_ff_up_{l}", comm=comm)
        if comm is not None:
            res, got = res
            landed(take, got)
        add_grad("w_ff_up", l, *res)
        take, comm = carry() if l == 0 else (([], []), None)
        res = _mm_tn(act, df, f"dw_ff_down_{l}", comm=comm)
        if comm is not None:
            res, got = res
            landed(take, got)
        add_grad("w_ff_down", l, *res)
        if l < n_pool:
            take, comm = carry() if l == 0 else (([], []), None)
            (dpool, dpw, dpool_scale[l], small["mix_post"][l]), got = _pool_bwd_a(
                dh, s["h"], mix_pre_g, pw, pscale, mix_post_g, l, comm=comm)
            landed(take, got)
            dpw = jnp.transpose(dpw.reshape(len(POOL_WINDOWS), NSH, PG // NSH, PG), (1, 0, 2, 3))
            dpw = dpw.reshape(NSH, len(POOL_WINDOWS) * PG // NSH, PG)
            add_grad("pool_w", l, dpw, dpw.astype(CDT))
            (dh, small["mix_pre"][l]), _ = _pool_bwd_b(dpool, dh, s["h"], mix_pre_g, l)
        else:
            j = l - n_pool
            wq, wo = wattn[l]
            do, dm, small["mix_post"][l] = _o_bwd(dh, s["m"], wo, mix_post_g, l, j)
            add_grad("w_o", j, *_mm_tn(s["o"], dm, f"dw_o_{l}"))
            dq, dk, dkh, dv, dvh, dsinks[j] = _attn_bwd(s["q"], kk, vv, s["o"], do, sinks, l, j)
            dks.append(dk); dkhs.append(dkh); dvs.append(dv); dvhs.append(dvh)
            dh, hn, dq0, small["mix_pre"][l] = _q_bwd(dq, dh, s["h"], mix_pre_g, wq, cos, sin, l, j)
            add_grad("w_q", j, *_mm_tn(hn, dq0, f"dw_q_{l}"))
    rest = 0
    while pending or to_swap:
        take, comm = carry()
        landed(take, _run_comm(comm, f"exchange_rest_{rest}"))
        rest += 1
    grad_x = dh[None]

    rows = [jnp.concatenate(small[k], axis=0) for k in ("mix_pre", "mix_post", "ffn_pre", "ffn_post", "ple")]
    rows += [dkv_g, jnp.concatenate(dpool_scale, axis=0)]
    rows += [jnp.pad(jnp.concatenate(dsinks, axis=0), ((0, 0), (0, D - LW))), jnp.pad(loss_row, ((0, 0), (0, D - 128)))]
    vec = jnp.concatenate(rows, axis=0)
    nrow = vec.shape[0]
    vec = _all_reduce_small(jnp.pad(vec, ((0, -nrow % 8), (0, 0))))
    o = 0
    red = {}
    for k in ("mix_pre_g", "mix_post_g", "ffn_pre_g", "ffn_post_g", "ple_norm_g"):
        red[k] = vec[o:o + depth]
        o += depth
    red["kv_norm_g"] = vec[o:o + 1]
    red["pool_scale"] = lax.dynamic_slice(vec[o + 1:o + 1 + n_pool], (0, me[0] * (D // NSH)), (n_pool, D // NSH))
    o += 1 + n_pool
    red["sinks"] = vec[o:o + n_attn, :GQ * NKV]
    loss = vec[o + n_attn, 0]


    given = dict(
        mix_pre_g=(mix_pre_g, m_mix_pre_g, v_mix_pre_g), mix_post_g=(mix_post_g, m_mix_post_g, v_mix_post_g),
        ffn_pre_g=(ffn_pre_g, m_ffn_pre_g, v_ffn_pre_g), ffn_post_g=(ffn_post_g, m_ffn_post_g, v_ffn_post_g),
        pool_w=(pool_w, m_pool_w, v_pool_w), pool_scale=(pool_scale, m_pool_scale, v_pool_scale),
        kv_norm_g=(kv_norm_g, m_kv_norm_g, v_kv_norm_g), w_k=(w_k, m_w_k, v_w_k), w_v=(w_v, m_w_v, v_w_v),
        w_q=(w_q, m_w_q, v_w_q), w_o=(w_o, m_w_o, v_w_o), sinks=(sinks, m_sinks, v_sinks),
        w_ff_gate=ff_t["w_ff_gate"], w_ff_up=ff_t["w_ff_up"],
        w_ff_down=(w_ff_down, m_w_ff_down, v_w_ff_down), ple_norm_g=(ple_norm_g, m_ple_norm_g, v_ple_norm_g),
        w_ple_gate=(w_ple_gate, m_w_ple_gate, v_w_ple_gate), w_ple_proj=(w_ple_proj, m_w_ple_proj, v_w_ple_proj))
    results = {}
    for k, g in red.items():
        w, m, v = given[k]
        outs = _adamw([[g.reshape(-1, g.shape[-1])]], *(t.reshape(1, -1, t.shape[-1]) for t in (w, m, v)), f"adamw_{k}")
        results[k] = [t.reshape(w.shape) for t in outs]
    per_layer = {}
    for e in entries:
        per_layer.setdefault(e["k"], {})[e["l"] or 0] = [e["part"], e["sib"]]
    for k, layers in per_layer.items():
        w, m, v = given[k]
        gs = [layers[l] for l in range(len(layers))]
        outs = _adamw(gs, *(t.reshape((len(gs),) + gs[0][0].shape) for t in (w, m, v)), f"adamw_{k}")
        results[k] = [t12(t) if k in ff_t else t.reshape(w.shape) for t in outs]

    order = ["mix_pre_g", "mix_post_g", "ffn_pre_g", "ffn_post_g", "pool_w", "pool_scale", "kv_norm_g", "w_k", "w_v",
             "w_q", "w_o", "sinks", "w_ff_gate", "w_ff_up", "w_ff_down", "ple_norm_g", "w_ple_gate", "w_ple_proj"]
    return (loss, grad_x, *[results[k][0] for k in order], *[results[k][1] for k in order],
            *[results[k][2] for k in order], *[results[k][3] for k in order])
```

```python
import functools

import jax
import jax.numpy as jnp
from jax import lax
from jax.experimental import pallas as pl
from jax.experimental.pallas import tpu as pltpu

F32 = jnp.float32
CDT = jnp.bfloat16

D = 1024
PG = 256
POOL_WINDOWS = (2, 4, 8, 16)
HALO = 16
HD = 64
NKV = 4
WIN = 128
FS = 704
NSH = 4
PLE = 256
ROPE_THETA = 10000.0
EPS = 1e-6
NEG = -1e30
QSCALE = HD ** -0.5

ADAM_LR, ADAM_B1, ADAM_B2, ADAM_EPS, ADAM_WD, ADAM_STEP = 0.001, 0.9, 0.999, 1e-08, 0.01, 10

VMEM_LIMIT = 56 * 1024 * 1024
TM = 1024
TM_FFN = 512
TM_FFN_BWD = 256
TQ = 512
TT = 4096
MESH = pl.DeviceIdType.MESH


def _cp(n_axes=1):
    return pltpu.CompilerParams(dimension_semantics=("arbitrary",) * n_axes, vmem_limit_bytes=VMEM_LIMIT)


def _dot(a, b):
    return jnp.dot(a, b, preferred_element_type=F32)


def _dot_nt(a, b):
    return lax.dot_general(a, b, (((1,), (1,)), ((), ())), preferred_element_type=F32)


def _dot_tn(a, b):
    return lax.dot_general(a, b, (((0,), (0,)), ((), ())), preferred_element_type=F32)


def _rms_fwd(x, g):
    r = lax.rsqrt(jnp.mean(x * x, axis=-1, keepdims=True) + EPS)
    return x * r * g


def _rms_bwd(dy, x, g):
    r = lax.rsqrt(jnp.mean(x * x, axis=-1, keepdims=True) + EPS)
    xh = x * r
    dg = jnp.sum(dy * xh, axis=0, keepdims=True)
    u = dy * g
    dx = r * (u - xh * jnp.mean(u * xh, axis=-1, keepdims=True))
    return dx, dg


def _sigmoid(a):
    return 1.0 / (1.0 + jnp.exp(-a))


def _swap_half(x):
    w = x.shape[1]
    lane = lax.broadcasted_iota(jnp.int32, x.shape, 1)
    return jnp.where((lane % HD) < HD // 2, pltpu.roll(x, w - HD // 2, axis=1), pltpu.roll(x, HD // 2, axis=1))


def _rope_fwd(x, cos, sin):
    return x * cos + _swap_half(x) * sin


def _rope_bwd(d, cos, sin):
    return d * cos + _swap_half(d * sin)


def _tile(tm, w):
    return pl.BlockSpec((tm, w), lambda i: (i, 0))


def _res(shape):
    return pl.BlockSpec(shape, lambda i: (0,) * len(shape), pipeline_mode=pl.Buffered(1))


def _layer_w(shape):
    return pl.BlockSpec((NSH,) + shape, lambda *_: (0, 0, 0), pipeline_mode=pl.Buffered(1))


def _acc(w):
    return pl.BlockSpec((1, w), lambda i: (0, 0))


def _zero_first(*refs):
    @pl.when(pl.program_id(0) == 0)
    def _():
        for r in refs:
            r[...] = jnp.zeros_like(r)


HBM = pl.BlockSpec(memory_space=pl.ANY)


def _place():
    x, y, c = lax.axis_index("x"), lax.axis_index("y"), lax.axis_index("c")
    chips = [(1 - x, y), (x, 1 - y), (1 - x, 1 - y)]
    return x, y, c, chips


class _Comm:
    def __init__(self, args, out_shape, sems, copies):
        self.args, self.out_shape, self.sems, self.copies = list(args), list(out_shape), list(sems), copies

    def start(self, cin, cout, sems):
        local, sends, _ = self.copies(cin, cout, sems)
        for cp in local + sends:
            cp.start()

    def wait(self, cin, cout, sems):
        local, sends, recvs = self.copies(cin, cout, sems)
        for cp in recvs:
            cp.wait_recv()
        for cp in sends:
            cp.wait_send()
        for cp in local:
            cp.wait()


def _gather_comm(srcs):
    n = len(srcs)
    shapes = [(a.shape if l is None else a.shape[1:]) for a, l in srcs]

    def copies(cin, cout, sems):
        send, recv, loc = sems
        x, y, c, chips = _place()
        me = 2 * x + y
        local, sends, recvs = [], [], []
        for a, (_, l) in enumerate(srcs):
            src = cin[a] if l is None else cin[a].at[l]
            local.append(pltpu.make_async_copy(src, cout[a].at[me], loc.at[a]))
            for j, (px, py) in enumerate(chips):
                kw = dict(src_ref=src, send_sem=send.at[3 * a + j], recv_sem=recv.at[3 * a + j],
                          device_id=(px, py, c), device_id_type=MESH)
                sends.append(pltpu.make_async_remote_copy(dst_ref=cout[a].at[me], **kw))
                recvs.append(pltpu.make_async_remote_copy(dst_ref=cout[a].at[2 * px + py], **kw))
        return local, sends, recvs

    return _Comm([a for a, _ in srcs], [jax.ShapeDtypeStruct((NSH,) + s, CDT) for s in shapes],
                 [pltpu.SemaphoreType.DMA((3 * n,)), pltpu.SemaphoreType.DMA((3 * n,)), pltpu.SemaphoreType.DMA((n,))],
                 copies)


class _GatherHalves(_Comm):
    def __init__(self, srcs):
        n = len(srcs)
        self.srcs = srcs
        super().__init__([a for a, _ in srcs], [jax.ShapeDtypeStruct((NSH,) + a.shape[1:], CDT) for a, _ in srcs],
                         [pltpu.SemaphoreType.DMA((3 * n,))] * 4 + [pltpu.SemaphoreType.DMA((n,))], None)

    def _copies(self, cin, cout, sems):
        isend, irecv, dsend, drecv, loc = sems
        x, y, c, chips = _place()
        me = 2 * x + y
        local, ici_out, ici_in, d2d_out, d2d_in = [], [], [], [], []
        for a, (arr, l) in enumerate(self.srcs):
            hr = arr.shape[1] // 2
            mine = pl.ds(pl.multiple_of(c * hr, hr), hr)
            other = pl.ds(pl.multiple_of((1 - c) * hr, hr), hr)
            src = cin[a].at[l]
            local.append(pltpu.make_async_copy(src, cout[a].at[me], loc.at[a]))
            for j, (px, py) in enumerate(chips):
                k, sj = 3 * a + j, 2 * px + py
                ici = dict(src_ref=src.at[mine], send_sem=isend.at[k], recv_sem=irecv.at[k],
                           device_id=(px, py, c), device_id_type=MESH)
                ici_out.append(pltpu.make_async_remote_copy(dst_ref=cout[a].at[me, mine], **ici))
                ici_in.append(pltpu.make_async_remote_copy(dst_ref=cout[a].at[sj, mine], **ici))
                d2d = dict(src_ref=cout[a].at[sj, mine], send_sem=dsend.at[k], recv_sem=drecv.at[k],
                           device_id=(x, y, 1 - c), device_id_type=MESH)
                d2d_out.append(pltpu.make_async_remote_copy(dst_ref=cout[a].at[sj, mine], **d2d))
                d2d_in.append(pltpu.make_async_remote_copy(dst_ref=cout[a].at[sj, other], **d2d))
        return local, ici_out, ici_in, d2d_out, d2d_in

    def start(self, cin, cout, sems):
        local, ici_out, _, _, _ = self._copies(cin, cout, sems)
        for cp in local + ici_out:
            cp.start()

    def wait(self, cin, cout, sems):
        local, ici_out, ici_in, d2d_out, d2d_in = self._copies(cin, cout, sems)
        for got, fwd in zip(ici_in, d2d_out):
            got.wait_recv()
            fwd.start()
        for cp in d2d_in:
            cp.wait_recv()
        for cp in d2d_out + ici_out:
            cp.wait_send()
        for cp in local:
            cp.wait()


def _scatter_comm(grads):
    n = len(grads)

    def copies(cin, cout, sems):
        send, recv = sems
        x, y, c, chips = _place()
        sends, recvs = [], []
        for a in range(n):
            for j, (px, py) in enumerate(chips):
                cp = pltpu.make_async_remote_copy(
                    src_ref=cin[a].at[2 * px + py], dst_ref=cout[a].at[j], send_sem=send.at[3 * a + j],
                    recv_sem=recv.at[3 * a + j], device_id=(px, py, c), device_id_type=MESH)
                sends.append(cp)
                recvs.append(cp)
        return [], sends, recvs

    return _Comm(grads, [jax.ShapeDtypeStruct((3,) + g.shape[1:], g.dtype) for g in grads],
                 [pltpu.SemaphoreType.DMA((3 * n,)), pltpu.SemaphoreType.DMA((3 * n,))], copies)


def _swap_comm(parts):
    n = len(parts)

    def copies(cin, cout, sems):
        send, recv = sems
        x, y, c, _ = _place()
        cps = [pltpu.make_async_remote_copy(
            src_ref=cin[a], dst_ref=cout[a], send_sem=send.at[a], recv_sem=recv.at[a],
            device_id=(x, y, 1 - c), device_id_type=MESH) for a in range(n)]
        return [], cps, cps

    return _Comm(parts, [jax.ShapeDtypeStruct(p.shape, p.dtype) for p in parts],
                 [pltpu.SemaphoreType.DMA((n,)), pltpu.SemaphoreType.DMA((n,))], copies)


def _join_comms(comms):
    comms = [c for c in comms if c is not None]
    if len(comms) <= 1:
        return comms[0] if comms else None

    def copies(cin, cout, sems):
        out = ([], [], [])
        i = o = k = 0
        for c in comms:
            part = c.copies(cin[i:i + len(c.args)], cout[o:o + len(c.out_shape)], sems[k:k + len(c.sems)])
            for acc, cps in zip(out, part):
                acc.extend(cps)
            i, o, k = i + len(c.args), o + len(c.out_shape), k + len(c.sems)
        return out

    return _Comm([a for c in comms for a in c.args], [s for c in comms for s in c.out_shape],
                 [s for c in comms for s in c.sems], copies)


def _run_comm(comm, name):
    k_in, k_out = len(comm.args), len(comm.out_shape)

    def body(*refs):
        cin, cout, sems = refs[:k_in], refs[k_in:k_in + k_out], refs[k_in + k_out:]
        comm.start(cin, cout, sems)
        comm.wait(cin, cout, sems)

    return pl.pallas_call(body, name=name, in_specs=[HBM] * k_in, out_specs=[HBM] * k_out, out_shape=comm.out_shape,
                          scratch_shapes=comm.sems)(*comm.args)


def _pallas(body, *, name, grid, in_specs, out_specs, out_shape, args, scratch=(), comm=None):
    out_specs, out_shape, scratch = list(out_specs), list(out_shape), list(scratch)
    params = _cp(len(grid))
    if comm is None:
        return pl.pallas_call(body, grid=grid, name=name, in_specs=in_specs, out_specs=out_specs, out_shape=out_shape,
                              scratch_shapes=scratch, compiler_params=params)(*args), []
    n_in, n_out, n_sc = len(args), len(out_shape), len(scratch)
    k_in, k_out = len(comm.args), len(comm.out_shape)

    def carrying(*refs):
        ins, refs = refs[:n_in], refs[n_in:]
        cin, refs = refs[:k_in], refs[k_in:]
        outs, refs = refs[:n_out], refs[n_out:]
        cout, refs = refs[:k_out], refs[k_out:]
        sc, sems = refs[:n_sc], refs[n_sc:]
        first = functools.reduce(jnp.logical_and, [pl.program_id(d) == 0 for d in range(len(grid))])
        last = functools.reduce(jnp.logical_and, [pl.program_id(d) == g - 1 for d, g in enumerate(grid)])

        @pl.when(first)
        def _():
            comm.start(cin, cout, sems)

        body(*ins, *outs, *sc)

        @pl.when(last)
        def _():
            comm.wait(cin, cout, sems)

    res = pl.pallas_call(
        carrying, grid=grid, name=name, in_specs=list(in_specs) + [HBM] * k_in, out_specs=out_specs + [HBM] * k_out,
        out_shape=out_shape + comm.out_shape, scratch_shapes=scratch + comm.sems, compiler_params=params,
    )(*args, *comm.args)
    return res[:n_out], res[n_out:]


def _pool_normed_ext(h_ref, halo_ref, g, tm):
    xe = jnp.concatenate([halo_ref[...], h_ref[...]], axis=0)
    hn = _rms_fwd(xe, g)
    row = lax.broadcasted_iota(jnp.int32, (tm + HALO, 1), 0)
    return jnp.where((row >= HALO) | (pl.program_id(0) > 0), hn, 0.0)


WBLK = 128


def _window_sums(xe, tm, w, trailing):
    blk = min(WBLK, tm)
    r = lax.broadcasted_iota(jnp.int32, (blk, blk + HALO), 0)
    j = lax.broadcasted_iota(jnp.int32, (blk, blk + HALO), 1)
    d = r + HALO - j if trailing else j - r
    band = ((d >= 1) & (d < w)).astype(CDT)
    xc = xe.astype(CDT)
    return jnp.concatenate([_dot(band, xc[b * blk:(b + 1) * blk + HALO, :]) for b in range(tm // blk)], axis=0)


def _pool_windows(hn_e, tm):
    t = lax.broadcasted_iota(jnp.int32, (tm, PG), 0) + pl.program_id(0) * tm
    outs = []
    for g, w in enumerate(POOL_WINDOWS):
        xe = hn_e[:, g * PG:(g + 1) * PG]
        cnt = jnp.minimum(t + 1, w).astype(F32)
        x = xe[HALO:, :]
        outs.append((_window_sums(xe, tm, w, True) + x) / cnt - x)
    return jnp.concatenate(outs, axis=1)


def _pool_apply(pooled, w_ref):
    return jnp.concatenate([_dot(pooled[:, g * PG:(g + 1) * PG], w_ref[g]) for g in range(len(POOL_WINDOWS))], axis=1)


def _halo_prev(tm):
    return pl.BlockSpec((HALO, D), lambda i: (jnp.maximum(i * (tm // HALO) - 1, 0), 0))


def _pool_fwd(h, g_pre, pw, pscale, g_post, l, comm=None):
    T = h.shape[0]
    tm = min(TM, T)

    def body(h_ref, halo_ref, gpre_ref, w_ref, sc_ref, gpost_ref, o_ref):
        hn_e = _pool_normed_ext(h_ref, halo_ref, gpre_ref[l:l + 1, :], tm)
        pooled = _pool_windows(hn_e, tm).astype(CDT)
        m = _pool_apply(pooled, w_ref) * sc_ref[l:l + 1, :]
        o_ref[...] = h_ref[...] + _rms_fwd(m, gpost_ref[l:l + 1, :])

    return _pallas(
        body, grid=(T // tm,), name=f"pool_fwd_{l}",
        in_specs=[_tile(tm, D), _halo_prev(tm), _res(g_pre.shape),
                  pl.BlockSpec((None,) + pw.shape[1:], lambda i: (l, 0, 0, 0), pipeline_mode=pl.Buffered(1)),
                  _res(pscale.shape), _res(g_post.shape)],
        out_specs=[_tile(tm, D)], out_shape=[jax.ShapeDtypeStruct((T, D), F32)],
        args=(h, h, g_pre, pw, pscale, g_post), comm=comm)


def _pool_bwd_a(dh1, h, g_pre, pw, pscale, g_post, l, comm=None):
    T = h.shape[0]
    tm = min(TM, T)
    ng = len(POOL_WINDOWS)

    def body(dh_ref, h_ref, halo_ref, gpre_ref, w_ref, sc_ref, gpost_ref, dp_ref, dw_ref, dsc_ref, dgpost_ref):
        _zero_first(dw_ref, dsc_ref, dgpost_ref)
        hn_e = _pool_normed_ext(h_ref, halo_ref, gpre_ref[l:l + 1, :], tm)
        pooled = _pool_windows(hn_e, tm).astype(CDT)
        y = _pool_apply(pooled, w_ref)
        sc = sc_ref[l:l + 1, :]
        dm, dg = _rms_bwd(dh_ref[...], y * sc, gpost_ref[l:l + 1, :])
        dgpost_ref[...] += dg
        dsc_ref[...] += jnp.sum(dm * y, axis=0, keepdims=True)
        dy = (dm * sc).astype(CDT)
        dps = []
        for g in range(ng):
            dyg = dy[:, g * PG:(g + 1) * PG]
            dw_ref[g] += _dot_tn(pooled[:, g * PG:(g + 1) * PG], dyg)
            dps.append(_dot_nt(dyg, w_ref[g]))
        dp_ref[...] = jnp.concatenate(dps, axis=1)

    return _pallas(
        body, grid=(T // tm,), name=f"pool_bwd_a_{l}",
        in_specs=[_tile(tm, D), _tile(tm, D), _halo_prev(tm), _res(g_pre.shape),
                  pl.BlockSpec((None,) + pw.shape[1:], lambda i: (l, 0, 0, 0), pipeline_mode=pl.Buffered(1)),
                  _res(pscale.shape), _res(g_post.shape)],
        out_specs=[_tile(tm, D), pl.BlockSpec((ng, PG, PG), lambda i: (0, 0, 0)), _acc(D), _acc(D)],
        out_shape=[jax.ShapeDtypeStruct((T, D), F32), jax.ShapeDtypeStruct((ng, PG, PG), F32),
                   jax.ShapeDtypeStruct((1, D), F32), jax.ShapeDtypeStruct((1, D), F32)],
        args=(dh1, h, h, g_pre, pw, pscale, g_post), comm=comm)


def _pool_bwd_b(dpool, dh1, h, g_pre, l, comm=None):
    T = h.shape[0]
    tm = min(TM, T)
    nt = T // tm

    def body(dp_ref, nxt_ref, dh_ref, h_ref, gpre_ref, o_ref, dgpre_ref):
        _zero_first(dgpre_ref)
        i = pl.program_id(0)
        dp = dp_ref[...]
        e_e = jnp.concatenate([dp, jnp.where(i < nt - 1, nxt_ref[...], 0.0)], axis=0)
        t = lax.broadcasted_iota(jnp.int32, (tm + HALO, PG), 0) + i * tm
        outs = []
        for g, w in enumerate(POOL_WINDOWS):
            e = e_e[:, g * PG:(g + 1) * PG] / jnp.minimum(t + 1, w).astype(F32)
            outs.append(_window_sums(e, tm, w, False) + (e[:tm, :] - dp[:, g * PG:(g + 1) * PG]))
        dx, dg = _rms_bwd(jnp.concatenate(outs, axis=1), h_ref[...], gpre_ref[l:l + 1, :])
        dgpre_ref[...] += dg
        o_ref[...] = dh_ref[...] + dx

    nxt = pl.BlockSpec((HALO, D), lambda i: (jnp.minimum((i + 1) * (tm // HALO), T // HALO - 1), 0))
    return _pallas(
        body, grid=(nt,), name=f"pool_bwd_b_{l}",
        in_specs=[_tile(tm, D), nxt, _tile(tm, D), _tile(tm, D), _res(g_pre.shape)],
        out_specs=[_tile(tm, D), _acc(D)],
        out_shape=[jax.ShapeDtypeStruct((T, D), F32), jax.ShapeDtypeStruct((1, D), F32)],
        args=(dpool, dpool, dh1, h, g_pre), comm=comm)


def _proj_rows(x, w_ref):
    k = w_ref.shape[1]
    out = _dot(x[:, :k], w_ref[0])
    for s in range(1, NSH):
        out = out + _dot(x[:, s * k:(s + 1) * k], w_ref[s])
    return out


def _proj_rows_t(dy, w_ref):
    return jnp.concatenate([_dot_nt(dy, w_ref[s]) for s in range(NSH)], axis=1)


def _rope_tiles(cos_ref, sin_ref, width):
    reps = width // cos_ref.shape[1]
    return jnp.tile(cos_ref[...], (1, reps)), jnp.tile(sin_ref[...], (1, reps))


def _q_fwd(h, g_pre, wq, cos, sin, l, j):
    T = h.shape[0]
    tm = min(TM, T)

    def body(h_ref, g_ref, w_ref, cos_ref, sin_ref, q_ref):
        hn = _rms_fwd(h_ref[...], g_ref[l:l + 1, :]).astype(CDT)
        c, s = _rope_tiles(cos_ref, sin_ref, D)
        q_ref[...] = (_rope_fwd(_proj_rows(hn, w_ref), c, s) * QSCALE).astype(CDT)

    return pl.pallas_call(
        body, grid=(T // tm,), name=f"q_fwd_{l}",
        in_specs=[_tile(tm, D), _res(g_pre.shape), _layer_w((D // NSH, D)), _tile(tm, 2 * HD), _tile(tm, 2 * HD)],
        out_specs=_tile(tm, D), out_shape=jax.ShapeDtypeStruct((T, D), CDT), compiler_params=_cp(),
    )(h, g_pre, wq, cos, sin)


def _q_bwd(dq, dh1, h, g_pre, wq, cos, sin, l, j):
    T = h.shape[0]
    tm = min(TM, T)

    def body(dq_ref, dh_ref, h_ref, g_ref, w_ref, cos_ref, sin_ref, o_ref, hn_ref, dq0_ref, dg_ref):
        _zero_first(dg_ref)
        g = g_ref[l:l + 1, :]
        x = h_ref[...]
        hn_ref[...] = _rms_fwd(x, g).astype(CDT)
        c, s = _rope_tiles(cos_ref, sin_ref, D)
        dq0 = _rope_bwd(dq_ref[...].astype(F32) * QSCALE, c, s).astype(CDT)
        dq0_ref[...] = dq0
        dx, dg = _rms_bwd(_proj_rows_t(dq0, w_ref), x, g)
        dg_ref[...] += dg
        o_ref[...] = dh_ref[...] + dx

    return pl.pallas_call(
        body, grid=(T // tm,), name=f"q_bwd_{l}",
        in_specs=[_tile(tm, D), _tile(tm, D), _tile(tm, D), _res(g_pre.shape), _layer_w((D // NSH, D)),
                  _tile(tm, 2 * HD), _tile(tm, 2 * HD)],
        out_specs=[_tile(tm, D), _tile(tm, D), _tile(tm, D), _acc(D)],
        out_shape=[jax.ShapeDtypeStruct((T, D), F32), jax.ShapeDtypeStruct((T, D), CDT),
                   jax.ShapeDtypeStruct((T, D), CDT), jax.ShapeDtypeStruct((1, D), F32)],
        compiler_params=_cp(),
    )(dq, dh1, h, g_pre, wq, cos, sin)


def _o_fwd(o, h, wo, g_post, l, j):
    T = h.shape[0]
    tm = min(TM, T)

    def body(o_ref, h_ref, w_ref, g_ref, h1_ref, m_ref):
        m = _proj_rows(o_ref[...], w_ref)
        m_ref[...] = m.astype(CDT)
        h1_ref[...] = h_ref[...] + _rms_fwd(m, g_ref[l:l + 1, :])

    return pl.pallas_call(
        body, grid=(T // tm,), name=f"o_fwd_{l}",
        in_specs=[_tile(tm, D), _tile(tm, D), _layer_w((D // NSH, D)), _res(g_post.shape)],
        out_specs=[_tile(tm, D), _tile(tm, D)],
        out_shape=[jax.ShapeDtypeStruct((T, D), F32), jax.ShapeDtypeStruct((T, D), CDT)], compiler_params=_cp(),
    )(o, h, wo, g_post)


def _o_bwd(dh1, m, wo, g_post, l, j):
    T = m.shape[0]
    tm = min(TM, T)

    def body(dh_ref, m_ref, w_ref, g_ref, do_ref, dm_ref, dg_ref):
        _zero_first(dg_ref)
        dm, dg = _rms_bwd(dh_ref[...], m_ref[...].astype(F32), g_ref[l:l + 1, :])
        dg_ref[...] += dg
        dmc = dm.astype(CDT)
        dm_ref[...] = dmc
        do_ref[...] = _proj_rows_t(dmc, w_ref).astype(CDT)

    return pl.pallas_call(
        body, grid=(T // tm,), name=f"o_bwd_{l}",
        in_specs=[_tile(tm, D), _tile(tm, D), _layer_w((D // NSH, D)), _res(g_post.shape)],
        out_specs=[_tile(tm, D), _tile(tm, D), _acc(D)],
        out_shape=[jax.ShapeDtypeStruct((T, D), CDT), jax.ShapeDtypeStruct((T, D), CDT),
                   jax.ShapeDtypeStruct((1, D), F32)],
        compiler_params=_cp(),
    )(dh1, m, wo, g_post)


KVW = 2 * NKV * HD


def _kv_fwd(h, g_kv, wk2, wv2, cos, sin):
    T = h.shape[0]
    tm = min(TM, T)

    def body(h_ref, g_ref, wk_ref, wv_ref, cos_ref, sin_ref, k_ref, v_ref):
        hk = _rms_fwd(h_ref[...], g_ref[...]).astype(CDT)
        c, s = _rope_tiles(cos_ref, sin_ref, KVW)
        k_ref[...] = _rope_fwd(_dot(hk, wk_ref[...]), c, s).astype(CDT)
        v_ref[...] = _dot(hk, wv_ref[...]).astype(CDT)

    return pl.pallas_call(
        body, grid=(T // tm,), name="kv_fwd",
        in_specs=[_tile(tm, D), _res(g_kv.shape), _res(wk2.shape), _res(wv2.shape), _tile(tm, 2 * HD), _tile(tm, 2 * HD)],
        out_specs=[_tile(tm, KVW), _tile(tm, KVW)],
        out_shape=[jax.ShapeDtypeStruct((T, KVW), CDT)] * 2, compiler_params=_cp(),
    )(h, g_kv, wk2, wv2, cos, sin)


def _kv_bwd(dh, h, dks, dkhs, dvs, dvhs, g_kv, wk2, wv2, cos, sin):
    T = h.shape[0]
    tq = min(TQ, T)
    nt = T // tq
    n = len(dks)

    def body(*refs):
        dh_ref, h_ref = refs[:2]
        main = refs[2:2 + 2 * n]
        halo = refs[2 + 2 * n:2 + 4 * n]
        g_ref, wk_ref, wv_ref, cos_ref, sin_ref, o_ref, hk_ref, dk0_ref, dv_ref, dg_ref = refs[2 + 4 * n:]
        _zero_first(dg_ref)
        i = pl.program_id(0)

        def total(mains, halos):
            d = mains[0][...]
            for r in mains[1:]:
                d = d + r[...]
            hl = halos[0][...]
            for r in halos[1:]:
                hl = hl + r[...]
            row = lax.broadcasted_iota(jnp.int32, (tq, 1), 0)
            if tq > WIN:
                hl = jnp.concatenate([jnp.zeros((tq - WIN, KVW), F32), hl], axis=0)
            return d + jnp.where((row >= tq - WIN) & (i < nt - 1), hl, 0.0)

        dk = total(main[:n], halo[:n])
        dv = total(main[n:], halo[n:])
        x = h_ref[...]
        g = g_ref[...]
        hk_ref[...] = _rms_fwd(x, g).astype(CDT)
        c, s = _rope_tiles(cos_ref, sin_ref, KVW)
        dk0 = _rope_bwd(dk, c, s).astype(CDT)
        dvc = dv.astype(CDT)
        dk0_ref[...] = dk0
        dv_ref[...] = dvc
        dx, dg = _rms_bwd(_dot_nt(dk0, wk_ref[...]) + _dot_nt(dvc, wv_ref[...]), x, g)
        dg_ref[...] += dg
        o_ref[...] = dh_ref[...] + dx

    nxt = pl.BlockSpec((WIN, KVW), lambda i: (jnp.minimum(i + 1, nt - 1), 0))
    return pl.pallas_call(
        body, grid=(nt,), name="kv_bwd",
        in_specs=[_tile(tq, D), _tile(tq, D)] + [_tile(tq, KVW)] * (2 * n) + [nxt] * (2 * n)
        + [_res(g_kv.shape), _res(wk2.shape), _res(wv2.shape), _tile(tq, 2 * HD), _tile(tq, 2 * HD)],
        out_specs=[_tile(tq, D), _tile(tq, D), _tile(tq, KVW), _tile(tq, KVW), _acc(D)],
        out_shape=[jax.ShapeDtypeStruct((T, D), F32), jax.ShapeDtypeStruct((T, D), CDT),
                   jax.ShapeDtypeStruct((T, KVW), CDT), jax.ShapeDtypeStruct((T, KVW), CDT),
                   jax.ShapeDtypeStruct((1, D), F32)],
        compiler_params=_cp(),
    )(dh, h, *dks, *dvs, *dkhs, *dvhs, g_kv, wk2, wv2, cos, sin)


GQ = 4
LW = 2 * HD


def _from_prev():
    r = lax.broadcasted_iota(jnp.int32, (GQ * WIN, WIN), 0) % WIN
    j = lax.broadcasted_iota(jnp.int32, (GQ * WIN, WIN), 1)
    return j > r


def _fold(x2, prev):
    return jnp.where(prev, x2[:, :WIN], x2[:, WIN:])


def _unfold(x, prev):
    zero = jnp.zeros_like(x)
    return jnp.concatenate([jnp.where(prev, x, zero), jnp.where(prev, zero, x)], axis=1)


def _stack_heads(ref, rows, g):
    lane = lax.broadcasted_iota(jnp.int32, (WIN, LW), 1)
    parts = []
    for pr in range(2):
        x = ref[rows, (2 * g + pr) * LW:(2 * g + pr + 1) * LW]
        parts += [jnp.where(lane < HD, x, jnp.zeros_like(x)), jnp.where(lane >= HD, x, jnp.zeros_like(x))]
    return jnp.concatenate(parts, axis=0)


def _unstack_heads(x4):
    lane = lax.broadcasted_iota(jnp.int32, (WIN, LW), 1)
    return [jnp.where(lane < HD, x4[(2 * pr) * WIN:(2 * pr + 1) * WIN], x4[(2 * pr + 1) * WIN:(2 * pr + 2) * WIN])
            for pr in range(2)]


def _sink_col(sink_ref, j, g):
    blk = lax.broadcasted_iota(jnp.int32, (GQ * WIN, 1), 0) // WIN
    col = jnp.zeros((GQ * WIN, 1), F32)
    for a in range(GQ):
        col = jnp.where(blk == a, sink_ref[j, GQ * g + a], col)
    return col


def _softmax_sink(s, hidden, sink):
    if hidden is not None:
        s = jnp.where(hidden, NEG, s)
    m = jnp.maximum(jnp.max(s, axis=-1, keepdims=True), sink)
    e = jnp.exp(s - m)
    es = jnp.exp(sink - m)
    l = jnp.sum(e, axis=-1, keepdims=True) + es
    return e / l, es / l


def _kv_halo(tq):
    return pl.BlockSpec((WIN, KVW), lambda i: (jnp.maximum(i * (tq // WIN) - 1, 0), 0))


def _attn_fwd(q, kk, vv, sinks, l, j):
    T = q.shape[0]
    tq = min(TQ, T)

    def body(sink_ref, q_ref, k_ref, kh_ref, v_ref, vh_ref, o_ref):
        i = pl.program_id(0)
        ke = jnp.concatenate([kh_ref[...], k_ref[...]], axis=0)
        ve = jnp.concatenate([vh_ref[...], v_ref[...]], axis=0)
        prev = _from_prev()
        for n in range(tq // WIN):
            rows = slice(n * WIN, (n + 1) * WIN)
            hidden = prev & (i == 0) if n == 0 else None
            for g in range(NKV):
                kg = ke[n * WIN:(n + 2) * WIN, g * LW:(g + 1) * LW]
                vg = ve[n * WIN:(n + 2) * WIN, g * LW:(g + 1) * LW]
                s = _fold(_dot_nt(_stack_heads(q_ref, rows, g), kg), prev)
                p, _ = _softmax_sink(s, hidden, _sink_col(sink_ref, j, g))
                pairs = _unstack_heads(_dot(_unfold(p.astype(CDT), prev), vg))
                for pr in range(2):
                    o_ref[rows, (2 * g + pr) * LW:(2 * g + pr + 1) * LW] = pairs[pr].astype(CDT)

    return pl.pallas_call(
        body, grid=(T // tq,), name=f"attn_fwd_{l}",
        in_specs=[pl.BlockSpec(memory_space=pltpu.SMEM), _tile(tq, D), _tile(tq, KVW), _kv_halo(tq),
                  _tile(tq, KVW), _kv_halo(tq)],
        out_specs=_tile(tq, D), out_shape=jax.ShapeDtypeStruct((T, D), CDT), compiler_params=_cp(),
    )(sinks, q, kk, kk, vv, vv)


def _attn_bwd(q, kk, vv, o, do, sinks, l, j):
    T = q.shape[0]
    tq = min(TQ, T)
    nt = T // tq

    def body(sink_ref, q_ref, k_ref, kh_ref, v_ref, vh_ref, o_ref, do_ref,
             dq_ref, dk_ref, dkh_ref, dv_ref, dvh_ref, dsink_ref, dke, dve):
        _zero_first(dsink_ref)
        i = pl.program_id(0)
        ke = jnp.concatenate([kh_ref[...], k_ref[...]], axis=0)
        ve = jnp.concatenate([vh_ref[...], v_ref[...]], axis=0)
        dke[...] = jnp.zeros_like(dke)
        dve[...] = jnp.zeros_like(dve)
        lane = lax.broadcasted_iota(jnp.int32, (1, LW), 1)
        blk = lax.broadcasted_iota(jnp.int32, (GQ * WIN, 1), 0) // WIN
        dsink = jnp.zeros((1, LW), F32)
        prev = _from_prev()
        for n in range(tq // WIN):
            rows = slice(n * WIN, (n + 1) * WIN)
            krows = slice(n * WIN, (n + 2) * WIN)
            hidden = prev & (i == 0) if n == 0 else None
            for g in range(NKV):
                cols = slice(g * LW, (g + 1) * LW)
                kg = ke[krows, cols]
                vg = ve[krows, cols]
                q4 = _stack_heads(q_ref, rows, g)
                do4 = _stack_heads(do_ref, rows, g)
                o4 = _stack_heads(o_ref, rows, g)
                p, ps = _softmax_sink(_fold(_dot_nt(q4, kg), prev), hidden, _sink_col(sink_ref, j, g))
                delta = jnp.sum(do4.astype(F32) * o4.astype(F32), axis=-1, keepdims=True)
                ds = _unfold((p * (_fold(_dot_nt(do4, vg), prev) - delta)).astype(CDT), prev)
                pc = _unfold(p.astype(CDT), prev)
                pairs = _unstack_heads(_dot(ds, kg))
                for pr in range(2):
                    dq_ref[rows, (2 * g + pr) * LW:(2 * g + pr + 1) * LW] = pairs[pr].astype(CDT)
                dke[krows, cols] += _dot_tn(ds, q4)
                dve[krows, cols] += _dot_tn(pc, do4)
                t = ps * delta
                for a in range(GQ):
                    dsink = dsink - jnp.where(lane == GQ * g + a, jnp.sum(jnp.where(blk == a, t, 0.0)), 0.0)
        dsink_ref[...] += dsink
        dkh_ref[...] = dke[:WIN, :]
        dk_ref[...] = dke[WIN:, :]
        dvh_ref[...] = dve[:WIN, :]
        dv_ref[...] = dve[WIN:, :]

    halo_out = pl.BlockSpec((WIN, KVW), lambda i: (i, 0))
    return pl.pallas_call(
        body, grid=(nt,), name=f"attn_bwd_{l}",
        in_specs=[pl.BlockSpec(memory_space=pltpu.SMEM), _tile(tq, D), _tile(tq, KVW), _kv_halo(tq),
                  _tile(tq, KVW), _kv_halo(tq), _tile(tq, D), _tile(tq, D)],
        out_specs=[_tile(tq, D), _tile(tq, KVW), halo_out, _tile(tq, KVW), halo_out, _acc(LW)],
        out_shape=[jax.ShapeDtypeStruct((T, D), CDT), jax.ShapeDtypeStruct((T, KVW), F32),
                   jax.ShapeDtypeStruct((nt * WIN, KVW), F32), jax.ShapeDtypeStruct((T, KVW), F32),
                   jax.ShapeDtypeStruct((nt * WIN, KVW), F32), jax.ShapeDtypeStruct((1, LW), F32)],
        scratch_shapes=[pltpu.VMEM((WIN + tq, KVW), F32), pltpu.VMEM((WIN + tq, KVW), F32)],
        compiler_params=_cp(),
    )(sinks, q, kk, kk, vv, vv, o, do)


def _chunks(tm):
    return pl.BlockSpec((NSH, tm, FS), lambda i: (0, i, 0))


def _ffn_fwd(h1, g_pre, wg, wu, wd, g_post, l, comm=None):
    T = h1.shape[0]
    tm = min(TM_FFN, T)

    def body(h_ref, gpre_ref, wg_ref, wu_ref, wd_ref, gpost_ref, h2_ref, a_ref, b_ref, f_ref):
        x = h_ref[...]
        fn = _rms_fwd(x, gpre_ref[l:l + 1, :]).astype(CDT)
        f = jnp.zeros((tm, D), F32)
        for s in range(NSH):
            a = _dot_nt(fn, wg_ref[s])
            b = _dot_nt(fn, wu_ref[s])
            a_ref[s] = a.astype(CDT)
            b_ref[s] = b.astype(CDT)
            f = f + _dot((a * _sigmoid(a) * b).astype(CDT), wd_ref[s])
        f_ref[...] = f.astype(CDT)
        h2_ref[...] = x + _rms_fwd(f, gpost_ref[l:l + 1, :])

    return _pallas(
        body, grid=(T // tm,), name=f"ffn_fwd_{l}",
        in_specs=[_tile(tm, D), _res(g_pre.shape), _layer_w((FS, D)), _layer_w((FS, D)), _layer_w((FS, D)),
                  _res(g_post.shape)],
        out_specs=[_tile(tm, D), _chunks(tm), _chunks(tm), _tile(tm, D)],
        out_shape=[jax.ShapeDtypeStruct((T, D), F32), jax.ShapeDtypeStruct((NSH, T, FS), CDT),
                   jax.ShapeDtypeStruct((NSH, T, FS), CDT), jax.ShapeDtypeStruct((T, D), CDT)],
        args=(h1, g_pre, wg, wu, wd, g_post), comm=comm)


def _ffn_bwd(dh2, h1, f, a, b, g_pre, wg, wu, wd, g_post, l, comm=None):
    T = h1.shape[0]
    tm = min(TM_FFN_BWD, T)

    def body(dh_ref, h_ref, f_ref, a_ref, b_ref, gpre_ref, wg_ref, wu_ref, wd_ref, gpost_ref,
             dh1_ref, fn_ref, df_ref, act_ref, da_ref, db_ref, dgpre_ref, dgpost_ref):
        _zero_first(dgpre_ref, dgpost_ref)
        x = h_ref[...]
        gpre = gpre_ref[l:l + 1, :]
        fn_ref[...] = _rms_fwd(x, gpre).astype(CDT)
        dh2 = dh_ref[...]
        df, dg = _rms_bwd(dh2, f_ref[...].astype(F32), gpost_ref[l:l + 1, :])
        dgpost_ref[...] += dg
        dfc = df.astype(CDT)
        df_ref[...] = dfc
        dfn = jnp.zeros((tm, D), F32)
        for s in range(NSH):
            av = a_ref[s].astype(F32)
            bv = b_ref[s].astype(F32)
            sg = _sigmoid(av)
            silu = av * sg
            act_ref[s] = (silu * bv).astype(CDT)
            dact = _dot_nt(dfc, wd_ref[s])
            da = (dact * bv * (sg * (1.0 + av * (1.0 - sg)))).astype(CDT)
            db = (dact * silu).astype(CDT)
            da_ref[s] = da
            db_ref[s] = db
            dfn = dfn + _dot(da, wg_ref[s]) + _dot(db, wu_ref[s])
        dx, dg = _rms_bwd(dfn, x, gpre)
        dgpre_ref[...] += dg
        dh1_ref[...] = dh2 + dx

    return _pallas(
        body, grid=(T // tm,), name=f"ffn_bwd_{l}",
        in_specs=[_tile(tm, D), _tile(tm, D), _tile(tm, D), _chunks(tm), _chunks(tm), _res(g_pre.shape),
                  _layer_w((FS, D)), _layer_w((FS, D)), _layer_w((FS, D)), _res(g_post.shape)],
        out_specs=[_tile(tm, D), _tile(tm, D), _tile(tm, D), _chunks(tm), _chunks(tm), _chunks(tm), _acc(D), _acc(D)],
        out_shape=[jax.ShapeDtypeStruct((T, D), F32), jax.ShapeDtypeStruct((T, D), CDT), jax.ShapeDtypeStruct((T, D), CDT),
                   jax.ShapeDtypeStruct((NSH, T, FS), CDT), jax.ShapeDtypeStruct((NSH, T, FS), CDT),
                   jax.ShapeDtypeStruct((NSH, T, FS), CDT), jax.ShapeDtypeStruct((1, D), F32),
                   jax.ShapeDtypeStruct((1, D), F32)],
        args=(dh2, h1, f, a, b, g_pre, wg, wu, wd, g_post), comm=comm)


def _ple_proj(p, wpp_ref):
    return jnp.concatenate([_dot(p, wpp_ref[s]) for s in range(NSH)], axis=1)


def _ple_fwd(h2, p, g_ple, wpg, wpp, l, comm=None):
    T = h2.shape[0]
    tm = min(TM, T)

    def body(h_ref, p_ref, g_ref, wpg_ref, wpp_ref, o_ref):
        x = h_ref[...]
        pn = _rms_fwd(x, g_ref[l:l + 1, :]).astype(CDT)
        gate = _sigmoid(_proj_rows(pn, wpg_ref))
        o_ref[...] = x + _ple_proj(p_ref[...].astype(CDT), wpp_ref) * gate

    return _pallas(
        body, grid=(T // tm,), name=f"ple_fwd_{l}",
        in_specs=[_tile(tm, D), pl.BlockSpec((None, tm, PLE), lambda i: (l, i, 0)), _res(g_ple.shape),
                  _layer_w((D // NSH, D)), _layer_w((PLE, D // NSH))],
        out_specs=[_tile(tm, D)], out_shape=[jax.ShapeDtypeStruct((T, D), F32)],
        args=(h2, p, g_ple, wpg, wpp), comm=comm)


def _ple_fwd_loss(h2, p, g_ple, wpg, wpp, tgt, l):
    T = h2.shape[0]
    tm = min(TM, T)

    def body(h_ref, p_ref, g_ref, wpg_ref, wpp_ref, t_ref, dy_ref, loss_ref):
        _zero_first(loss_ref)
        x = h_ref[...]
        pn = _rms_fwd(x, g_ref[l:l + 1, :]).astype(CDT)
        gate = _sigmoid(_proj_rows(pn, wpg_ref))
        err = (x + _ple_proj(p_ref[...].astype(CDT), wpp_ref) * gate) - t_ref[...]
        dy_ref[...] = err * (1.0 / D)
        lane = lax.broadcasted_iota(jnp.int32, (1, 128), 1)
        loss_ref[...] += jnp.where(lane == 0, (0.5 / D) * jnp.sum(err * err), 0.0)

    return pl.pallas_call(
        body, grid=(T // tm,), name=f"ple_fwd_loss_{l}",
        in_specs=[_tile(tm, D), pl.BlockSpec((None, tm, PLE), lambda i: (l, i, 0)), _res(g_ple.shape),
                  _layer_w((D // NSH, D)), _layer_w((PLE, D // NSH)), _tile(tm, D)],
        out_specs=[_tile(tm, D), _acc(128)],
        out_shape=[jax.ShapeDtypeStruct((T, D), F32), jax.ShapeDtypeStruct((1, 128), F32)], compiler_params=_cp(),
    )(h2, p, g_ple, wpg, wpp, tgt)


def _ple_bwd(dh3, h2, p, g_ple, wpg, wpp, l):
    T = h2.shape[0]
    tm = min(TM, T)

    def body(dh_ref, h_ref, p_ref, g_ref, wpg_ref, wpp_ref, o_ref, pn_ref, dz_ref, dpp_ref, pc_ref, dg_ref):
        _zero_first(dg_ref)
        x = h_ref[...]
        g = g_ref[l:l + 1, :]
        pn = _rms_fwd(x, g).astype(CDT)
        pn_ref[...] = pn
        gate = _sigmoid(_proj_rows(pn, wpg_ref))
        pc = p_ref[...].astype(CDT)
        pc_ref[...] = pc
        pp = _ple_proj(pc, wpp_ref)
        dh3 = dh_ref[...]
        dpp_ref[...] = (dh3 * gate).astype(CDT)
        dz = (dh3 * pp * gate * (1.0 - gate)).astype(CDT)
        dz_ref[...] = dz
        dx, dg = _rms_bwd(_proj_rows_t(dz, wpg_ref), x, g)
        dg_ref[...] += dg
        o_ref[...] = dh3 + dx

    return pl.pallas_call(
        body, grid=(T // tm,), name=f"ple_bwd_{l}",
        in_specs=[_tile(tm, D), _tile(tm, D), pl.BlockSpec((None, tm, PLE), lambda i: (l, i, 0)), _res(g_ple.shape),
                  _layer_w((D // NSH, D)), _layer_w((PLE, D // NSH))],
        out_specs=[_tile(tm, D), _tile(tm, D), _tile(tm, D), _tile(tm, D), _tile(tm, PLE), _acc(D)],
        out_shape=[jax.ShapeDtypeStruct((T, D), F32), jax.ShapeDtypeStruct((T, D), CDT), jax.ShapeDtypeStruct((T, D), CDT),
                   jax.ShapeDtypeStruct((T, D), CDT), jax.ShapeDtypeStruct((T, PLE), CDT), jax.ShapeDtypeStruct((1, D), F32)],
        compiler_params=_cp(),
    )(dh3, h2, p, g_ple, wpg, wpp)


def _loss_grad(y, tgt):
    T = y.shape[0]
    tm = min(TM, T)

    def body(y_ref, t_ref, dy_ref, loss_ref):
        _zero_first(loss_ref)
        err = y_ref[...] - t_ref[...]
        dy_ref[...] = err * (1.0 / D)
        lane = lax.broadcasted_iota(jnp.int32, (1, 128), 1)
        loss_ref[...] += jnp.where(lane == 0, (0.5 / D) * jnp.sum(err * err), 0.0)

    return pl.pallas_call(
        body, grid=(T // tm,), name="loss_grad", in_specs=[_tile(tm, D), _tile(tm, D)],
        out_specs=[_tile(tm, D), _acc(128)],
        out_shape=[jax.ShapeDtypeStruct((T, D), F32), jax.ShapeDtypeStruct((1, 128), F32)], compiler_params=_cp(),
    )(y, tgt)


def _mm_tn(x, dy, name, n_split=1, comm=None):
    xb, yb = x.ndim == 3, dy.ndim == 3
    T, K = x.shape[-2:]
    N = dy.shape[-1] // n_split
    B = x.shape[0] if xb else dy.shape[0] if yb else n_split
    tt = min(TT, T)
    nt = T // tt

    def body(x_ref, dy_ref, o_ref, oc_ref):
        t = pl.program_id(1)

        @pl.when(t == 0)
        def _():
            o_ref[...] = jnp.zeros_like(o_ref)

        o_ref[...] += _dot_tn(x_ref[...].astype(CDT), dy_ref[...])

        @pl.when(t == nt - 1)
        def _():
            oc_ref[...] = o_ref[...].astype(CDT)

    x_spec = pl.BlockSpec((None, tt, K), lambda b, t: (b, t, 0)) if xb else pl.BlockSpec((tt, K), lambda b, t: (t, 0))
    if yb:
        y_spec = pl.BlockSpec((None, tt, N), lambda b, t: (b, t, 0))
    else:
        y_spec = pl.BlockSpec((tt, N), lambda b, t: (t, b if n_split > 1 else 0))
    o_spec = pl.BlockSpec((None, K, N), lambda b, t: (b, 0, 0))
    res, got = _pallas(
        body, grid=(B, nt), name=name, in_specs=[x_spec, y_spec], out_specs=[o_spec, o_spec],
        out_shape=[jax.ShapeDtypeStruct((B, K, N), F32), jax.ShapeDtypeStruct((B, K, N), CDT)], args=(x, dy), comm=comm)
    return res if comm is None else (res, got)


def _fold_dup(dw2, name):
    def body(x_ref, o_ref, oc_ref):
        x = x_ref[...]
        y = jnp.concatenate([x[:, g * LW:g * LW + HD] + x[:, g * LW + HD:(g + 1) * LW] for g in range(NKV)], axis=1)
        o_ref[...] = y
        oc_ref[...] = y.astype(CDT)

    return pl.pallas_call(
        body, grid=(NSH,), name=name, in_specs=[_tile(D // NSH, KVW)], out_specs=[_tile(D // NSH, NKV * HD)] * 2,
        out_shape=[jax.ShapeDtypeStruct((D, NKV * HD), F32), jax.ShapeDtypeStruct((D, NKV * HD), CDT)],
        compiler_params=_cp(),
    )(dw2)


def _all_reduce_small(vec):
    R = vec.shape[0]
    ndev = 8

    def body(v_ref, o_ref, buf, send, recv):
        x, y, c, _ = _place()
        me = 4 * x + 2 * y + c
        buf[me] = v_ref[...]
        copies = []
        for k in range(1, ndev):
            peer = (x if not k & 4 else 1 - x, y if not k & 2 else 1 - y, c if not k & 1 else 1 - c)
            copies.append(pltpu.make_async_remote_copy(
                src_ref=v_ref, dst_ref=buf.at[me], send_sem=send.at[k - 1], recv_sem=recv.at[k - 1],
                device_id=peer, device_id_type=MESH))
            copies[-1].start()
        for k in range(1, ndev):
            peer = (x if not k & 4 else 1 - x, y if not k & 2 else 1 - y, c if not k & 1 else 1 - c)
            pltpu.make_async_remote_copy(
                src_ref=v_ref, dst_ref=buf.at[4 * peer[0] + 2 * peer[1] + peer[2]], send_sem=send.at[k - 1],
                recv_sem=recv.at[k - 1], device_id=peer, device_id_type=MESH).wait_recv()
        for cp in copies:
            cp.wait_send()
        tot = buf[0]
        for d in range(1, ndev):
            tot = tot + buf[d]
        o_ref[...] = tot

    return pl.pallas_call(
        body, name="all_reduce_small", in_specs=[pl.BlockSpec(memory_space=pltpu.VMEM)],
        out_specs=pl.BlockSpec(memory_space=pltpu.VMEM), out_shape=jax.ShapeDtypeStruct(vec.shape, F32),
        scratch_shapes=[pltpu.VMEM((ndev, R, D), F32), pltpu.SemaphoreType.DMA((ndev - 1,)),
                        pltpu.SemaphoreType.DMA((ndev - 1,))],
    )(vec)


def _row_tile(rows):
    for t in range(min(rows, 512) // 8 * 8, 7, -8):
        if rows % t == 0:
            return t
    return rows


def _sum_own_and_received(own, got, me, name):
    _, R, C = own.shape
    tr = _row_tile(R)

    def body(me_ref, own_ref, got_ref, o_ref):
        o_ref[...] = ((own_ref[...] + got_ref[0].astype(F32)) + got_ref[1].astype(F32)) + got_ref[2].astype(F32)

    return pl.pallas_call(
        body, name=name,
        grid_spec=pltpu.PrefetchScalarGridSpec(
            num_scalar_prefetch=1, grid=(R // tr,),
            in_specs=[pl.BlockSpec((None, tr, C), lambda i, me_ref: (me_ref[0], i, 0)),
                      pl.BlockSpec((3, tr, C), lambda i, me_ref: (0, i, 0))],
            out_specs=pl.BlockSpec((tr, C), lambda i, me_ref: (i, 0))),
        out_shape=jax.ShapeDtypeStruct((R, C), F32), compiler_params=_cp(),
    )(me, own, got)


def _adamw(gs, w, m, v, name):
    L, R, C = w.shape
    tr = _row_tile(R)
    ng = len(gs[0])
    c1 = 1.0 - ADAM_B1 ** ADAM_STEP
    c2 = 1.0 - ADAM_B2 ** ADAM_STEP

    def body(*refs):
        w_ref, m_ref, v_ref, g_ref, d_ref, nm_ref, nv_ref = refs[L * ng:]
        lay = pl.program_id(0)
        g = None
        for l in range(L):
            gl = refs[l * ng][...]
            for r in refs[l * ng + 1:(l + 1) * ng]:
                gl = gl + r[...]
            g = gl if g is None else jnp.where(lay == l, gl, g)
        nm = ADAM_B1 * m_ref[...] + (1.0 - ADAM_B1) * g
        nv = ADAM_B2 * v_ref[...] + (1.0 - ADAM_B2) * (g * g)
        g_ref[...] = g
        nm_ref[...] = nm
        nv_ref[...] = nv
        d_ref[...] = -ADAM_LR * ((nm / c1) / (jnp.sqrt(nv / c2) + ADAM_EPS) + ADAM_WD * w_ref[...])

    gspecs = [pl.BlockSpec((tr, C), lambda lay, i, l=l: (jnp.where(lay == l, i, 0), 0)) for l in range(L) for _ in range(ng)]
    spec = pl.BlockSpec((None, tr, C), lambda lay, i: (lay, i, 0))
    return pl.pallas_call(
        body, grid=(L, R // tr), name=name, in_specs=gspecs + [spec] * 3, out_specs=[spec] * 4,
        out_shape=[jax.ShapeDtypeStruct((L, R, C), F32)] * 4, compiler_params=_cp(2),
    )(*[g for gl in gs for g in gl], w, m, v)


def _rope_tables(T):
    inv = 1.0 / (ROPE_THETA ** (jnp.arange(0, HD, 2, dtype=F32) / HD))
    ang = jnp.arange(T, dtype=F32)[:, None] * inv[None, :]
    c, s = jnp.cos(ang), jnp.sin(ang)
    return jnp.tile(jnp.concatenate([c, c], axis=1), (1, 2)), jnp.tile(jnp.concatenate([-s, s], axis=1), (1, 2))


def _dup_cols(w):
    return jnp.broadcast_to(w.reshape(D, NKV, 1, HD), (D, NKV, 2, HD)).reshape(D, KVW)


def kernel(x, p, mix_pre_g, mix_post_g, ffn_pre_g, ffn_post_g, pool_w, pool_scale, kv_norm_g, w_k, w_v, w_q, w_o, sinks, w_ff_gate, w_ff_up, w_ff_down, ple_norm_g, w_ple_gate, w_ple_proj, loss_target, m_mix_pre_g, m_mix_post_g, m_ffn_pre_g, m_ffn_post_g, m_pool_w, m_pool_scale, m_kv_norm_g, m_w_k, m_w_v, m_w_q, m_w_o, m_sinks, m_w_ff_gate, m_w_ff_up, m_w_ff_down, m_ple_norm_g, m_w_ple_gate, m_w_ple_proj, v_mix_pre_g, v_mix_post_g, v_ffn_pre_g, v_ffn_post_g, v_pool_w, v_pool_scale, v_kv_norm_g, v_w_k, v_w_v, v_w_q, v_w_o, v_sinks, v_w_ff_gate, v_w_ff_up, v_w_ff_down, v_ple_norm_g, v_w_ple_gate, v_w_ple_proj):
    depth = mix_pre_g.shape[0]
    n_pool = pool_w.shape[0]
    n_attn = w_q.shape[0]
    T = x.shape[1]
    h = x[0]
    p = p[:, 0]
    tgt = loss_target[0]
    me = (2 * lax.axis_index("x") + lax.axis_index("y")).astype(jnp.int32).reshape(1)

    def t12(a):
        return jnp.transpose(a, (0, 2, 1))

    ff_t = {k: tuple(t12(a) for a in v) for k, v in dict(
        w_ff_gate=(w_ff_gate, m_w_ff_gate, v_w_ff_gate), w_ff_up=(w_ff_up, m_w_ff_up, v_w_ff_up)).items()}
    c_pool, c_wk, c_wv, c_wq, c_wo, c_wg, c_wu, c_wd, c_wpg, c_wpp = (w.astype(CDT) for w in (
        pool_w, w_k, w_v, w_q, w_o, ff_t["w_ff_gate"][0], ff_t["w_ff_up"][0], w_ff_down, w_ple_gate, w_ple_proj))

    def ffn_srcs(l):
        return [(c_wg, l), (c_wu, l), (c_wd, l)]

    def ple_srcs(l):
        return [(c_wpg, l), (c_wpp, l)]

    (gpw,) = _run_comm(_gather_comm([(c_pool, None)]), "gather_pool_w")
    wff, wple, wattn = {}, {}, {}
    pw = jnp.transpose(gpw, (1, 2, 0, 3, 4)).reshape(n_pool, len(POOL_WINDOWS), PG, PG)
    kv_g = kv_norm_g.reshape(1, D)
    cos, sin = _rope_tables(T)

    pscale = _all_reduce_small(jnp.pad(
        lax.dynamic_update_slice(jnp.zeros((n_pool, D), F32), pool_scale, (0, me[0] * (D // NSH))),
        ((0, 8 - n_pool), (0, 0))))[:n_pool] * 0.5

    saved = []
    kk = vv = wk2 = wv2 = None
    for l in range(depth):
        s = {"h": h}
        if l < n_pool:
            (h1,), got = _pool_fwd(h, mix_pre_g, pw, pscale, mix_post_g, l,
                                   comm=_GatherHalves(ffn_srcs(0)) if l == 0 else None)
            if l == 0:
                wff[0] = got
        else:
            j = l - n_pool
            wq, wo = wattn[l]
            s["q"] = _q_fwd(h, mix_pre_g, wq, cos, sin, l, j)
            s["o"] = _attn_fwd(s["q"], kk, vv, sinks, l, j)
            h1, s["m"] = _o_fwd(s["o"], h, wo, mix_post_g, l, j)
        s["h1"] = h1
        srcs = []
        if l + 1 < depth:
            srcs = ffn_srcs(l + 1)
            if l + 1 >= n_pool:
                srcs += [(c_wq, l + 1 - n_pool), (c_wo, l + 1 - n_pool)]
            if l + 1 == n_pool:
                srcs += [(c_wk, None), (c_wv, None)]
        if l == 0:
            srcs += ple_srcs(0)
        (h2, s["a"], s["b"], s["f"]), got = _ffn_fwd(
            h1, ffn_pre_g, *wff[l], ffn_post_g, l, comm=_gather_comm(srcs) if srcs else None)
        if l == 0:
            wple[0] = got[-2:]
        if l + 1 < depth:
            wff[l + 1] = got[:3]
            if l + 1 >= n_pool:
                wattn[l + 1] = got[3:5]
            if l + 1 == n_pool:
                wk2 = _dup_cols(got[5].reshape(D, NKV * HD))
                wv2 = _dup_cols(got[6].reshape(D, NKV * HD))
        s["h2"] = h2
        if l + 1 == depth:
            dh, loss_row = _ple_fwd_loss(h2, p, ple_norm_g, *wple[l], tgt, l)
            saved.append(s)
            break
        (h,), got = _ple_fwd(h2, p, ple_norm_g, *wple[l], l, comm=_gather_comm(ple_srcs(l + 1)))
        wple[l + 1] = got
        if l == n_pool - 1:
            s["hkv"] = h
            kk, vv = _kv_fwd(h, kv_g, wk2, wv2, cos, sin)
        saved.append(s)

    entries, pending, to_swap = [], [], []

    def add_grad(k, l, g32, g16):
        def by_shard(g):
            return g.reshape(NSH, -1, g.shape[-1])
        e = dict(k=k, l=l, own=by_shard(g32), sent=by_shard(g16))
        entries.append(e)
        pending.append(e)

    def carry():
        take, swap = list(pending), list(to_swap)
        del pending[:], to_swap[:]
        return (take, swap), _join_comms([_scatter_comm([e["sent"] for e in take]) if take else None,
                                          _swap_comm([e["part"] for e in swap]) if swap else None])

    def landed(taken, got):
        take, swap = taken
        for e, g in zip(take, got[:len(take)]):
            e["part"] = _sum_own_and_received(e["own"], g, me, f"sum_{e['k']}_{e['l']}")
            to_swap.append(e)
        for e, g in zip(swap, got[len(take):]):
            e["sib"] = g

    small = {k: [None] * depth for k in ("mix_pre", "mix_post", "ffn_pre", "ffn_post", "ple")}
    dpool_scale = [None] * n_pool
    dsinks = [None] * n_attn
    dks, dkhs, dvs, dvhs = [], [], [], []
    for l in reversed(range(depth)):
        s = saved[l]
        if l == n_pool - 1:
            dh, hk, dk0, dvc, dkv_g = _kv_bwd(dh, s["hkv"], dks, dkhs, dvs, dvhs, kv_g, wk2, wv2, cos, sin)
            add_grad("w_k", None, *_fold_dup(_mm_tn(hk, dk0, "dw_k")[0][0], "fold_w_k"))
            add_grad("w_v", None, *_fold_dup(_mm_tn(hk, dvc, "dw_v")[0][0], "fold_w_v"))
        dh, pn, dz, dpp, pc, small["ple"][l] = _ple_bwd(dh, s["h2"], p, ple_norm_g, *wple[l], l)
        add_grad("w_ple_gate", l, *_mm_tn(pn, dz, f"dw_ple_gate_{l}"))
        add_grad("w_ple_proj", l, *_mm_tn(pc, dpp, f"dw_ple_proj_{l}", n_split=NSH))
        take, comm = carry()
        (dh, fn, df, act, da, db, small["ffn_pre"][l], small["ffn_post"][l]), got = _ffn_bwd(
            dh, s["h1"], s["f"], s["a"], s["b"], ffn_pre_g, *wff[l], ffn_post_g, l, comm=comm)
        landed(take, got)
        add_grad("w_ff_gate", l, *_mm_tn(da, fn, f"dw_ff_gate_{l}"))
        take, comm = carry() if l == 0 else (([], []), None)
        res = _mm_tn(db, fn, f"dw_ff_up_{l}", comm=comm)
        if comm is not None:
            res, got = res
            landed(take, got)
        add_grad("w_ff_up", l, *res)
        take, comm = carry() if l == 0 else (([], []), None)
        res = _mm_tn(act, df, f"dw_ff_down_{l}", comm=comm)
        if comm is not None:
            res, got = res
            landed(take, got)
        add_grad("w_ff_down", l, *res)
        if l < n_pool:
            take, comm = carry() if l == 0 else (([], []), None)
            (dpool, dpw, dpool_scale[l], small["mix_post"][l]), got = _pool_bwd_a(
                dh, s["h"], mix_pre_g, pw, pscale, mix_post_g, l, comm=comm)
            landed(take, got)
            dpw = jnp.transpose(dpw.reshape(len(POOL_WINDOWS), NSH, PG // NSH, PG), (1, 0, 2, 3))
            dpw = dpw.reshape(NSH, len(POOL_WINDOWS) * PG // NSH, PG)
            add_grad("pool_w", l, dpw, dpw.astype(CDT))
            (dh, small["mix_pre"][l]), _ = _pool_bwd_b(dpool, dh, s["h"], mix_pre_g, l)
        else:
            j = l - n_pool
            wq, wo = wattn[l]
            do, dm, small["mix_post"][l] = _o_bwd(dh, s["m"], wo, mix_post_g, l, j)
            add_grad("w_o", j, *_mm_tn(s["o"], dm, f"dw_o_{l}"))
            dq, dk, dkh, dv, dvh, dsinks[j] = _attn_bwd(s["q"], kk, vv, s["o"], do, sinks, l, j)
            dks.append(dk); dkhs.append(dkh); dvs.append(dv); dvhs.append(dvh)
            dh, hn, dq0, small["mix_pre"][l] = _q_bwd(dq, dh, s["h"], mix_pre_g, wq, cos, sin, l, j)
            add_grad("w_q", j, *_mm_tn(hn, dq0, f"dw_q_{l}"))
    rest = 0
    while pending or to_swap:
        take, comm = carry()
        landed(take, _run_comm(comm, f"exchange_rest_{rest}"))
        rest += 1
    grad_x = dh[None]

    rows = [jnp.concatenate(small[k], axis=0) for k in ("mix_pre", "mix_post", "ffn_pre", "ffn_post", "ple")]
    rows += [dkv_g, jnp.concatenate(dpool_scale, axis=0)]
    rows += [jnp.pad(jnp.concatenate(dsinks, axis=0), ((0, 0), (0, D - LW))), jnp.pad(loss_row, ((0, 0), (0, D - 128)))]
    vec = jnp.concatenate(rows, axis=0)
    nrow = vec.shape[0]
    vec = _all_reduce_small(jnp.pad(vec, ((0, -nrow % 8), (0, 0))))
    o = 0
    red = {}
    for k in ("mix_pre_g", "mix_post_g", "ffn_pre_g", "ffn_post_g", "ple_norm_g"):
        red[k] = vec[o:o + depth]
        o += depth
    red["kv_norm_g"] = vec[o:o + 1]
    red["pool_scale"] = lax.dynamic_slice(vec[o + 1:o + 1 + n_pool], (0, me[0] * (D // NSH)), (n_pool, D // NSH))
    o += 1 + n_pool
    red["sinks"] = vec[o:o + n_attn, :GQ * NKV]
    loss = vec[o + n_attn, 0]


    given = dict(
        mix_pre_g=(mix_pre_g, m_mix_pre_g, v_mix_pre_g), mix_post_g=(mix_post_g, m_mix_post_g, v_mix_post_g),
        ffn_pre_g=(ffn_pre_g, m_ffn_pre_g, v_ffn_pre_g), ffn_post_g=(ffn_post_g, m_ffn_post_g, v_ffn_post_g),
        pool_w=(pool_w, m_pool_w, v_pool_w), pool_scale=(pool_scale, m_pool_scale, v_pool_scale),
        kv_norm_g=(kv_norm_g, m_kv_norm_g, v_kv_norm_g), w_k=(w_k, m_w_k, v_w_k), w_v=(w_v, m_w_v, v_w_v),
        w_q=(w_q, m_w_q, v_w_q), w_o=(w_o, m_w_o, v_w_o), sinks=(sinks, m_sinks, v_sinks),
        w_ff_gate=ff_t["w_ff_gate"], w_ff_up=ff_t["w_ff_up"],
        w_ff_down=(w_ff_down, m_w_ff_down, v_w_ff_down), ple_norm_g=(ple_norm_g, m_ple_norm_g, v_ple_norm_g),
        w_ple_gate=(w_ple_gate, m_w_ple_gate, v_w_ple_gate), w_ple_proj=(w_ple_proj, m_w_ple_proj, v_w_ple_proj))
    results = {}
    for k, g in red.items():
        w, m, v = given[k]
        outs = _adamw([[g.reshape(-1, g.shape[-1])]], *(t.reshape(1, -1, t.shape[-1]) for t in (w, m, v)), f"adamw_{k}")
        results[k] = [t.reshape(w.shape) for t in outs]
    per_layer = {}
    for e in entries:
        per_layer.setdefault(e["k"], {})[e["l"] or 0] = [e["part"], e["sib"]]
    for k, layers in per_layer.items():
        w, m, v = given[k]
        gs = [layers[l] for l in range(len(layers))]
        outs = _adamw(gs, *(t.reshape((len(gs),) + gs[0][0].shape) for t in (w, m, v)), f"adamw_{k}")
        results[k] = [t12(t) if k in ff_t else t.reshape(w.shape) for t in outs]

    order = ["mix_pre_g", "mix_post_g", "ffn_pre_g", "ffn_post_g", "pool_w", "pool_scale", "kv_norm_g", "w_k", "w_v",
             "w_q", "w_o", "sinks", "w_ff_gate", "w_ff_up", "w_ff_down", "ple_norm_g", "w_ple_gate", "w_ple_proj"]
    return (loss, grad_x, *[results[k][0] for k in order], *[results[k][1] for k in order],
            *[results[k][2] for k in order], *[results[k][3] for k in order])
```

```python
import functools

import jax
import jax.numpy as jnp
from jax import lax
from jax.experimental import pallas as pl
from jax.experimental.pallas import tpu as pltpu

F32 = jnp.float32
CDT = jnp.bfloat16

D = 1024
PG = 256
POOL_WINDOWS = (2, 4, 8, 16)
HALO = 16
HD = 64
NKV = 4
WIN = 128
FS = 704
NSH = 4
PLE = 256
ROPE_THETA = 10000.0
EPS = 1e-6
NEG = -1e30
QSCALE = HD ** -0.5

ADAM_LR, ADAM_B1, ADAM_B2, ADAM_EPS, ADAM_WD, ADAM_STEP = 0.001, 0.9, 0.999, 1e-08, 0.01, 10

VMEM_LIMIT = 56 * 1024 * 1024
TM = 1024
TM_FFN = 512
TM_FFN_BWD = 256
TQ = 512
TT = 4096
MESH = pl.DeviceIdType.MESH


def _cp(n_axes=1):
    return pltpu.CompilerParams(dimension_semantics=("arbitrary",) * n_axes, vmem_limit_bytes=VMEM_LIMIT)


def _dot(a, b):
    return jnp.dot(a, b, preferred_element_type=F32)


def _dot_nt(a, b):
    return lax.dot_general(a, b, (((1,), (1,)), ((), ())), preferred_element_type=F32)


def _dot_tn(a, b):
    return lax.dot_general(a, b, (((0,), (0,)), ((), ())), preferred_element_type=F32)


def _rms_fwd(x, g):
    r = lax.rsqrt(jnp.mean(x * x, axis=-1, keepdims=True) + EPS)
    return x * r * g


def _rms_bwd(dy, x, g):
    r = lax.rsqrt(jnp.mean(x * x, axis=-1, keepdims=True) + EPS)
    xh = x * r
    dg = jnp.sum(dy * xh, axis=0, keepdims=True)
    u = dy * g
    dx = r * (u - xh * jnp.mean(u * xh, axis=-1, keepdims=True))
    return dx, dg


def _sigmoid(a):
    return 1.0 / (1.0 + jnp.exp(-a))


def _swap_half(x):
    w = x.shape[1]
    lane = lax.broadcasted_iota(jnp.int32, x.shape, 1)
    return jnp.where((lane % HD) < HD // 2, pltpu.roll(x, w - HD // 2, axis=1), pltpu.roll(x, HD // 2, axis=1))


def _rope_fwd(x, cos, sin):
    return x * cos + _swap_half(x) * sin


def _rope_bwd(d, cos, sin):
    return d * cos + _swap_half(d * sin)


def _tile(tm, w):
    return pl.BlockSpec((tm, w), lambda i: (i, 0))


def _res(shape):
    return pl.BlockSpec(shape, lambda i: (0,) * len(shape), pipeline_mode=pl.Buffered(1))


def _layer_w(shape):
    return pl.BlockSpec((NSH,) + shape, lambda *_: (0, 0, 0), pipeline_mode=pl.Buffered(1))


def _acc(w):
    return pl.BlockSpec((1, w), lambda i: (0, 0))


def _zero_first(*refs):
    @pl.when(pl.program_id(0) == 0)
    def _():
        for r in refs:
            r[...] = jnp.zeros_like(r)


HBM = pl.BlockSpec(memory_space=pl.ANY)


def _place():
    x, y, c = lax.axis_index("x"), lax.axis_index("y"), lax.axis_index("c")
    chips = [(1 - x, y), (x, 1 - y), (1 - x, 1 - y)]
    return x, y, c, chips


class _Comm:
    def __init__(self, args, out_shape, sems, copies):
        self.args, self.out_shape, self.sems, self.copies = list(args), list(out_shape), list(sems), copies

    def start(self, cin, cout, sems):
        local, sends, _ = self.copies(cin, cout, sems)
        for cp in local + sends:
            cp.start()

    def wait(self, cin, cout, sems):
        local, sends, recvs = self.copies(cin, cout, sems)
        for cp in recvs:
            cp.wait_recv()
        for cp in sends:
            cp.wait_send()
        for cp in local:
            cp.wait()


def _gather_comm(srcs):
    n = len(srcs)
    shapes = [(a.shape if l is None else a.shape[1:]) for a, l in srcs]

    def copies(cin, cout, sems):
        send, recv, loc = sems
        x, y, c, chips = _place()
        me = 2 * x + y
        local, sends, recvs = [], [], []
        for a, (_, l) in enumerate(srcs):
            src = cin[a] if l is None else cin[a].at[l]
            local.append(pltpu.make_async_copy(src, cout[a].at[me], loc.at[a]))
            for j, (px, py) in enumerate(chips):
                kw = dict(src_ref=src, send_sem=send.at[3 * a + j], recv_sem=recv.at[3 * a + j],
                          device_id=(px, py, c), device_id_type=MESH)
                sends.append(pltpu.make_async_remote_copy(dst_ref=cout[a].at[me], **kw))
                recvs.append(pltpu.make_async_remote_copy(dst_ref=cout[a].at[2 * px + py], **kw))
        return local, sends, recvs

    return _Comm([a for a, _ in srcs], [jax.ShapeDtypeStruct((NSH,) + s, CDT) for s in shapes],
                 [pltpu.SemaphoreType.DMA((3 * n,)), pltpu.SemaphoreType.DMA((3 * n,)), pltpu.SemaphoreType.DMA((n,))],
                 copies)


class _GatherHalves(_Comm):
    def __init__(self, srcs):
        n = len(srcs)
        self.srcs = srcs
        super().__init__([a for a, _ in srcs], [jax.ShapeDtypeStruct((NSH,) + a.shape[1:], CDT) for a, _ in srcs],
                         [pltpu.SemaphoreType.DMA((3 * n,))] * 4 + [pltpu.SemaphoreType.DMA((n,))], None)

    def _copies(self, cin, cout, sems):
        isend, irecv, dsend, drecv, loc = sems
        x, y, c, chips = _place()
        me = 2 * x + y
        local, ici_out, ici_in, d2d_out, d2d_in = [], [], [], [], []
        for a, (arr, l) in enumerate(self.srcs):
            hr = arr.shape[1] // 2
            mine = pl.ds(pl.multiple_of(c * hr, hr), hr)
            other = pl.ds(pl.multiple_of((1 - c) * hr, hr), hr)
            src = cin[a].at[l]
            local.append(pltpu.make_async_copy(src, cout[a].at[me], loc.at[a]))
            for j, (px, py) in enumerate(chips):
                k, sj = 3 * a + j, 2 * px + py
                ici = dict(src_ref=src.at[mine], send_sem=isend.at[k], recv_sem=irecv.at[k],
                           device_id=(px, py, c), device_id_type=MESH)
                ici_out.append(pltpu.make_async_remote_copy(dst_ref=cout[a].at[me, mine], **ici))
                ici_in.append(pltpu.make_async_remote_copy(dst_ref=cout[a].at[sj, mine], **ici))
                d2d = dict(src_ref=cout[a].at[sj, mine], send_sem=dsend.at[k], recv_sem=drecv.at[k],
                           device_id=(x, y, 1 - c), device_id_type=MESH)
                d2d_out.append(pltpu.make_async_remote_copy(dst_ref=cout[a].at[sj, mine], **d2d))
                d2d_in.append(pltpu.make_async_remote_copy(dst_ref=cout[a].at[sj, other], **d2d))
        return local, ici_out, ici_in, d2d_out, d2d_in

    def start(self, cin, cout, sems):
        local, ici_out, _, _, _ = self._copies(cin, cout, sems)
        for cp in local + ici_out:
            cp.start()

    def wait(self, cin, cout, sems):
        local, ici_out, ici_in, d2d_out, d2d_in = self._copies(cin, cout, sems)
        for got, fwd in zip(ici_in, d2d_out):
            got.wait_recv()
            fwd.start()
        for cp in d2d_in:
            cp.wait_recv()
        for cp in d2d_out + ici_out:
            cp.wait_send()
        for cp in local:
            cp.wait()


def _scatter_comm(grads):
    n = len(grads)

    def copies(cin, cout, sems):
        send, recv = sems
        x, y, c, chips = _place()
        sends, recvs = [], []
        for a in range(n):
            for j, (px, py) in enumerate(chips):
                cp = pltpu.make_async_remote_copy(
                    src_ref=cin[a].at[2 * px + py], dst_ref=cout[a].at[j], send_sem=send.at[3 * a + j],
                    recv_sem=recv.at[3 * a + j], device_id=(px, py, c), device_id_type=MESH)
                sends.append(cp)
                recvs.append(cp)
        return [], sends, recvs

    return _Comm(grads, [jax.ShapeDtypeStruct((3,) + g.shape[1:], g.dtype) for g in grads],
                 [pltpu.SemaphoreType.DMA((3 * n,)), pltpu.SemaphoreType.DMA((3 * n,))], copies)


def _swap_comm(parts):
    n = len(parts)

    def copies(cin, cout, sems):
        send, recv = sems
        x, y, c, _ = _place()
        cps = [pltpu.make_async_remote_copy(
            src_ref=cin[a], dst_ref=cout[a], send_sem=send.at[a], recv_sem=recv.at[a],
            device_id=(x, y, 1 - c), device_id_type=MESH) for a in range(n)]
        return [], cps, cps

    return _Comm(parts, [jax.ShapeDtypeStruct(p.shape, p.dtype) for p in parts],
                 [pltpu.SemaphoreType.DMA((n,)), pltpu.SemaphoreType.DMA((n,))], copies)


def _join_comms(comms):
    comms = [c for c in comms if c is not None]
    if len(comms) <= 1:
        return comms[0] if comms else None

    def copies(cin, cout, sems):
        out = ([], [], [])
        i = o = k = 0
        for c in comms:
            part = c.copies(cin[i:i + len(c.args)], cout[o:o + len(c.out_shape)], sems[k:k + len(c.sems)])
            for acc, cps in zip(out, part):
                acc.extend(cps)
            i, o, k = i + len(c.args), o + len(c.out_shape), k + len(c.sems)
        return out

    return _Comm([a for c in comms for a in c.args], [s for c in comms for s in c.out_shape],
                 [s for c in comms for s in c.sems], copies)


def _run_comm(comm, name):
    k_in, k_out = len(comm.args), len(comm.out_shape)

    def body(*refs):
        cin, cout, sems = refs[:k_in], refs[k_in:k_in + k_out], refs[k_in + k_out:]
        comm.start(cin, cout, sems)
        comm.wait(cin, cout, sems)

    return pl.pallas_call(body, name=name, in_specs=[HBM] * k_in, out_specs=[HBM] * k_out, out_shape=comm.out_shape,
                          scratch_shapes=comm.sems)(*comm.args)


def _pallas(body, *, name, grid, in_specs, out_specs, out_shape, args, scratch=(), comm=None):
    out_specs, out_shape, scratch = list(out_specs), list(out_shape), list(scratch)
    params = _cp(len(grid))
    if comm is None:
        return pl.pallas_call(body, grid=grid, name=name, in_specs=in_specs, out_specs=out_specs, out_shape=out_shape,
                              scratch_shapes=scratch, compiler_params=params)(*args), []
    n_in, n_out, n_sc = len(args), len(out_shape), len(scratch)
    k_in, k_out = len(comm.args), len(comm.out_shape)

    def carrying(*refs):
        ins, refs = refs[:n_in], refs[n_in:]
        cin, refs = refs[:k_in], refs[k_in:]
        outs, refs = refs[:n_out], refs[n_out:]
        cout, refs = refs[:k_out], refs[k_out:]
        sc, sems = refs[:n_sc], refs[n_sc:]
        first = functools.reduce(jnp.logical_and, [pl.program_id(d) == 0 for d in range(len(grid))])
        last = functools.reduce(jnp.logical_and, [pl.program_id(d) == g - 1 for d, g in enumerate(grid)])

        @pl.when(first)
        def _():
            comm.start(cin, cout, sems)

        body(*ins, *outs, *sc)

        @pl.when(last)
        def _():
            comm.wait(cin, cout, sems)

    res = pl.pallas_call(
        carrying, grid=grid, name=name, in_specs=list(in_specs) + [HBM] * k_in, out_specs=out_specs + [HBM] * k_out,
        out_shape=out_shape + comm.out_shape, scratch_shapes=scratch + comm.sems, compiler_params=params,
    )(*args, *comm.args)
    return res[:n_out], res[n_out:]


def _pool_normed_ext(h_ref, halo_ref, g, tm):
    xe = jnp.concatenate([halo_ref[...], h_ref[...]], axis=0)
    hn = _rms_fwd(xe, g)
    row = lax.broadcasted_iota(jnp.int32, (tm + HALO, 1), 0)
    return jnp.where((row >= HALO) | (pl.program_id(0) > 0), hn, 0.0)


WBLK = 128


def _window_sums(xe, tm, w, trailing):
    blk = min(WBLK, tm)
    r = lax.broadcasted_iota(jnp.int32, (blk, blk + HALO), 0)
    j = lax.broadcasted_iota(jnp.int32, (blk, blk + HALO), 1)
    d = r + HALO - j if trailing else j - r
    band = ((d >= 1) & (d < w)).astype(CDT)
    xc = xe.astype(CDT)
    return jnp.concatenate([_dot(band, xc[b * blk:(b + 1) * blk + HALO, :]) for b in range(tm // blk)], axis=0)


def _pool_windows(hn_e, tm):
    t = lax.broadcasted_iota(jnp.int32, (tm, PG), 0) + pl.program_id(0) * tm
    outs = []
    for g, w in enumerate(POOL_WINDOWS):
        xe = hn_e[:, g * PG:(g + 1) * PG]
        cnt = jnp.minimum(t + 1, w).astype(F32)
        x = xe[HALO:, :]
        outs.append((_window_sums(xe, tm, w, True) + x) / cnt - x)
    return jnp.concatenate(outs, axis=1)


def _pool_apply(pooled, w_ref):
    return jnp.concatenate([_dot(pooled[:, g * PG:(g + 1) * PG], w_ref[g]) for g in range(len(POOL_WINDOWS))], axis=1)


def _halo_prev(tm):
    return pl.BlockSpec((HALO, D), lambda i: (jnp.maximum(i * (tm // HALO) - 1, 0), 0))


def _pool_fwd(h, g_pre, pw, pscale, g_post, l, comm=None):
    T = h.shape[0]
    tm = min(TM, T)

    def body(h_ref, halo_ref, gpre_ref, w_ref, sc_ref, gpost_ref, o_ref):
        hn_e = _pool_normed_ext(h_ref, halo_ref, gpre_ref[l:l + 1, :], tm)
        pooled = _pool_windows(hn_e, tm).astype(CDT)
        m = _pool_apply(pooled, w_ref) * sc_ref[l:l + 1, :]
        o_ref[...] = h_ref[...] + _rms_fwd(m, gpost_ref[l:l + 1, :])

    return _pallas(
        body, grid=(T // tm,), name=f"pool_fwd_{l}",
        in_specs=[_tile(tm, D), _halo_prev(tm), _res(g_pre.shape),
                  pl.BlockSpec((None,) + pw.shape[1:], lambda i: (l, 0, 0, 0), pipeline_mode=pl.Buffered(1)),
                  _res(pscale.shape), _res(g_post.shape)],
        out_specs=[_tile(tm, D)], out_shape=[jax.ShapeDtypeStruct((T, D), F32)],
        args=(h, h, g_pre, pw, pscale, g_post), comm=comm)


def _pool_bwd_a(dh1, h, g_pre, pw, pscale, g_post, l, comm=None):
    T = h.shape[0]
    tm = min(TM, T)
    ng = len(POOL_WINDOWS)

    def body(dh_ref, h_ref, halo_ref, gpre_ref, w_ref, sc_ref, gpost_ref, dp_ref, dw_ref, dsc_ref, dgpost_ref):
        _zero_first(dw_ref, dsc_ref, dgpost_ref)
        hn_e = _pool_normed_ext(h_ref, halo_ref, gpre_ref[l:l + 1, :], tm)
        pooled = _pool_windows(hn_e, tm).astype(CDT)
        y = _pool_apply(pooled, w_ref)
        sc = sc_ref[l:l + 1, :]
        dm, dg = _rms_bwd(dh_ref[...], y * sc, gpost_ref[l:l + 1, :])
        dgpost_ref[...] += dg
        dsc_ref[...] += jnp.sum(dm * y, axis=0, keepdims=True)
        dy = (dm * sc).astype(CDT)
        dps = []
        for g in range(ng):
            dyg = dy[:, g * PG:(g + 1) * PG]
            dw_ref[g] += _dot_tn(pooled[:, g * PG:(g + 1) * PG], dyg)
            dps.append(_dot_nt(dyg, w_ref[g]))
        dp_ref[...] = jnp.concatenate(dps, axis=1)

    return _pallas(
        body, grid=(T // tm,), name=f"pool_bwd_a_{l}",
        in_specs=[_tile(tm, D), _tile(tm, D), _halo_prev(tm), _res(g_pre.shape),
                  pl.BlockSpec((None,) + pw.shape[1:], lambda i: (l, 0, 0, 0), pipeline_mode=pl.Buffered(1)),
                  _res(pscale.shape), _res(g_post.shape)],
        out_specs=[_tile(tm, D), pl.BlockSpec((ng, PG, PG), lambda i: (0, 0, 0)), _acc(D), _acc(D)],
        out_shape=[jax.ShapeDtypeStruct((T, D), F32), jax.ShapeDtypeStruct((ng, PG, PG), F32),
                   jax.ShapeDtypeStruct((1, D), F32), jax.ShapeDtypeStruct((1, D), F32)],
        args=(dh1, h, h, g_pre, pw, pscale, g_post), comm=comm)


def _pool_bwd_b(dpool, dh1, h, g_pre, l, comm=None):
    T = h.shape[0]
    tm = min(TM, T)
    nt = T // tm

    def body(dp_ref, nxt_ref, dh_ref, h_ref, gpre_ref, o_ref, dgpre_ref):
        _zero_first(dgpre_ref)
        i = pl.program_id(0)
        dp = dp_ref[...]
        e_e = jnp.concatenate([dp, jnp.where(i < nt - 1, nxt_ref[...], 0.0)], axis=0)
        t = lax.broadcasted_iota(jnp.int32, (tm + HALO, PG), 0) + i * tm
        outs = []
        for g, w in enumerate(POOL_WINDOWS):
            e = e_e[:, g * PG:(g + 1) * PG] / jnp.minimum(t + 1, w).astype(F32)
            outs.append(_window_sums(e, tm, w, False) + (e[:tm, :] - dp[:, g * PG:(g + 1) * PG]))
        dx, dg = _rms_bwd(jnp.concatenate(outs, axis=1), h_ref[...], gpre_ref[l:l + 1, :])
        dgpre_ref[...] += dg
        o_ref[...] = dh_ref[...] + dx

    nxt = pl.BlockSpec((HALO, D), lambda i: (jnp.minimum((i + 1) * (tm // HALO), T // HALO - 1), 0))
    return _pallas(
        body, grid=(nt,), name=f"pool_bwd_b_{l}",
        in_specs=[_tile(tm, D), nxt, _tile(tm, D), _tile(tm, D), _res(g_pre.shape)],
        out_specs=[_tile(tm, D), _acc(D)],
        out_shape=[jax.ShapeDtypeStruct((T, D), F32), jax.ShapeDtypeStruct((1, D), F32)],
        args=(dpool, dpool, dh1, h, g_pre), comm=comm)


def _proj_rows(x, w_ref):
    k = w_ref.shape[1]
    out = _dot(x[:, :k], w_ref[0])
    for s in range(1, NSH):
        out = out + _dot(x[:, s * k:(s + 1) * k], w_ref[s])
    return out


def _proj_rows_t(dy, w_ref):
    return jnp.concatenate([_dot_nt(dy, w_ref[s]) for s in range(NSH)], axis=1)


def _rope_tiles(cos_ref, sin_ref, width):
    reps = width // cos_ref.shape[1]
    return jnp.tile(cos_ref[...], (1, reps)), jnp.tile(sin_ref[...], (1, reps))


def _q_fwd(h, g_pre, wq, cos, sin, l, j):
    T = h.shape[0]
    tm = min(TM, T)

    def body(h_ref, g_ref, w_ref, cos_ref, sin_ref, q_ref):
        hn = _rms_fwd(h_ref[...], g_ref[l:l + 1, :]).astype(CDT)
        c, s = _rope_tiles(cos_ref, sin_ref, D)
        q_ref[...] = (_rope_fwd(_proj_rows(hn, w_ref), c, s) * QSCALE).astype(CDT)

    return pl.pallas_call(
        body, grid=(T // tm,), name=f"q_fwd_{l}",
        in_specs=[_tile(tm, D), _res(g_pre.shape), _layer_w((D // NSH, D)), _tile(tm, 2 * HD), _tile(tm, 2 * HD)],
        out_specs=_tile(tm, D), out_shape=jax.ShapeDtypeStruct((T, D), CDT), compiler_params=_cp(),
    )(h, g_pre, wq, cos, sin)


def _q_bwd(dq, dh1, h, g_pre, wq, cos, sin, l, j):
    T = h.shape[0]
    tm = min(TM, T)

    def body(dq_ref, dh_ref, h_ref, g_ref, w_ref, cos_ref, sin_ref, o_ref, hn_ref, dq0_ref, dg_ref):
        _zero_first(dg_ref)
        g = g_ref[l:l + 1, :]
        x = h_ref[...]
        hn_ref[...] = _rms_fwd(x, g).astype(CDT)
        c, s = _rope_tiles(cos_ref, sin_ref, D)
        dq0 = _rope_bwd(dq_ref[...].astype(F32) * QSCALE, c, s).astype(CDT)
        dq0_ref[...] = dq0
        dx, dg = _rms_bwd(_proj_rows_t(dq0, w_ref), x, g)
        dg_ref[...] += dg
        o_ref[...] = dh_ref[...] + dx

    return pl.pallas_call(
        body, grid=(T // tm,), name=f"q_bwd_{l}",
        in_specs=[_tile(tm, D), _tile(tm, D), _tile(tm, D), _res(g_pre.shape), _layer_w((D // NSH, D)),
                  _tile(tm, 2 * HD), _tile(tm, 2 * HD)],
        out_specs=[_tile(tm, D), _tile(tm, D), _tile(tm, D), _acc(D)],
        out_shape=[jax.ShapeDtypeStruct((T, D), F32), jax.ShapeDtypeStruct((T, D), CDT),
                   jax.ShapeDtypeStruct((T, D), CDT), jax.ShapeDtypeStruct((1, D), F32)],
        compiler_params=_cp(),
    )(dq, dh1, h, g_pre, wq, cos, sin)


def _o_fwd(o, h, wo, g_post, l, j):
    T = h.shape[0]
    tm = min(TM, T)

    def body(o_ref, h_ref, w_ref, g_ref, h1_ref, m_ref):
        m = _proj_rows(o_ref[...], w_ref)
        m_ref[...] = m.astype(CDT)
        h1_ref[...] = h_ref[...] + _rms_fwd(m, g_ref[l:l + 1, :])

    return pl.pallas_call(
        body, grid=(T // tm,), name=f"o_fwd_{l}",
        in_specs=[_tile(tm, D), _tile(tm, D), _layer_w((D // NSH, D)), _res(g_post.shape)],
        out_specs=[_tile(tm, D), _tile(tm, D)],
        out_shape=[jax.ShapeDtypeStruct((T, D), F32), jax.ShapeDtypeStruct((T, D), CDT)], compiler_params=_cp(),
    )(o, h, wo, g_post)


def _o_bwd(dh1, m, wo, g_post, l, j):
    T = m.shape[0]
    tm = min(TM, T)

    def body(dh_ref, m_ref, w_ref, g_ref, do_ref, dm_ref, dg_ref):
        _zero_first(dg_ref)
        dm, dg = _rms_bwd(dh_ref[...], m_ref[...].astype(F32), g_ref[l:l + 1, :])
        dg_ref[...] += dg
        dmc = dm.astype(CDT)
        dm_ref[...] = dmc
        do_ref[...] = _proj_rows_t(dmc, w_ref).astype(CDT)

    return pl.pallas_call(
        body, grid=(T // tm,), name=f"o_bwd_{l}",
        in_specs=[_tile(tm, D), _tile(tm, D), _layer_w((D // NSH, D)), _res(g_post.shape)],
        out_specs=[_tile(tm, D), _tile(tm, D), _acc(D)],
        out_shape=[jax.ShapeDtypeStruct((T, D), CDT), jax.ShapeDtypeStruct((T, D), CDT),
                   jax.ShapeDtypeStruct((1, D), F32)],
        compiler_params=_cp(),
    )(dh1, m, wo, g_post)


KVW = 2 * NKV * HD


def _kv_fwd(h, g_kv, wk2, wv2, cos, sin):
    T = h.shape[0]
    tm = min(TM, T)

    def body(h_ref, g_ref, wk_ref, wv_ref, cos_ref, sin_ref, k_ref, v_ref):
        hk = _rms_fwd(h_ref[...], g_ref[...]).astype(CDT)
        c, s = _rope_tiles(cos_ref, sin_ref, KVW)
        k_ref[...] = _rope_fwd(_dot(hk, wk_ref[...]), c, s).astype(CDT)
        v_ref[...] = _dot(hk, wv_ref[...]).astype(CDT)

    return pl.pallas_call(
        body, grid=(T // tm,), name="kv_fwd",
        in_specs=[_tile(tm, D), _res(g_kv.shape), _res(wk2.shape), _res(wv2.shape), _tile(tm, 2 * HD), _tile(tm, 2 * HD)],
        out_specs=[_tile(tm, KVW), _tile(tm, KVW)],
        out_shape=[jax.ShapeDtypeStruct((T, KVW), CDT)] * 2, compiler_params=_cp(),
    )(h, g_kv, wk2, wv2, cos, sin)


def _kv_bwd(dh, h, dks, dkhs, dvs, dvhs, g_kv, wk2, wv2, cos, sin):
    T = h.shape[0]
    tq = min(TQ, T)
    nt = T // tq
    n = len(dks)

    def body(*refs):
        dh_ref, h_ref = refs[:2]
        main = refs[2:2 + 2 * n]
        halo = refs[2 + 2 * n:2 + 4 * n]
        g_ref, wk_ref, wv_ref, cos_ref, sin_ref, o_ref, hk_ref, dk0_ref, dv_ref, dg_ref = refs[2 + 4 * n:]
        _zero_first(dg_ref)
        i = pl.program_id(0)

        def total(mains, halos):
            d = mains[0][...]
            for r in mains[1:]:
                d = d + r[...]
            hl = halos[0][...]
            for r in halos[1:]:
                hl = hl + r[...]
            row = lax.broadcasted_iota(jnp.int32, (tq, 1), 0)
            if tq > WIN:
                hl = jnp.concatenate([jnp.zeros((tq - WIN, KVW), F32), hl], axis=0)
            return d + jnp.where((row >= tq - WIN) & (i < nt - 1), hl, 0.0)

        dk = total(main[:n], halo[:n])
        dv = total(main[n:], halo[n:])
        x = h_ref[...]
        g = g_ref[...]
        hk_ref[...] = _rms_fwd(x, g).astype(CDT)
        c, s = _rope_tiles(cos_ref, sin_ref, KVW)
        dk0 = _rope_bwd(dk, c, s).astype(CDT)
        dvc = dv.astype(CDT)
        dk0_ref[...] = dk0
        dv_ref[...] = dvc
        dx, dg = _rms_bwd(_dot_nt(dk0, wk_ref[...]) + _dot_nt(dvc, wv_ref[...]), x, g)
        dg_ref[...] += dg
        o_ref[...] = dh_ref[...] + dx

    nxt = pl.BlockSpec((WIN, KVW), lambda i: (jnp.minimum(i + 1, nt - 1), 0))
    return pl.pallas_call(
        body, grid=(nt,), name="kv_bwd",
        in_specs=[_tile(tq, D), _tile(tq, D)] + [_tile(tq, KVW)] * (2 * n) + [nxt] * (2 * n)
        + [_res(g_kv.shape), _res(wk2.shape), _res(wv2.shape), _tile(tq, 2 * HD), _tile(tq, 2 * HD)],
        out_specs=[_tile(tq, D), _tile(tq, D), _tile(tq, KVW), _tile(tq, KVW), _acc(D)],
        out_shape=[jax.ShapeDtypeStruct((T, D), F32), jax.ShapeDtypeStruct((T, D), CDT),
                   jax.ShapeDtypeStruct((T, KVW), CDT), jax.ShapeDtypeStruct((T, KVW), CDT),
                   jax.ShapeDtypeStruct((1, D), F32)],
        compiler_params=_cp(),
    )(dh, h, *dks, *dvs, *dkhs, *dvhs, g_kv, wk2, wv2, cos, sin)


GQ = 4
LW = 2 * HD


def _from_prev():
    r = lax.broadcasted_iota(jnp.int32, (GQ * WIN, WIN), 0) % WIN
    j = lax.broadcasted_iota(jnp.int32, (GQ * WIN, WIN), 1)
    return j > r


def _fold(x2, prev):
    return jnp.where(prev, x2[:, :WIN], x2[:, WIN:])


def _unfold(x, prev):
    zero = jnp.zeros_like(x)
    return jnp.concatenate([jnp.where(prev, x, zero), jnp.where(prev, zero, x)], axis=1)


def _stack_heads(ref, rows, g):
    lane = lax.broadcasted_iota(jnp.int32, (WIN, LW), 1)
    parts = []
    for pr in range(2):
        x = ref[rows, (2 * g + pr) * LW:(2 * g + pr + 1) * LW]
        parts += [jnp.where(lane < HD, x, jnp.zeros_like(x)), jnp.where(lane >= HD, x, jnp.zeros_like(x))]
    return jnp.concatenate(parts, axis=0)


def _unstack_heads(x4):
    lane = lax.broadcasted_iota(jnp.int32, (WIN, LW), 1)
    return [jnp.where(lane < HD, x4[(2 * pr) * WIN:(2 * pr + 1) * WIN], x4[(2 * pr + 1) * WIN:(2 * pr + 2) * WIN])
            for pr in range(2)]


def _sink_col(sink_ref, j, g):
    blk = lax.broadcasted_iota(jnp.int32, (GQ * WIN, 1), 0) // WIN
    col = jnp.zeros((GQ * WIN, 1), F32)
    for a in range(GQ):
        col = jnp.where(blk == a, sink_ref[j, GQ * g + a], col)
    return col


def _softmax_sink(s, hidden, sink):
    if hidden is not None:
        s = jnp.where(hidden, NEG, s)
    m = jnp.maximum(jnp.max(s, axis=-1, keepdims=True), sink)
    e = jnp.exp(s - m)
    es = jnp.exp(sink - m)
    l = jnp.sum(e, axis=-1, keepdims=True) + es
    return e / l, es / l


def _kv_halo(tq):
    return pl.BlockSpec((WIN, KVW), lambda i: (jnp.maximum(i * (tq // WIN) - 1, 0), 0))


def _attn_fwd(q, kk, vv, sinks, l, j):
    T = q.shape[0]
    tq = min(TQ, T)

    def body(sink_ref, q_ref, k_ref, kh_ref, v_ref, vh_ref, o_ref):
        i = pl.program_id(0)
        ke = jnp.concatenate([kh_ref[...], k_ref[...]], axis=0)
        ve = jnp.concatenate([vh_ref[...], v_ref[...]], axis=0)
        prev = _from_prev()
        for n in range(tq // WIN):
            rows = slice(n * WIN, (n + 1) * WIN)
            hidden = prev & (i == 0) if n == 0 else None
            for g in range(NKV):
                kg = ke[n * WIN:(n + 2) * WIN, g * LW:(g + 1) * LW]
                vg = ve[n * WIN:(n + 2) * WIN, g * LW:(g + 1) * LW]
                s = _fold(_dot_nt(_stack_heads(q_ref, rows, g), kg), prev)
                p, _ = _softmax_sink(s, hidden, _sink_col(sink_ref, j, g))
                pairs = _unstack_heads(_dot(_unfold(p.astype(CDT), prev), vg))
                for pr in range(2):
                    o_ref[rows, (2 * g + pr) * LW:(2 * g + pr + 1) * LW] = pairs[pr].astype(CDT)

    return pl.pallas_call(
        body, grid=(T // tq,), name=f"attn_fwd_{l}",
        in_specs=[pl.BlockSpec(memory_space=pltpu.SMEM), _tile(tq, D), _tile(tq, KVW), _kv_halo(tq),
                  _tile(tq, KVW), _kv_halo(tq)],
        out_specs=_tile(tq, D), out_shape=jax.ShapeDtypeStruct((T, D), CDT), compiler_params=_cp(),
    )(sinks, q, kk, kk, vv, vv)


def _attn_bwd(q, kk, vv, o, do, sinks, l, j):
    T = q.shape[0]
    tq = min(TQ, T)
    nt = T // tq

    def body(sink_ref, q_ref, k_ref, kh_ref, v_ref, vh_ref, o_ref, do_ref,
             dq_ref, dk_ref, dkh_ref, dv_ref, dvh_ref, dsink_ref, dke, dve):
        _zero_first(dsink_ref)
        i = pl.program_id(0)
        ke = jnp.concatenate([kh_ref[...], k_ref[...]], axis=0)
        ve = jnp.concatenate([vh_ref[...], v_ref[...]], axis=0)
        dke[...] = jnp.zeros_like(dke)
        dve[...] = jnp.zeros_like(dve)
        lane = lax.broadcasted_iota(jnp.int32, (1, LW), 1)
        blk = lax.broadcasted_iota(jnp.int32, (GQ * WIN, 1), 0) // WIN
        dsink = jnp.zeros((1, LW), F32)
        prev = _from_prev()
        for n in range(tq // WIN):
            rows = slice(n * WIN, (n + 1) * WIN)
            krows = slice(n * WIN, (n + 2) * WIN)
            hidden = prev & (i == 0) if n == 0 else None
            for g in range(NKV):
                cols = slice(g * LW, (g + 1) * LW)
                kg = ke[krows, cols]
                vg = ve[krows, cols]
                q4 = _stack_heads(q_ref, rows, g)
                do4 = _stack_heads(do_ref, rows, g)
                o4 = _stack_heads(o_ref, rows, g)
                p, ps = _softmax_sink(_fold(_dot_nt(q4, kg), prev), hidden, _sink_col(sink_ref, j, g))
                delta = jnp.sum(do4.astype(F32) * o4.astype(F32), axis=-1, keepdims=True)
                ds = _unfold((p * (_fold(_dot_nt(do4, vg), prev) - delta)).astype(CDT), prev)
                pc = _unfold(p.astype(CDT), prev)
                pairs = _unstack_heads(_dot(ds, kg))
                for pr in range(2):
                    dq_ref[rows, (2 * g + pr) * LW:(2 * g + pr + 1) * LW] = pairs[pr].astype(CDT)
                dke[krows, cols] += _dot_tn(ds, q4)
                dve[krows, cols] += _dot_tn(pc, do4)
                t = ps * delta
                for a in range(GQ):
                    dsink = dsink - jnp.where(lane == GQ * g + a, jnp.sum(jnp.where(blk == a, t, 0.0)), 0.0)
        dsink_ref[...] += dsink
        dkh_ref[...] = dke[:WIN, :]
        dk_ref[...] = dke[WIN:, :]
        dvh_ref[...] = dve[:WIN, :]
        dv_ref[...] = dve[WIN:, :]

    halo_out = pl.BlockSpec((WIN, KVW), lambda i: (i, 0))
    return pl.pallas_call(
        body, grid=(nt,), name=f"attn_bwd_{l}",
        in_specs=[pl.BlockSpec(memory_space=pltpu.SMEM), _tile(tq, D), _tile(tq, KVW), _kv_halo(tq),
                  _tile(tq, KVW), _kv_halo(tq), _tile(tq, D), _tile(tq, D)],
        out_specs=[_tile(tq, D), _tile(tq, KVW), halo_out, _tile(tq, KVW), halo_out, _acc(LW)],
        out_shape=[jax.ShapeDtypeStruct((T, D), CDT), jax.ShapeDtypeStruct((T, KVW), F32),
                   jax.ShapeDtypeStruct((nt * WIN, KVW), F32), jax.ShapeDtypeStruct((T, KVW), F32),
                   jax.ShapeDtypeStruct((nt * WIN, KVW), F32), jax.ShapeDtypeStruct((1, LW), F32)],
        scratch_shapes=[pltpu.VMEM((WIN + tq, KVW), F32), pltpu.VMEM((WIN + tq, KVW), F32)],
        compiler_params=_cp(),
    )(sinks, q, kk, kk, vv, vv, o, do)


def _chunks(tm):
    return pl.BlockSpec((NSH, tm, FS), lambda i: (0, i, 0))


def _ffn_fwd(h1, g_pre, wg, wu, wd, g_post, l, comm=None):
    T = h1.shape[0]
    tm = min(TM_FFN, T)

    def body(h_ref, gpre_ref, wg_ref, wu_ref, wd_ref, gpost_ref, h2_ref, a_ref, b_ref, f_ref):
        x = h_ref[...]
        fn = _rms_fwd(x, gpre_ref[l:l + 1, :]).astype(CDT)
        f = jnp.zeros((tm, D), F32)
        for s in range(NSH):
            a = _dot(fn, wg_ref[s])
            b = _dot(fn, wu_ref[s])
            a_ref[s] = a.astype(CDT)
            b_ref[s] = b.astype(CDT)
            f = f + _dot((a * _sigmoid(a) * b).astype(CDT), wd_ref[s])
        f_ref[...] = f.astype(CDT)
        h2_ref[...] = x + _rms_fwd(f, gpost_ref[l:l + 1, :])

    return _pallas(
        body, grid=(T // tm,), name=f"ffn_fwd_{l}",
        in_specs=[_tile(tm, D), _res(g_pre.shape), _layer_w((D, FS)), _layer_w((D, FS)), _layer_w((FS, D)),
                  _res(g_post.shape)],
        out_specs=[_tile(tm, D), _chunks(tm), _chunks(tm), _tile(tm, D)],
        out_shape=[jax.ShapeDtypeStruct((T, D), F32), jax.ShapeDtypeStruct((NSH, T, FS), CDT),
                   jax.ShapeDtypeStruct((NSH, T, FS), CDT), jax.ShapeDtypeStruct((T, D), CDT)],
        args=(h1, g_pre, wg, wu, wd, g_post), comm=comm)


def _ffn_bwd(dh2, h1, f, a, b, g_pre, wg, wu, wd, g_post, l, comm=None):
    T = h1.shape[0]
    tm = min(TM_FFN_BWD, T)

    def body(dh_ref, h_ref, f_ref, a_ref, b_ref, gpre_ref, wg_ref, wu_ref, wd_ref, gpost_ref,
             dh1_ref, fn_ref, df_ref, act_ref, da_ref, db_ref, dgpre_ref, dgpost_ref):
        _zero_first(dgpre_ref, dgpost_ref)
        x = h_ref[...]
        gpre = gpre_ref[l:l + 1, :]
        fn_ref[...] = _rms_fwd(x, gpre).astype(CDT)
        dh2 = dh_ref[...]
        df, dg = _rms_bwd(dh2, f_ref[...].astype(F32), gpost_ref[l:l + 1, :])
        dgpost_ref[...] += dg
        dfc = df.astype(CDT)
        df_ref[...] = dfc
        dfn = jnp.zeros((tm, D), F32)
        for s in range(NSH):
            av = a_ref[s].astype(F32)
            bv = b_ref[s].astype(F32)
            sg = _sigmoid(av)
            silu = av * sg
            act_ref[s] = (silu * bv).astype(CDT)
            dact = _dot_nt(dfc, wd_ref[s])
            da = (dact * bv * (sg * (1.0 + av * (1.0 - sg)))).astype(CDT)
            db = (dact * silu).astype(CDT)
            da_ref[s] = da
            db_ref[s] = db
            dfn = dfn + _dot_nt(da, wg_ref[s]) + _dot_nt(db, wu_ref[s])
        dx, dg = _rms_bwd(dfn, x, gpre)
        dgpre_ref[...] += dg
        dh1_ref[...] = dh2 + dx

    return _pallas(
        body, grid=(T // tm,), name=f"ffn_bwd_{l}",
        in_specs=[_tile(tm, D), _tile(tm, D), _tile(tm, D), _chunks(tm), _chunks(tm), _res(g_pre.shape),
                  _layer_w((D, FS)), _layer_w((D, FS)), _layer_w((FS, D)), _res(g_post.shape)],
        out_specs=[_tile(tm, D), _tile(tm, D), _tile(tm, D), _chunks(tm), _chunks(tm), _chunks(tm), _acc(D), _acc(D)],
        out_shape=[jax.ShapeDtypeStruct((T, D), F32), jax.ShapeDtypeStruct((T, D), CDT), jax.ShapeDtypeStruct((T, D), CDT),
                   jax.ShapeDtypeStruct((NSH, T, FS), CDT), jax.ShapeDtypeStruct((NSH, T, FS), CDT),
                   jax.ShapeDtypeStruct((NSH, T, FS), CDT), jax.ShapeDtypeStruct((1, D), F32),
                   jax.ShapeDtypeStruct((1, D), F32)],
        args=(dh2, h1, f, a, b, g_pre, wg, wu, wd, g_post), comm=comm)


def _ple_proj(p, wpp_ref):
    return jnp.concatenate([_dot(p, wpp_ref[s]) for s in range(NSH)], axis=1)


def _ple_fwd(h2, p, g_ple, wpg, wpp, l, comm=None):
    T = h2.shape[0]
    tm = min(TM, T)

    def body(h_ref, p_ref, g_ref, wpg_ref, wpp_ref, o_ref):
        x = h_ref[...]
        pn = _rms_fwd(x, g_ref[l:l + 1, :]).astype(CDT)
        gate = _sigmoid(_proj_rows(pn, wpg_ref))
        o_ref[...] = x + _ple_proj(p_ref[...].astype(CDT), wpp_ref) * gate

    return _pallas(
        body, grid=(T // tm,), name=f"ple_fwd_{l}",
        in_specs=[_tile(tm, D), pl.BlockSpec((None, tm, PLE), lambda i: (l, i, 0)), _res(g_ple.shape),
                  _layer_w((D // NSH, D)), _layer_w((PLE, D // NSH))],
        out_specs=[_tile(tm, D)], out_shape=[jax.ShapeDtypeStruct((T, D), F32)],
        args=(h2, p, g_ple, wpg, wpp), comm=comm)


def _ple_fwd_loss(h2, p, g_ple, wpg, wpp, tgt, l):
    T = h2.shape[0]
    tm = min(TM, T)

    def body(h_ref, p_ref, g_ref, wpg_ref, wpp_ref, t_ref, dy_ref, loss_ref):
        _zero_first(loss_ref)
        x = h_ref[...]
        pn = _rms_fwd(x, g_ref[l:l + 1, :]).astype(CDT)
        gate = _sigmoid(_proj_rows(pn, wpg_ref))
        err = (x + _ple_proj(p_ref[...].astype(CDT), wpp_ref) * gate) - t_ref[...]
        dy_ref[...] = err * (1.0 / D)
        lane = lax.broadcasted_iota(jnp.int32, (1, 128), 1)
        loss_ref[...] += jnp.where(lane == 0, (0.5 / D) * jnp.sum(err * err), 0.0)

    return pl.pallas_call(
        body, grid=(T // tm,), name=f"ple_fwd_loss_{l}",
        in_specs=[_tile(tm, D), pl.BlockSpec((None, tm, PLE), lambda i: (l, i, 0)), _res(g_ple.shape),
                  _layer_w((D // NSH, D)), _layer_w((PLE, D // NSH)), _tile(tm, D)],
        out_specs=[_tile(tm, D), _acc(128)],
        out_shape=[jax.ShapeDtypeStruct((T, D), F32), jax.ShapeDtypeStruct((1, 128), F32)], compiler_params=_cp(),
    )(h2, p, g_ple, wpg, wpp, tgt)


def _ple_bwd(dh3, h2, p, g_ple, wpg, wpp, l):
    T = h2.shape[0]
    tm = min(TM, T)

    def body(dh_ref, h_ref, p_ref, g_ref, wpg_ref, wpp_ref, o_ref, pn_ref, dz_ref, dpp_ref, pc_ref, dg_ref):
        _zero_first(dg_ref)
        x = h_ref[...]
        g = g_ref[l:l + 1, :]
        pn = _rms_fwd(x, g).astype(CDT)
        pn_ref[...] = pn
        gate = _sigmoid(_proj_rows(pn, wpg_ref))
        pc = p_ref[...].astype(CDT)
        pc_ref[...] = pc
        pp = _ple_proj(pc, wpp_ref)
        dh3 = dh_ref[...]
        dpp_ref[...] = (dh3 * gate).astype(CDT)
        dz = (dh3 * pp * gate * (1.0 - gate)).astype(CDT)
        dz_ref[...] = dz
        dx, dg = _rms_bwd(_proj_rows_t(dz, wpg_ref), x, g)
        dg_ref[...] += dg
        o_ref[...] = dh3 + dx

    return pl.pallas_call(
        body, grid=(T // tm,), name=f"ple_bwd_{l}",
        in_specs=[_tile(tm, D), _tile(tm, D), pl.BlockSpec((None, tm, PLE), lambda i: (l, i, 0)), _res(g_ple.shape),
                  _layer_w((D // NSH, D)), _layer_w((PLE, D // NSH))],
        out_specs=[_tile(tm, D), _tile(tm, D), _tile(tm, D), _tile(tm, D), _tile(tm, PLE), _acc(D)],
        out_shape=[jax.ShapeDtypeStruct((T, D), F32), jax.ShapeDtypeStruct((T, D), CDT), jax.ShapeDtypeStruct((T, D), CDT),
                   jax.ShapeDtypeStruct((T, D), CDT), jax.ShapeDtypeStruct((T, PLE), CDT), jax.ShapeDtypeStruct((1, D), F32)],
        compiler_params=_cp(),
    )(dh3, h2, p, g_ple, wpg, wpp)


def _loss_grad(y, tgt):
    T = y.shape[0]
    tm = min(TM, T)

    def body(y_ref, t_ref, dy_ref, loss_ref):
        _zero_first(loss_ref)
        err = y_ref[...] - t_ref[...]
        dy_ref[...] = err * (1.0 / D)
        lane = lax.broadcasted_iota(jnp.int32, (1, 128), 1)
        loss_ref[...] += jnp.where(lane == 0, (0.5 / D) * jnp.sum(err * err), 0.0)

    return pl.pallas_call(
        body, grid=(T // tm,), name="loss_grad", in_specs=[_tile(tm, D), _tile(tm, D)],
        out_specs=[_tile(tm, D), _acc(128)],
        out_shape=[jax.ShapeDtypeStruct((T, D), F32), jax.ShapeDtypeStruct((1, 128), F32)], compiler_params=_cp(),
    )(y, tgt)


def _mm_tn(x, dy, name, n_split=1, comm=None):
    xb, yb = x.ndim == 3, dy.ndim == 3
    T, K = x.shape[-2:]
    N = dy.shape[-1] // n_split
    B = x.shape[0] if xb else dy.shape[0] if yb else n_split
    tt = min(TT, T)
    nt = T // tt

    def body(x_ref, dy_ref, o_ref, oc_ref):
        t = pl.program_id(1)

        @pl.when(t == 0)
        def _():
            o_ref[...] = jnp.zeros_like(o_ref)

        o_ref[...] += _dot_tn(x_ref[...].astype(CDT), dy_ref[...])

        @pl.when(t == nt - 1)
        def _():
            oc_ref[...] = o_ref[...].astype(CDT)

    x_spec = pl.BlockSpec((None, tt, K), lambda b, t: (b, t, 0)) if xb else pl.BlockSpec((tt, K), lambda b, t: (t, 0))
    if yb:
        y_spec = pl.BlockSpec((None, tt, N), lambda b, t: (b, t, 0))
    else:
        y_spec = pl.BlockSpec((tt, N), lambda b, t: (t, b if n_split > 1 else 0))
    o_spec = pl.BlockSpec((None, K, N), lambda b, t: (b, 0, 0))
    res, got = _pallas(
        body, grid=(B, nt), name=name, in_specs=[x_spec, y_spec], out_specs=[o_spec, o_spec],
        out_shape=[jax.ShapeDtypeStruct((B, K, N), F32), jax.ShapeDtypeStruct((B, K, N), CDT)], args=(x, dy), comm=comm)
    return res if comm is None else (res, got)


def _fold_dup(dw2, name):
    def body(x_ref, o_ref, oc_ref):
        x = x_ref[...]
        y = jnp.concatenate([x[:, g * LW:g * LW + HD] + x[:, g * LW + HD:(g + 1) * LW] for g in range(NKV)], axis=1)
        o_ref[...] = y
        oc_ref[...] = y.astype(CDT)

    return pl.pallas_call(
        body, grid=(NSH,), name=name, in_specs=[_tile(D // NSH, KVW)], out_specs=[_tile(D // NSH, NKV * HD)] * 2,
        out_shape=[jax.ShapeDtypeStruct((D, NKV * HD), F32), jax.ShapeDtypeStruct((D, NKV * HD), CDT)],
        compiler_params=_cp(),
    )(dw2)


def _all_reduce_small(vec):
    R = vec.shape[0]
    ndev = 8

    def body(v_ref, o_ref, buf, send, recv):
        x, y, c, _ = _place()
        me = 4 * x + 2 * y + c
        buf[me] = v_ref[...]
        copies = []
        for k in range(1, ndev):
            peer = (x if not k & 4 else 1 - x, y if not k & 2 else 1 - y, c if not k & 1 else 1 - c)
            copies.append(pltpu.make_async_remote_copy(
                src_ref=v_ref, dst_ref=buf.at[me], send_sem=send.at[k - 1], recv_sem=recv.at[k - 1],
                device_id=peer, device_id_type=MESH))
            copies[-1].start()
        for k in range(1, ndev):
            peer = (x if not k & 4 else 1 - x, y if not k & 2 else 1 - y, c if not k & 1 else 1 - c)
            pltpu.make_async_remote_copy(
                src_ref=v_ref, dst_ref=buf.at[4 * peer[0] + 2 * peer[1] + peer[2]], send_sem=send.at[k - 1],
                recv_sem=recv.at[k - 1], device_id=peer, device_id_type=MESH).wait_recv()
        for cp in copies:
            cp.wait_send()
        tot = buf[0]
        for d in range(1, ndev):
            tot = tot + buf[d]
        o_ref[...] = tot

    return pl.pallas_call(
        body, name="all_reduce_small", in_specs=[pl.BlockSpec(memory_space=pltpu.VMEM)],
        out_specs=pl.BlockSpec(memory_space=pltpu.VMEM), out_shape=jax.ShapeDtypeStruct(vec.shape, F32),
        scratch_shapes=[pltpu.VMEM((ndev, R, D), F32), pltpu.SemaphoreType.DMA((ndev - 1,)),
                        pltpu.SemaphoreType.DMA((ndev - 1,))],
    )(vec)


def _row_tile(rows):
    for t in range(min(rows, 512) // 8 * 8, 7, -8):
        if rows % t == 0:
            return t
    return rows


def _sum_own_and_received(own, got, me, name):
    _, R, C = own.shape
    tr = _row_tile(R)

    def body(me_ref, own_ref, got_ref, o_ref):
        o_ref[...] = ((own_ref[...] + got_ref[0].astype(F32)) + got_ref[1].astype(F32)) + got_ref[2].astype(F32)

    return pl.pallas_call(
        body, name=name,
        grid_spec=pltpu.PrefetchScalarGridSpec(
            num_scalar_prefetch=1, grid=(R // tr,),
            in_specs=[pl.BlockSpec((None, tr, C), lambda i, me_ref: (me_ref[0], i, 0)),
                      pl.BlockSpec((3, tr, C), lambda i, me_ref: (0, i, 0))],
            out_specs=pl.BlockSpec((tr, C), lambda i, me_ref: (i, 0))),
        out_shape=jax.ShapeDtypeStruct((R, C), F32), compiler_params=_cp(),
    )(me, own, got)


def _adamw(gs, w, m, v, name):
    L, R, C = w.shape
    tr = _row_tile(R)
    ng = len(gs[0])
    c1 = 1.0 - ADAM_B1 ** ADAM_STEP
    c2 = 1.0 - ADAM_B2 ** ADAM_STEP

    def body(*refs):
        w_ref, m_ref, v_ref, g_ref, d_ref, nm_ref, nv_ref = refs[L * ng:]
        lay = pl.program_id(0)
        g = None
        for l in range(L):
            gl = refs[l * ng][...]
            for r in refs[l * ng + 1:(l + 1) * ng]:
                gl = gl + r[...]
            g = gl if g is None else jnp.where(lay == l, gl, g)
        nm = ADAM_B1 * m_ref[...] + (1.0 - ADAM_B1) * g
        nv = ADAM_B2 * v_ref[...] + (1.0 - ADAM_B2) * (g * g)
        g_ref[...] = g
        nm_ref[...] = nm
        nv_ref[...] = nv
        d_ref[...] = -ADAM_LR * ((nm / c1) / (jnp.sqrt(nv / c2) + ADAM_EPS) + ADAM_WD * w_ref[...])

    gspecs = [pl.BlockSpec((tr, C), lambda lay, i, l=l: (jnp.where(lay == l, i, 0), 0)) for l in range(L) for _ in range(ng)]
    spec = pl.BlockSpec((None, tr, C), lambda lay, i: (lay, i, 0))
    return pl.pallas_call(
        body, grid=(L, R // tr), name=name, in_specs=gspecs + [spec] * 3, out_specs=[spec] * 4,
        out_shape=[jax.ShapeDtypeStruct((L, R, C), F32)] * 4, compiler_params=_cp(2),
    )(*[g for gl in gs for g in gl], w, m, v)


def _rope_tables(T):
    inv = 1.0 / (ROPE_THETA ** (jnp.arange(0, HD, 2, dtype=F32) / HD))
    ang = jnp.arange(T, dtype=F32)[:, None] * inv[None, :]
    c, s = jnp.cos(ang), jnp.sin(ang)
    return jnp.tile(jnp.concatenate([c, c], axis=1), (1, 2)), jnp.tile(jnp.concatenate([-s, s], axis=1), (1, 2))


def _dup_cols(w):
    return jnp.broadcast_to(w.reshape(D, NKV, 1, HD), (D, NKV, 2, HD)).reshape(D, KVW)


def kernel(x, p, mix_pre_g, mix_post_g, ffn_pre_g, ffn_post_g, pool_w, pool_scale, kv_norm_g, w_k, w_v, w_q, w_o, sinks, w_ff_gate, w_ff_up, w_ff_down, ple_norm_g, w_ple_gate, w_ple_proj, loss_target, m_mix_pre_g, m_mix_post_g, m_ffn_pre_g, m_ffn_post_g, m_pool_w, m_pool_scale, m_kv_norm_g, m_w_k, m_w_v, m_w_q, m_w_o, m_sinks, m_w_ff_gate, m_w_ff_up, m_w_ff_down, m_ple_norm_g, m_w_ple_gate, m_w_ple_proj, v_mix_pre_g, v_mix_post_g, v_ffn_pre_g, v_ffn_post_g, v_pool_w, v_pool_scale, v_kv_norm_g, v_w_k, v_w_v, v_w_q, v_w_o, v_sinks, v_w_ff_gate, v_w_ff_up, v_w_ff_down, v_ple_norm_g, v_w_ple_gate, v_w_ple_proj):
    depth = mix_pre_g.shape[0]
    n_pool = pool_w.shape[0]
    n_attn = w_q.shape[0]
    T = x.shape[1]
    h = x[0]
    p = p[:, 0]
    tgt = loss_target[0]
    me = (2 * lax.axis_index("x") + lax.axis_index("y")).astype(jnp.int32).reshape(1)

    def t12(a):
        return jnp.transpose(a, (0, 2, 1))

    ff_t = {k: tuple(t12(a) for a in v) for k, v in dict(
        w_ff_gate=(w_ff_gate, m_w_ff_gate, v_w_ff_gate), w_ff_up=(w_ff_up, m_w_ff_up, v_w_ff_up)).items()}
    c_pool, c_wk, c_wv, c_wq, c_wo, c_wg, c_wu, c_wd, c_wpg, c_wpp = (w.astype(CDT) for w in (
        pool_w, w_k, w_v, w_q, w_o, ff_t["w_ff_gate"][0], ff_t["w_ff_up"][0], w_ff_down, w_ple_gate, w_ple_proj))
    c_wg, c_wu = t12(c_wg), t12(c_wu)

    def ffn_srcs(l):
        return [(c_wg, l), (c_wu, l), (c_wd, l)]

    def ple_srcs(l):
        return [(c_wpg, l), (c_wpp, l)]

    (gpw,) = _run_comm(_gather_comm([(c_pool, None)]), "gather_pool_w")
    wff, wple, wattn = {}, {}, {}
    pw = jnp.transpose(gpw, (1, 2, 0, 3, 4)).reshape(n_pool, len(POOL_WINDOWS), PG, PG)
    kv_g = kv_norm_g.reshape(1, D)
    cos, sin = _rope_tables(T)

    pscale = _all_reduce_small(jnp.pad(
        lax.dynamic_update_slice(jnp.zeros((n_pool, D), F32), pool_scale, (0, me[0] * (D // NSH))),
        ((0, 8 - n_pool), (0, 0))))[:n_pool] * 0.5

    saved = []
    kk = vv = wk2 = wv2 = None
    for l in range(depth):
        s = {"h": h}
        if l < n_pool:
            (h1,), got = _pool_fwd(h, mix_pre_g, pw, pscale, mix_post_g, l,
                                   comm=_GatherHalves(ffn_srcs(0)) if l == 0 else None)
            if l == 0:
                wff[0] = got
        else:
            j = l - n_pool
            wq, wo = wattn[l]
            s["q"] = _q_fwd(h, mix_pre_g, wq, cos, sin, l, j)
            s["o"] = _attn_fwd(s["q"], kk, vv, sinks, l, j)
            h1, s["m"] = _o_fwd(s["o"], h, wo, mix_post_g, l, j)
        s["h1"] = h1
        srcs = []
        if l + 1 < depth:
            srcs = ffn_srcs(l + 1)
            if l + 1 >= n_pool:
                srcs += [(c_wq, l + 1 - n_pool), (c_wo, l + 1 - n_pool)]
            if l + 1 == n_pool:
                srcs += [(c_wk, None), (c_wv, None)]
        if l == 0:
            srcs += ple_srcs(0)
        (h2, s["a"], s["b"], s["f"]), got = _ffn_fwd(
            h1, ffn_pre_g, *wff[l], ffn_post_g, l, comm=_gather_comm(srcs) if srcs else None)
        if l == 0:
            wple[0] = got[-2:]
        if l + 1 < depth:
            wff[l + 1] = got[:3]
            if l + 1 >= n_pool:
                wattn[l + 1] = got[3:5]
            if l + 1 == n_pool:
                wk2 = _dup_cols(got[5].reshape(D, NKV * HD))
                wv2 = _dup_cols(got[6].reshape(D, NKV * HD))
        s["h2"] = h2
        if l + 1 == depth:
            dh, loss_row = _ple_fwd_loss(h2, p, ple_norm_g, *wple[l], tgt, l)
            saved.append(s)
            break
        (h,), got = _ple_fwd(h2, p, ple_norm_g, *wple[l], l, comm=_gather_comm(ple_srcs(l + 1)))
        wple[l + 1] = got
        if l == n_pool - 1:
            s["hkv"] = h
            kk, vv = _kv_fwd(h, kv_g, wk2, wv2, cos, sin)
        saved.append(s)

    entries, pending, to_swap = [], [], []

    def add_grad(k, l, g32, g16):
        def by_shard(g):
            return g.reshape(NSH, -1, g.shape[-1])
        e = dict(k=k, l=l, own=by_shard(g32), sent=by_shard(g16))
        entries.append(e)
        pending.append(e)

    def carry():
        take, swap = list(pending), list(to_swap)
        del pending[:], to_swap[:]
        return (take, swap), _join_comms([_scatter_comm([e["sent"] for e in take]) if take else None,
                                          _swap_comm([e["part"] for e in swap]) if swap else None])

    def landed(taken, got):
        take, swap = taken
        for e, g in zip(take, got[:len(take)]):
            e["part"] = _sum_own_and_received(e["own"], g, me, f"sum_{e['k']}_{e['l']}")
            to_swap.append(e)
        for e, g in zip(swap, got[len(take):]):
            e["sib"] = g

    small = {k: [None] * depth for k in ("mix_pre", "mix_post", "ffn_pre", "ffn_post", "ple")}
    dpool_scale = [None] * n_pool
    dsinks = [None] * n_attn
    dks, dkhs, dvs, dvhs = [], [], [], []
    for l in reversed(range(depth)):
        s = saved[l]
        if l == n_pool - 1:
            dh, hk, dk0, dvc, dkv_g = _kv_bwd(dh, s["hkv"], dks, dkhs, dvs, dvhs, kv_g, wk2, wv2, cos, sin)
            add_grad("w_k", None, *_fold_dup(_mm_tn(hk, dk0, "dw_k")[0][0], "fold_w_k"))
            add_grad("w_v", None, *_fold_dup(_mm_tn(hk, dvc, "dw_v")[0][0], "fold_w_v"))
        dh, pn, dz, dpp, pc, small["ple"][l] = _ple_bwd(dh, s["h2"], p, ple_norm_g, *wple[l], l)
        add_grad("w_ple_gate", l, *_mm_tn(pn, dz, f"dw_ple_gate_{l}"))
        add_grad("w_ple_proj", l, *_mm_tn(pc, dpp, f"dw_ple_proj_{l}", n_split=NSH))
        take, comm = carry()
        (dh, fn, df, act, da, db, small["ffn_pre"][l], small["ffn_post"][l]), got = _ffn_bwd(
            dh, s["h1"], s["f"], s["a"], s["b"], ffn_pre_g, *wff[l], ffn_post_g, l, comm=comm)
        landed(take, got)
        add_grad("w_ff_gate", l, *_mm_tn(da, fn, f"dw_ff_gate_{l}"))
        take, comm = carry() if l == 0 else (([], []), None)
        res = _mm_tn(db, fn, f"dw_ff_up_{l}", comm=comm)
        if comm is not None:
            res, got = res
            landed(take, got)
        add_grad("w_ff_up", l, *res)
        take, comm = carry() if l == 0 else (([], []), None)
        res = _mm_tn(act, df, f"dw_ff_down_{l}", comm=comm)
        if comm is not None:
            res, got = res
            landed(take, got)
        add_grad("w_ff_down", l, *res)
        if l < n_pool:
            take, comm = carry() if l == 0 else (([], []), None)
            (dpool, dpw, dpool_scale[l], small["mix_post"][l]), got = _pool_bwd_a(
                dh, s["h"], mix_pre_g, pw, pscale, mix_post_g, l, comm=comm)
            landed(take, got)
            dpw = jnp.transpose(dpw.reshape(len(POOL_WINDOWS), NSH, PG // NSH, PG), (1, 0, 2, 3))
            dpw = dpw.reshape(NSH, len(POOL_WINDOWS) * PG // NSH, PG)
            add_grad("pool_w", l, dpw, dpw.astype(CDT))
            (dh, small["mix_pre"][l]), _ = _pool_bwd_b(dpool, dh, s["h"], mix_pre_g, l)
        else:
            j = l - n_pool
            wq, wo = wattn[l]
            do, dm, small["mix_post"][l] = _o_bwd(dh, s["m"], wo, mix_post_g, l, j)
            add_grad("w_o", j, *_mm_tn(s["o"], dm, f"dw_o_{l}"))
            dq, dk, dkh, dv, dvh, dsinks[j] = _attn_bwd(s["q"], kk, vv, s["o"], do, sinks, l, j)
            dks.append(dk); dkhs.append(dkh); dvs.append(dv); dvhs.append(dvh)
            dh, hn, dq0, small["mix_pre"][l] = _q_bwd(dq, dh, s["h"], mix_pre_g, wq, cos, sin, l, j)
            add_grad("w_q", j, *_mm_tn(hn, dq0, f"dw_q_{l}"))
    rest = 0
    while pending or to_swap:
        take, comm = carry()
        landed(take, _run_comm(comm, f"exchange_rest_{rest}"))
        rest += 1
    grad_x = dh[None]

    rows = [jnp.concatenate(small[k], axis=0) for k in ("mix_pre", "mix_post", "ffn_pre", "ffn_post", "ple")]
    rows += [dkv_g, jnp.concatenate(dpool_scale, axis=0)]
    rows += [jnp.pad(jnp.concatenate(dsinks, axis=0), ((0, 0), (0, D - LW))), jnp.pad(loss_row, ((0, 0), (0, D - 128)))]
    vec = jnp.concatenate(rows, axis=0)
    nrow = vec.shape[0]
    vec = _all_reduce_small(jnp.pad(vec, ((0, -nrow % 8), (0, 0))))
    o = 0
    red = {}
    for k in ("mix_pre_g", "mix_post_g", "ffn_pre_g", "ffn_post_g", "ple_norm_g"):
        red[k] = vec[o:o + depth]
        o += depth
    red["kv_norm_g"] = vec[o:o + 1]
    red["pool_scale"] = lax.dynamic_slice(vec[o + 1:o + 1 + n_pool], (0, me[0] * (D // NSH)), (n_pool, D // NSH))
    o += 1 + n_pool
    red["sinks"] = vec[o:o + n_attn, :GQ * NKV]
    loss = vec[o + n_attn, 0]


    given = dict(
        mix_pre_g=(mix_pre_g, m_mix_pre_g, v_mix_pre_g), mix_post_g=(mix_post_g, m_mix_post_g, v_mix_post_g),
        ffn_pre_g=(ffn_pre_g, m_ffn_pre_g, v_ffn_pre_g), ffn_post_g=(ffn_post_g, m_ffn_post_g, v_ffn_post_g),
        pool_w=(pool_w, m_pool_w, v_pool_w), pool_scale=(pool_scale, m_pool_scale, v_pool_scale),
        kv_norm_g=(kv_norm_g, m_kv_norm_g, v_kv_norm_g), w_k=(w_k, m_w_k, v_w_k), w_v=(w_v, m_w_v, v_w_v),
        w_q=(w_q, m_w_q, v_w_q), w_o=(w_o, m_w_o, v_w_o), sinks=(sinks, m_sinks, v_sinks),
        w_ff_gate=ff_t["w_ff_gate"], w_ff_up=ff_t["w_ff_up"],
        w_ff_down=(w_ff_down, m_w_ff_down, v_w_ff_down), ple_norm_g=(ple_norm_g, m_ple_norm_g, v_ple_norm_g),
        w_ple_gate=(w_ple_gate, m_w_ple_gate, v_w_ple_gate), w_ple_proj=(w_ple_proj, m_w_ple_proj, v_w_ple_proj))
    results = {}
    for k, g in red.items():
        w, m, v = given[k]
        outs = _adamw([[g.reshape(-1, g.shape[-1])]], *(t.reshape(1, -1, t.shape[-1]) for t in (w, m, v)), f"adamw_{k}")
        results[k] = [t.reshape(w.shape) for t in outs]
    per_layer = {}
    for e in entries:
        per_layer.setdefault(e["k"], {})[e["l"] or 0] = [e["part"], e["sib"]]
    for k, layers in per_layer.items():
        w, m, v = given[k]
        gs = [layers[l] for l in range(len(layers))]
        outs = _adamw(gs, *(t.reshape((len(gs),) + gs[0][0].shape) for t in (w, m, v)), f"adamw_{k}")
        results[k] = [t12(t) if k in ff_t else t.reshape(w.shape) for t in outs]

    order = ["mix_pre_g", "mix_post_g", "ffn_pre_g", "ffn_post_g", "pool_w", "pool_scale", "kv_norm_g", "w_k", "w_v",
             "w_q", "w_o", "sinks", "w_ff_gate", "w_ff_up", "w_ff_down", "ple_norm_g", "w_ple_gate", "w_ple_proj"]
    return (loss, grad_x, *[results[k][0] for k in order], *[results[k][1] for k in order],
            *[results[k][2] for k in order], *[results[k][3] for k in order])
```
